```python
import jax, jax.numpy as jnp
from jax import lax
import numpy as np

D_MODEL = 2048
BATCH = 2
SEQ = 4096
DEPTH = 1

GRID_W = 64
N_MEM = 256

NA_HEADS = 8
NA_HEAD_DIM = 128
NA_MAX_ROWS = 8
NA_COLS = 16

RET_HEADS = 8
RET_QK_DIM = 128
RET_V_DIM = 256
RET_CHUNK = 128
ROPE_BASE = 10000.0

XA_HEADS = 4
XA_HEAD_DIM = 256

D_FF = 4 * D_MODEL
N_BRANCH = 3
EPS = 1e-6

NA_W = NA_HEADS * NA_HEAD_DIM
RET_QK_W = RET_HEADS * RET_QK_DIM
RET_V_W = RET_HEADS * RET_V_DIM
XA_W = XA_HEADS * XA_HEAD_DIM
IN_SPLITS = (NA_W, NA_W, NA_W, RET_QK_W, RET_QK_W, RET_V_W, RET_V_W, XA_W, D_MODEL, D_MODEL, D_MODEL)
D_IN = sum(IN_SPLITS)

kernel_name = 'hybrid_na_retention_memxattn_encoder'


def rmsnorm(x, g):
    xf = x.astype(jnp.float32)
    y = xf * lax.rsqrt(jnp.mean(jnp.square(xf), axis=-1, keepdims=True) + EPS)
    return (y * g.astype(jnp.float32)).astype(x.dtype)


def split_heads(t, n_heads):
    b, s, w = t.shape
    return t.reshape(b, s, n_heads, w // n_heads).transpose(0, 2, 1, 3)


def merge_heads(t):
    b, h, s, d = t.shape
    return t.transpose(0, 2, 1, 3).reshape(b, s, h * d)


def rope(t):
    s, d = t.shape[2], t.shape[3]
    half = d // 2
    inv = jnp.power(jnp.float32(ROPE_BASE), -jnp.arange(half, dtype=jnp.float32) / half)
    ang = jnp.arange(s, dtype=jnp.float32)[:, None] * inv[None, :]
    cos = jnp.cos(ang).astype(t.dtype)
    sin = jnp.sin(ang).astype(t.dtype)
    t1, t2 = t[..., :half], t[..., half:]
    return jnp.concatenate([t1 * cos - t2 * sin, t1 * sin + t2 * cos], axis=-1)


def neighbourhood_attention(q, k, v, rpb):
    b, h, s, dh = q.shape
    rows = s // GRID_W
    kr = min(NA_MAX_ROWS, rows)
    kc = NA_COLS
    r = jnp.arange(rows)
    row_start = jnp.clip(r - kr // 2, 0, rows - kr)
    row_idx = row_start[:, None] + jnp.arange(kr)[None, :]
    c = jnp.arange(GRID_W)
    col_start = jnp.clip(c - kc // 2, 0, GRID_W - kc)
    col_ok = (c[None, :] >= col_start[:, None]) & (c[None, :] < col_start[:, None] + kc)
    qg = q.reshape(b, h, rows, GRID_W, dh)
    kg = k.reshape(b, h, rows, GRID_W, dh)[:, :, row_idx]
    vg = v.reshape(b, h, rows, GRID_W, dh)[:, :, row_idx]
    sc = jnp.einsum('bhrqd,bhrkwd->bhrqkw', qg, kg).astype(jnp.float32)
    dr = row_idx - r[:, None] + (NA_MAX_ROWS - 1)
    dc = jnp.clip(c[None, :] - c[:, None], -(kc - 1), kc - 1) + (kc - 1)
    bias = rpb[:, dr[:, None, :, None], dc[None, :, None, :]].astype(jnp.float32)
    sc = sc + bias[None]
    sc = jnp.where(col_ok[:, None, :], sc, -jnp.inf)
    p = jax.nn.softmax(sc.reshape(b, h, rows, GRID_W, kr * GRID_W), axis=-1)
    p = p.reshape(sc.shape).astype(v.dtype)
    o = jnp.einsum('bhrqkw,bhrkwd->bhrqd', p, vg)
    return o.reshape(b, h, s, dh)


def retention_one_direction(q, k, v, log_g, strict):
    b, h, s, dk = q.shape
    dv = v.shape[-1]
    c = RET_CHUNK
    n = s // c
    dt = q.dtype
    qc = q.reshape(b, h, n, c, dk)
    kc = k.reshape(b, h, n, c, dk)
    vc = v.reshape(b, h, n, c, dv)
    i = jnp.arange(c, dtype=jnp.float32)
    diff = i[:, None] - i[None, :]
    mask = (diff > 0) if strict else (diff >= 0)
    d_intra = jnp.where(mask[None], jnp.exp(log_g[:, None, None] * jnp.maximum(diff, 0.0)[None]), 0.0).astype(dt)
    sc = jnp.einsum('bhnid,bhnjd->bhnij', qc, kc) * d_intra[:, None]
    o_intra = jnp.einsum('bhnij,bhnje->bhnie', sc, vc)
    k_decay = jnp.exp(log_g[:, None] * (c - 1 - i)[None]).astype(dt)
    q_decay = jnp.exp(log_g[:, None] * (i + 1)[None]).astype(dt)
    kv = jnp.einsum('bhncd,bhnce->nbhde', kc * k_decay[:, None, :, None], vc)
    chunk_decay = jnp.exp(log_g * c).astype(dt)[:, None, None]

    def step(state, kv_n):
        return chunk_decay * state + kv_n, state

    _, states = lax.scan(step, jnp.zeros((b, h, dk, dv), kv.dtype), kv)
    o_inter = jnp.einsum('bhncd,nbhde->bhnce', qc * q_decay[:, None, :, None], states)
    return (o_intra + o_inter).reshape(b, h, s, dv)


def head_group_norm(o, g):
    of = o.astype(jnp.float32)
    mu = jnp.mean(of, axis=-1, keepdims=True)
    var = jnp.mean(jnp.square(of - mu), axis=-1, keepdims=True)
    y = (of - mu) * lax.rsqrt(var + EPS)
    return (merge_heads(y) * g.astype(jnp.float32)).astype(o.dtype)


def setup_inputs(seed: int = 0) -> dict:
    key = jax.random.key(seed)
    ks = jax.random.split(key, 24)
    f32 = jnp.float32

    def nrm(k, shape, scale):
        return jax.random.normal(k, shape, f32) * scale

    def gain(k, shape):
        return 1.0 + 0.02 * jax.random.normal(k, shape, f32)

    e = 5.0 + np.arange(RET_HEADS, dtype=np.float32)
    base_logit = jnp.asarray(np.log(np.power(2.0, e) - 1.0).astype(np.float32))
    return {
        'x': nrm(ks[0], (BATCH, SEQ, D_MODEL), 1.0),
        'mem': nrm(ks[1], (BATCH, N_MEM, D_MODEL), 1.0),
        'norm_mix_g': gain(ks[2], (DEPTH, D_MODEL)),
        'w_in': nrm(ks[3], (DEPTH, D_MODEL, D_IN), D_MODEL ** -0.5),
        'na_q_norm_g': gain(ks[4], (DEPTH, NA_HEAD_DIM)),
        'na_k_norm_g': gain(ks[5], (DEPTH, NA_HEAD_DIM)),
        'na_rpb': nrm(ks[6], (DEPTH, NA_HEADS, 2 * NA_MAX_ROWS - 1, 2 * NA_COLS - 1), 0.02),
        'ret_decay_logit_fwd': base_logit[None] + nrm(ks[7], (DEPTH, RET_HEADS), 0.01),
        'ret_decay_logit_bwd': base_logit[None] + nrm(ks[8], (DEPTH, RET_HEADS), 0.01),
        'ret_gn_g': gain(ks[9], (DEPTH, RET_V_W)),
        'mem_norm_g': gain(ks[10], (DEPTH, D_MODEL)),
        'w_mem_kv': nrm(ks[11], (DEPTH, D_MODEL, 2 * XA_W), D_MODEL ** -0.5),
        'xa_q_norm_g': gain(ks[12], (DEPTH, XA_HEAD_DIM)),
        'xa_k_norm_g': gain(ks[13], (DEPTH, XA_HEAD_DIM)),
        'w_br_na': nrm(ks[14], (DEPTH, NA_W, D_MODEL), NA_W ** -0.5),
        'w_br_ret': nrm(ks[15], (DEPTH, RET_V_W, D_MODEL), RET_V_W ** -0.5),
        'w_br_mem': nrm(ks[16], (DEPTH, XA_W, D_MODEL), XA_W ** -0.5),
        'w_out': nrm(ks[17], (DEPTH, D_MODEL, D_MODEL), D_MODEL ** -0.5),
        'norm_ffn_g': gain(ks[18], (DEPTH, D_MODEL)),
        'w_ff1': nrm(ks[19], (DEPTH, D_MODEL, D_FF), D_MODEL ** -0.5),
        'w_ff2': nrm(ks[20], (DEPTH, D_FF, D_MODEL), D_FF ** -0.5),
    }


def reference(x, mem, norm_mix_g, w_in, na_q_norm_g, na_k_norm_g, na_rpb,
              ret_decay_logit_fwd, ret_decay_logit_bwd, ret_gn_g, mem_norm_g, w_mem_kv,
              xa_q_norm_g, xa_k_norm_g, w_br_na, w_br_ret, w_br_mem, w_out,
              norm_ffn_g, w_ff1, w_ff2):
    split_at = [int(v) for v in np.cumsum(IN_SPLITS)[:-1]]
    for l in range(DEPTH):
        h = rmsnorm(x, norm_mix_g[l])
        proj = h @ w_in[l]
        (na_q, na_k, na_v, rq, rk, rv, rg, xq, g_na, g_ret, g_mem) = jnp.split(proj, split_at, axis=-1)

        qa = rmsnorm(split_heads(na_q, NA_HEADS), na_q_norm_g[l]) * (NA_HEAD_DIM ** -0.5)
        ka = rmsnorm(split_heads(na_k, NA_HEADS), na_k_norm_g[l])
        va = split_heads(na_v, NA_HEADS)
        o_na = merge_heads(neighbourhood_attention(qa, ka, va, na_rpb[l]))

        qr = rope(split_heads(rq, RET_HEADS))
        kr = rope(split_heads(rk, RET_HEADS)) * (RET_QK_DIM ** -0.5)
        vr = split_heads(rv, RET_HEADS)
        lg_f = jax.nn.log_sigmoid(ret_decay_logit_fwd[l].astype(jnp.float32))
        lg_b = jax.nn.log_sigmoid(ret_decay_logit_bwd[l].astype(jnp.float32))
        o_f = retention_one_direction(qr, kr, vr, lg_f, False)
        o_b = jnp.flip(retention_one_direction(jnp.flip(qr, 2), jnp.flip(kr, 2), jnp.flip(vr, 2), lg_b, True), 2)
        o_ret = head_group_norm(o_f + o_b, ret_gn_g[l]) * jax.nn.silu(rg)

        mkv = rmsnorm(mem, mem_norm_g[l]) @ w_mem_kv[l]
        mk, mv = jnp.split(mkv, 2, axis=-1)
        qx = rmsnorm(split_heads(xq, XA_HEADS), xa_q_norm_g[l]) * (XA_HEAD_DIM ** -0.5)
        kx = rmsnorm(split_heads(mk, XA_HEADS), xa_k_norm_g[l])
        vx = split_heads(mv, XA_HEADS)
        px = jax.nn.softmax(jnp.einsum('bhsd,bhmd->bhsm', qx, kx).astype(jnp.float32), axis=-1).astype(vx.dtype)
        o_mem = merge_heads(jnp.einsum('bhsm,bhmd->bhsd', px, vx))

        merged = (jax.nn.sigmoid(g_na) * (o_na @ w_br_na[l])
                  + jax.nn.sigmoid(g_ret) * (o_ret @ w_br_ret[l])
                  + jax.nn.sigmoid(g_mem) * (o_mem @ w_br_mem[l]))
        x = x + merged @ w_out[l]

        h2 = rmsnorm(x, norm_ffn_g[l])
        x = x + jnp.square(jax.nn.relu(h2 @ w_ff1[l])) @ w_ff2[l]
    return x
```

```python
import functools

import jax
import jax.numpy as jnp
import numpy as np
from jax import lax
from jax.experimental import pallas as pl
from jax.experimental.pallas import tpu as pltpu

F32 = jnp.float32
BF16 = jnp.bfloat16

LANE = 128
EPS = 1e-6
NEG = -1e30

GRID_W = 64
NA_HEADS = 8
NA_HEAD_DIM = 128
NA_MAX_ROWS = 8
NA_COLS = 16
NA_DR = 2 * NA_MAX_ROWS - 1
NA_DC = 2 * NA_COLS - 1
NA_TBL = NA_DR + 1
NA_WIN = NA_MAX_ROWS + 2

RET_HEADS = 8
RET_QK_DIM = 128
RET_V_DIM = 256
RET_CHUNK = 128
ROPE_BASE = 10000.0

XA_HEADS = 4
XA_HEAD_DIM = 256

SLAB_NA_Q, SLAB_NA_K, SLAB_NA_V = 0, 8, 16
SLAB_RQ, SLAB_RK, SLAB_RV, SLAB_RG = 24, 32, 40, 56
SLAB_XQ = 72
SLAB_G_NA, SLAB_G_RET, SLAB_G_MEM = 80, 96, 112


def _params(sem, vmem_mib):
    return pltpu.CompilerParams(dimension_semantics=sem, vmem_limit_bytes=vmem_mib * 2**20)


def _rms(x, g):
    return x * lax.rsqrt(jnp.mean(x * x, axis=-1, keepdims=True) + EPS) * g


def _norm_proj_kernel(x_ref, g_ref, w_ref, o_ref, hn_ref):
    @pl.when(pl.program_id(1) == 0)
    def _():
        hn_ref[...] = _rms(x_ref[...], g_ref[...]).astype(BF16)

    acc = jnp.dot(hn_ref[...], w_ref[...], preferred_element_type=F32)
    for j in range(o_ref.shape[0]):
        o_ref[j] = acc[:, j * LANE:(j + 1) * LANE].astype(BF16)


def _norm_proj(x2d, g, w, tm, tn):
    m, k = x2d.shape
    n = w.shape[1]
    return pl.pallas_call(
        _norm_proj_kernel,
        grid=(m // tm, n // tn),
        in_specs=[
            pl.BlockSpec((tm, k), lambda i, j: (i, 0)),
            pl.BlockSpec((1, k), lambda i, j: (0, 0)),
            pl.BlockSpec((k, tn), lambda i, j: (0, j)),
        ],
        out_specs=pl.BlockSpec((tn // LANE, tm, LANE), lambda i, j: (j, i, 0)),
        out_shape=jax.ShapeDtypeStruct((n // LANE, m, LANE), BF16),
        scratch_shapes=[pltpu.VMEM((tm, k), BF16)],
        compiler_params=_params(("parallel", "arbitrary"), 48),
        name="norm_proj",
    )(x2d, g.reshape(1, k), w)


def _na_table_kernel(rpb_ref, tl_ref, tr_ref):
    h = pl.program_id(0)
    qc = lax.broadcasted_iota(jnp.int32, (GRID_W, LANE), 0)
    lane = lax.broadcasted_iota(jnp.int32, (GRID_W, LANE), 1)
    kc = lane & (GRID_W - 1)
    d = jnp.clip(kc - qc, -(NA_COLS - 1), NA_COLS - 1) + (NA_COLS - 1)
    cs = jnp.clip(qc - NA_COLS // 2, 0, GRID_W - NA_COLS)
    col_ok = (kc >= cs) & (kc < cs + NA_COLS)
    left = lane < GRID_W

    def body(dr, _):
        base = (h * NA_DR + dr) * NA_DC
        val = jnp.zeros((GRID_W, LANE), F32)
        for dd in range(NA_DC):
            val = jnp.where(d == dd, rpb_ref[base + dd], val)
        t = jnp.where(col_ok, val, NEG)
        tl_ref[0, dr] = jnp.where(left, t, 0.0)
        tr_ref[0, dr] = jnp.where(left, 0.0, t)
        return 0

    lax.fori_loop(0, NA_DR, body, 0)
    tl_ref[0, NA_DR] = jnp.where(left, NEG, 0.0)
    tr_ref[0, NA_DR] = jnp.where(left, 0.0, NEG)


def _na_tables(rpb):
    shp = jax.ShapeDtypeStruct((NA_HEADS, NA_TBL, GRID_W, LANE), F32)
    spec = pl.BlockSpec((1, NA_TBL, GRID_W, LANE), lambda h: (h, 0, 0, 0))
    return pl.pallas_call(
        _na_table_kernel,
        grid=(NA_HEADS,),
        in_specs=[pl.BlockSpec(memory_space=pltpu.SMEM)],
        out_specs=[spec, spec],
        out_shape=[shp, shp],
        compiler_params=_params(("arbitrary",), 16),
        name="na_tables",
    )(rpb.reshape(-1))


def _na_kernel(q_ref, k_ref, v_ref, tl_ref, tr_ref, gq_ref, gk_ref, o_ref, kn_ref):
    s_len = q_ref.shape[1]
    rows = s_len // GRID_W
    pair_tok = 2 * GRID_W
    win_tok = NA_WIN * GRID_W
    norm_rows = 512
    gq = gq_ref[...] * (NA_HEAD_DIM ** -0.5)
    gk = gk_ref[...]

    def knorm(c, _):
        sl = pl.ds(pl.multiple_of(c * norm_rows, norm_rows), norm_rows)
        kn_ref[sl, :] = _rms(k_ref[0, sl, :].astype(F32), gk).astype(BF16)
        return 0

    lax.fori_loop(0, s_len // norm_rows, knorm, 0)

    def pair(p, _):
        r0 = 2 * p
        ws = jnp.clip(r0 - NA_MAX_ROWS // 2, 0, rows - NA_WIN)
        qsl = pl.ds(pl.multiple_of(p * pair_tok, pair_tok), pair_tok)
        wsl = pl.ds(pl.multiple_of(ws * GRID_W, pair_tok), win_tok)
        qn = _rms(q_ref[0, qsl, :].astype(F32), gq).astype(BF16)
        s = lax.dot_general(qn, kn_ref[wsl, :], (((1,), (1,)), ((), ())),
                            preferred_element_type=F32)
        bias_rows = []
        for qr in range(2):
            r = r0 + qr
            rs = jnp.clip(r - NA_MAX_ROWS // 2, 0, rows - NA_MAX_ROWS)
            tiles = []
            for t in range(NA_WIN // 2):
                ka = ws + 2 * t
                kb = ka + 1
                ia = jnp.where((ka >= rs) & (ka < rs + NA_MAX_ROWS), ka - r + (NA_MAX_ROWS - 1), NA_DR)
                ib = jnp.where((kb >= rs) & (kb < rs + NA_MAX_ROWS), kb - r + (NA_MAX_ROWS - 1), NA_DR)
                tiles.append(tl_ref[0, ia] + tr_ref[0, ib])
            bias_rows.append(jnp.concatenate(tiles, axis=1))
        s = s + jnp.concatenate(bias_rows, axis=0)
        e = jnp.exp(s - jnp.max(s, axis=-1, keepdims=True))
        l = jnp.sum(e, axis=-1, keepdims=True)
        o = jnp.dot(e.astype(BF16), v_ref[0, wsl, :], preferred_element_type=F32)
        o_ref[qsl, :] = (o / l).astype(BF16)
        return 0

    lax.fori_loop(0, rows // 2, pair, 0)


def _na_attention(proj, tl, tr, gq, gk, batch, s_len):
    t = batch * s_len
    tbl_spec = pl.BlockSpec((1, NA_TBL, GRID_W, LANE), lambda b, h: (h, 0, 0, 0))
    g_spec = pl.BlockSpec((1, NA_HEAD_DIM), lambda b, h: (0, 0))
    return pl.pallas_call(
        _na_kernel,
        grid=(batch, NA_HEADS),
        in_specs=[
            pl.BlockSpec((1, s_len, LANE), lambda b, h: (SLAB_NA_Q + h, b, 0)),
            pl.BlockSpec((1, s_len, LANE), lambda b, h: (SLAB_NA_K + h, b, 0)),
            pl.BlockSpec((1, s_len, LANE), lambda b, h: (SLAB_NA_V + h, b, 0)),
            tbl_spec, tbl_spec, g_spec, g_spec,
        ],
        out_specs=pl.BlockSpec((s_len, NA_HEAD_DIM), lambda b, h: (b, h)),
        out_shape=jax.ShapeDtypeStruct((t, NA_HEADS * NA_HEAD_DIM), BF16),
        scratch_shapes=[pltpu.VMEM((s_len, NA_HEAD_DIM), BF16)],
        compiler_params=_params(("parallel", "parallel"), 32),
        name="na_attention",
    )(proj, proj, proj, tl, tr, gq.reshape(1, -1), gk.reshape(1, -1))


def _log_sigmoid(x):
    return -(jnp.maximum(-x, 0.0) + jnp.log1p(jnp.exp(-jnp.abs(x))))


def _ret_kernel(q_ref, k_ref, v_ref, g_ref, cos_ref, sin_ref, lf_ref, lb_ref, gn_ref,
                o_ref, qr_ref, kr_ref, sb_ref, sacc_ref):
    s_len = q_ref.shape[1]
    c = RET_CHUNK
    n_chunks = s_len // c
    half = RET_QK_DIM // 2

    lgf = _log_sigmoid(lf_ref[0][:, :1])
    lgb = _log_sigmoid(lb_ref[0][:, :1])
    ic = lax.broadcasted_iota(jnp.int32, (c, 1), 0).astype(F32)
    diff = (lax.broadcasted_iota(jnp.int32, (c, c), 0)
            - lax.broadcasted_iota(jnp.int32, (c, c), 1)).astype(F32)
    d_intra = jnp.where(diff >= 0, jnp.exp(lgf * jnp.maximum(diff, 0.0)),
                        jnp.exp(lgb * jnp.maximum(-diff, 0.0)))
    qdec_f = jnp.exp(lgf * (ic + 1.0))
    kdec_f = jnp.exp(lgf * (c - 1.0 - ic))
    qdec_b = jnp.exp(lgb * (c - ic))
    kdec_b = jnp.exp(lgb * ic)
    cd_f = jnp.exp(lgf * c)
    cd_b = jnp.exp(lgb * c)

    rope_rows = 512

    def rope(i, _):
        sl = pl.ds(pl.multiple_of(i * rope_rows, rope_rows), rope_rows)
        cos = cos_ref[sl, :]
        sin = sin_ref[sl, :]
        q = q_ref[0, sl, :].astype(F32)
        k = k_ref[0, sl, :].astype(F32)
        qr_ref[sl, :] = (q * cos + pltpu.roll(q, half, 1) * sin).astype(BF16)
        kr_ref[sl, :] = ((k * cos + pltpu.roll(k, half, 1) * sin) * (RET_QK_DIM ** -0.5)).astype(BF16)
        return 0

    lax.fori_loop(0, s_len // rope_rows, rope, 0)

    def chunk_v(sl):
        return jnp.concatenate([v_ref[0, sl, :], v_ref[1, sl, :]], axis=1)

    def kv_update(sl, kdec, cd):
        kd = (kr_ref[sl, :].astype(F32) * kdec).T.astype(BF16)
        kv = jnp.dot(kd, chunk_v(sl), preferred_element_type=F32)
        sacc_ref[...] = cd * sacc_ref[...] + kv

    sacc_ref[...] = jnp.zeros_like(sacc_ref)

    def bwd(t, _):
        n = n_chunks - 1 - t
        sl = pl.ds(pl.multiple_of(n * c, c), c)
        sb_ref[n] = sacc_ref[...].astype(BF16)
        kv_update(sl, kdec_b, cd_b)
        return 0

    lax.fori_loop(0, n_chunks, bwd, 0)

    sacc_ref[...] = jnp.zeros_like(sacc_ref)
    gn = gn_ref[0]

    def fwd(n, _):
        sl = pl.ds(pl.multiple_of(n * c, c), c)
        q = qr_ref[sl, :]
        qf32 = q.astype(F32)
        a = lax.dot_general(q, kr_ref[sl, :], (((1,), (1,)), ((), ())),
                            preferred_element_type=F32) * d_intra
        o = jnp.dot(a.astype(BF16), chunk_v(sl), preferred_element_type=F32)
        o += jnp.dot((qf32 * qdec_f).astype(BF16), sacc_ref[...].astype(BF16),
                     preferred_element_type=F32)
        o += jnp.dot((qf32 * qdec_b).astype(BF16), sb_ref[n], preferred_element_type=F32)
        kv_update(sl, kdec_f, cd_f)
        mu = jnp.mean(o, axis=-1, keepdims=True)
        oc = o - mu
        y = oc * lax.rsqrt(jnp.mean(oc * oc, axis=-1, keepdims=True) + EPS) * gn
        gate = jnp.concatenate([g_ref[0, sl, :], g_ref[1, sl, :]], axis=1).astype(F32)
        o_ref[sl, :] = (y * gate * jax.nn.sigmoid(gate)).astype(BF16)
        return 0

    lax.fori_loop(0, n_chunks, fwd, 0)


def _retention(proj, cos2, sin2, lf, lb, gn, batch, s_len):
    t = batch * s_len
    n_chunks = s_len // RET_CHUNK
    dec_spec = pl.BlockSpec((1, 1, LANE), lambda b, h: (h, 0, 0))
    return pl.pallas_call(
        _ret_kernel,
        grid=(batch, RET_HEADS),
        in_specs=[
            pl.BlockSpec((1, s_len, LANE), lambda b, h: (SLAB_RQ + h, b, 0)),
            pl.BlockSpec((1, s_len, LANE), lambda b, h: (SLAB_RK + h, b, 0)),
            pl.BlockSpec((2, s_len, LANE), lambda b, h: (SLAB_RV // 2 + h, b, 0)),
            pl.BlockSpec((2, s_len, LANE), lambda b, h: (SLAB_RG // 2 + h, b, 0)),
            pl.BlockSpec((s_len, RET_QK_DIM), lambda b, h: (0, 0)),
            pl.BlockSpec((s_len, RET_QK_DIM), lambda b, h: (0, 0)),
            dec_spec, dec_spec,
            pl.BlockSpec((1, 1, RET_V_DIM), lambda b, h: (h, 0, 0)),
        ],
        out_specs=pl.BlockSpec((s_len, RET_V_DIM), lambda b, h: (b, h)),
        out_shape=jax.ShapeDtypeStruct((t, RET_HEADS * RET_V_DIM), BF16),
        scratch_shapes=[
            pltpu.VMEM((s_len, RET_QK_DIM), BF16),
            pltpu.VMEM((s_len, RET_QK_DIM), BF16),
            pltpu.VMEM((n_chunks, RET_QK_DIM, RET_V_DIM), BF16),
            pltpu.VMEM((RET_QK_DIM, RET_V_DIM), F32),
        ],
        compiler_params=_params(("parallel", "parallel"), 48),
        name="retention",
    )(proj, proj, proj, proj, cos2, sin2, lf, lb, gn.reshape(RET_HEADS, 1, RET_V_DIM))


def _xa_kernel(q_ref, k_ref, v_ref, gq_ref, gk_ref, o_ref):
    s_len = q_ref.shape[1]
    tq = 512
    gq = gq_ref[...] * (XA_HEAD_DIM ** -0.5)
    k = jnp.concatenate([k_ref[0], k_ref[1]], axis=1).astype(F32)
    kn = _rms(k, gk_ref[...]).astype(BF16)
    v = jnp.concatenate([v_ref[0], v_ref[1]], axis=1)

    def body(i, _):
        sl = pl.ds(pl.multiple_of(i * tq, tq), tq)
        q = jnp.concatenate([q_ref[0, sl, :], q_ref[1, sl, :]], axis=1).astype(F32)
        qn = _rms(q, gq).astype(BF16)
        s = lax.dot_general(qn, kn, (((1,), (1,)), ((), ())), preferred_element_type=F32)
        e = jnp.exp(s - jnp.max(s, axis=-1, keepdims=True))
        l = jnp.sum(e, axis=-1, keepdims=True)
        o = jnp.dot(e.astype(BF16), v, preferred_element_type=F32)
        o_ref[sl, :] = (o / l).astype(BF16)
        return 0

    lax.fori_loop(0, s_len // tq, body, 0)


def _mem_xattn(proj, mkv, gq, gk, batch, s_len, n_mem):
    t = batch * s_len
    g_spec = pl.BlockSpec((1, XA_HEAD_DIM), lambda b, h: (0, 0))
    return pl.pallas_call(
        _xa_kernel,
        grid=(batch, XA_HEADS),
        in_specs=[
            pl.BlockSpec((2, s_len, LANE), lambda b, h: (SLAB_XQ // 2 + h, b, 0)),
            pl.BlockSpec((2, n_mem, LANE), lambda b, h: (h, b, 0)),
            pl.BlockSpec((2, n_mem, LANE), lambda b, h: (XA_HEADS + h, b, 0)),
            g_spec, g_spec,
        ],
        out_specs=pl.BlockSpec((s_len, XA_HEAD_DIM), lambda b, h: (b, h)),
        out_shape=jax.ShapeDtypeStruct((t, XA_HEADS * XA_HEAD_DIM), BF16),
        compiler_params=_params(("parallel", "parallel"), 32),
        name="mem_xattn",
    )(proj, mkv, mkv, gq.reshape(1, -1), gk.reshape(1, -1))


def _merge_kernel(ona_ref, oret_ref, omem_ref, wna_ref, wret_ref, wmem_ref,
                  gna_ref, gret_ref, gmem_ref, o_ref):
    y_na = jnp.dot(ona_ref[...], wna_ref[...], preferred_element_type=F32)
    y_ret = jnp.dot(oret_ref[...], wret_ref[...], preferred_element_type=F32)
    y_mem = jnp.dot(omem_ref[...], wmem_ref[...], preferred_element_type=F32)
    for j in range(gna_ref.shape[0]):
        cs = slice(j * LANE, (j + 1) * LANE)
        o_ref[:, cs] = (jax.nn.sigmoid(gna_ref[j].astype(F32)) * y_na[:, cs]
                        + jax.nn.sigmoid(gret_ref[j].astype(F32)) * y_ret[:, cs]
                        + jax.nn.sigmoid(gmem_ref[j].astype(F32)) * y_mem[:, cs]).astype(BF16)


def _merge(o_na, o_ret, o_mem, w_na, w_ret, w_mem, proj, tm, tn):
    t = o_na.shape[0]
    d = w_na.shape[1]
    ns = tn // LANE

    def lhs_spec(a):
        return pl.BlockSpec((tm, a.shape[1]), lambda i, j: (i, 0))

    def w_spec(w):
        return pl.BlockSpec((w.shape[0], tn), lambda i, j: (0, j))

    def gate_spec(first):
        return pl.BlockSpec((ns, tm, LANE), lambda i, j: (first // ns + j, i, 0))

    return pl.pallas_call(
        _merge_kernel,
        grid=(t // tm, d // tn),
        in_specs=[lhs_spec(o_na), lhs_spec(o_ret), lhs_spec(o_mem),
                  w_spec(w_na), w_spec(w_ret), w_spec(w_mem),
                  gate_spec(SLAB_G_NA), gate_spec(SLAB_G_RET), gate_spec(SLAB_G_MEM)],
        out_specs=pl.BlockSpec((tm, tn), lambda i, j: (i, j)),
        out_shape=jax.ShapeDtypeStruct((t, d), BF16),
        compiler_params=_params(("parallel", "arbitrary"), 48),
        name="merge",
    )(o_na, o_ret, o_mem, w_na, w_ret, w_mem, proj, proj, proj)


def _out_proj_kernel(m_ref, w_ref, x_ref, o_ref):
    o_ref[...] = x_ref[...] + jnp.dot(m_ref[...], w_ref[...], preferred_element_type=F32)


def _out_proj(merged, w, x2d, tm, tn):
    t, k = merged.shape
    d = w.shape[1]
    return pl.pallas_call(
        _out_proj_kernel,
        grid=(t // tm, d // tn),
        in_specs=[pl.BlockSpec((tm, k), lambda i, j: (i, 0)),
                  pl.BlockSpec((k, tn), lambda i, j: (0, j)),
                  pl.BlockSpec((tm, tn), lambda i, j: (i, j))],
        out_specs=pl.BlockSpec((tm, tn), lambda i, j: (i, j)),
        out_shape=jax.ShapeDtypeStruct((t, d), F32),
        compiler_params=_params(("parallel", "arbitrary"), 48),
        name="out_proj",
    )(merged, w, x2d)


def _ffn_kernel(x_ref, g_ref, w1_ref, w2_ref, o_ref, h_ref):
    @pl.when(pl.program_id(1) == 0)
    def _():
        x = x_ref[...]
        h_ref[...] = _rms(x, g_ref[...]).astype(BF16)
        o_ref[...] = x

    a = jnp.maximum(jnp.dot(h_ref[...], w1_ref[...], preferred_element_type=F32), 0.0)
    o_ref[...] += jnp.dot((a * a).astype(BF16), w2_ref[...], preferred_element_type=F32)


def _ffn(x1, g, w1, w2, tm, tf):
    t, d = x1.shape
    dff = w1.shape[1]
    return pl.pallas_call(
        _ffn_kernel,
        grid=(t // tm, dff // tf),
        in_specs=[pl.BlockSpec((tm, d), lambda i, f: (i, 0)),
                  pl.BlockSpec((1, d), lambda i, f: (0, 0)),
                  pl.BlockSpec((d, tf), lambda i, f: (0, f)),
                  pl.BlockSpec((tf, d), lambda i, f: (f, 0))],
        out_specs=pl.BlockSpec((tm, d), lambda i, f: (i, 0)),
        out_shape=jax.ShapeDtypeStruct((t, d), F32),
        scratch_shapes=[pltpu.VMEM((tm, d), BF16)],
        compiler_params=_params(("parallel", "arbitrary"), 48),
        name="ffn",
    )(x1, g.reshape(1, d), w1, w2)


def _rope_tables(s_len):
    half = RET_QK_DIM // 2
    inv = jnp.power(jnp.float32(ROPE_BASE), -jnp.arange(half, dtype=F32) / half)
    ang = jnp.arange(s_len, dtype=F32)[:, None] * inv[None, :]
    cos, sin = jnp.cos(ang), jnp.sin(ang)
    return jnp.concatenate([cos, cos], axis=1), jnp.concatenate([-sin, sin], axis=1)


def kernel(x, mem, norm_mix_g, w_in, na_q_norm_g, na_k_norm_g, na_rpb, ret_decay_logit_fwd, ret_decay_logit_bwd, ret_gn_g, mem_norm_g, w_mem_kv, xa_q_norm_g, xa_k_norm_g, w_br_na, w_br_ret, w_br_mem, w_out, norm_ffn_g, w_ff1, w_ff2):
    batch, s_len, d = x.shape
    n_mem = mem.shape[1]
    t = batch * s_len
    depth = w_in.shape[0]
    tm = min(1024, t)
    cos2, sin2 = _rope_tables(s_len)
    x2d = x.reshape(t, d)
    mem2d = mem.reshape(batch * n_mem, d)

    def lane_bcast(v):
        return jnp.broadcast_to(v.astype(F32)[:, None, None], (v.shape[0], 1, LANE))

    for l in range(depth):
        proj = _norm_proj(x2d, norm_mix_g[l], w_in[l].astype(BF16), tm, 512)
        mkv = _norm_proj(mem2d, mem_norm_g[l], w_mem_kv[l].astype(BF16), batch * n_mem, 1024)

        tl, tr = _na_tables(na_rpb[l])
        o_na = _na_attention(proj, tl, tr, na_q_norm_g[l], na_k_norm_g[l], batch, s_len)
        o_ret = _retention(proj, cos2, sin2, lane_bcast(ret_decay_logit_fwd[l]),
                           lane_bcast(ret_decay_logit_bwd[l]), ret_gn_g[l], batch, s_len)
        o_mem = _mem_xattn(proj, mkv, xa_q_norm_g[l], xa_k_norm_g[l], batch, s_len, n_mem)

        merged = _merge(o_na, o_ret, o_mem, w_br_na[l].astype(BF16), w_br_ret[l].astype(BF16),
                        w_br_mem[l].astype(BF16), proj, min(512, t), 512)
        x1 = _out_proj(merged, w_out[l].astype(BF16), x2d, min(512, t), 1024)
        x2d = _ffn(x1, norm_ffn_g[l], w_ff1[l].astype(BF16), w_ff2[l].astype(BF16), min(512, t), 512)
    return x2d.reshape(batch, s_len, d)
```

```python
import functools

import jax
import jax.numpy as jnp
import numpy as np
from jax import lax
from jax.experimental import pallas as pl
from jax.experimental.pallas import tpu as pltpu

F32 = jnp.float32
BF16 = jnp.bfloat16

LANE = 128
EPS = 1e-6
NEG = -1e30

GRID_W = 64
NA_HEADS = 8
NA_HEAD_DIM = 128
NA_MAX_ROWS = 8
NA_COLS = 16
NA_DR = 2 * NA_MAX_ROWS - 1
NA_DC = 2 * NA_COLS - 1
NA_TBL = NA_DR + 1
NA_WIN = NA_MAX_ROWS + 2
NA_UNROLL = 4

RET_HEADS = 8
RET_QK_DIM = 128
RET_V_DIM = 256
RET_BLOCK = 512
ROPE_BASE = 10000.0

XA_HEADS = 4
XA_HEAD_DIM = 256

SLAB_NA_Q, SLAB_NA_K, SLAB_NA_V = 0, 8, 16
SLAB_RQ, SLAB_RK, SLAB_RV, SLAB_RG = 24, 32, 40, 56
SLAB_XQ = 72
SLAB_G_NA, SLAB_G_RET, SLAB_G_MEM = 80, 96, 112


def _params(sem, vmem_mib):
    return pltpu.CompilerParams(dimension_semantics=sem, vmem_limit_bytes=vmem_mib * 2**20)


def _rms(x, g):
    return x * lax.rsqrt(jnp.mean(x * x, axis=-1, keepdims=True) + EPS) * g


def _norm_proj_kernel(x_ref, g_ref, w_ref, o_ref, hn_ref):
    @pl.when(pl.program_id(1) == 0)
    def _():
        hn_ref[...] = _rms(x_ref[...], g_ref[...]).astype(BF16)

    acc = jnp.dot(hn_ref[...], w_ref[...], preferred_element_type=F32)
    for j in range(o_ref.shape[0]):
        o_ref[j] = acc[:, j * LANE:(j + 1) * LANE].astype(BF16)


def _norm_proj(x2d, g, w, tm, tn):
    m, k = x2d.shape
    n = w.shape[1]
    return pl.pallas_call(
        _norm_proj_kernel,
        grid=(m // tm, n // tn),
        in_specs=[
            pl.BlockSpec((tm, k), lambda i, j: (i, 0)),
            pl.BlockSpec((1, k), lambda i, j: (0, 0)),
            pl.BlockSpec((k, tn), lambda i, j: (0, j)),
        ],
        out_specs=pl.BlockSpec((tn // LANE, tm, LANE), lambda i, j: (j, i, 0)),
        out_shape=jax.ShapeDtypeStruct((n // LANE, m, LANE), BF16),
        scratch_shapes=[pltpu.VMEM((tm, k), BF16)],
        compiler_params=_params(("parallel", "arbitrary"), 48),
        name="norm_proj",
    )(x2d, g.reshape(1, k), w)


def _na_table_kernel(rpb_ref, tl_ref, tr_ref):
    h = pl.program_id(0)
    qc = lax.broadcasted_iota(jnp.int32, (GRID_W, LANE), 0)
    lane = lax.broadcasted_iota(jnp.int32, (GRID_W, LANE), 1)
    kc = lane & (GRID_W - 1)
    d = jnp.clip(kc - qc, -(NA_COLS - 1), NA_COLS - 1) + (NA_COLS - 1)
    cs = jnp.clip(qc - NA_COLS // 2, 0, GRID_W - NA_COLS)
    col_ok = (kc >= cs) & (kc < cs + NA_COLS)
    left = lane < GRID_W

    def body(dr, _):
        base = (h * NA_DR + dr) * NA_DC
        val = jnp.zeros((GRID_W, LANE), F32)
        for dd in range(NA_DC):
            val = jnp.where(d == dd, rpb_ref[base + dd], val)
        t = jnp.where(col_ok, val, NEG)
        tl_ref[0, dr] = jnp.where(left, t, 0.0)
        tr_ref[0, dr] = jnp.where(left, 0.0, t)
        return 0

    lax.fori_loop(0, NA_DR, body, 0)
    tl_ref[0, NA_DR] = jnp.where(left, NEG, 0.0)
    tr_ref[0, NA_DR] = jnp.where(left, 0.0, NEG)


def _na_tables(rpb):
    shp = jax.ShapeDtypeStruct((NA_HEADS, NA_TBL, GRID_W, LANE), F32)
    spec = pl.BlockSpec((1, NA_TBL, GRID_W, LANE), lambda h: (h, 0, 0, 0))
    return pl.pallas_call(
        _na_table_kernel,
        grid=(NA_HEADS,),
        in_specs=[pl.BlockSpec(memory_space=pltpu.SMEM)],
        out_specs=[spec, spec],
        out_shape=[shp, shp],
        compiler_params=_params(("arbitrary",), 16),
        name="na_tables",
    )(rpb.reshape(-1))


def _na_kernel(q_ref, k_ref, v_ref, tl_ref, tr_ref, gq_ref, gk_ref, o_ref, kn_ref):
    s_len = q_ref.shape[1]
    rows = s_len // GRID_W
    pair_tok = 2 * GRID_W
    win_tok = NA_WIN * GRID_W
    norm_rows = 512
    gq = gq_ref[...] * (NA_HEAD_DIM ** -0.5)
    gk = gk_ref[...]

    def knorm(c, _):
        sl = pl.ds(pl.multiple_of(c * norm_rows, norm_rows), norm_rows)
        kn_ref[sl, :] = _rms(k_ref[0, sl, :].astype(F32), gk).astype(BF16)
        return 0

    lax.fori_loop(0, s_len // norm_rows, knorm, 0)

    def pair(p, _):
        r0 = 2 * p
        ws = jnp.clip(r0 - NA_MAX_ROWS // 2, 0, rows - NA_WIN)
        qsl = pl.ds(pl.multiple_of(p * pair_tok, pair_tok), pair_tok)
        wsl = pl.ds(pl.multiple_of(ws * GRID_W, pair_tok), win_tok)
        qn = _rms(q_ref[0, qsl, :].astype(F32), gq).astype(BF16)
        s = lax.dot_general(qn, kn_ref[wsl, :], (((1,), (1,)), ((), ())),
                            preferred_element_type=F32)
        bias_rows = []
        for qr in range(2):
            r = r0 + qr
            rs = jnp.clip(r - NA_MAX_ROWS // 2, 0, rows - NA_MAX_ROWS)
            tiles = []
            for t in range(NA_WIN // 2):
                ka = ws + 2 * t
                kb = ka + 1
                ia = jnp.where((ka >= rs) & (ka < rs + NA_MAX_ROWS), ka - r + (NA_MAX_ROWS - 1), NA_DR)
                ib = jnp.where((kb >= rs) & (kb < rs + NA_MAX_ROWS), kb - r + (NA_MAX_ROWS - 1), NA_DR)
                tiles.append(tl_ref[0, ia] + tr_ref[0, ib])
            bias_rows.append(jnp.concatenate(tiles, axis=1))
        s = s + jnp.concatenate(bias_rows, axis=0)
        e = jnp.exp(s - jnp.max(s, axis=-1, keepdims=True))
        l = jnp.sum(e, axis=-1, keepdims=True)
        o = jnp.dot(e.astype(BF16), v_ref[0, wsl, :], preferred_element_type=F32)
        o_ref[qsl, :] = (o / l).astype(BF16)
        return 0

    lax.fori_loop(0, rows // 2, pair, 0, unroll=NA_UNROLL)


def _na_attention(proj, tl, tr, gq, gk, batch, s_len):
    t = batch * s_len
    tbl_spec = pl.BlockSpec((1, NA_TBL, GRID_W, LANE), lambda b, h: (h, 0, 0, 0))
    g_spec = pl.BlockSpec((1, NA_HEAD_DIM), lambda b, h: (0, 0))
    return pl.pallas_call(
        _na_kernel,
        grid=(batch, NA_HEADS),
        in_specs=[
            pl.BlockSpec((1, s_len, LANE), lambda b, h: (SLAB_NA_Q + h, b, 0)),
            pl.BlockSpec((1, s_len, LANE), lambda b, h: (SLAB_NA_K + h, b, 0)),
            pl.BlockSpec((1, s_len, LANE), lambda b, h: (SLAB_NA_V + h, b, 0)),
            tbl_spec, tbl_spec, g_spec, g_spec,
        ],
        out_specs=pl.BlockSpec((s_len, NA_HEAD_DIM), lambda b, h: (b, h)),
        out_shape=jax.ShapeDtypeStruct((t, NA_HEADS * NA_HEAD_DIM), BF16),
        scratch_shapes=[pltpu.VMEM((s_len, NA_HEAD_DIM), BF16)],
        compiler_params=_params(("parallel", "parallel"), 32),
        name="na_attention",
    )(proj, proj, proj, tl, tr, gq.reshape(1, -1), gk.reshape(1, -1))


def _log_sigmoid(x):
    return -(jnp.maximum(-x, 0.0) + jnp.log1p(jnp.exp(-jnp.abs(x))))


def _ret_kernel(q_ref, k_ref, v_ref, g_ref, cos_ref, sin_ref, lf_ref, lb_ref, gn_ref,
                o_ref, qr_ref, kt_ref, d_ref, qdf_ref, qdb_ref, kv_ref, s_ref):
    s_len = q_ref.shape[1]
    c = RET_BLOCK
    nb = s_len // c
    dk = RET_QK_DIM
    half = dk // 2

    lgf = _log_sigmoid(lf_ref[0][:, :1])
    lgb = _log_sigmoid(lb_ref[0][:, :1])
    ic = lax.broadcasted_iota(jnp.int32, (c, 1), 0).astype(F32)
    jr = lax.broadcasted_iota(jnp.int32, (1, c), 1).astype(F32)
    diff = ic - jr
    d_ref[...] = jnp.where(diff >= 0, jnp.exp(lgf * jnp.maximum(diff, 0.0)),
                           jnp.exp(lgb * jnp.maximum(-diff, 0.0)))
    qdf_ref[...] = jnp.broadcast_to(jnp.exp(lgf * (ic + 1.0)), (c, dk))
    qdb_ref[...] = jnp.broadcast_to(jnp.exp(lgb * (c - ic)), (c, dk))
    kdf = jnp.exp(lgf * (c - 1.0 - jr))
    kdb = jnp.exp(lgb * jr)
    cd_f = jnp.exp(lgf * c)
    cd_b = jnp.exp(lgb * c)

    def block_v(sl):
        return jnp.concatenate([v_ref[0, sl, :], v_ref[1, sl, :]], axis=1)

    def prep(n, _):
        sl = pl.ds(pl.multiple_of(n * c, c), c)
        cos = cos_ref[sl, :]
        sin = sin_ref[sl, :]
        q = q_ref[0, sl, :].astype(F32)
        k = k_ref[0, sl, :].astype(F32)
        qr_ref[sl, :] = (q * cos + pltpu.roll(q, half, 1) * sin).astype(BF16)
        kt = ((k * cos + pltpu.roll(k, half, 1) * sin) * (dk ** -0.5)).T
        kt_ref[n] = kt.astype(BF16)
        lhs = jnp.concatenate([(kt * kdf).astype(BF16), (kt * kdb).astype(BF16)], axis=0)
        kv_ref[n] = jnp.dot(lhs, block_v(sl), preferred_element_type=F32)
        return 0

    lax.fori_loop(0, nb, prep, 0, unroll=2)

    def scan_f(n, sf):
        s_ref[n, :dk, :] = sf.astype(BF16)
        return cd_f * sf + kv_ref[n, :dk, :]

    def scan_b(t, sb):
        n = nb - 1 - t
        s_ref[n, dk:, :] = sb.astype(BF16)
        return cd_b * sb + kv_ref[n, dk:, :]

    zero = jnp.zeros((dk, RET_V_DIM), F32)
    lax.fori_loop(0, nb, scan_f, zero)
    lax.fori_loop(0, nb, scan_b, zero)

    gn = gn_ref[0]

    def out(n, _):
        sl = pl.ds(pl.multiple_of(n * c, c), c)
        q = qr_ref[sl, :]
        qf32 = q.astype(F32)
        a = jnp.dot(q, kt_ref[n], preferred_element_type=F32) * d_ref[...]
        lhs = jnp.concatenate([a.astype(BF16), (qf32 * qdf_ref[...]).astype(BF16),
                               (qf32 * qdb_ref[...]).astype(BF16)], axis=1)
        rhs = jnp.concatenate([block_v(sl), s_ref[n]], axis=0)
        o = jnp.dot(lhs, rhs, preferred_element_type=F32)
        mu = jnp.mean(o, axis=-1, keepdims=True)
        oc = o - mu
        y = oc * lax.rsqrt(jnp.mean(oc * oc, axis=-1, keepdims=True) + EPS) * gn
        gate = jnp.concatenate([g_ref[0, sl, :], g_ref[1, sl, :]], axis=1).astype(F32)
        o_ref[sl, :] = (y * gate * jax.nn.sigmoid(gate)).astype(BF16)
        return 0

    lax.fori_loop(0, nb, out, 0, unroll=2)


def _retention(proj, cos2, sin2, lf, lb, gn, batch, s_len):
    t = batch * s_len
    nb = s_len // RET_BLOCK
    dec_spec = pl.BlockSpec((1, 1, LANE), lambda b, h: (h, 0, 0))
    return pl.pallas_call(
        _ret_kernel,
        grid=(batch, RET_HEADS),
        in_specs=[
            pl.BlockSpec((1, s_len, LANE), lambda b, h: (SLAB_RQ + h, b, 0)),
            pl.BlockSpec((1, s_len, LANE), lambda b, h: (SLAB_RK + h, b, 0)),
            pl.BlockSpec((2, s_len, LANE), lambda b, h: (SLAB_RV // 2 + h, b, 0)),
            pl.BlockSpec((2, s_len, LANE), lambda b, h: (SLAB_RG // 2 + h, b, 0)),
            pl.BlockSpec((s_len, RET_QK_DIM), lambda b, h: (0, 0)),
            pl.BlockSpec((s_len, RET_QK_DIM), lambda b, h: (0, 0)),
            dec_spec, dec_spec,
            pl.BlockSpec((1, 1, RET_V_DIM), lambda b, h: (h, 0, 0)),
        ],
        out_specs=pl.BlockSpec((s_len, RET_V_DIM), lambda b, h: (b, h)),
        out_shape=jax.ShapeDtypeStruct((t, RET_HEADS * RET_V_DIM), BF16),
        scratch_shapes=[
            pltpu.VMEM((s_len, RET_QK_DIM), BF16),
            pltpu.VMEM((nb, RET_QK_DIM, RET_BLOCK), BF16),
            pltpu.VMEM((RET_BLOCK, RET_BLOCK), F32),
            pltpu.VMEM((RET_BLOCK, RET_QK_DIM), F32),
            pltpu.VMEM((RET_BLOCK, RET_QK_DIM), F32),
            pltpu.VMEM((nb, 2 * RET_QK_DIM, RET_V_DIM), F32),
            pltpu.VMEM((nb, 2 * RET_QK_DIM, RET_V_DIM), BF16),
        ],
        compiler_params=_params(("parallel", "parallel"), 48),
        name="retention",
    )(proj, proj, proj, proj, cos2, sin2, lf, lb, gn.reshape(RET_HEADS, 1, RET_V_DIM))


def _xa_kernel(q_ref, k_ref, v_ref, gq_ref, gk_ref, o_ref):
    s_len = q_ref.shape[1]
    tq = 512
    gq = gq_ref[...] * (XA_HEAD_DIM ** -0.5)
    k = jnp.concatenate([k_ref[0], k_ref[1]], axis=1).astype(F32)
    kn = _rms(k, gk_ref[...]).astype(BF16)
    v = jnp.concatenate([v_ref[0], v_ref[1]], axis=1)

    def body(i, _):
        sl = pl.ds(pl.multiple_of(i * tq, tq), tq)
        q = jnp.concatenate([q_ref[0, sl, :], q_ref[1, sl, :]], axis=1).astype(F32)
        qn = _rms(q, gq).astype(BF16)
        s = lax.dot_general(qn, kn, (((1,), (1,)), ((), ())), preferred_element_type=F32)
        e = jnp.exp(s - jnp.max(s, axis=-1, keepdims=True))
        l = jnp.sum(e, axis=-1, keepdims=True)
        o = jnp.dot(e.astype(BF16), v, preferred_element_type=F32)
        o_ref[sl, :] = (o / l).astype(BF16)
        return 0

    lax.fori_loop(0, s_len // tq, body, 0)


def _mem_xattn(proj, mkv, gq, gk, batch, s_len, n_mem):
    t = batch * s_len
    g_spec = pl.BlockSpec((1, XA_HEAD_DIM), lambda b, h: (0, 0))
    return pl.pallas_call(
        _xa_kernel,
        grid=(batch, XA_HEADS),
        in_specs=[
            pl.BlockSpec((2, s_len, LANE), lambda b, h: (SLAB_XQ // 2 + h, b, 0)),
            pl.BlockSpec((2, n_mem, LANE), lambda b, h: (h, b, 0)),
            pl.BlockSpec((2, n_mem, LANE), lambda b, h: (XA_HEADS + h, b, 0)),
            g_spec, g_spec,
        ],
        out_specs=pl.BlockSpec((s_len, XA_HEAD_DIM), lambda b, h: (b, h)),
        out_shape=jax.ShapeDtypeStruct((t, XA_HEADS * XA_HEAD_DIM), BF16),
        compiler_params=_params(("parallel", "parallel"), 32),
        name="mem_xattn",
    )(proj, mkv, mkv, gq.reshape(1, -1), gk.reshape(1, -1))


def _merge_kernel(ona_ref, oret_ref, omem_ref, wna_ref, wret_ref, wmem_ref,
                  gna_ref, gret_ref, gmem_ref, o_ref):
    y_na = jnp.dot(ona_ref[...], wna_ref[...], preferred_element_type=F32)
    y_ret = jnp.dot(oret_ref[...], wret_ref[...], preferred_element_type=F32)
    y_mem = jnp.dot(omem_ref[...], wmem_ref[...], preferred_element_type=F32)
    for j in range(gna_ref.shape[0]):
        cs = slice(j * LANE, (j + 1) * LANE)
        o_ref[:, cs] = (jax.nn.sigmoid(gna_ref[j].astype(F32)) * y_na[:, cs]
                        + jax.nn.sigmoid(gret_ref[j].astype(F32)) * y_ret[:, cs]
                        + jax.nn.sigmoid(gmem_ref[j].astype(F32)) * y_mem[:, cs]).astype(BF16)


def _merge(o_na, o_ret, o_mem, w_na, w_ret, w_mem, proj, tm, tn):
    t = o_na.shape[0]
    d = w_na.shape[1]
    ns = tn // LANE

    def lhs_spec(a):
        return pl.BlockSpec((tm, a.shape[1]), lambda i, j: (i, 0))

    def w_spec(w):
        return pl.BlockSpec((w.shape[0], tn), lambda i, j: (0, j))

    def gate_spec(first):
        return pl.BlockSpec((ns, tm, LANE), lambda i, j: (first // ns + j, i, 0))

    return pl.pallas_call(
        _merge_kernel,
        grid=(t // tm, d // tn),
        in_specs=[lhs_spec(o_na), lhs_spec(o_ret), lhs_spec(o_mem),
                  w_spec(w_na), w_spec(w_ret), w_spec(w_mem),
                  gate_spec(SLAB_G_NA), gate_spec(SLAB_G_RET), gate_spec(SLAB_G_MEM)],
        out_specs=pl.BlockSpec((tm, tn), lambda i, j: (i, j)),
        out_shape=jax.ShapeDtypeStruct((t, d), BF16),
        compiler_params=_params(("parallel", "arbitrary"), 48),
        name="merge",
    )(o_na, o_ret, o_mem, w_na, w_ret, w_mem, proj, proj, proj)


def _out_proj_kernel(m_ref, w_ref, x_ref, o_ref):
    o_ref[...] = x_ref[...] + jnp.dot(m_ref[...], w_ref[...], preferred_element_type=F32)


def _out_proj(merged, w, x2d, tm, tn):
    t, k = merged.shape
    d = w.shape[1]
    return pl.pallas_call(
        _out_proj_kernel,
        grid=(t // tm, d // tn),
        in_specs=[pl.BlockSpec((tm, k), lambda i, j: (i, 0)),
                  pl.BlockSpec((k, tn), lambda i, j: (0, j)),
                  pl.BlockSpec((tm, tn), lambda i, j: (i, j))],
        out_specs=pl.BlockSpec((tm, tn), lambda i, j: (i, j)),
        out_shape=jax.ShapeDtypeStruct((t, d), F32),
        compiler_params=_params(("parallel", "arbitrary"), 48),
        name="out_proj",
    )(merged, w, x2d)


def _ffn_kernel(x_ref, g_ref, w1_ref, w2_ref, o_ref, h_ref):
    @pl.when(pl.program_id(1) == 0)
    def _():
        x = x_ref[...]
        h_ref[...] = _rms(x, g_ref[...]).astype(BF16)
        o_ref[...] = x

    a = jnp.maximum(jnp.dot(h_ref[...], w1_ref[...], preferred_element_type=F32), 0.0)
    o_ref[...] += jnp.dot((a * a).astype(BF16), w2_ref[...], preferred_element_type=F32)


def _ffn(x1, g, w1, w2, tm, tf):
    t, d = x1.shape
    dff = w1.shape[1]
    return pl.pallas_call(
        _ffn_kernel,
        grid=(t // tm, dff // tf),
        in_specs=[pl.BlockSpec((tm, d), lambda i, f: (i, 0)),
                  pl.BlockSpec((1, d), lambda i, f: (0, 0)),
                  pl.BlockSpec((d, tf), lambda i, f: (0, f)),
                  pl.BlockSpec((tf, d), lambda i, f: (f, 0))],
        out_specs=pl.BlockSpec((tm, d), lambda i, f: (i, 0)),
        out_shape=jax.ShapeDtypeStruct((t, d), F32),
        scratch_shapes=[pltpu.VMEM((tm, d), BF16)],
        compiler_params=_params(("parallel", "arbitrary"), 48),
        name="ffn",
    )(x1, g.reshape(1, d), w1, w2)


def _rope_tables(s_len):
    half = RET_QK_DIM // 2
    inv = jnp.power(jnp.float32(ROPE_BASE), -jnp.arange(half, dtype=F32) / half)
    ang = jnp.arange(s_len, dtype=F32)[:, None] * inv[None, :]
    cos, sin = jnp.cos(ang), jnp.sin(ang)
    return jnp.concatenate([cos, cos], axis=1), jnp.concatenate([-sin, sin], axis=1)


def kernel(x, mem, norm_mix_g, w_in, na_q_norm_g, na_k_norm_g, na_rpb, ret_decay_logit_fwd, ret_decay_logit_bwd, ret_gn_g, mem_norm_g, w_mem_kv, xa_q_norm_g, xa_k_norm_g, w_br_na, w_br_ret, w_br_mem, w_out, norm_ffn_g, w_ff1, w_ff2):
    batch, s_len, d = x.shape
    n_mem = mem.shape[1]
    t = batch * s_len
    depth = w_in.shape[0]
    tm = min(1024, t)
    cos2, sin2 = _rope_tables(s_len)
    x2d = x.reshape(t, d)
    mem2d = mem.reshape(batch * n_mem, d)

    def lane_bcast(v):
        return jnp.broadcast_to(v.astype(F32)[:, None, None], (v.shape[0], 1, LANE))

    for l in range(depth):
        proj = _norm_proj(x2d, norm_mix_g[l], w_in[l].astype(BF16), tm, 512)
        mkv = _norm_proj(mem2d, mem_norm_g[l], w_mem_kv[l].astype(BF16), batch * n_mem, 1024)

        tl, tr = _na_tables(na_rpb[l])
        o_na = _na_attention(proj, tl, tr, na_q_norm_g[l], na_k_norm_g[l], batch, s_len)
        o_ret = _retention(proj, cos2, sin2, lane_bcast(ret_decay_logit_fwd[l]),
                           lane_bcast(ret_decay_logit_bwd[l]), ret_gn_g[l], batch, s_len)
        o_mem = _mem_xattn(proj, mkv, xa_q_norm_g[l], xa_k_norm_g[l], batch, s_len, n_mem)

        merged = _merge(o_na, o_ret, o_mem, w_br_na[l].astype(BF16), w_br_ret[l].astype(BF16),
                        w_br_mem[l].astype(BF16), proj, min(512, t), 512)
        x1 = _out_proj(merged, w_out[l].astype(BF16), x2d, min(512, t), 1024)
        x2d = _ffn(x1, norm_ffn_g[l], w_ff1[l].astype(BF16), w_ff2[l].astype(BF16), min(512, t), 512)
    return x2d.reshape(batch, s_len, d)
```

```python
import functools

import jax
import jax.numpy as jnp
import numpy as np
from jax import lax
from jax.experimental import pallas as pl
from jax.experimental.pallas import tpu as pltpu

F32 = jnp.float32
BF16 = jnp.bfloat16

LANE = 128
EPS = 1e-6
NEG = -1e30

GRID_W = 64
NA_HEADS = 8
NA_HEAD_DIM = 128
NA_MAX_ROWS = 8
NA_COLS = 16
NA_DR = 2 * NA_MAX_ROWS - 1
NA_DC = 2 * NA_COLS - 1
NA_TBL = 3 * NA_DR + 1
NA_QROWS = 4
NA_WIN = NA_QROWS + NA_MAX_ROWS
NA_UNROLL = 16

RET_HEADS = 8
RET_QK_DIM = 128
RET_V_DIM = 256
RET_BLOCK = 512
ROPE_BASE = 10000.0

XA_HEADS = 4
XA_HEAD_DIM = 256

SLAB_NA_Q, SLAB_NA_K, SLAB_NA_V = 0, 8, 16
SLAB_RQ, SLAB_RK, SLAB_RV, SLAB_RG = 24, 32, 40, 56
SLAB_XQ = 72
SLAB_G_NA, SLAB_G_RET, SLAB_G_MEM = 80, 96, 112


def _params(sem, vmem_mib):
    return pltpu.CompilerParams(dimension_semantics=sem, vmem_limit_bytes=vmem_mib * 2**20)


def _rms(x, g):
    return x * lax.rsqrt(jnp.mean(x * x, axis=-1, keepdims=True) + EPS) * g


def _norm_proj_kernel(x_ref, g_ref, w_ref, o_ref, hn_ref):
    @pl.when(pl.program_id(1) == 0)
    def _():
        hn_ref[...] = _rms(x_ref[...], g_ref[...]).astype(BF16)

    acc = jnp.dot(hn_ref[...], w_ref[...], preferred_element_type=F32)
    for j in range(o_ref.shape[0]):
        o_ref[j] = acc[:, j * LANE:(j + 1) * LANE].astype(BF16)


def _norm_proj(x2d, g, w, tm, tn):
    m, k = x2d.shape
    n = w.shape[1]
    return pl.pallas_call(
        _norm_proj_kernel,
        grid=(m // tm, n // tn),
        in_specs=[
            pl.BlockSpec((tm, k), lambda i, j: (i, 0)),
            pl.BlockSpec((1, k), lambda i, j: (0, 0)),
            pl.BlockSpec((k, tn), lambda i, j: (0, j)),
        ],
        out_specs=pl.BlockSpec((tn // LANE, tm, LANE), lambda i, j: (j, i, 0)),
        out_shape=jax.ShapeDtypeStruct((n // LANE, m, LANE), BF16),
        scratch_shapes=[pltpu.VMEM((tm, k), BF16)],
        compiler_params=_params(("parallel", "arbitrary"), 48),
        name="norm_proj",
    )(x2d, g.reshape(1, k), w)


def _na_table_kernel(rpb_ref, t_ref, base_ref):
    h = pl.program_id(0)
    qc = lax.broadcasted_iota(jnp.int32, (GRID_W, LANE), 0)
    lane = lax.broadcasted_iota(jnp.int32, (GRID_W, LANE), 1)
    kc = lane & (GRID_W - 1)
    d = jnp.clip(kc - qc, -(NA_COLS - 1), NA_COLS - 1) + (NA_COLS - 1)
    cs = jnp.clip(qc - NA_COLS // 2, 0, GRID_W - NA_COLS)
    col_ok = (kc >= cs) & (kc < cs + NA_COLS)
    left = lane < GRID_W
    neg = jnp.full((GRID_W, LANE), NEG, F32)

    def body(dr, _):
        base = (h * NA_DR + dr) * NA_DC
        val = jnp.zeros((GRID_W, LANE), F32)
        for dd in range(NA_DC):
            val = jnp.where(d == dd, rpb_ref[base + dd], val)
        base_ref[dr] = jnp.where(col_ok, val, NEG)
        return 0

    lax.fori_loop(0, NA_DR, body, 0)
    for dr in range(NA_DR):
        second = base_ref[dr + 1] if dr + 1 < NA_DR else neg
        t_ref[0, dr] = jnp.where(left, base_ref[dr], second)
        t_ref[0, NA_DR + dr] = jnp.where(left, base_ref[dr], neg)
        t_ref[0, 2 * NA_DR + dr] = jnp.where(left, neg, base_ref[dr])
    t_ref[0, 3 * NA_DR] = neg


def _na_tables(rpb):
    return pl.pallas_call(
        _na_table_kernel,
        grid=(NA_HEADS,),
        in_specs=[pl.BlockSpec(memory_space=pltpu.SMEM)],
        out_specs=pl.BlockSpec((1, NA_TBL, GRID_W, LANE), lambda h: (h, 0, 0, 0)),
        out_shape=jax.ShapeDtypeStruct((NA_HEADS, NA_TBL, GRID_W, LANE), F32),
        scratch_shapes=[pltpu.VMEM((NA_DR, GRID_W, LANE), F32)],
        compiler_params=_params(("arbitrary",), 16),
        name="na_tables",
    )(rpb.reshape(-1))


def _na_kernel(q_ref, k_ref, v_ref, t_ref, gq_ref, gk_ref, o_ref, kt_ref, va_ref):
    s_len = q_ref.shape[1]
    rows = s_len // GRID_W
    step_tok = NA_QROWS * GRID_W
    win_tok = NA_WIN * GRID_W
    n_tiles = NA_WIN // 2
    gq = gq_ref[...] * (NA_HEAD_DIM ** -0.5)
    gk = gk_ref[...]
    va_ref[:, :NA_HEAD_DIM] = v_ref[0]
    va_ref[:, NA_HEAD_DIM:] = jnp.ones((s_len, NA_HEAD_DIM), BF16)

    def knorm(c, _):
        sl = pl.ds(pl.multiple_of(c * LANE, LANE), LANE)
        kt_ref[c] = _rms(k_ref[0, sl, :].astype(F32), gk).astype(BF16).T
        return 0

    lax.fori_loop(0, s_len // LANE, knorm, 0, unroll=4)

    def step(i, _):
        r0 = NA_QROWS * i
        ws = jnp.clip(r0 - NA_MAX_ROWS // 2, 0, rows - NA_WIN)
        wp = ws // 2
        qsl = pl.ds(pl.multiple_of(i * step_tok, step_tok), step_tok)
        wsl = pl.ds(pl.multiple_of(ws * GRID_W, LANE), win_tok)
        qn = _rms(q_ref[0, qsl, :].astype(F32), gq).astype(BF16)
        kwin = jnp.concatenate([kt_ref[wp + t] for t in range(n_tiles)], axis=1)
        s = jnp.dot(qn, kwin, preferred_element_type=F32)
        bias_rows = []
        for qr in range(NA_QROWS):
            r = r0 + qr
            rs = jnp.clip(r - NA_MAX_ROWS // 2, 0, rows - NA_MAX_ROWS)
            tiles = []
            for t in range(n_tiles):
                ka = ws + 2 * t
                dr = ka - r + (NA_MAX_ROWS - 1)
                va = (ka >= rs) & (ka < rs + NA_MAX_ROWS)
                vb = (ka + 1 >= rs) & (ka + 1 < rs + NA_MAX_ROWS)
                idx = jnp.where(va, jnp.where(vb, dr, NA_DR + dr),
                                jnp.where(vb, 2 * NA_DR + dr + 1, 3 * NA_DR))
                tiles.append(t_ref[0, idx])
            bias_rows.append(jnp.concatenate(tiles, axis=1))
        s = s + jnp.concatenate(bias_rows, axis=0)
        e = jnp.exp(s - jnp.max(s, axis=-1, keepdims=True))
        o = jnp.dot(e.astype(BF16), va_ref[wsl, :], preferred_element_type=F32)
        o_ref[qsl, :] = (o[:, :NA_HEAD_DIM] / o[:, NA_HEAD_DIM:]).astype(BF16)
        return 0

    lax.fori_loop(0, rows // NA_QROWS, step, 0, unroll=NA_UNROLL)


def _na_attention(proj, tbl, gq, gk, batch, s_len):
    t = batch * s_len
    g_spec = pl.BlockSpec((1, NA_HEAD_DIM), lambda b, h: (0, 0))
    return pl.pallas_call(
        _na_kernel,
        grid=(batch, NA_HEADS),
        in_specs=[
            pl.BlockSpec((1, s_len, LANE), lambda b, h: (SLAB_NA_Q + h, b, 0)),
            pl.BlockSpec((1, s_len, LANE), lambda b, h: (SLAB_NA_K + h, b, 0)),
            pl.BlockSpec((1, s_len, LANE), lambda b, h: (SLAB_NA_V + h, b, 0)),
            pl.BlockSpec((1, NA_TBL, GRID_W, LANE), lambda b, h: (h, 0, 0, 0)),
            g_spec, g_spec,
        ],
        out_specs=pl.BlockSpec((s_len, NA_HEAD_DIM), lambda b, h: (b, h)),
        out_shape=jax.ShapeDtypeStruct((t, NA_HEADS * NA_HEAD_DIM), BF16),
        scratch_shapes=[pltpu.VMEM((s_len // LANE, NA_HEAD_DIM, LANE), BF16),
                        pltpu.VMEM((s_len, 2 * NA_HEAD_DIM), BF16)],
        compiler_params=_params(("parallel", "parallel"), 32),
        name="na_attention",
    )(proj, proj, proj, tbl, gq.reshape(1, -1), gk.reshape(1, -1))


def _log_sigmoid(x):
    return -(jnp.maximum(-x, 0.0) + jnp.log1p(jnp.exp(-jnp.abs(x))))


def _ret_kernel(q_ref, k_ref, v_ref, g_ref, cos_ref, sin_ref, lf_ref, lb_ref, gn_ref,
                o_ref, qr_ref, kt_ref, d_ref, qdf_ref, qdb_ref, kv_ref, s_ref):
    s_len = q_ref.shape[1]
    c = RET_BLOCK
    nb = s_len // c
    dk = RET_QK_DIM
    half = dk // 2

    lgf = _log_sigmoid(lf_ref[0][:, :1])
    lgb = _log_sigmoid(lb_ref[0][:, :1])
    ic = lax.broadcasted_iota(jnp.int32, (c, 1), 0).astype(F32)
    jr = lax.broadcasted_iota(jnp.int32, (1, c), 1).astype(F32)
    diff = ic - jr
    d_ref[...] = jnp.where(diff >= 0, jnp.exp(lgf * jnp.maximum(diff, 0.0)),
                           jnp.exp(lgb * jnp.maximum(-diff, 0.0)))
    qdf_ref[...] = jnp.broadcast_to(jnp.exp(lgf * (ic + 1.0)), (c, dk))
    qdb_ref[...] = jnp.broadcast_to(jnp.exp(lgb * (c - ic)), (c, dk))
    kdf = jnp.exp(lgf * (c - 1.0 - jr))
    kdb = jnp.exp(lgb * jr)
    cd_f = jnp.exp(lgf * c)
    cd_b = jnp.exp(lgb * c)

    def block_v(sl):
        return jnp.concatenate([v_ref[0, sl, :], v_ref[1, sl, :]], axis=1)

    def prep(n, _):
        sl = pl.ds(pl.multiple_of(n * c, c), c)
        cos = cos_ref[sl, :]
        sin = sin_ref[sl, :]
        q = q_ref[0, sl, :].astype(F32)
        k = k_ref[0, sl, :].astype(F32)
        qr_ref[sl, :] = (q * cos + pltpu.roll(q, half, 1) * sin).astype(BF16)
        kt = ((k * cos + pltpu.roll(k, half, 1) * sin) * (dk ** -0.5)).T
        kt_ref[n] = kt.astype(BF16)
        lhs = jnp.concatenate([(kt * kdf).astype(BF16), (kt * kdb).astype(BF16)], axis=0)
        kv_ref[n] = jnp.dot(lhs, block_v(sl), preferred_element_type=F32)
        return 0

    lax.fori_loop(0, nb, prep, 0, unroll=2)

    def scan_f(n, sf):
        s_ref[n, :dk, :] = sf.astype(BF16)
        return cd_f * sf + kv_ref[n, :dk, :]

    def scan_b(t, sb):
        n = nb - 1 - t
        s_ref[n, dk:, :] = sb.astype(BF16)
        return cd_b * sb + kv_ref[n, dk:, :]

    zero = jnp.zeros((dk, RET_V_DIM), F32)
    lax.fori_loop(0, nb, scan_f, zero)
    lax.fori_loop(0, nb, scan_b, zero)

    gn = gn_ref[0]

    def out(n, _):
        sl = pl.ds(pl.multiple_of(n * c, c), c)
        q = qr_ref[sl, :]
        qf32 = q.astype(F32)
        a = jnp.dot(q, kt_ref[n], preferred_element_type=F32) * d_ref[...]
        lhs = jnp.concatenate([a.astype(BF16), (qf32 * qdf_ref[...]).astype(BF16),
                               (qf32 * qdb_ref[...]).astype(BF16)], axis=1)
        rhs = jnp.concatenate([block_v(sl), s_ref[n]], axis=0)
        o = jnp.dot(lhs, rhs, preferred_element_type=F32)
        mu = jnp.mean(o, axis=-1, keepdims=True)
        oc = o - mu
        y = oc * lax.rsqrt(jnp.mean(oc * oc, axis=-1, keepdims=True) + EPS) * gn
        gate = jnp.concatenate([g_ref[0, sl, :], g_ref[1, sl, :]], axis=1).astype(F32)
        o_ref[sl, :] = (y * gate * jax.nn.sigmoid(gate)).astype(BF16)
        return 0

    lax.fori_loop(0, nb, out, 0, unroll=2)


def _retention(proj, cos2, sin2, lf, lb, gn, batch, s_len):
    t = batch * s_len
    nb = s_len // RET_BLOCK
    dec_spec = pl.BlockSpec((1, 1, LANE), lambda b, h: (h, 0, 0))
    return pl.pallas_call(
        _ret_kernel,
        grid=(batch, RET_HEADS),
        in_specs=[
            pl.BlockSpec((1, s_len, LANE), lambda b, h: (SLAB_RQ + h, b, 0)),
            pl.BlockSpec((1, s_len, LANE), lambda b, h: (SLAB_RK + h, b, 0)),
            pl.BlockSpec((2, s_len, LANE), lambda b, h: (SLAB_RV // 2 + h, b, 0)),
            pl.BlockSpec((2, s_len, LANE), lambda b, h: (SLAB_RG // 2 + h, b, 0)),
            pl.BlockSpec((s_len, RET_QK_DIM), lambda b, h: (0, 0)),
            pl.BlockSpec((s_len, RET_QK_DIM), lambda b, h: (0, 0)),
            dec_spec, dec_spec,
            pl.BlockSpec((1, 1, RET_V_DIM), lambda b, h: (h, 0, 0)),
        ],
        out_specs=pl.BlockSpec((s_len, RET_V_DIM), lambda b, h: (b, h)),
        out_shape=jax.ShapeDtypeStruct((t, RET_HEADS * RET_V_DIM), BF16),
        scratch_shapes=[
            pltpu.VMEM((s_len, RET_QK_DIM), BF16),
            pltpu.VMEM((nb, RET_QK_DIM, RET_BLOCK), BF16),
            pltpu.VMEM((RET_BLOCK, RET_BLOCK), F32),
            pltpu.VMEM((RET_BLOCK, RET_QK_DIM), F32),
            pltpu.VMEM((RET_BLOCK, RET_QK_DIM), F32),
            pltpu.VMEM((nb, 2 * RET_QK_DIM, RET_V_DIM), F32),
            pltpu.VMEM((nb, 2 * RET_QK_DIM, RET_V_DIM), BF16),
        ],
        compiler_params=_params(("parallel", "parallel"), 48),
        name="retention",
    )(proj, proj, proj, proj, cos2, sin2, lf, lb, gn.reshape(RET_HEADS, 1, RET_V_DIM))


def _xa_kernel(q_ref, k_ref, v_ref, gq_ref, gk_ref, o_ref):
    s_len = q_ref.shape[1]
    tq = 512
    gq = gq_ref[...] * (XA_HEAD_DIM ** -0.5)
    k = jnp.concatenate([k_ref[0], k_ref[1]], axis=1).astype(F32)
    kn = _rms(k, gk_ref[...]).astype(BF16)
    v = jnp.concatenate([v_ref[0], v_ref[1]], axis=1)

    def body(i, _):
        sl = pl.ds(pl.multiple_of(i * tq, tq), tq)
        q = jnp.concatenate([q_ref[0, sl, :], q_ref[1, sl, :]], axis=1).astype(F32)
        qn = _rms(q, gq).astype(BF16)
        s = lax.dot_general(qn, kn, (((1,), (1,)), ((), ())), preferred_element_type=F32)
        e = jnp.exp(s - jnp.max(s, axis=-1, keepdims=True))
        l = jnp.sum(e, axis=-1, keepdims=True)
        o = jnp.dot(e.astype(BF16), v, preferred_element_type=F32)
        o_ref[sl, :] = (o / l).astype(BF16)
        return 0

    lax.fori_loop(0, s_len // tq, body, 0)


def _mem_xattn(proj, mkv, gq, gk, batch, s_len, n_mem):
    t = batch * s_len
    g_spec = pl.BlockSpec((1, XA_HEAD_DIM), lambda b, h: (0, 0))
    return pl.pallas_call(
        _xa_kernel,
        grid=(batch, XA_HEADS),
        in_specs=[
            pl.BlockSpec((2, s_len, LANE), lambda b, h: (SLAB_XQ // 2 + h, b, 0)),
            pl.BlockSpec((2, n_mem, LANE), lambda b, h: (h, b, 0)),
            pl.BlockSpec((2, n_mem, LANE), lambda b, h: (XA_HEADS + h, b, 0)),
            g_spec, g_spec,
        ],
        out_specs=pl.BlockSpec((s_len, XA_HEAD_DIM), lambda b, h: (b, h)),
        out_shape=jax.ShapeDtypeStruct((t, XA_HEADS * XA_HEAD_DIM), BF16),
        compiler_params=_params(("parallel", "parallel"), 32),
        name="mem_xattn",
    )(proj, mkv, mkv, gq.reshape(1, -1), gk.reshape(1, -1))


def _merge_kernel(ona_ref, oret_ref, omem_ref, wna_ref, wret_ref, wmem_ref,
                  gna_ref, gret_ref, gmem_ref, o_ref):
    y_na = jnp.dot(ona_ref[...], wna_ref[...], preferred_element_type=F32)
    y_ret = jnp.dot(oret_ref[...], wret_ref[...], preferred_element_type=F32)
    y_mem = jnp.dot(omem_ref[...], wmem_ref[...], preferred_element_type=F32)
    for j in range(gna_ref.shape[0]):
        cs = slice(j * LANE, (j + 1) * LANE)
        o_ref[:, cs] = (jax.nn.sigmoid(gna_ref[j].astype(F32)) * y_na[:, cs]
                        + jax.nn.sigmoid(gret_ref[j].astype(F32)) * y_ret[:, cs]
                        + jax.nn.sigmoid(gmem_ref[j].astype(F32)) * y_mem[:, cs]).astype(BF16)


def _merge(o_na, o_ret, o_mem, w_na, w_ret, w_mem, proj, tm, tn):
    t = o_na.shape[0]
    d = w_na.shape[1]
    ns = tn // LANE

    def lhs_spec(a):
        return pl.BlockSpec((tm, a.shape[1]), lambda i, j: (i, 0))

    def w_spec(w):
        return pl.BlockSpec((w.shape[0], tn), lambda i, j: (0, j))

    def gate_spec(first):
        return pl.BlockSpec((ns, tm, LANE), lambda i, j: (first // ns + j, i, 0))

    return pl.pallas_call(
        _merge_kernel,
        grid=(t // tm, d // tn),
        in_specs=[lhs_spec(o_na), lhs_spec(o_ret), lhs_spec(o_mem),
                  w_spec(w_na), w_spec(w_ret), w_spec(w_mem),
                  gate_spec(SLAB_G_NA), gate_spec(SLAB_G_RET), gate_spec(SLAB_G_MEM)],
        out_specs=pl.BlockSpec((tm, tn), lambda i, j: (i, j)),
        out_shape=jax.ShapeDtypeStruct((t, d), BF16),
        compiler_params=_params(("parallel", "arbitrary"), 48),
        name="merge",
    )(o_na, o_ret, o_mem, w_na, w_ret, w_mem, proj, proj, proj)


def _out_proj_kernel(m_ref, w_ref, x_ref, o_ref):
    o_ref[...] = x_ref[...] + jnp.dot(m_ref[...], w_ref[...], preferred_element_type=F32)


def _out_proj(merged, w, x2d, tm, tn):
    t, k = merged.shape
    d = w.shape[1]
    return pl.pallas_call(
        _out_proj_kernel,
        grid=(t // tm, d // tn),
        in_specs=[pl.BlockSpec((tm, k), lambda i, j: (i, 0)),
                  pl.BlockSpec((k, tn), lambda i, j: (0, j)),
                  pl.BlockSpec((tm, tn), lambda i, j: (i, j))],
        out_specs=pl.BlockSpec((tm, tn), lambda i, j: (i, j)),
        out_shape=jax.ShapeDtypeStruct((t, d), F32),
        compiler_params=_params(("parallel", "arbitrary"), 48),
        name="out_proj",
    )(merged, w, x2d)


def _ffn_kernel(x_ref, g_ref, w1_ref, w2_ref, o_ref, h_ref):
    @pl.when(pl.program_id(1) == 0)
    def _():
        x = x_ref[...]
        h_ref[...] = _rms(x, g_ref[...]).astype(BF16)
        o_ref[...] = x

    a = jnp.maximum(jnp.dot(h_ref[...], w1_ref[...], preferred_element_type=F32), 0.0)
    o_ref[...] += jnp.dot((a * a).astype(BF16), w2_ref[...], preferred_element_type=F32)


def _ffn(x1, g, w1, w2, tm, tf):
    t, d = x1.shape
    dff = w1.shape[1]
    return pl.pallas_call(
        _ffn_kernel,
        grid=(t // tm, dff // tf),
        in_specs=[pl.BlockSpec((tm, d), lambda i, f: (i, 0)),
                  pl.BlockSpec((1, d), lambda i, f: (0, 0)),
                  pl.BlockSpec((d, tf), lambda i, f: (0, f)),
                  pl.BlockSpec((tf, d), lambda i, f: (f, 0))],
        out_specs=pl.BlockSpec((tm, d), lambda i, f: (i, 0)),
        out_shape=jax.ShapeDtypeStruct((t, d), F32),
        scratch_shapes=[pltpu.VMEM((tm, d), BF16)],
        compiler_params=_params(("parallel", "arbitrary"), 48),
        name="ffn",
    )(x1, g.reshape(1, d), w1, w2)


def _rope_tables(s_len):
    half = RET_QK_DIM // 2
    inv = jnp.power(jnp.float32(ROPE_BASE), -jnp.arange(half, dtype=F32) / half)
    ang = jnp.arange(s_len, dtype=F32)[:, None] * inv[None, :]
    cos, sin = jnp.cos(ang), jnp.sin(ang)
    return jnp.concatenate([cos, cos], axis=1), jnp.concatenate([-sin, sin], axis=1)


def kernel(x, mem, norm_mix_g, w_in, na_q_norm_g, na_k_norm_g, na_rpb, ret_decay_logit_fwd, ret_decay_logit_bwd, ret_gn_g, mem_norm_g, w_mem_kv, xa_q_norm_g, xa_k_norm_g, w_br_na, w_br_ret, w_br_mem, w_out, norm_ffn_g, w_ff1, w_ff2):
    batch, s_len, d = x.shape
    n_mem = mem.shape[1]
    t = batch * s_len
    depth = w_in.shape[0]
    tm = min(1024, t)
    cos2, sin2 = _rope_tables(s_len)
    x2d = x.reshape(t, d)
    mem2d = mem.reshape(batch * n_mem, d)

    def lane_bcast(v):
        return jnp.broadcast_to(v.astype(F32)[:, None, None], (v.shape[0], 1, LANE))

    for l in range(depth):
        proj = _norm_proj(x2d, norm_mix_g[l], w_in[l].astype(BF16), tm, 512)
        mkv = _norm_proj(mem2d, mem_norm_g[l], w_mem_kv[l].astype(BF16), batch * n_mem, 1024)

        o_na = _na_attention(proj, _na_tables(na_rpb[l]), na_q_norm_g[l], na_k_norm_g[l], batch, s_len)
        o_ret = _retention(proj, cos2, sin2, lane_bcast(ret_decay_logit_fwd[l]),
                           lane_bcast(ret_decay_logit_bwd[l]), ret_gn_g[l], batch, s_len)
        o_mem = _mem_xattn(proj, mkv, xa_q_norm_g[l], xa_k_norm_g[l], batch, s_len, n_mem)

        merged = _merge(o_na, o_ret, o_mem, w_br_na[l].astype(BF16), w_br_ret[l].astype(BF16),
                        w_br_mem[l].astype(BF16), proj, min(512, t), 512)
        x1 = _out_proj(merged, w_out[l].astype(BF16), x2d, min(512, t), 1024)
        x2d = _ffn(x1, norm_ffn_g[l], w_ff1[l].astype(BF16), w_ff2[l].astype(BF16), min(512, t), 512)
    return x2d.reshape(batch, s_len, d)
```

```python
import functools

import jax
import jax.numpy as jnp
import numpy as np
from jax import lax
from jax.experimental import pallas as pl
from jax.experimental.pallas import tpu as pltpu

F32 = jnp.float32
BF16 = jnp.bfloat16

LANE = 128
EPS = 1e-6
NEG = -1e30

GRID_W = 64
NA_HEADS = 8
NA_HEAD_DIM = 128
NA_MAX_ROWS = 8
NA_COLS = 16
NA_DR = 2 * NA_MAX_ROWS - 1
NA_DC = 2 * NA_COLS - 1
NA_TBL = 3 * NA_DR + 1
NA_QROWS = 4
NA_WIN = NA_QROWS + NA_MAX_ROWS
NA_UNROLL = 16

RET_HEADS = 8
RET_QK_DIM = 128
RET_V_DIM = 256
RET_BLOCK = 512
ROPE_BASE = 10000.0

XA_HEADS = 4
XA_HEAD_DIM = 256

SLAB_NA_Q, SLAB_NA_K, SLAB_NA_V = 0, 8, 16
SLAB_RQ, SLAB_RK, SLAB_RV, SLAB_RG = 24, 32, 40, 56
SLAB_XQ = 72
SLAB_G_NA, SLAB_G_RET, SLAB_G_MEM = 80, 96, 112


def _params(sem, vmem_mib):
    return pltpu.CompilerParams(dimension_semantics=sem, vmem_limit_bytes=vmem_mib * 2**20)


def _rms(x, g):
    return x * lax.rsqrt(jnp.mean(x * x, axis=-1, keepdims=True) + EPS) * g


def _norm_proj_kernel(n_side, x_ref, g_ref, w_ref, *refs):
    side_in, o_ref = refs[:n_side], refs[n_side]
    side_out, hn_ref = refs[n_side + 1:2 * n_side + 1], refs[2 * n_side + 1]

    @pl.when(pl.program_id(1) == 0)
    def _():
        hn_ref[...] = _rms(x_ref[...], g_ref[...]).astype(BF16)

    acc = jnp.dot(hn_ref[...], w_ref[...].astype(BF16), preferred_element_type=F32)
    for j in range(o_ref.shape[0]):
        o_ref[j] = acc[:, j * LANE:(j + 1) * LANE].astype(BF16)
    for src, dst in zip(side_in, side_out):
        dst[...] = src[...].astype(BF16)


def _norm_proj(x2d, g, w, tm, tn, side=()):
    m, k = x2d.shape
    n = w.shape[1]
    side_specs = [pl.BlockSpec(blk, imap) for _, blk, imap in side]
    out = pl.pallas_call(
        functools.partial(_norm_proj_kernel, len(side)),
        grid=(m // tm, n // tn),
        in_specs=[
            pl.BlockSpec((tm, k), lambda i, j: (i, 0)),
            pl.BlockSpec((1, k), lambda i, j: (0, 0)),
            pl.BlockSpec((k, tn), lambda i, j: (0, j)),
        ] + side_specs,
        out_specs=[pl.BlockSpec((tn // LANE, tm, LANE), lambda i, j: (j, i, 0))] + side_specs,
        out_shape=[jax.ShapeDtypeStruct((n // LANE, m, LANE), BF16)]
        + [jax.ShapeDtypeStruct(a.shape, BF16) for a, _, _ in side],
        scratch_shapes=[pltpu.VMEM((tm, k), BF16)],
        compiler_params=_params(("parallel", "arbitrary"), 48),
        name="norm_proj",
    )(x2d, g.reshape(1, k), w, *[a for a, _, _ in side])
    return out[0], out[1:]


def _na_table_kernel(rpb_ref, t_ref, base_ref):
    h = pl.program_id(0)
    qc = lax.broadcasted_iota(jnp.int32, (GRID_W, LANE), 0)
    lane = lax.broadcasted_iota(jnp.int32, (GRID_W, LANE), 1)
    kc = lane & (GRID_W - 1)
    d = jnp.clip(kc - qc, -(NA_COLS - 1), NA_COLS - 1) + (NA_COLS - 1)
    cs = jnp.clip(qc - NA_COLS // 2, 0, GRID_W - NA_COLS)
    col_ok = (kc >= cs) & (kc < cs + NA_COLS)
    left = lane < GRID_W
    neg = jnp.full((GRID_W, LANE), NEG, F32)

    def body(dr, _):
        base = (h * NA_DR + dr) * NA_DC
        val = jnp.zeros((GRID_W, LANE), F32)
        for dd in range(NA_DC):
            val = jnp.where(d == dd, rpb_ref[base + dd], val)
        base_ref[dr] = jnp.where(col_ok, val, NEG)
        return 0

    lax.fori_loop(0, NA_DR, body, 0)
    for dr in range(NA_DR):
        second = base_ref[dr + 1] if dr + 1 < NA_DR else neg
        t_ref[0, dr] = jnp.where(left, base_ref[dr], second)
        t_ref[0, NA_DR + dr] = jnp.where(left, base_ref[dr], neg)
        t_ref[0, 2 * NA_DR + dr] = jnp.where(left, neg, base_ref[dr])
    t_ref[0, 3 * NA_DR] = neg


def _na_tables(rpb):
    return pl.pallas_call(
        _na_table_kernel,
        grid=(NA_HEADS,),
        in_specs=[pl.BlockSpec(memory_space=pltpu.SMEM)],
        out_specs=pl.BlockSpec((1, NA_TBL, GRID_W, LANE), lambda h: (h, 0, 0, 0)),
        out_shape=jax.ShapeDtypeStruct((NA_HEADS, NA_TBL, GRID_W, LANE), F32),
        scratch_shapes=[pltpu.VMEM((NA_DR, GRID_W, LANE), F32)],
        compiler_params=_params(("arbitrary",), 16),
        name="na_tables",
    )(rpb.reshape(-1))


def _na_kernel(q_ref, k_ref, v_ref, t_ref, gq_ref, gk_ref, o_ref, kt_ref, va_ref):
    s_len = q_ref.shape[1]
    rows = s_len // GRID_W
    step_tok = NA_QROWS * GRID_W
    win_tok = NA_WIN * GRID_W
    n_tiles = NA_WIN // 2
    gq = gq_ref[...] * (NA_HEAD_DIM ** -0.5)
    gk = gk_ref[...]
    va_ref[:, :NA_HEAD_DIM] = v_ref[0]
    va_ref[:, NA_HEAD_DIM:] = jnp.ones((s_len, NA_HEAD_DIM), BF16)

    def knorm(c, _):
        sl = pl.ds(pl.multiple_of(c * LANE, LANE), LANE)
        kt_ref[c] = _rms(k_ref[0, sl, :].astype(F32), gk).astype(BF16).T
        return 0

    lax.fori_loop(0, s_len // LANE, knorm, 0, unroll=4)

    def step(i, _):
        r0 = NA_QROWS * i
        ws = jnp.clip(r0 - NA_MAX_ROWS // 2, 0, rows - NA_WIN)
        wp = ws // 2
        qsl = pl.ds(pl.multiple_of(i * step_tok, step_tok), step_tok)
        wsl = pl.ds(pl.multiple_of(ws * GRID_W, LANE), win_tok)
        qn = _rms(q_ref[0, qsl, :].astype(F32), gq).astype(BF16)
        kwin = jnp.concatenate([kt_ref[wp + t] for t in range(n_tiles)], axis=1)
        s = jnp.dot(qn, kwin, preferred_element_type=F32)
        bias_rows = []
        for qr in range(NA_QROWS):
            r = r0 + qr
            rs = jnp.clip(r - NA_MAX_ROWS // 2, 0, rows - NA_MAX_ROWS)
            tiles = []
            for t in range(n_tiles):
                ka = ws + 2 * t
                dr = ka - r + (NA_MAX_ROWS - 1)
                va = (ka >= rs) & (ka < rs + NA_MAX_ROWS)
                vb = (ka + 1 >= rs) & (ka + 1 < rs + NA_MAX_ROWS)
                idx = jnp.where(va, jnp.where(vb, dr, NA_DR + dr),
                                jnp.where(vb, 2 * NA_DR + dr + 1, 3 * NA_DR))
                tiles.append(t_ref[0, idx])
            bias_rows.append(jnp.concatenate(tiles, axis=1))
        s = s + jnp.concatenate(bias_rows, axis=0)
        e = jnp.exp(s - jnp.max(s, axis=-1, keepdims=True))
        o = jnp.dot(e.astype(BF16), va_ref[wsl, :], preferred_element_type=F32)
        o_ref[qsl, :] = (o[:, :NA_HEAD_DIM] / o[:, NA_HEAD_DIM:]).astype(BF16)
        return 0

    lax.fori_loop(0, rows // NA_QROWS, step, 0, unroll=NA_UNROLL)


def _na_attention(proj, tbl, gq, gk, batch, s_len):
    t = batch * s_len
    g_spec = pl.BlockSpec((1, NA_HEAD_DIM), lambda b, h: (0, 0))
    return pl.pallas_call(
        _na_kernel,
        grid=(batch, NA_HEADS),
        in_specs=[
            pl.BlockSpec((1, s_len, LANE), lambda b, h: (SLAB_NA_Q + h, b, 0)),
            pl.BlockSpec((1, s_len, LANE), lambda b, h: (SLAB_NA_K + h, b, 0)),
            pl.BlockSpec((1, s_len, LANE), lambda b, h: (SLAB_NA_V + h, b, 0)),
            pl.BlockSpec((1, NA_TBL, GRID_W, LANE), lambda b, h: (h, 0, 0, 0)),
            g_spec, g_spec,
        ],
        out_specs=pl.BlockSpec((s_len, NA_HEAD_DIM), lambda b, h: (b, h)),
        out_shape=jax.ShapeDtypeStruct((t, NA_HEADS * NA_HEAD_DIM), BF16),
        scratch_shapes=[pltpu.VMEM((s_len // LANE, NA_HEAD_DIM, LANE), BF16),
                        pltpu.VMEM((s_len, 2 * NA_HEAD_DIM), BF16)],
        compiler_params=_params(("parallel", "parallel"), 32),
        name="na_attention",
    )(proj, proj, proj, tbl, gq.reshape(1, -1), gk.reshape(1, -1))


def _log_sigmoid(x):
    return -(jnp.maximum(-x, 0.0) + jnp.log1p(jnp.exp(-jnp.abs(x))))


def _ret_kernel(q_ref, k_ref, v_ref, g_ref, cos_ref, sin_ref, lf_ref, lb_ref, gn_ref,
                o_ref, qr_ref, kt_ref, d_ref, qdf_ref, qdb_ref, kv_ref, s_ref):
    s_len = q_ref.shape[1]
    c = RET_BLOCK
    nb = s_len // c
    dk = RET_QK_DIM
    half = dk // 2

    lgf = _log_sigmoid(lf_ref[0][:, :1])
    lgb = _log_sigmoid(lb_ref[0][:, :1])
    ic = lax.broadcasted_iota(jnp.int32, (c, 1), 0).astype(F32)
    jr = lax.broadcasted_iota(jnp.int32, (1, c), 1).astype(F32)
    diff = ic - jr
    d_ref[...] = jnp.where(diff >= 0, jnp.exp(lgf * jnp.maximum(diff, 0.0)),
                           jnp.exp(lgb * jnp.maximum(-diff, 0.0)))
    qdf_ref[...] = jnp.broadcast_to(jnp.exp(lgf * (ic + 1.0)), (c, dk))
    qdb_ref[...] = jnp.broadcast_to(jnp.exp(lgb * (c - ic)), (c, dk))
    kdf = jnp.exp(lgf * (c - 1.0 - jr))
    kdb = jnp.exp(lgb * jr)
    cd_f = jnp.exp(lgf * c)
    cd_b = jnp.exp(lgb * c)

    def block_v(sl):
        return jnp.concatenate([v_ref[0, sl, :], v_ref[1, sl, :]], axis=1)

    def prep(n, _):
        sl = pl.ds(pl.multiple_of(n * c, c), c)
        cos = cos_ref[sl, :]
        sin = sin_ref[sl, :]
        q = q_ref[0, sl, :].astype(F32)
        k = k_ref[0, sl, :].astype(F32)
        qr_ref[sl, :] = (q * cos + pltpu.roll(q, half, 1) * sin).astype(BF16)
        kt = ((k * cos + pltpu.roll(k, half, 1) * sin) * (dk ** -0.5)).T
        kt_ref[n] = kt.astype(BF16)
        lhs = jnp.concatenate([(kt * kdf).astype(BF16), (kt * kdb).astype(BF16)], axis=0)
        kv_ref[n] = jnp.dot(lhs, block_v(sl), preferred_element_type=F32)
        return 0

    lax.fori_loop(0, nb, prep, 0, unroll=2)

    def scan_f(n, sf):
        s_ref[n, :dk, :] = sf.astype(BF16)
        return cd_f * sf + kv_ref[n, :dk, :]

    def scan_b(t, sb):
        n = nb - 1 - t
        s_ref[n, dk:, :] = sb.astype(BF16)
        return cd_b * sb + kv_ref[n, dk:, :]

    zero = jnp.zeros((dk, RET_V_DIM), F32)
    lax.fori_loop(0, nb, scan_f, zero)
    lax.fori_loop(0, nb, scan_b, zero)

    gn = gn_ref[0]

    def out(n, _):
        sl = pl.ds(pl.multiple_of(n * c, c), c)
        q = qr_ref[sl, :]
        qf32 = q.astype(F32)
        a = jnp.dot(q, kt_ref[n], preferred_element_type=F32) * d_ref[...]
        lhs = jnp.concatenate([a.astype(BF16), (qf32 * qdf_ref[...]).astype(BF16),
                               (qf32 * qdb_ref[...]).astype(BF16)], axis=1)
        rhs = jnp.concatenate([block_v(sl), s_ref[n]], axis=0)
        o = jnp.dot(lhs, rhs, preferred_element_type=F32)
        mu = jnp.mean(o, axis=-1, keepdims=True)
        oc = o - mu
        y = oc * lax.rsqrt(jnp.mean(oc * oc, axis=-1, keepdims=True) + EPS) * gn
        gate = jnp.concatenate([g_ref[0, sl, :], g_ref[1, sl, :]], axis=1).astype(F32)
        o_ref[sl, :] = (y * gate * jax.nn.sigmoid(gate)).astype(BF16)
        return 0

    lax.fori_loop(0, nb, out, 0, unroll=2)


def _retention(proj, cos2, sin2, lf, lb, gn, batch, s_len):
    t = batch * s_len
    nb = s_len // RET_BLOCK
    dec_spec = pl.BlockSpec((1, 1, LANE), lambda b, h: (h, 0, 0))
    return pl.pallas_call(
        _ret_kernel,
        grid=(batch, RET_HEADS),
        in_specs=[
            pl.BlockSpec((1, s_len, LANE), lambda b, h: (SLAB_RQ + h, b, 0)),
            pl.BlockSpec((1, s_len, LANE), lambda b, h: (SLAB_RK + h, b, 0)),
            pl.BlockSpec((2, s_len, LANE), lambda b, h: (SLAB_RV // 2 + h, b, 0)),
            pl.BlockSpec((2, s_len, LANE), lambda b, h: (SLAB_RG // 2 + h, b, 0)),
            pl.BlockSpec((s_len, RET_QK_DIM), lambda b, h: (0, 0)),
            pl.BlockSpec((s_len, RET_QK_DIM), lambda b, h: (0, 0)),
            dec_spec, dec_spec,
            pl.BlockSpec((1, 1, RET_V_DIM), lambda b, h: (h, 0, 0)),
        ],
        out_specs=pl.BlockSpec((s_len, RET_V_DIM), lambda b, h: (b, h)),
        out_shape=jax.ShapeDtypeStruct((t, RET_HEADS * RET_V_DIM), BF16),
        scratch_shapes=[
            pltpu.VMEM((s_len, RET_QK_DIM), BF16),
            pltpu.VMEM((nb, RET_QK_DIM, RET_BLOCK), BF16),
            pltpu.VMEM((RET_BLOCK, RET_BLOCK), F32),
            pltpu.VMEM((RET_BLOCK, RET_QK_DIM), F32),
            pltpu.VMEM((RET_BLOCK, RET_QK_DIM), F32),
            pltpu.VMEM((nb, 2 * RET_QK_DIM, RET_V_DIM), F32),
            pltpu.VMEM((nb, 2 * RET_QK_DIM, RET_V_DIM), BF16),
        ],
        compiler_params=_params(("parallel", "parallel"), 48),
        name="retention",
    )(proj, proj, proj, proj, cos2, sin2, lf, lb, gn.reshape(RET_HEADS, 1, RET_V_DIM))


def _xa_kernel(q_ref, k_ref, v_ref, gq_ref, gk_ref, o_ref):
    s_len = q_ref.shape[1]
    tq = 512
    gq = gq_ref[...] * (XA_HEAD_DIM ** -0.5)
    k = jnp.concatenate([k_ref[0], k_ref[1]], axis=1).astype(F32)
    kn = _rms(k, gk_ref[...]).astype(BF16)
    v = jnp.concatenate([v_ref[0], v_ref[1]], axis=1)

    def body(i, _):
        sl = pl.ds(pl.multiple_of(i * tq, tq), tq)
        q = jnp.concatenate([q_ref[0, sl, :], q_ref[1, sl, :]], axis=1).astype(F32)
        qn = _rms(q, gq).astype(BF16)
        s = lax.dot_general(qn, kn, (((1,), (1,)), ((), ())), preferred_element_type=F32)
        e = jnp.exp(s - jnp.max(s, axis=-1, keepdims=True))
        l = jnp.sum(e, axis=-1, keepdims=True)
        o = jnp.dot(e.astype(BF16), v, preferred_element_type=F32)
        o_ref[sl, :] = (o / l).astype(BF16)
        return 0

    lax.fori_loop(0, s_len // tq, body, 0)


def _mem_xattn(proj, mkv, gq, gk, batch, s_len, n_mem):
    t = batch * s_len
    g_spec = pl.BlockSpec((1, XA_HEAD_DIM), lambda b, h: (0, 0))
    return pl.pallas_call(
        _xa_kernel,
        grid=(batch, XA_HEADS),
        in_specs=[
            pl.BlockSpec((2, s_len, LANE), lambda b, h: (SLAB_XQ // 2 + h, b, 0)),
            pl.BlockSpec((2, n_mem, LANE), lambda b, h: (h, b, 0)),
            pl.BlockSpec((2, n_mem, LANE), lambda b, h: (XA_HEADS + h, b, 0)),
            g_spec, g_spec,
        ],
        out_specs=pl.BlockSpec((s_len, XA_HEAD_DIM), lambda b, h: (b, h)),
        out_shape=jax.ShapeDtypeStruct((t, XA_HEADS * XA_HEAD_DIM), BF16),
        compiler_params=_params(("parallel", "parallel"), 32),
        name="mem_xattn",
    )(proj, mkv, mkv, gq.reshape(1, -1), gk.reshape(1, -1))


def _merge_kernel(ona_ref, oret_ref, omem_ref, wna_ref, wret_ref, wmem_ref,
                  gna_ref, gret_ref, gmem_ref, o_ref):
    y_na = jnp.dot(ona_ref[...], wna_ref[...].astype(BF16), preferred_element_type=F32)
    y_ret = jnp.dot(oret_ref[...], wret_ref[...].astype(BF16), preferred_element_type=F32)
    y_mem = jnp.dot(omem_ref[...], wmem_ref[...].astype(BF16), preferred_element_type=F32)
    for j in range(gna_ref.shape[0]):
        cs = slice(j * LANE, (j + 1) * LANE)
        o_ref[:, cs] = (jax.nn.sigmoid(gna_ref[j].astype(F32)) * y_na[:, cs]
                        + jax.nn.sigmoid(gret_ref[j].astype(F32)) * y_ret[:, cs]
                        + jax.nn.sigmoid(gmem_ref[j].astype(F32)) * y_mem[:, cs]).astype(BF16)


def _merge(o_na, o_ret, o_mem, w_na, w_ret, w_mem, proj, tm, tn):
    t = o_na.shape[0]
    d = w_na.shape[1]
    ns = tn // LANE

    def lhs_spec(a):
        return pl.BlockSpec((tm, a.shape[1]), lambda i, j: (i, 0))

    def w_spec(w):
        return pl.BlockSpec((w.shape[0], tn), lambda i, j: (0, j))

    def gate_spec(first):
        return pl.BlockSpec((ns, tm, LANE), lambda i, j: (first // ns + j, i, 0))

    return pl.pallas_call(
        _merge_kernel,
        grid=(t // tm, d // tn),
        in_specs=[lhs_spec(o_na), lhs_spec(o_ret), lhs_spec(o_mem),
                  w_spec(w_na), w_spec(w_ret), w_spec(w_mem),
                  gate_spec(SLAB_G_NA), gate_spec(SLAB_G_RET), gate_spec(SLAB_G_MEM)],
        out_specs=pl.BlockSpec((tm, tn), lambda i, j: (i, j)),
        out_shape=jax.ShapeDtypeStruct((t, d), BF16),
        compiler_params=_params(("parallel", "arbitrary"), 48),
        name="merge",
    )(o_na, o_ret, o_mem, w_na, w_ret, w_mem, proj, proj, proj)


def _out_proj_kernel(m_ref, w_ref, x_ref, o_ref):
    o_ref[...] = x_ref[...] + jnp.dot(m_ref[...], w_ref[...].astype(BF16), preferred_element_type=F32)


def _out_proj(merged, w, x2d, tm, tn):
    t, k = merged.shape
    d = w.shape[1]
    return pl.pallas_call(
        _out_proj_kernel,
        grid=(t // tm, d // tn),
        in_specs=[pl.BlockSpec((tm, k), lambda i, j: (i, 0)),
                  pl.BlockSpec((k, tn), lambda i, j: (0, j)),
                  pl.BlockSpec((tm, tn), lambda i, j: (i, j))],
        out_specs=pl.BlockSpec((tm, tn), lambda i, j: (i, j)),
        out_shape=jax.ShapeDtypeStruct((t, d), F32),
        compiler_params=_params(("parallel", "arbitrary"), 48),
        name="out_proj",
    )(merged, w, x2d)


def _ffn_kernel(x_ref, g_ref, w1_ref, w2_ref, o_ref, h_ref):
    @pl.when(pl.program_id(1) == 0)
    def _():
        x = x_ref[...]
        h_ref[...] = _rms(x, g_ref[...]).astype(BF16)
        o_ref[...] = x

    a = jnp.maximum(jnp.dot(h_ref[...], w1_ref[...], preferred_element_type=F32), 0.0)
    o_ref[...] += jnp.dot((a * a).astype(BF16), w2_ref[...], preferred_element_type=F32)


def _ffn(x1, g, w1, w2, tm, tf):
    t, d = x1.shape
    dff = w1.shape[1]
    return pl.pallas_call(
        _ffn_kernel,
        grid=(t // tm, dff // tf),
        in_specs=[pl.BlockSpec((tm, d), lambda i, f: (i, 0)),
                  pl.BlockSpec((1, d), lambda i, f: (0, 0)),
                  pl.BlockSpec((d, tf), lambda i, f: (0, f)),
                  pl.BlockSpec((tf, d), lambda i, f: (f, 0))],
        out_specs=pl.BlockSpec((tm, d), lambda i, f: (i, 0)),
        out_shape=jax.ShapeDtypeStruct((t, d), F32),
        scratch_shapes=[pltpu.VMEM((tm, d), BF16)],
        compiler_params=_params(("parallel", "arbitrary"), 48),
        name="ffn",
    )(x1, g.reshape(1, d), w1, w2)


def _rope_tables(s_len):
    half = RET_QK_DIM // 2
    inv = jnp.power(jnp.float32(ROPE_BASE), -jnp.arange(half, dtype=F32) / half)
    ang = jnp.arange(s_len, dtype=F32)[:, None] * inv[None, :]
    cos, sin = jnp.cos(ang), jnp.sin(ang)
    return jnp.concatenate([cos, cos], axis=1), jnp.concatenate([-sin, sin], axis=1)


def kernel(x, mem, norm_mix_g, w_in, na_q_norm_g, na_k_norm_g, na_rpb, ret_decay_logit_fwd, ret_decay_logit_bwd, ret_gn_g, mem_norm_g, w_mem_kv, xa_q_norm_g, xa_k_norm_g, w_br_na, w_br_ret, w_br_mem, w_out, norm_ffn_g, w_ff1, w_ff2):
    batch, s_len, d = x.shape
    n_mem = mem.shape[1]
    t = batch * s_len
    depth = w_in.shape[0]
    tm = min(1024, t)
    cos2, sin2 = _rope_tables(s_len)
    x2d = x.reshape(t, d)
    mem2d = mem.reshape(batch * n_mem, d)

    def lane_bcast(v):
        return jnp.broadcast_to(v.astype(F32)[:, None, None], (v.shape[0], 1, LANE))

    tn_in = 512
    gm, gn = t // tm, w_in.shape[2] // tn_in

    def side_job(w):
        r, c = w.shape
        if r % (gm * gn * 16) == 0:
            return w, (r // (gm * gn), c), lambda i, j: (i * gn + j, 0)
        assert r % (gn * 16) == 0 and c % (gm * LANE) == 0, (w.shape, gm, gn)
        return w, (r // gn, c // gm), lambda i, j: (j, i)

    for l in range(depth):
        proj, (w_ff1_bf, w_ff2_bf) = _norm_proj(x2d, norm_mix_g[l], w_in[l], tm, tn_in,
                                                side=(side_job(w_ff1[l]), side_job(w_ff2[l])))
        mkv, _ = _norm_proj(mem2d, mem_norm_g[l], w_mem_kv[l], batch * n_mem, 1024)

        o_na = _na_attention(proj, _na_tables(na_rpb[l]), na_q_norm_g[l], na_k_norm_g[l], batch, s_len)
        o_ret = _retention(proj, cos2, sin2, lane_bcast(ret_decay_logit_fwd[l]),
                           lane_bcast(ret_decay_logit_bwd[l]), ret_gn_g[l], batch, s_len)
        o_mem = _mem_xattn(proj, mkv, xa_q_norm_g[l], xa_k_norm_g[l], batch, s_len, n_mem)

        merged = _merge(o_na, o_ret, o_mem, w_br_na[l], w_br_ret[l], w_br_mem[l], proj, tm, 256)
        x1 = _out_proj(merged, w_out[l], x2d, tm, 512)
        x2d = _ffn(x1, norm_ffn_g[l], w_ff1_bf, w_ff2_bf, min(512, t), 512)
    return x2d.reshape(batch, s_len, d)
```

```python
import functools

import jax
import jax.numpy as jnp
import numpy as np
from jax import lax
from jax.experimental import pallas as pl
from jax.experimental.pallas import tpu as pltpu

F32 = jnp.float32
BF16 = jnp.bfloat16

LANE = 128
EPS = 1e-6
NEG = -1e30

GRID_W = 64
NA_HEADS = 8
NA_HEAD_DIM = 128
NA_MAX_ROWS = 8
NA_COLS = 16
NA_DR = 2 * NA_MAX_ROWS - 1
NA_DC = 2 * NA_COLS - 1
NA_TBL = 3 * NA_DR + 1
NA_QROWS = 4
NA_WIN = NA_QROWS + NA_MAX_ROWS
NA_UNROLL = 16

RET_HEADS = 8
RET_QK_DIM = 128
RET_V_DIM = 256
RET_BLOCK = 512
ROPE_BASE = 10000.0

XA_HEADS = 4
XA_HEAD_DIM = 256

SLAB_NA_Q, SLAB_NA_K, SLAB_NA_V = 0, 8, 16
SLAB_RQ, SLAB_RK, SLAB_RV, SLAB_RG = 24, 32, 40, 56
SLAB_XQ = 72
SLAB_G_NA, SLAB_G_RET, SLAB_G_MEM = 80, 96, 112


def _params(sem, vmem_mib):
    return pltpu.CompilerParams(dimension_semantics=sem, vmem_limit_bytes=vmem_mib * 2**20)


def _rms(x, g):
    return x * lax.rsqrt(jnp.mean(x * x, axis=-1, keepdims=True) + EPS) * g


def _norm_proj_kernel(n_side, x_ref, g_ref, w_ref, *refs):
    side_in, o_ref = refs[:n_side], refs[n_side]
    side_out, hn_ref = refs[n_side + 1:2 * n_side + 1], refs[2 * n_side + 1]

    @pl.when(pl.program_id(1) == 0)
    def _():
        hn_ref[...] = _rms(x_ref[...], g_ref[...]).astype(BF16)

    acc = jnp.dot(hn_ref[...], w_ref[...].astype(BF16), preferred_element_type=F32)
    for j in range(o_ref.shape[0]):
        o_ref[j] = acc[:, j * LANE:(j + 1) * LANE].astype(BF16)
    for src, dst in zip(side_in, side_out):
        dst[...] = src[...].astype(BF16)


def _norm_proj(x2d, g, w, tm, tn, side=()):
    m, k = x2d.shape
    n = w.shape[1]
    side_specs = [pl.BlockSpec(blk, imap) for _, blk, imap in side]
    out = pl.pallas_call(
        functools.partial(_norm_proj_kernel, len(side)),
        grid=(m // tm, n // tn),
        in_specs=[
            pl.BlockSpec((tm, k), lambda i, j: (i, 0)),
            pl.BlockSpec((1, k), lambda i, j: (0, 0)),
            pl.BlockSpec((k, tn), lambda i, j: (0, j)),
        ] + side_specs,
        out_specs=[pl.BlockSpec((tn // LANE, tm, LANE), lambda i, j: (j, i, 0))] + side_specs,
        out_shape=[jax.ShapeDtypeStruct((n // LANE, m, LANE), BF16)]
        + [jax.ShapeDtypeStruct(a.shape, BF16) for a, _, _ in side],
        scratch_shapes=[pltpu.VMEM((tm, k), BF16)],
        compiler_params=_params(("parallel", "arbitrary"), 58),
        name="norm_proj",
    )(x2d, g.reshape(1, k), w, *[a for a, _, _ in side])
    return out[0], out[1:]


def _na_table_kernel(rpb_ref, t_ref, base_ref):
    h = pl.program_id(0)
    qc = lax.broadcasted_iota(jnp.int32, (GRID_W, LANE), 0)
    lane = lax.broadcasted_iota(jnp.int32, (GRID_W, LANE), 1)
    kc = lane & (GRID_W - 1)
    d = jnp.clip(kc - qc, -(NA_COLS - 1), NA_COLS - 1) + (NA_COLS - 1)
    cs = jnp.clip(qc - NA_COLS // 2, 0, GRID_W - NA_COLS)
    col_ok = (kc >= cs) & (kc < cs + NA_COLS)
    left = lane < GRID_W
    neg = jnp.full((GRID_W, LANE), NEG, F32)

    def body(dr, _):
        base = (h * NA_DR + dr) * NA_DC
        val = jnp.zeros((GRID_W, LANE), F32)
        for dd in range(NA_DC):
            val = jnp.where(d == dd, rpb_ref[base + dd], val)
        base_ref[dr] = jnp.where(col_ok, val, NEG)
        return 0

    lax.fori_loop(0, NA_DR, body, 0)
    for dr in range(NA_DR):
        second = base_ref[dr + 1] if dr + 1 < NA_DR else neg
        t_ref[0, dr] = jnp.where(left, base_ref[dr], second)
        t_ref[0, NA_DR + dr] = jnp.where(left, base_ref[dr], neg)
        t_ref[0, 2 * NA_DR + dr] = jnp.where(left, neg, base_ref[dr])
    t_ref[0, 3 * NA_DR] = neg


def _na_tables(rpb):
    return pl.pallas_call(
        _na_table_kernel,
        grid=(NA_HEADS,),
        in_specs=[pl.BlockSpec(memory_space=pltpu.SMEM)],
        out_specs=pl.BlockSpec((1, NA_TBL, GRID_W, LANE), lambda h: (h, 0, 0, 0)),
        out_shape=jax.ShapeDtypeStruct((NA_HEADS, NA_TBL, GRID_W, LANE), F32),
        scratch_shapes=[pltpu.VMEM((NA_DR, GRID_W, LANE), F32)],
        compiler_params=_params(("arbitrary",), 16),
        name="na_tables",
    )(rpb.reshape(-1))


def _na_kernel(q_ref, k_ref, v_ref, t_ref, gq_ref, gk_ref, o_ref, kt_ref, va_ref):
    s_len = q_ref.shape[1]
    rows = s_len // GRID_W
    step_tok = NA_QROWS * GRID_W
    win_tok = NA_WIN * GRID_W
    n_tiles = NA_WIN // 2
    gq = gq_ref[...] * (NA_HEAD_DIM ** -0.5)
    gk = gk_ref[...]
    va_ref[:, :NA_HEAD_DIM] = v_ref[0]
    va_ref[:, NA_HEAD_DIM:] = jnp.ones((s_len, NA_HEAD_DIM), BF16)

    def knorm(c, _):
        sl = pl.ds(pl.multiple_of(c * LANE, LANE), LANE)
        kt_ref[c] = _rms(k_ref[0, sl, :].astype(F32), gk).astype(BF16).T
        return 0

    lax.fori_loop(0, s_len // LANE, knorm, 0, unroll=4)

    def step(i, _):
        r0 = NA_QROWS * i
        ws = jnp.clip(r0 - NA_MAX_ROWS // 2, 0, rows - NA_WIN)
        wp = ws // 2
        qsl = pl.ds(pl.multiple_of(i * step_tok, step_tok), step_tok)
        wsl = pl.ds(pl.multiple_of(ws * GRID_W, LANE), win_tok)
        qn = _rms(q_ref[0, qsl, :].astype(F32), gq).astype(BF16)
        kwin = jnp.concatenate([kt_ref[wp + t] for t in range(n_tiles)], axis=1)
        s = jnp.dot(qn, kwin, preferred_element_type=F32)
        bias_rows = []
        for qr in range(NA_QROWS):
            r = r0 + qr
            rs = jnp.clip(r - NA_MAX_ROWS // 2, 0, rows - NA_MAX_ROWS)
            tiles = []
            for t in range(n_tiles):
                ka = ws + 2 * t
                dr = ka - r + (NA_MAX_ROWS - 1)
                va = (ka >= rs) & (ka < rs + NA_MAX_ROWS)
                vb = (ka + 1 >= rs) & (ka + 1 < rs + NA_MAX_ROWS)
                idx = jnp.where(va, jnp.where(vb, dr, NA_DR + dr),
                                jnp.where(vb, 2 * NA_DR + dr + 1, 3 * NA_DR))
                tiles.append(t_ref[0, idx])
            bias_rows.append(jnp.concatenate(tiles, axis=1))
        s = s + jnp.concatenate(bias_rows, axis=0)
        e = jnp.exp(s - jnp.max(s, axis=-1, keepdims=True))
        o = jnp.dot(e.astype(BF16), va_ref[wsl, :], preferred_element_type=F32)
        o_ref[qsl, :] = (o[:, :NA_HEAD_DIM] / o[:, NA_HEAD_DIM:]).astype(BF16)
        return 0

    lax.fori_loop(0, rows // NA_QROWS, step, 0, unroll=NA_UNROLL)


def _na_attention(proj, tbl, gq, gk, batch, s_len):
    t = batch * s_len
    g_spec = pl.BlockSpec((1, NA_HEAD_DIM), lambda b, h: (0, 0))
    return pl.pallas_call(
        _na_kernel,
        grid=(batch, NA_HEADS),
        in_specs=[
            pl.BlockSpec((1, s_len, LANE), lambda b, h: (SLAB_NA_Q + h, b, 0)),
            pl.BlockSpec((1, s_len, LANE), lambda b, h: (SLAB_NA_K + h, b, 0)),
            pl.BlockSpec((1, s_len, LANE), lambda b, h: (SLAB_NA_V + h, b, 0)),
            pl.BlockSpec((1, NA_TBL, GRID_W, LANE), lambda b, h: (h, 0, 0, 0)),
            g_spec, g_spec,
        ],
        out_specs=pl.BlockSpec((s_len, NA_HEAD_DIM), lambda b, h: (b, h)),
        out_shape=jax.ShapeDtypeStruct((t, NA_HEADS * NA_HEAD_DIM), BF16),
        scratch_shapes=[pltpu.VMEM((s_len // LANE, NA_HEAD_DIM, LANE), BF16),
                        pltpu.VMEM((s_len, 2 * NA_HEAD_DIM), BF16)],
        compiler_params=_params(("parallel", "parallel"), 32),
        name="na_attention",
    )(proj, proj, proj, tbl, gq.reshape(1, -1), gk.reshape(1, -1))


def _log_sigmoid(x):
    return -(jnp.maximum(-x, 0.0) + jnp.log1p(jnp.exp(-jnp.abs(x))))


def _ret_kernel(q_ref, k_ref, v_ref, g_ref, cos_ref, sin_ref, lf_ref, lb_ref, gn_ref,
                o_ref, qr_ref, kt_ref, d_ref, qdf_ref, qdb_ref, kv_ref, s_ref):
    s_len = q_ref.shape[1]
    c = RET_BLOCK
    nb = s_len // c
    dk = RET_QK_DIM
    half = dk // 2

    lgf = _log_sigmoid(lf_ref[0][:, :1])
    lgb = _log_sigmoid(lb_ref[0][:, :1])
    ic = lax.broadcasted_iota(jnp.int32, (c, 1), 0).astype(F32)
    jr = lax.broadcasted_iota(jnp.int32, (1, c), 1).astype(F32)
    diff = ic - jr
    d_ref[...] = jnp.where(diff >= 0, jnp.exp(lgf * jnp.maximum(diff, 0.0)),
                           jnp.exp(lgb * jnp.maximum(-diff, 0.0)))
    qdf_ref[...] = jnp.broadcast_to(jnp.exp(lgf * (ic + 1.0)), (c, dk))
    qdb_ref[...] = jnp.broadcast_to(jnp.exp(lgb * (c - ic)), (c, dk))
    kdf = jnp.exp(lgf * (c - 1.0 - jr))
    kdb = jnp.exp(lgb * jr)
    cd_f = jnp.exp(lgf * c)
    cd_b = jnp.exp(lgb * c)

    def block_v(sl):
        return jnp.concatenate([v_ref[0, sl, :], v_ref[1, sl, :]], axis=1)

    def prep(n, _):
        sl = pl.ds(pl.multiple_of(n * c, c), c)
        cos = cos_ref[sl, :]
        sin = sin_ref[sl, :]
        q = q_ref[0, sl, :].astype(F32)
        k = k_ref[0, sl, :].astype(F32)
        qr_ref[sl, :] = (q * cos + pltpu.roll(q, half, 1) * sin).astype(BF16)
        kt = ((k * cos + pltpu.roll(k, half, 1) * sin) * (dk ** -0.5)).T
        kt_ref[n] = kt.astype(BF16)
        lhs = jnp.concatenate([(kt * kdf).astype(BF16), (kt * kdb).astype(BF16)], axis=0)
        kv_ref[n] = jnp.dot(lhs, block_v(sl), preferred_element_type=F32)
        return 0

    lax.fori_loop(0, nb, prep, 0, unroll=2)

    def scan_f(n, sf):
        s_ref[n, :dk, :] = sf.astype(BF16)
        return cd_f * sf + kv_ref[n, :dk, :]

    def scan_b(t, sb):
        n = nb - 1 - t
        s_ref[n, dk:, :] = sb.astype(BF16)
        return cd_b * sb + kv_ref[n, dk:, :]

    zero = jnp.zeros((dk, RET_V_DIM), F32)
    lax.fori_loop(0, nb, scan_f, zero)
    lax.fori_loop(0, nb, scan_b, zero)

    gn = gn_ref[0]

    def out(n, _):
        sl = pl.ds(pl.multiple_of(n * c, c), c)
        q = qr_ref[sl, :]
        qf32 = q.astype(F32)
        a = jnp.dot(q, kt_ref[n], preferred_element_type=F32) * d_ref[...]
        lhs = jnp.concatenate([a.astype(BF16), (qf32 * qdf_ref[...]).astype(BF16),
                               (qf32 * qdb_ref[...]).astype(BF16)], axis=1)
        rhs = jnp.concatenate([block_v(sl), s_ref[n]], axis=0)
        o = jnp.dot(lhs, rhs, preferred_element_type=F32)
        mu = jnp.mean(o, axis=-1, keepdims=True)
        oc = o - mu
        y = oc * lax.rsqrt(jnp.mean(oc * oc, axis=-1, keepdims=True) + EPS) * gn
        gate = jnp.concatenate([g_ref[0, sl, :], g_ref[1, sl, :]], axis=1).astype(F32)
        o_ref[sl, :] = (y * gate * jax.nn.sigmoid(gate)).astype(BF16)
        return 0

    lax.fori_loop(0, nb, out, 0, unroll=2)


def _retention(proj, cos2, sin2, lf, lb, gn, batch, s_len):
    t = batch * s_len
    nb = s_len // RET_BLOCK
    dec_spec = pl.BlockSpec((1, 1, LANE), lambda b, h: (h, 0, 0))
    return pl.pallas_call(
        _ret_kernel,
        grid=(batch, RET_HEADS),
        in_specs=[
            pl.BlockSpec((1, s_len, LANE), lambda b, h: (SLAB_RQ + h, b, 0)),
            pl.BlockSpec((1, s_len, LANE), lambda b, h: (SLAB_RK + h, b, 0)),
            pl.BlockSpec((2, s_len, LANE), lambda b, h: (SLAB_RV // 2 + h, b, 0)),
            pl.BlockSpec((2, s_len, LANE), lambda b, h: (SLAB_RG // 2 + h, b, 0)),
            pl.BlockSpec((s_len, RET_QK_DIM), lambda b, h: (0, 0)),
            pl.BlockSpec((s_len, RET_QK_DIM), lambda b, h: (0, 0)),
            dec_spec, dec_spec,
            pl.BlockSpec((1, 1, RET_V_DIM), lambda b, h: (h, 0, 0)),
        ],
        out_specs=pl.BlockSpec((s_len, RET_V_DIM), lambda b, h: (b, h)),
        out_shape=jax.ShapeDtypeStruct((t, RET_HEADS * RET_V_DIM), BF16),
        scratch_shapes=[
            pltpu.VMEM((s_len, RET_QK_DIM), BF16),
            pltpu.VMEM((nb, RET_QK_DIM, RET_BLOCK), BF16),
            pltpu.VMEM((RET_BLOCK, RET_BLOCK), F32),
            pltpu.VMEM((RET_BLOCK, RET_QK_DIM), F32),
            pltpu.VMEM((RET_BLOCK, RET_QK_DIM), F32),
            pltpu.VMEM((nb, 2 * RET_QK_DIM, RET_V_DIM), F32),
            pltpu.VMEM((nb, 2 * RET_QK_DIM, RET_V_DIM), BF16),
        ],
        compiler_params=_params(("parallel", "parallel"), 48),
        name="retention",
    )(proj, proj, proj, proj, cos2, sin2, lf, lb, gn.reshape(RET_HEADS, 1, RET_V_DIM))


def _xa_kernel(q_ref, k_ref, v_ref, gq_ref, gk_ref, o_ref):
    s_len = q_ref.shape[1]
    tq = 512
    gq = gq_ref[...] * (XA_HEAD_DIM ** -0.5)
    k = jnp.concatenate([k_ref[0], k_ref[1]], axis=1).astype(F32)
    kn = _rms(k, gk_ref[...]).astype(BF16)
    v = jnp.concatenate([v_ref[0], v_ref[1]], axis=1)

    def body(i, _):
        sl = pl.ds(pl.multiple_of(i * tq, tq), tq)
        q = jnp.concatenate([q_ref[0, sl, :], q_ref[1, sl, :]], axis=1).astype(F32)
        qn = _rms(q, gq).astype(BF16)
        s = lax.dot_general(qn, kn, (((1,), (1,)), ((), ())), preferred_element_type=F32)
        e = jnp.exp(s - jnp.max(s, axis=-1, keepdims=True))
        l = jnp.sum(e, axis=-1, keepdims=True)
        o = jnp.dot(e.astype(BF16), v, preferred_element_type=F32)
        o_ref[sl, :] = (o / l).astype(BF16)
        return 0

    lax.fori_loop(0, s_len // tq, body, 0)


def _mem_xattn(proj, mkv, gq, gk, batch, s_len, n_mem):
    t = batch * s_len
    g_spec = pl.BlockSpec((1, XA_HEAD_DIM), lambda b, h: (0, 0))
    return pl.pallas_call(
        _xa_kernel,
        grid=(batch, XA_HEADS),
        in_specs=[
            pl.BlockSpec((2, s_len, LANE), lambda b, h: (SLAB_XQ // 2 + h, b, 0)),
            pl.BlockSpec((2, n_mem, LANE), lambda b, h: (h, b, 0)),
            pl.BlockSpec((2, n_mem, LANE), lambda b, h: (XA_HEADS + h, b, 0)),
            g_spec, g_spec,
        ],
        out_specs=pl.BlockSpec((s_len, XA_HEAD_DIM), lambda b, h: (b, h)),
        out_shape=jax.ShapeDtypeStruct((t, XA_HEADS * XA_HEAD_DIM), BF16),
        compiler_params=_params(("parallel", "parallel"), 32),
        name="mem_xattn",
    )(proj, mkv, mkv, gq.reshape(1, -1), gk.reshape(1, -1))


def _merge_kernel(ona_ref, oret_ref, omem_ref, wna_ref, wret_ref, wmem_ref,
                  gna_ref, gret_ref, gmem_ref, o_ref):
    y_na = jnp.dot(ona_ref[...], wna_ref[...].astype(BF16), preferred_element_type=F32)
    y_ret = jnp.dot(oret_ref[...], wret_ref[...].astype(BF16), preferred_element_type=F32)
    y_mem = jnp.dot(omem_ref[...], wmem_ref[...].astype(BF16), preferred_element_type=F32)
    for j in range(gna_ref.shape[0]):
        cs = slice(j * LANE, (j + 1) * LANE)
        o_ref[:, cs] = (jax.nn.sigmoid(gna_ref[j].astype(F32)) * y_na[:, cs]
                        + jax.nn.sigmoid(gret_ref[j].astype(F32)) * y_ret[:, cs]
                        + jax.nn.sigmoid(gmem_ref[j].astype(F32)) * y_mem[:, cs]).astype(BF16)


def _merge(o_na, o_ret, o_mem, w_na, w_ret, w_mem, proj, tm, tn):
    t = o_na.shape[0]
    d = w_na.shape[1]
    ns = tn // LANE

    def lhs_spec(a):
        return pl.BlockSpec((tm, a.shape[1]), lambda i, j: (i, 0))

    def w_spec(w):
        return pl.BlockSpec((w.shape[0], tn), lambda i, j: (0, j))

    def gate_spec(first):
        return pl.BlockSpec((ns, tm, LANE), lambda i, j: (first // ns + j, i, 0))

    return pl.pallas_call(
        _merge_kernel,
        grid=(t // tm, d // tn),
        in_specs=[lhs_spec(o_na), lhs_spec(o_ret), lhs_spec(o_mem),
                  w_spec(w_na), w_spec(w_ret), w_spec(w_mem),
                  gate_spec(SLAB_G_NA), gate_spec(SLAB_G_RET), gate_spec(SLAB_G_MEM)],
        out_specs=pl.BlockSpec((tm, tn), lambda i, j: (i, j)),
        out_shape=jax.ShapeDtypeStruct((t, d), BF16),
        compiler_params=_params(("parallel", "arbitrary"), 48),
        name="merge",
    )(o_na, o_ret, o_mem, w_na, w_ret, w_mem, proj, proj, proj)


def _out_proj_kernel(m_ref, w_ref, x_ref, o_ref, wb_ref):
    @pl.when(pl.program_id(0) == 0)
    def _():
        wb_ref[...] = w_ref[...].astype(BF16)

    o_ref[...] = x_ref[...] + jnp.dot(m_ref[...], wb_ref[...], preferred_element_type=F32)


def _out_proj(merged, w, x2d, tm):
    t, k = merged.shape
    d = w.shape[1]
    return pl.pallas_call(
        _out_proj_kernel,
        grid=(t // tm,),
        in_specs=[pl.BlockSpec((tm, k), lambda i: (i, 0)),
                  pl.BlockSpec((k, d), lambda i: (0, 0), pipeline_mode=pl.Buffered(1)),
                  pl.BlockSpec((tm, d), lambda i: (i, 0))],
        out_specs=pl.BlockSpec((tm, d), lambda i: (i, 0)),
        out_shape=jax.ShapeDtypeStruct((t, d), F32),
        scratch_shapes=[pltpu.VMEM((k, d), BF16)],
        compiler_params=_params(("arbitrary",), 56),
        name="out_proj",
    )(merged, w, x2d)


def _ffn_kernel(x_ref, g_ref, w1_ref, w2_ref, o_ref, h_ref):
    @pl.when(pl.program_id(1) == 0)
    def _():
        x = x_ref[...]
        h_ref[...] = _rms(x, g_ref[...]).astype(BF16)
        o_ref[...] = x

    a = jnp.maximum(jnp.dot(h_ref[...], w1_ref[...], preferred_element_type=F32), 0.0)
    o_ref[...] += jnp.dot((a * a).astype(BF16), w2_ref[...], preferred_element_type=F32)


def _ffn(x1, g, w1, w2, tm, tf):
    t, d = x1.shape
    dff = w1.shape[1]
    return pl.pallas_call(
        _ffn_kernel,
        grid=(t // tm, dff // tf),
        in_specs=[pl.BlockSpec((tm, d), lambda i, f: (i, 0)),
                  pl.BlockSpec((1, d), lambda i, f: (0, 0)),
                  pl.BlockSpec((d, tf), lambda i, f: (0, f)),
                  pl.BlockSpec((tf, d), lambda i, f: (f, 0))],
        out_specs=pl.BlockSpec((tm, d), lambda i, f: (i, 0)),
        out_shape=jax.ShapeDtypeStruct((t, d), F32),
        scratch_shapes=[pltpu.VMEM((tm, d), BF16)],
        compiler_params=_params(("parallel", "arbitrary"), 48),
        name="ffn",
    )(x1, g.reshape(1, d), w1, w2)


def _rope_tables(s_len):
    half = RET_QK_DIM // 2
    inv = np.power(np.float64(ROPE_BASE), -np.arange(half, dtype=np.float64) / half)
    ang = np.arange(s_len, dtype=np.float64)[:, None] * inv[None, :]
    cos, sin = np.cos(ang), np.sin(ang)
    return (jnp.asarray(np.concatenate([cos, cos], axis=1), F32),
            jnp.asarray(np.concatenate([-sin, sin], axis=1), F32))


def kernel(x, mem, norm_mix_g, w_in, na_q_norm_g, na_k_norm_g, na_rpb, ret_decay_logit_fwd, ret_decay_logit_bwd, ret_gn_g, mem_norm_g, w_mem_kv, xa_q_norm_g, xa_k_norm_g, w_br_na, w_br_ret, w_br_mem, w_out, norm_ffn_g, w_ff1, w_ff2):
    batch, s_len, d = x.shape
    n_mem = mem.shape[1]
    t = batch * s_len
    depth = w_in.shape[0]
    tm = min(1024, t)
    cos2, sin2 = _rope_tables(s_len)
    x2d = x.reshape(t, d)
    mem2d = mem.reshape(batch * n_mem, d)

    def lane_bcast(v):
        return jnp.broadcast_to(v.astype(F32)[:, None, None], (v.shape[0], 1, LANE))

    tn_in = 1024
    gm, gn = t // tm, w_in.shape[2] // tn_in

    def side_job(w):
        r, c = w.shape
        if r % (gm * gn * 16) == 0:
            return w, (r // (gm * gn), c), lambda i, j: (i * gn + j, 0)
        assert r % (gn * 16) == 0 and c % (gm * LANE) == 0, (w.shape, gm, gn)
        return w, (r // gn, c // gm), lambda i, j: (j, i)

    for l in range(depth):
        proj, (w_ff1_bf, w_ff2_bf) = _norm_proj(x2d, norm_mix_g[l], w_in[l], tm, tn_in,
                                                side=(side_job(w_ff1[l]), side_job(w_ff2[l])))
        mkv, _ = _norm_proj(mem2d, mem_norm_g[l], w_mem_kv[l], batch * n_mem, 1024)

        o_na = _na_attention(proj, _na_tables(na_rpb[l]), na_q_norm_g[l], na_k_norm_g[l], batch, s_len)
        o_ret = _retention(proj, cos2, sin2, lane_bcast(ret_decay_logit_fwd[l]),
                           lane_bcast(ret_decay_logit_bwd[l]), ret_gn_g[l], batch, s_len)
        o_mem = _mem_xattn(proj, mkv, xa_q_norm_g[l], xa_k_norm_g[l], batch, s_len, n_mem)

        merged = _merge(o_na, o_ret, o_mem, w_br_na[l], w_br_ret[l], w_br_mem[l], proj, tm, 256)
        x1 = _out_proj(merged, w_out[l], x2d, min(512, t))
        x2d = _ffn(x1, norm_ffn_g[l], w_ff1_bf, w_ff2_bf, min(512, t), 512)
    return x2d.reshape(batch, s_len, d)
```

```python
import functools

import jax
import jax.numpy as jnp
import numpy as np
from jax import lax
from jax.experimental import pallas as pl
from jax.experimental.pallas import tpu as pltpu

F32 = jnp.float32
BF16 = jnp.bfloat16

LANE = 128
EPS = 1e-6
NEG = -1e30

GRID_W = 64
NA_HEADS = 8
NA_HEAD_DIM = 128
NA_MAX_ROWS = 8
NA_COLS = 16
NA_DR = 2 * NA_MAX_ROWS - 1
NA_DC = 2 * NA_COLS - 1
NA_TBL = 3 * NA_DR + 1
NA_QROWS = 4
NA_WIN = NA_QROWS + NA_MAX_ROWS
NA_UNROLL = 16

RET_HEADS = 8
RET_QK_DIM = 128
RET_V_DIM = 256
RET_BLOCK = 512
ROPE_BASE = 10000.0

XA_HEADS = 4
XA_HEAD_DIM = 256

SLAB_NA_Q, SLAB_NA_K, SLAB_NA_V = 0, 8, 16
SLAB_RQ, SLAB_RK, SLAB_RV, SLAB_RG = 24, 32, 40, 56
SLAB_XQ = 72
SLAB_G_NA, SLAB_G_RET, SLAB_G_MEM = 80, 96, 112


def _params(sem, vmem_mib):
    return pltpu.CompilerParams(dimension_semantics=sem, vmem_limit_bytes=vmem_mib * 2**20)


def _rms(x, g):
    return x * lax.rsqrt(jnp.mean(x * x, axis=-1, keepdims=True) + EPS) * g


def _sigmoid(x):
    return 0.5 * jnp.tanh(0.5 * x) + 0.5


def _norm_proj_kernel(n_side, x_ref, g_ref, w_ref, *refs):
    side_in, o_ref = refs[:n_side], refs[n_side]
    side_out, hn_ref = refs[n_side + 1:2 * n_side + 1], refs[2 * n_side + 1]

    @pl.when(pl.program_id(1) == 0)
    def _():
        hn_ref[...] = _rms(x_ref[...], g_ref[...]).astype(BF16)

    acc = jnp.dot(hn_ref[...], w_ref[...].astype(BF16), preferred_element_type=F32)
    for j in range(o_ref.shape[0]):
        o_ref[j] = acc[:, j * LANE:(j + 1) * LANE].astype(BF16)
    for src, dst in zip(side_in, side_out):
        dst[...] = src[...].astype(BF16)


def _norm_proj(x2d, g, w, tm, tn, side=()):
    m, k = x2d.shape
    n = w.shape[1]
    side_specs = [pl.BlockSpec(blk, imap) for _, blk, imap in side]
    out = pl.pallas_call(
        functools.partial(_norm_proj_kernel, len(side)),
        grid=(m // tm, n // tn),
        in_specs=[
            pl.BlockSpec((tm, k), lambda i, j: (i, 0)),
            pl.BlockSpec((1, k), lambda i, j: (0, 0)),
            pl.BlockSpec((k, tn), lambda i, j: (0, j)),
        ] + side_specs,
        out_specs=[pl.BlockSpec((tn // LANE, tm, LANE), lambda i, j: (j, i, 0))] + side_specs,
        out_shape=[jax.ShapeDtypeStruct((n // LANE, m, LANE), BF16)]
        + [jax.ShapeDtypeStruct(a.shape, BF16) for a, _, _ in side],
        scratch_shapes=[pltpu.VMEM((tm, k), BF16)],
        compiler_params=_params(("parallel", "arbitrary"), 58),
        name="norm_proj",
    )(x2d, g.reshape(1, k), w, *[a for a, _, _ in side])
    return out[0], out[1:]


def _na_table_kernel(rpb_ref, t_ref, base_ref):
    h = pl.program_id(0)
    qc = lax.broadcasted_iota(jnp.int32, (GRID_W, LANE), 0)
    lane = lax.broadcasted_iota(jnp.int32, (GRID_W, LANE), 1)
    kc = lane & (GRID_W - 1)
    d = jnp.clip(kc - qc, -(NA_COLS - 1), NA_COLS - 1) + (NA_COLS - 1)
    cs = jnp.clip(qc - NA_COLS // 2, 0, GRID_W - NA_COLS)
    col_ok = (kc >= cs) & (kc < cs + NA_COLS)
    left = lane < GRID_W
    neg = jnp.full((GRID_W, LANE), NEG, F32)

    def body(dr, _):
        base = (h * NA_DR + dr) * NA_DC
        val = jnp.zeros((GRID_W, LANE), F32)
        for dd in range(NA_DC):
            val = jnp.where(d == dd, rpb_ref[base + dd], val)
        base_ref[dr] = jnp.where(col_ok, val, NEG)
        return 0

    lax.fori_loop(0, NA_DR, body, 0)
    for dr in range(NA_DR):
        second = base_ref[dr + 1] if dr + 1 < NA_DR else neg
        t_ref[0, dr] = jnp.where(left, base_ref[dr], second)
        t_ref[0, NA_DR + dr] = jnp.where(left, base_ref[dr], neg)
        t_ref[0, 2 * NA_DR + dr] = jnp.where(left, neg, base_ref[dr])
    t_ref[0, 3 * NA_DR] = neg


def _na_tables(rpb):
    return pl.pallas_call(
        _na_table_kernel,
        grid=(NA_HEADS,),
        in_specs=[pl.BlockSpec(memory_space=pltpu.SMEM)],
        out_specs=pl.BlockSpec((1, NA_TBL, GRID_W, LANE), lambda h: (h, 0, 0, 0)),
        out_shape=jax.ShapeDtypeStruct((NA_HEADS, NA_TBL, GRID_W, LANE), F32),
        scratch_shapes=[pltpu.VMEM((NA_DR, GRID_W, LANE), F32)],
        compiler_params=_params(("arbitrary",), 16),
        name="na_tables",
    )(rpb.reshape(-1))


def _na_kernel(q_ref, k_ref, v_ref, t_ref, gq_ref, gk_ref, o_ref, kt_ref, va_ref):
    s_len = q_ref.shape[1]
    rows = s_len // GRID_W
    step_tok = NA_QROWS * GRID_W
    win_tok = NA_WIN * GRID_W
    n_tiles = NA_WIN // 2
    gq = gq_ref[...] * (NA_HEAD_DIM ** -0.5)
    gk = gk_ref[...]
    va_ref[:, :NA_HEAD_DIM] = v_ref[0]
    va_ref[:, NA_HEAD_DIM:] = jnp.ones((s_len, NA_HEAD_DIM), BF16)

    def knorm(c, _):
        sl = pl.ds(pl.multiple_of(c * LANE, LANE), LANE)
        kt_ref[c] = _rms(k_ref[0, sl, :].astype(F32), gk).astype(BF16).T
        return 0

    lax.fori_loop(0, s_len // LANE, knorm, 0, unroll=4)

    def step(i, _):
        r0 = NA_QROWS * i
        ws = jnp.clip(r0 - NA_MAX_ROWS // 2, 0, rows - NA_WIN)
        wp = ws // 2
        qsl = pl.ds(pl.multiple_of(i * step_tok, step_tok), step_tok)
        wsl = pl.ds(pl.multiple_of(ws * GRID_W, LANE), win_tok)
        qn = _rms(q_ref[0, qsl, :].astype(F32), gq).astype(BF16)
        kwin = jnp.concatenate([kt_ref[wp + t] for t in range(n_tiles)], axis=1)
        s = jnp.dot(qn, kwin, preferred_element_type=F32)
        bias_rows = []
        for qr in range(NA_QROWS):
            r = r0 + qr
            rs = jnp.clip(r - NA_MAX_ROWS // 2, 0, rows - NA_MAX_ROWS)
            tiles = []
            for t in range(n_tiles):
                ka = ws + 2 * t
                dr = ka - r + (NA_MAX_ROWS - 1)
                va = (ka >= rs) & (ka < rs + NA_MAX_ROWS)
                vb = (ka + 1 >= rs) & (ka + 1 < rs + NA_MAX_ROWS)
                idx = jnp.where(va, jnp.where(vb, dr, NA_DR + dr),
                                jnp.where(vb, 2 * NA_DR + dr + 1, 3 * NA_DR))
                tiles.append(t_ref[0, idx])
            bias_rows.append(jnp.concatenate(tiles, axis=1))
        s = s + jnp.concatenate(bias_rows, axis=0)
        e = jnp.exp(s - jnp.max(s, axis=-1, keepdims=True))
        o = jnp.dot(e.astype(BF16), va_ref[wsl, :], preferred_element_type=F32)
        o_ref[qsl, :] = (o[:, :NA_HEAD_DIM] / o[:, NA_HEAD_DIM:]).astype(BF16)
        return 0

    lax.fori_loop(0, rows // NA_QROWS, step, 0, unroll=NA_UNROLL)


def _na_attention(proj, tbl, gq, gk, batch, s_len):
    t = batch * s_len
    g_spec = pl.BlockSpec((1, NA_HEAD_DIM), lambda b, h: (0, 0))
    return pl.pallas_call(
        _na_kernel,
        grid=(batch, NA_HEADS),
        in_specs=[
            pl.BlockSpec((1, s_len, LANE), lambda b, h: (SLAB_NA_Q + h, b, 0)),
            pl.BlockSpec((1, s_len, LANE), lambda b, h: (SLAB_NA_K + h, b, 0)),
            pl.BlockSpec((1, s_len, LANE), lambda b, h: (SLAB_NA_V + h, b, 0)),
            pl.BlockSpec((1, NA_TBL, GRID_W, LANE), lambda b, h: (h, 0, 0, 0)),
            g_spec, g_spec,
        ],
        out_specs=pl.BlockSpec((s_len, NA_HEAD_DIM), lambda b, h: (b, h)),
        out_shape=jax.ShapeDtypeStruct((t, NA_HEADS * NA_HEAD_DIM), BF16),
        scratch_shapes=[pltpu.VMEM((s_len // LANE, NA_HEAD_DIM, LANE), BF16),
                        pltpu.VMEM((s_len, 2 * NA_HEAD_DIM), BF16)],
        compiler_params=_params(("parallel", "parallel"), 32),
        name="na_attention",
    )(proj, proj, proj, tbl, gq.reshape(1, -1), gk.reshape(1, -1))


def _log_sigmoid(x):
    return -(jnp.maximum(-x, 0.0) + jnp.log1p(jnp.exp(-jnp.abs(x))))


def _ret_kernel(q_ref, k_ref, v_ref, g_ref, cos_ref, sin_ref, lf_ref, lb_ref, gn_ref,
                o_ref, qr_ref, kt_ref, d_ref, qdf_ref, qdb_ref, kv_ref, s_ref):
    s_len = q_ref.shape[1]
    c = RET_BLOCK
    nb = s_len // c
    dk = RET_QK_DIM
    half = dk // 2

    lgf = _log_sigmoid(lf_ref[0][:, :1])
    lgb = _log_sigmoid(lb_ref[0][:, :1])
    ic = lax.broadcasted_iota(jnp.int32, (c, 1), 0).astype(F32)
    jr = lax.broadcasted_iota(jnp.int32, (1, c), 1).astype(F32)
    diff = ic - jr
    d_ref[...] = jnp.where(diff >= 0, jnp.exp(lgf * jnp.maximum(diff, 0.0)),
                           jnp.exp(lgb * jnp.maximum(-diff, 0.0)))
    qdf_ref[...] = jnp.broadcast_to(jnp.exp(lgf * (ic + 1.0)), (c, dk))
    qdb_ref[...] = jnp.broadcast_to(jnp.exp(lgb * (c - ic)), (c, dk))
    kdf = jnp.exp(lgf * (c - 1.0 - jr))
    kdb = jnp.exp(lgb * jr)
    cd_f = jnp.exp(lgf * c)
    cd_b = jnp.exp(lgb * c)

    def block_v(sl):
        return jnp.concatenate([v_ref[0, sl, :], v_ref[1, sl, :]], axis=1)

    def prep(n, _):
        sl = pl.ds(pl.multiple_of(n * c, c), c)
        cos = cos_ref[sl, :]
        sin = sin_ref[sl, :]
        q = q_ref[0, sl, :].astype(F32)
        k = k_ref[0, sl, :].astype(F32)
        qr_ref[sl, :] = (q * cos + pltpu.roll(q, half, 1) * sin).astype(BF16)
        kt = ((k * cos + pltpu.roll(k, half, 1) * sin) * (dk ** -0.5)).T
        kt_ref[n] = kt.astype(BF16)
        lhs = jnp.concatenate([(kt * kdf).astype(BF16), (kt * kdb).astype(BF16)], axis=0)
        kv_ref[n] = jnp.dot(lhs, block_v(sl), preferred_element_type=F32)
        return 0

    lax.fori_loop(0, nb, prep, 0, unroll=2)

    def scan_f(n, sf):
        s_ref[n, :dk, :] = sf.astype(BF16)
        return cd_f * sf + kv_ref[n, :dk, :]

    def scan_b(t, sb):
        n = nb - 1 - t
        s_ref[n, dk:, :] = sb.astype(BF16)
        return cd_b * sb + kv_ref[n, dk:, :]

    zero = jnp.zeros((dk, RET_V_DIM), F32)
    lax.fori_loop(0, nb, scan_f, zero)
    lax.fori_loop(0, nb, scan_b, zero)

    gn = gn_ref[0]

    def out(n, _):
        sl = pl.ds(pl.multiple_of(n * c, c), c)
        q = qr_ref[sl, :]
        qf32 = q.astype(F32)
        a = jnp.dot(q, kt_ref[n], preferred_element_type=F32) * d_ref[...]
        lhs = jnp.concatenate([a.astype(BF16), (qf32 * qdf_ref[...]).astype(BF16),
                               (qf32 * qdb_ref[...]).astype(BF16)], axis=1)
        rhs = jnp.concatenate([block_v(sl), s_ref[n]], axis=0)
        o = jnp.dot(lhs, rhs, preferred_element_type=F32)
        mu = jnp.mean(o, axis=-1, keepdims=True)
        oc = o - mu
        y = oc * lax.rsqrt(jnp.mean(oc * oc, axis=-1, keepdims=True) + EPS) * gn
        gate = jnp.concatenate([g_ref[0, sl, :], g_ref[1, sl, :]], axis=1).astype(F32)
        o_ref[sl, :] = (y * gate * _sigmoid(gate)).astype(BF16)
        return 0

    lax.fori_loop(0, nb, out, 0, unroll=2)


def _retention(proj, cos2, sin2, lf, lb, gn, batch, s_len):
    t = batch * s_len
    nb = s_len // RET_BLOCK
    dec_spec = pl.BlockSpec((1, 1, LANE), lambda b, h: (h, 0, 0))
    return pl.pallas_call(
        _ret_kernel,
        grid=(batch, RET_HEADS),
        in_specs=[
            pl.BlockSpec((1, s_len, LANE), lambda b, h: (SLAB_RQ + h, b, 0)),
            pl.BlockSpec((1, s_len, LANE), lambda b, h: (SLAB_RK + h, b, 0)),
            pl.BlockSpec((2, s_len, LANE), lambda b, h: (SLAB_RV // 2 + h, b, 0)),
            pl.BlockSpec((2, s_len, LANE), lambda b, h: (SLAB_RG // 2 + h, b, 0)),
            pl.BlockSpec((s_len, RET_QK_DIM), lambda b, h: (0, 0)),
            pl.BlockSpec((s_len, RET_QK_DIM), lambda b, h: (0, 0)),
            dec_spec, dec_spec,
            pl.BlockSpec((1, 1, RET_V_DIM), lambda b, h: (h, 0, 0)),
        ],
        out_specs=pl.BlockSpec((s_len, RET_V_DIM), lambda b, h: (b, h)),
        out_shape=jax.ShapeDtypeStruct((t, RET_HEADS * RET_V_DIM), BF16),
        scratch_shapes=[
            pltpu.VMEM((s_len, RET_QK_DIM), BF16),
            pltpu.VMEM((nb, RET_QK_DIM, RET_BLOCK), BF16),
            pltpu.VMEM((RET_BLOCK, RET_BLOCK), F32),
            pltpu.VMEM((RET_BLOCK, RET_QK_DIM), F32),
            pltpu.VMEM((RET_BLOCK, RET_QK_DIM), F32),
            pltpu.VMEM((nb, 2 * RET_QK_DIM, RET_V_DIM), F32),
            pltpu.VMEM((nb, 2 * RET_QK_DIM, RET_V_DIM), BF16),
        ],
        compiler_params=_params(("parallel", "parallel"), 48),
        name="retention",
    )(proj, proj, proj, proj, cos2, sin2, lf, lb, gn.reshape(RET_HEADS, 1, RET_V_DIM))


def _xa_kernel(q_ref, k_ref, v_ref, gq_ref, gk_ref, o_ref):
    s_len = q_ref.shape[1]
    tq = 512
    gq = gq_ref[...] * (XA_HEAD_DIM ** -0.5)
    k = jnp.concatenate([k_ref[0], k_ref[1]], axis=1).astype(F32)
    kn = _rms(k, gk_ref[...]).astype(BF16)
    v = jnp.concatenate([v_ref[0], v_ref[1]], axis=1)

    def body(i, _):
        sl = pl.ds(pl.multiple_of(i * tq, tq), tq)
        q = jnp.concatenate([q_ref[0, sl, :], q_ref[1, sl, :]], axis=1).astype(F32)
        qn = _rms(q, gq).astype(BF16)
        s = lax.dot_general(qn, kn, (((1,), (1,)), ((), ())), preferred_element_type=F32)
        e = jnp.exp(s - jnp.max(s, axis=-1, keepdims=True))
        l = jnp.sum(e, axis=-1, keepdims=True)
        o = jnp.dot(e.astype(BF16), v, preferred_element_type=F32)
        o_ref[sl, :] = (o / l).astype(BF16)
        return 0

    lax.fori_loop(0, s_len // tq, body, 0, unroll=True)


def _mem_xattn(proj, mkv, gq, gk, batch, s_len, n_mem):
    t = batch * s_len
    g_spec = pl.BlockSpec((1, XA_HEAD_DIM), lambda b, h: (0, 0))
    return pl.pallas_call(
        _xa_kernel,
        grid=(batch, XA_HEADS),
        in_specs=[
            pl.BlockSpec((2, s_len, LANE), lambda b, h: (SLAB_XQ // 2 + h, b, 0)),
            pl.BlockSpec((2, n_mem, LANE), lambda b, h: (h, b, 0)),
            pl.BlockSpec((2, n_mem, LANE), lambda b, h: (XA_HEADS + h, b, 0)),
            g_spec, g_spec,
        ],
        out_specs=pl.BlockSpec((s_len, XA_HEAD_DIM), lambda b, h: (b, h)),
        out_shape=jax.ShapeDtypeStruct((t, XA_HEADS * XA_HEAD_DIM), BF16),
        compiler_params=_params(("parallel", "parallel"), 32),
        name="mem_xattn",
    )(proj, mkv, mkv, gq.reshape(1, -1), gk.reshape(1, -1))


def _merge_kernel(ona_ref, oret_ref, omem_ref, wna_ref, wret_ref, wmem_ref,
                  gna_ref, gret_ref, gmem_ref, o_ref):
    y_na = jnp.dot(ona_ref[...], wna_ref[...].astype(BF16), preferred_element_type=F32)
    y_ret = jnp.dot(oret_ref[...], wret_ref[...].astype(BF16), preferred_element_type=F32)
    y_mem = jnp.dot(omem_ref[...], wmem_ref[...].astype(BF16), preferred_element_type=F32)
    for j in range(gna_ref.shape[0]):
        cs = slice(j * LANE, (j + 1) * LANE)
        o_ref[:, cs] = (_sigmoid(gna_ref[j].astype(F32)) * y_na[:, cs]
                        + _sigmoid(gret_ref[j].astype(F32)) * y_ret[:, cs]
                        + _sigmoid(gmem_ref[j].astype(F32)) * y_mem[:, cs]).astype(BF16)


def _merge(o_na, o_ret, o_mem, w_na, w_ret, w_mem, proj, tm, tn):
    t = o_na.shape[0]
    d = w_na.shape[1]
    ns = tn // LANE

    def lhs_spec(a):
        return pl.BlockSpec((tm, a.shape[1]), lambda i, j: (i, 0))

    def w_spec(w):
        return pl.BlockSpec((w.shape[0], tn), lambda i, j: (0, j))

    def gate_spec(first):
        return pl.BlockSpec((ns, tm, LANE), lambda i, j: (first // ns + j, i, 0))

    return pl.pallas_call(
        _merge_kernel,
        grid=(t // tm, d // tn),
        in_specs=[lhs_spec(o_na), lhs_spec(o_ret), lhs_spec(o_mem),
                  w_spec(w_na), w_spec(w_ret), w_spec(w_mem),
                  gate_spec(SLAB_G_NA), gate_spec(SLAB_G_RET), gate_spec(SLAB_G_MEM)],
        out_specs=pl.BlockSpec((tm, tn), lambda i, j: (i, j)),
        out_shape=jax.ShapeDtypeStruct((t, d), BF16),
        compiler_params=_params(("parallel", "arbitrary"), 58),
        name="merge",
    )(o_na, o_ret, o_mem, w_na, w_ret, w_mem, proj, proj, proj)


def _out_proj_kernel(m_ref, w_ref, x_ref, o_ref, wb_ref):
    @pl.when(pl.program_id(0) == 0)
    def _():
        wb_ref[...] = w_ref[...].astype(BF16)

    o_ref[...] = x_ref[...] + jnp.dot(m_ref[...], wb_ref[...], preferred_element_type=F32)


def _out_proj(merged, w, x2d, tm):
    t, k = merged.shape
    d = w.shape[1]
    return pl.pallas_call(
        _out_proj_kernel,
        grid=(t // tm,),
        in_specs=[pl.BlockSpec((tm, k), lambda i: (i, 0)),
                  pl.BlockSpec((k, d), lambda i: (0, 0), pipeline_mode=pl.Buffered(1)),
                  pl.BlockSpec((tm, d), lambda i: (i, 0))],
        out_specs=pl.BlockSpec((tm, d), lambda i: (i, 0)),
        out_shape=jax.ShapeDtypeStruct((t, d), F32),
        scratch_shapes=[pltpu.VMEM((k, d), BF16)],
        compiler_params=_params(("arbitrary",), 56),
        name="out_proj",
    )(merged, w, x2d)


def _ffn_kernel(x_ref, g_ref, w1_ref, w2_ref, o_ref, h_ref):
    @pl.when(pl.program_id(1) == 0)
    def _():
        x = x_ref[...]
        h_ref[...] = _rms(x, g_ref[...]).astype(BF16)
        o_ref[...] = x

    a = jnp.maximum(jnp.dot(h_ref[...], w1_ref[...], preferred_element_type=F32), 0.0)
    o_ref[...] += jnp.dot((a * a).astype(BF16), w2_ref[...], preferred_element_type=F32)


def _ffn(x1, g, w1, w2, tm, tf):
    t, d = x1.shape
    dff = w1.shape[1]
    return pl.pallas_call(
        _ffn_kernel,
        grid=(t // tm, dff // tf),
        in_specs=[pl.BlockSpec((tm, d), lambda i, f: (i, 0)),
                  pl.BlockSpec((1, d), lambda i, f: (0, 0)),
                  pl.BlockSpec((d, tf), lambda i, f: (0, f)),
                  pl.BlockSpec((tf, d), lambda i, f: (f, 0))],
        out_specs=pl.BlockSpec((tm, d), lambda i, f: (i, 0)),
        out_shape=jax.ShapeDtypeStruct((t, d), F32),
        scratch_shapes=[pltpu.VMEM((tm, d), BF16)],
        compiler_params=_params(("parallel", "arbitrary"), 48),
        name="ffn",
    )(x1, g.reshape(1, d), w1, w2)


def _rope_tables(s_len):
    half = RET_QK_DIM // 2
    inv = np.power(np.float64(ROPE_BASE), -np.arange(half, dtype=np.float64) / half)
    ang = np.arange(s_len, dtype=np.float64)[:, None] * inv[None, :]
    cos, sin = np.cos(ang), np.sin(ang)
    return (jnp.asarray(np.concatenate([cos, cos], axis=1), F32),
            jnp.asarray(np.concatenate([-sin, sin], axis=1), F32))


def kernel(x, mem, norm_mix_g, w_in, na_q_norm_g, na_k_norm_g, na_rpb, ret_decay_logit_fwd, ret_decay_logit_bwd, ret_gn_g, mem_norm_g, w_mem_kv, xa_q_norm_g, xa_k_norm_g, w_br_na, w_br_ret, w_br_mem, w_out, norm_ffn_g, w_ff1, w_ff2):
    batch, s_len, d = x.shape
    n_mem = mem.shape[1]
    t = batch * s_len
    depth = w_in.shape[0]
    tm = min(1024, t)
    cos2, sin2 = _rope_tables(s_len)
    x2d = x.reshape(t, d)
    mem2d = mem.reshape(batch * n_mem, d)

    def lane_bcast(v):
        return jnp.broadcast_to(v.astype(F32)[:, None, None], (v.shape[0], 1, LANE))

    tn_in = 1024
    gm, gn = t // tm, w_in.shape[2] // tn_in

    def side_job(w):
        r, c = w.shape
        if r % (gm * gn * 16) == 0:
            return w, (r // (gm * gn), c), lambda i, j: (i * gn + j, 0)
        assert r % (gn * 16) == 0 and c % (gm * LANE) == 0, (w.shape, gm, gn)
        return w, (r // gn, c // gm), lambda i, j: (j, i)

    for l in range(depth):
        proj, (w_ff1_bf, w_ff2_bf) = _norm_proj(x2d, norm_mix_g[l], w_in[l], tm, tn_in,
                                                side=(side_job(w_ff1[l]), side_job(w_ff2[l])))
        mkv, _ = _norm_proj(mem2d, mem_norm_g[l], w_mem_kv[l], batch * n_mem, 1024)

        o_na = _na_attention(proj, _na_tables(na_rpb[l]), na_q_norm_g[l], na_k_norm_g[l], batch, s_len)
        o_ret = _retention(proj, cos2, sin2, lane_bcast(ret_decay_logit_fwd[l]),
                           lane_bcast(ret_decay_logit_bwd[l]), ret_gn_g[l], batch, s_len)
        o_mem = _mem_xattn(proj, mkv, xa_q_norm_g[l], xa_k_norm_g[l], batch, s_len, n_mem)

        merged = _merge(o_na, o_ret, o_mem, w_br_na[l], w_br_ret[l], w_br_mem[l], proj, tm, 512)
        x1 = _out_proj(merged, w_out[l], x2d, min(512, t))
        x2d = _ffn(x1, norm_ffn_g[l], w_ff1_bf, w_ff2_bf, min(512, t), 1024)
    return x2d.reshape(batch, s_len, d)
```

```python
import functools

import jax
import jax.numpy as jnp
import numpy as np
from jax import lax
from jax.experimental import pallas as pl
from jax.experimental.pallas import tpu as pltpu

F32 = jnp.float32
BF16 = jnp.bfloat16

LANE = 128
EPS = 1e-6
NEG = -1e30

GRID_W = 64
NA_HEADS = 8
NA_HEAD_DIM = 128
NA_MAX_ROWS = 8
NA_COLS = 16
NA_DR = 2 * NA_MAX_ROWS - 1
NA_DC = 2 * NA_COLS - 1
NA_TBL = 3 * NA_DR + 1
NA_QROWS = 4
NA_WIN = NA_QROWS + NA_MAX_ROWS
NA_UNROLL = 16

RET_HEADS = 8
RET_QK_DIM = 128
RET_V_DIM = 256
RET_BLOCK = 512
ROPE_BASE = 10000.0

XA_HEADS = 4
XA_HEAD_DIM = 256

SLAB_NA_Q, SLAB_NA_K, SLAB_NA_V = 0, 8, 16
SLAB_RQ, SLAB_RK, SLAB_RV, SLAB_RG = 24, 32, 40, 56
SLAB_XQ = 72
SLAB_G_NA, SLAB_G_RET, SLAB_G_MEM = 80, 96, 112


def _params(sem, vmem_mib):
    return pltpu.CompilerParams(dimension_semantics=sem, vmem_limit_bytes=vmem_mib * 2**20)


def _rms(x, g):
    return x * lax.rsqrt(jnp.mean(x * x, axis=-1, keepdims=True) + EPS) * g


def _sigmoid(x):
    return 0.5 * jnp.tanh(0.5 * x) + 0.5


def _norm_proj_kernel(n_side, x_ref, g_ref, w_ref, *refs):
    side_in, o_ref = refs[:n_side], refs[n_side]
    side_out, hn_ref = refs[n_side + 1:2 * n_side + 1], refs[2 * n_side + 1]

    @pl.when(pl.program_id(1) == 0)
    def _():
        hn_ref[...] = _rms(x_ref[...], g_ref[...]).astype(BF16)

    acc = jnp.dot(hn_ref[...], w_ref[...].astype(BF16), preferred_element_type=F32)
    for j in range(o_ref.shape[0]):
        o_ref[j] = acc[:, j * LANE:(j + 1) * LANE].astype(BF16)
    for src, dst in zip(side_in, side_out):
        dst[...] = src[...].astype(BF16)


def _norm_proj(x2d, g, w, tm, tn, side=()):
    m, k = x2d.shape
    n = w.shape[1]
    side_specs = [pl.BlockSpec(blk, imap) for _, blk, imap in side]
    out = pl.pallas_call(
        functools.partial(_norm_proj_kernel, len(side)),
        grid=(m // tm, n // tn),
        in_specs=[
            pl.BlockSpec((tm, k), lambda i, j: (i, 0)),
            pl.BlockSpec((1, k), lambda i, j: (0, 0)),
            pl.BlockSpec((k, tn), lambda i, j: (0, j)),
        ] + side_specs,
        out_specs=[pl.BlockSpec((tn // LANE, tm, LANE), lambda i, j: (j, i, 0))] + side_specs,
        out_shape=[jax.ShapeDtypeStruct((n // LANE, m, LANE), BF16)]
        + [jax.ShapeDtypeStruct(a.shape, BF16) for a, _, _ in side],
        scratch_shapes=[pltpu.VMEM((tm, k), BF16)],
        compiler_params=_params(("parallel", "arbitrary"), 58),
        name="norm_proj",
    )(x2d, g.reshape(1, k), w, *[a for a, _, _ in side])
    return out[0], out[1:]


def _na_table_kernel(rpb_ref, t_ref, base_ref):
    h = pl.program_id(0)
    qc = lax.broadcasted_iota(jnp.int32, (GRID_W, LANE), 0)
    lane = lax.broadcasted_iota(jnp.int32, (GRID_W, LANE), 1)
    kc = lane & (GRID_W - 1)
    d = jnp.clip(kc - qc, -(NA_COLS - 1), NA_COLS - 1) + (NA_COLS - 1)
    cs = jnp.clip(qc - NA_COLS // 2, 0, GRID_W - NA_COLS)
    col_ok = (kc >= cs) & (kc < cs + NA_COLS)
    left = lane < GRID_W
    neg = jnp.full((GRID_W, LANE), NEG, F32)

    def body(dr, _):
        base = (h * NA_DR + dr) * NA_DC
        val = jnp.zeros((GRID_W, LANE), F32)
        for dd in range(NA_DC):
            val = jnp.where(d == dd, rpb_ref[base + dd], val)
        base_ref[dr] = jnp.where(col_ok, val, NEG)
        return 0

    lax.fori_loop(0, NA_DR, body, 0)
    for dr in range(NA_DR):
        second = base_ref[dr + 1] if dr + 1 < NA_DR else neg
        t_ref[0, dr] = jnp.where(left, base_ref[dr], second)
        t_ref[0, NA_DR + dr] = jnp.where(left, base_ref[dr], neg)
        t_ref[0, 2 * NA_DR + dr] = jnp.where(left, neg, base_ref[dr])
    t_ref[0, 3 * NA_DR] = neg


def _na_tables(rpb):
    return pl.pallas_call(
        _na_table_kernel,
        grid=(NA_HEADS,),
        in_specs=[pl.BlockSpec(memory_space=pltpu.SMEM)],
        out_specs=pl.BlockSpec((1, NA_TBL, GRID_W, LANE), lambda h: (h, 0, 0, 0)),
        out_shape=jax.ShapeDtypeStruct((NA_HEADS, NA_TBL, GRID_W, LANE), F32),
        scratch_shapes=[pltpu.VMEM((NA_DR, GRID_W, LANE), F32)],
        compiler_params=_params(("arbitrary",), 16),
        name="na_tables",
    )(rpb.reshape(-1))


def _na_kernel(q_ref, k_ref, v_ref, t_ref, gq_ref, gk_ref, o_ref, kt_ref, va_ref):
    s_len = q_ref.shape[1]
    rows = s_len // GRID_W
    step_tok = NA_QROWS * GRID_W
    win_tok = NA_WIN * GRID_W
    n_tiles = NA_WIN // 2
    gq = gq_ref[...] * (NA_HEAD_DIM ** -0.5)
    gk = gk_ref[...]
    va_ref[:, :NA_HEAD_DIM] = v_ref[0]
    va_ref[:, NA_HEAD_DIM:] = jnp.ones((s_len, NA_HEAD_DIM), BF16)

    def knorm(c, _):
        sl = pl.ds(pl.multiple_of(c * LANE, LANE), LANE)
        kt_ref[c] = _rms(k_ref[0, sl, :].astype(F32), gk).astype(BF16).T
        return 0

    lax.fori_loop(0, s_len // LANE, knorm, 0, unroll=True)

    def step(i, _):
        r0 = NA_QROWS * i
        ws = jnp.clip(r0 - NA_MAX_ROWS // 2, 0, rows - NA_WIN)
        wp = ws // 2
        qsl = pl.ds(pl.multiple_of(i * step_tok, step_tok), step_tok)
        wsl = pl.ds(pl.multiple_of(ws * GRID_W, LANE), win_tok)
        qn = _rms(q_ref[0, qsl, :].astype(F32), gq).astype(BF16)
        kwin = jnp.concatenate([kt_ref[wp + t] for t in range(n_tiles)], axis=1)
        s = jnp.dot(qn, kwin, preferred_element_type=F32)
        bias_rows = []
        for qr in range(NA_QROWS):
            r = r0 + qr
            rs = jnp.clip(r - NA_MAX_ROWS // 2, 0, rows - NA_MAX_ROWS)
            tiles = []
            for t in range(n_tiles):
                ka = ws + 2 * t
                dr = ka - r + (NA_MAX_ROWS - 1)
                va = (ka >= rs) & (ka < rs + NA_MAX_ROWS)
                vb = (ka + 1 >= rs) & (ka + 1 < rs + NA_MAX_ROWS)
                idx = jnp.where(va, jnp.where(vb, dr, NA_DR + dr),
                                jnp.where(vb, 2 * NA_DR + dr + 1, 3 * NA_DR))
                tiles.append(t_ref[0, idx])
            bias_rows.append(jnp.concatenate(tiles, axis=1))
        s = s + jnp.concatenate(bias_rows, axis=0)
        e = jnp.exp(s - jnp.max(s, axis=-1, keepdims=True))
        o = jnp.dot(e.astype(BF16), va_ref[wsl, :], preferred_element_type=F32)
        o_ref[qsl, :] = (o[:, :NA_HEAD_DIM] / o[:, NA_HEAD_DIM:]).astype(BF16)
        return 0

    lax.fori_loop(0, rows // NA_QROWS, step, 0, unroll=NA_UNROLL)


def _na_attention(proj, tbl, gq, gk, batch, s_len):
    t = batch * s_len
    g_spec = pl.BlockSpec((1, NA_HEAD_DIM), lambda b, h: (0, 0))
    return pl.pallas_call(
        _na_kernel,
        grid=(batch, NA_HEADS),
        in_specs=[
            pl.BlockSpec((1, s_len, LANE), lambda b, h: (SLAB_NA_Q + h, b, 0)),
            pl.BlockSpec((1, s_len, LANE), lambda b, h: (SLAB_NA_K + h, b, 0)),
            pl.BlockSpec((1, s_len, LANE), lambda b, h: (SLAB_NA_V + h, b, 0)),
            pl.BlockSpec((1, NA_TBL, GRID_W, LANE), lambda b, h: (h, 0, 0, 0)),
            g_spec, g_spec,
        ],
        out_specs=pl.BlockSpec((s_len, NA_HEAD_DIM), lambda b, h: (b, h)),
        out_shape=jax.ShapeDtypeStruct((t, NA_HEADS * NA_HEAD_DIM), BF16),
        scratch_shapes=[pltpu.VMEM((s_len // LANE, NA_HEAD_DIM, LANE), BF16),
                        pltpu.VMEM((s_len, 2 * NA_HEAD_DIM), BF16)],
        compiler_params=_params(("parallel", "parallel"), 32),
        name="na_attention",
    )(proj, proj, proj, tbl, gq.reshape(1, -1), gk.reshape(1, -1))


def _log_sigmoid(x):
    return -(jnp.maximum(-x, 0.0) + jnp.log1p(jnp.exp(-jnp.abs(x))))


def _ret_kernel(q_ref, k_ref, v_ref, g_ref, cos_ref, sin_ref, lf_ref, lb_ref, gn_ref,
                o_ref, qr_ref, kt_ref, d_ref, qdf_ref, qdb_ref, kv_ref, s_ref):
    s_len = q_ref.shape[1]
    c = RET_BLOCK
    nb = s_len // c
    dk = RET_QK_DIM
    half = dk // 2

    lgf = _log_sigmoid(lf_ref[0][:, :1])
    lgb = _log_sigmoid(lb_ref[0][:, :1])
    ic = lax.broadcasted_iota(jnp.int32, (c, 1), 0).astype(F32)
    jr = lax.broadcasted_iota(jnp.int32, (1, c), 1).astype(F32)
    diff = ic - jr
    scale = dk ** -0.5
    d_ref[...] = jnp.exp(jnp.where(diff >= 0, lgf, lgb) * jnp.abs(diff)) * scale
    qdf_ref[...] = jnp.broadcast_to(jnp.exp(lgf * (ic + 1.0)), (c, dk))
    qdb_ref[...] = jnp.broadcast_to(jnp.exp(lgb * (c - ic)), (c, dk))
    kdf = jnp.exp(lgf * (c - 1.0 - jr)) * scale
    kdb = jnp.exp(lgb * jr) * scale
    cd_f = jnp.exp(lgf * c)
    cd_b = jnp.exp(lgb * c)

    def block_v(sl):
        return jnp.concatenate([v_ref[0, sl, :], v_ref[1, sl, :]], axis=1)

    def prep(n, _):
        sl = pl.ds(pl.multiple_of(n * c, c), c)
        cos = cos_ref[sl, :]
        sin = sin_ref[sl, :]
        q = q_ref[0, sl, :].astype(F32)
        k = k_ref[0, sl, :].astype(F32)
        qr_ref[sl, :] = (q * cos + pltpu.roll(q, half, 1) * sin).astype(BF16)
        kt = (k * cos + pltpu.roll(k, half, 1) * sin).T
        kt_ref[n] = kt.astype(BF16)
        lhs = jnp.concatenate([(kt * kdf).astype(BF16), (kt * kdb).astype(BF16)], axis=0)
        kv_ref[n] = jnp.dot(lhs, block_v(sl), preferred_element_type=F32)
        return 0

    lax.fori_loop(0, nb, prep, 0, unroll=True)

    def scan_f(n, sf):
        s_ref[n, :dk, :] = sf.astype(BF16)
        return cd_f * sf + kv_ref[n, :dk, :]

    def scan_b(t, sb):
        n = nb - 1 - t
        s_ref[n, dk:, :] = sb.astype(BF16)
        return cd_b * sb + kv_ref[n, dk:, :]

    zero = jnp.zeros((dk, RET_V_DIM), F32)

    def scan(n, carry):
        return scan_f(n, carry[0]), scan_b(n, carry[1])

    lax.fori_loop(0, nb, scan, (zero, zero), unroll=True)

    gn = gn_ref[0]

    def out(n, _):
        sl = pl.ds(pl.multiple_of(n * c, c), c)
        q = qr_ref[sl, :]
        qf32 = q.astype(F32)
        a = jnp.dot(q, kt_ref[n], preferred_element_type=F32) * d_ref[...]
        lhs = jnp.concatenate([a.astype(BF16), (qf32 * qdf_ref[...]).astype(BF16),
                               (qf32 * qdb_ref[...]).astype(BF16)], axis=1)
        rhs = jnp.concatenate([block_v(sl), s_ref[n]], axis=0)
        o = jnp.dot(lhs, rhs, preferred_element_type=F32)
        mu = jnp.mean(o, axis=-1, keepdims=True)
        oc = o - mu
        y = oc * lax.rsqrt(jnp.mean(oc * oc, axis=-1, keepdims=True) + EPS) * gn
        gate = jnp.concatenate([g_ref[0, sl, :], g_ref[1, sl, :]], axis=1).astype(F32)
        o_ref[sl, :] = (y * gate * _sigmoid(gate)).astype(BF16)
        return 0

    lax.fori_loop(0, nb, out, 0, unroll=True)


def _retention(proj, cos2, sin2, lf, lb, gn, batch, s_len):
    t = batch * s_len
    nb = s_len // RET_BLOCK
    dec_spec = pl.BlockSpec((1, 1, LANE), lambda b, h: (h, 0, 0))
    return pl.pallas_call(
        _ret_kernel,
        grid=(batch, RET_HEADS),
        in_specs=[
            pl.BlockSpec((1, s_len, LANE), lambda b, h: (SLAB_RQ + h, b, 0)),
            pl.BlockSpec((1, s_len, LANE), lambda b, h: (SLAB_RK + h, b, 0)),
            pl.BlockSpec((2, s_len, LANE), lambda b, h: (SLAB_RV // 2 + h, b, 0)),
            pl.BlockSpec((2, s_len, LANE), lambda b, h: (SLAB_RG // 2 + h, b, 0)),
            pl.BlockSpec((s_len, RET_QK_DIM), lambda b, h: (0, 0)),
            pl.BlockSpec((s_len, RET_QK_DIM), lambda b, h: (0, 0)),
            dec_spec, dec_spec,
            pl.BlockSpec((1, 1, RET_V_DIM), lambda b, h: (h, 0, 0)),
        ],
        out_specs=pl.BlockSpec((s_len, RET_V_DIM), lambda b, h: (b, h)),
        out_shape=jax.ShapeDtypeStruct((t, RET_HEADS * RET_V_DIM), BF16),
        scratch_shapes=[
            pltpu.VMEM((s_len, RET_QK_DIM), BF16),
            pltpu.VMEM((nb, RET_QK_DIM, RET_BLOCK), BF16),
            pltpu.VMEM((RET_BLOCK, RET_BLOCK), F32),
            pltpu.VMEM((RET_BLOCK, RET_QK_DIM), F32),
            pltpu.VMEM((RET_BLOCK, RET_QK_DIM), F32),
            pltpu.VMEM((nb, 2 * RET_QK_DIM, RET_V_DIM), F32),
            pltpu.VMEM((nb, 2 * RET_QK_DIM, RET_V_DIM), BF16),
        ],
        compiler_params=_params(("parallel", "parallel"), 48),
        name="retention",
    )(proj, proj, proj, proj, cos2, sin2, lf, lb, gn.reshape(RET_HEADS, 1, RET_V_DIM))


def _xa_kernel(q_ref, k_ref, v_ref, gq_ref, gk_ref, o_ref):
    s_len = q_ref.shape[1]
    tq = 512
    gq = gq_ref[...] * (XA_HEAD_DIM ** -0.5)
    k = jnp.concatenate([k_ref[0], k_ref[1]], axis=1).astype(F32)
    kn = _rms(k, gk_ref[...]).astype(BF16)
    v = jnp.concatenate([v_ref[0], v_ref[1]], axis=1)

    def body(i, _):
        sl = pl.ds(pl.multiple_of(i * tq, tq), tq)
        q = jnp.concatenate([q_ref[0, sl, :], q_ref[1, sl, :]], axis=1).astype(F32)
        qn = _rms(q, gq).astype(BF16)
        s = lax.dot_general(qn, kn, (((1,), (1,)), ((), ())), preferred_element_type=F32)
        e = jnp.exp(s - jnp.max(s, axis=-1, keepdims=True))
        l = jnp.sum(e, axis=-1, keepdims=True)
        o = jnp.dot(e.astype(BF16), v, preferred_element_type=F32)
        o_ref[sl, :] = (o / l).astype(BF16)
        return 0

    lax.fori_loop(0, s_len // tq, body, 0, unroll=True)


def _mem_xattn(proj, mkv, gq, gk, batch, s_len, n_mem):
    t = batch * s_len
    g_spec = pl.BlockSpec((1, XA_HEAD_DIM), lambda b, h: (0, 0))
    return pl.pallas_call(
        _xa_kernel,
        grid=(batch, XA_HEADS),
        in_specs=[
            pl.BlockSpec((2, s_len, LANE), lambda b, h: (SLAB_XQ // 2 + h, b, 0)),
            pl.BlockSpec((2, n_mem, LANE), lambda b, h: (h, b, 0)),
            pl.BlockSpec((2, n_mem, LANE), lambda b, h: (XA_HEADS + h, b, 0)),
            g_spec, g_spec,
        ],
        out_specs=pl.BlockSpec((s_len, XA_HEAD_DIM), lambda b, h: (b, h)),
        out_shape=jax.ShapeDtypeStruct((t, XA_HEADS * XA_HEAD_DIM), BF16),
        compiler_params=_params(("parallel", "parallel"), 32),
        name="mem_xattn",
    )(proj, mkv, mkv, gq.reshape(1, -1), gk.reshape(1, -1))


def _merge_kernel(ona_ref, oret_ref, omem_ref, wna_ref, wret_ref, wmem_ref,
                  gna_ref, gret_ref, gmem_ref, o_ref):
    y_na = jnp.dot(ona_ref[...], wna_ref[...].astype(BF16), preferred_element_type=F32)
    y_ret = jnp.dot(oret_ref[...], wret_ref[...].astype(BF16), preferred_element_type=F32)
    y_mem = jnp.dot(omem_ref[...], wmem_ref[...].astype(BF16), preferred_element_type=F32)
    for j in range(gna_ref.shape[0]):
        cs = slice(j * LANE, (j + 1) * LANE)
        o_ref[:, cs] = (_sigmoid(gna_ref[j].astype(F32)) * y_na[:, cs]
                        + _sigmoid(gret_ref[j].astype(F32)) * y_ret[:, cs]
                        + _sigmoid(gmem_ref[j].astype(F32)) * y_mem[:, cs]).astype(BF16)


def _merge(o_na, o_ret, o_mem, w_na, w_ret, w_mem, proj, tm, tn):
    t = o_na.shape[0]
    d = w_na.shape[1]
    ns = tn // LANE

    def lhs_spec(a):
        return pl.BlockSpec((tm, a.shape[1]), lambda i, j: (i, 0))

    def w_spec(w):
        return pl.BlockSpec((w.shape[0], tn), lambda i, j: (0, j))

    def gate_spec(first):
        return pl.BlockSpec((ns, tm, LANE), lambda i, j: (first // ns + j, i, 0))

    return pl.pallas_call(
        _merge_kernel,
        grid=(t // tm, d // tn),
        in_specs=[lhs_spec(o_na), lhs_spec(o_ret), lhs_spec(o_mem),
                  w_spec(w_na), w_spec(w_ret), w_spec(w_mem),
                  gate_spec(SLAB_G_NA), gate_spec(SLAB_G_RET), gate_spec(SLAB_G_MEM)],
        out_specs=pl.BlockSpec((tm, tn), lambda i, j: (i, j)),
        out_shape=jax.ShapeDtypeStruct((t, d), BF16),
        compiler_params=_params(("parallel", "arbitrary"), 58),
        name="merge",
    )(o_na, o_ret, o_mem, w_na, w_ret, w_mem, proj, proj, proj)


def _out_proj_kernel(m_ref, w_ref, x_ref, o_ref, wb_ref):
    @pl.when(pl.program_id(0) == 0)
    def _():
        wb_ref[...] = w_ref[...].astype(BF16)

    o_ref[...] = x_ref[...] + jnp.dot(m_ref[...], wb_ref[...], preferred_element_type=F32)


def _out_proj(merged, w, x2d, tm):
    t, k = merged.shape
    d = w.shape[1]
    return pl.pallas_call(
        _out_proj_kernel,
        grid=(t // tm,),
        in_specs=[pl.BlockSpec((tm, k), lambda i: (i, 0)),
                  pl.BlockSpec((k, d), lambda i: (0, 0), pipeline_mode=pl.Buffered(1)),
                  pl.BlockSpec((tm, d), lambda i: (i, 0))],
        out_specs=pl.BlockSpec((tm, d), lambda i: (i, 0)),
        out_shape=jax.ShapeDtypeStruct((t, d), F32),
        scratch_shapes=[pltpu.VMEM((k, d), BF16)],
        compiler_params=_params(("arbitrary",), 56),
        name="out_proj",
    )(merged, w, x2d)


def _ffn_kernel(x_ref, g_ref, w1_ref, w2_ref, o_ref, h_ref):
    @pl.when(pl.program_id(1) == 0)
    def _():
        x = x_ref[...]
        h_ref[...] = _rms(x, g_ref[...]).astype(BF16)
        o_ref[...] = x

    a = jnp.maximum(jnp.dot(h_ref[...], w1_ref[...], preferred_element_type=F32), 0.0)
    o_ref[...] += jnp.dot((a * a).astype(BF16), w2_ref[...], preferred_element_type=F32)


def _ffn(x1, g, w1, w2, tm, tf):
    t, d = x1.shape
    dff = w1.shape[1]
    return pl.pallas_call(
        _ffn_kernel,
        grid=(t // tm, dff // tf),
        in_specs=[pl.BlockSpec((tm, d), lambda i, f: (i, 0)),
                  pl.BlockSpec((1, d), lambda i, f: (0, 0)),
                  pl.BlockSpec((d, tf), lambda i, f: (0, f)),
                  pl.BlockSpec((tf, d), lambda i, f: (f, 0))],
        out_specs=pl.BlockSpec((tm, d), lambda i, f: (i, 0)),
        out_shape=jax.ShapeDtypeStruct((t, d), F32),
        scratch_shapes=[pltpu.VMEM((tm, d), BF16)],
        compiler_params=_params(("parallel", "arbitrary"), 48),
        name="ffn",
    )(x1, g.reshape(1, d), w1, w2)


def _rope_tables(s_len):
    half = RET_QK_DIM // 2
    inv = np.power(np.float64(ROPE_BASE), -np.arange(half, dtype=np.float64) / half)
    ang = np.arange(s_len, dtype=np.float64)[:, None] * inv[None, :]
    cos, sin = np.cos(ang), np.sin(ang)
    return (jnp.asarray(np.concatenate([cos, cos], axis=1), F32),
            jnp.asarray(np.concatenate([-sin, sin], axis=1), F32))


def kernel(x, mem, norm_mix_g, w_in, na_q_norm_g, na_k_norm_g, na_rpb, ret_decay_logit_fwd, ret_decay_logit_bwd, ret_gn_g, mem_norm_g, w_mem_kv, xa_q_norm_g, xa_k_norm_g, w_br_na, w_br_ret, w_br_mem, w_out, norm_ffn_g, w_ff1, w_ff2):
    batch, s_len, d = x.shape
    n_mem = mem.shape[1]
    t = batch * s_len
    depth = w_in.shape[0]
    tm = min(1024, t)
    cos2, sin2 = _rope_tables(s_len)
    x2d = x.reshape(t, d)
    mem2d = mem.reshape(batch * n_mem, d)

    def lane_bcast(v):
        return jnp.broadcast_to(v.astype(F32)[:, None, None], (v.shape[0], 1, LANE))

    tn_in = 1024
    gm, gn = t // tm, w_in.shape[2] // tn_in

    def side_job(w):
        r, c = w.shape
        if r % (gm * gn * 16) == 0:
            return w, (r // (gm * gn), c), lambda i, j: (i * gn + j, 0)
        assert r % (gn * 16) == 0 and c % (gm * LANE) == 0, (w.shape, gm, gn)
        return w, (r // gn, c // gm), lambda i, j: (j, i)

    for l in range(depth):
        proj, (w_ff1_bf, w_ff2_bf) = _norm_proj(x2d, norm_mix_g[l], w_in[l], tm, tn_in,
                                                side=(side_job(w_ff1[l]), side_job(w_ff2[l])))
        mkv, _ = _norm_proj(mem2d, mem_norm_g[l], w_mem_kv[l], batch * n_mem, 1024)

        o_na = _na_attention(proj, _na_tables(na_rpb[l]), na_q_norm_g[l], na_k_norm_g[l], batch, s_len)
        o_ret = _retention(proj, cos2, sin2, lane_bcast(ret_decay_logit_fwd[l]),
                           lane_bcast(ret_decay_logit_bwd[l]), ret_gn_g[l], batch, s_len)
        o_mem = _mem_xattn(proj, mkv, xa_q_norm_g[l], xa_k_norm_g[l], batch, s_len, n_mem)

        merged = _merge(o_na, o_ret, o_mem, w_br_na[l], w_br_ret[l], w_br_mem[l], proj, tm, 512)
        x1 = _out_proj(merged, w_out[l], x2d, min(512, t))
        x2d = _ffn(x1, norm_ffn_g[l], w_ff1_bf, w_ff2_bf, min(512, t), 1024)
    return x2d.reshape(batch, s_len, d)
```

```python
import functools

import jax
import jax.numpy as jnp
import numpy as np
from jax import lax
from jax.experimental import pallas as pl
from jax.experimental.pallas import tpu as pltpu

F32 = jnp.float32
BF16 = jnp.bfloat16

LANE = 128
EPS = 1e-6
NEG = -1e30

GRID_W = 64
NA_HEADS = 8
NA_HEAD_DIM = 128
NA_MAX_ROWS = 8
NA_COLS = 16
NA_DR = 2 * NA_MAX_ROWS - 1
NA_DC = 2 * NA_COLS - 1
NA_TBL = 3 * NA_DR + 1
NA_QROWS = 4
NA_WIN = NA_QROWS + NA_MAX_ROWS
NA_UNROLL = 16

RET_HEADS = 8
RET_QK_DIM = 128
RET_V_DIM = 256
RET_BLOCK = 512
ROPE_BASE = 10000.0

XA_HEADS = 4
XA_HEAD_DIM = 256

SLAB_NA_Q, SLAB_NA_K, SLAB_NA_V = 0, 8, 16
SLAB_RQ, SLAB_RK, SLAB_RV, SLAB_RG = 24, 32, 40, 56
SLAB_XQ = 72
SLAB_G_NA, SLAB_G_RET, SLAB_G_MEM = 80, 96, 112


def _params(sem, vmem_mib):
    return pltpu.CompilerParams(dimension_semantics=sem, vmem_limit_bytes=vmem_mib * 2**20)


def _rms(x, g):
    return x * lax.rsqrt(jnp.mean(x * x, axis=-1, keepdims=True) + EPS) * g


def _sigmoid(x):
    return 0.5 * jnp.tanh(0.5 * x) + 0.5


def _norm_proj_kernel(n_side, x_ref, g_ref, w_ref, *refs):
    side_in, o_ref = refs[:n_side], refs[n_side]
    side_out, hn_ref = refs[n_side + 1:2 * n_side + 1], refs[2 * n_side + 1]

    @pl.when(pl.program_id(1) == 0)
    def _():
        hn_ref[...] = _rms(x_ref[...], g_ref[...]).astype(BF16)

    acc = jnp.dot(hn_ref[...], w_ref[...].astype(BF16), preferred_element_type=F32)
    for j in range(o_ref.shape[0]):
        o_ref[j] = acc[:, j * LANE:(j + 1) * LANE].astype(BF16)
    for src, dst in zip(side_in, side_out):
        dst[...] = src[...].astype(BF16)


def _norm_proj(x2d, g, w, tm, tn, side=()):
    m, k = x2d.shape
    n = w.shape[1]
    side_specs = [pl.BlockSpec(blk, imap) for _, blk, imap in side]
    out = pl.pallas_call(
        functools.partial(_norm_proj_kernel, len(side)),
        grid=(m // tm, n // tn),
        in_specs=[
            pl.BlockSpec((tm, k), lambda i, j: (i, 0)),
            pl.BlockSpec((1, k), lambda i, j: (0, 0)),
            pl.BlockSpec((k, tn), lambda i, j: (0, j)),
        ] + side_specs,
        out_specs=[pl.BlockSpec((tn // LANE, tm, LANE), lambda i, j: (j, i, 0))] + side_specs,
        out_shape=[jax.ShapeDtypeStruct((n // LANE, m, LANE), BF16)]
        + [jax.ShapeDtypeStruct(a.shape, BF16) for a, _, _ in side],
        scratch_shapes=[pltpu.VMEM((tm, k), BF16)],
        compiler_params=_params(("parallel", "arbitrary"), 58),
        name="norm_proj",
    )(x2d, g.reshape(1, k), w, *[a for a, _, _ in side])
    return out[0], out[1:]


def _na_build_tables(h, rpb_ref, t_ref, base_ref):
    qc = lax.broadcasted_iota(jnp.int32, (GRID_W, LANE), 0)
    lane = lax.broadcasted_iota(jnp.int32, (GRID_W, LANE), 1)
    kc = lane & (GRID_W - 1)
    d = jnp.clip(kc - qc, -(NA_COLS - 1), NA_COLS - 1) + (NA_COLS - 1)
    cs = jnp.clip(qc - NA_COLS // 2, 0, GRID_W - NA_COLS)
    col_ok = (kc >= cs) & (kc < cs + NA_COLS)
    left = lane < GRID_W
    neg = jnp.full((GRID_W, LANE), NEG, F32)

    def body(dr, _):
        base = (h * NA_DR + dr) * NA_DC
        val = jnp.zeros((GRID_W, LANE), F32)
        for dd in range(NA_DC):
            val = jnp.where(d == dd, rpb_ref[base + dd], val)
        base_ref[dr] = jnp.where(col_ok, val, NEG)
        return 0

    lax.fori_loop(0, NA_DR, body, 0)
    for dr in range(NA_DR):
        second = base_ref[dr + 1] if dr + 1 < NA_DR else neg
        t_ref[dr] = jnp.where(left, base_ref[dr], second)
        t_ref[NA_DR + dr] = jnp.where(left, base_ref[dr], neg)
        t_ref[2 * NA_DR + dr] = jnp.where(left, neg, base_ref[dr])
    t_ref[3 * NA_DR] = neg


def _na_kernel(rpb_ref, q_ref, k_ref, v_ref, gq_ref, gk_ref, o_ref, kt_ref, va_ref, t_ref, base_ref):
    @pl.when(pl.program_id(1) == 0)
    def _():
        _na_build_tables(pl.program_id(0), rpb_ref, t_ref, base_ref)

    s_len = q_ref.shape[1]
    rows = s_len // GRID_W
    step_tok = NA_QROWS * GRID_W
    win_tok = NA_WIN * GRID_W
    n_tiles = NA_WIN // 2
    gq = gq_ref[...] * (NA_HEAD_DIM ** -0.5)
    gk = gk_ref[...]
    va_ref[:, :NA_HEAD_DIM] = v_ref[0]
    va_ref[:, NA_HEAD_DIM:] = jnp.ones((s_len, NA_HEAD_DIM), BF16)

    def knorm(c, _):
        sl = pl.ds(pl.multiple_of(c * LANE, LANE), LANE)
        kt_ref[c] = _rms(k_ref[0, sl, :].astype(F32), gk).astype(BF16).T
        return 0

    lax.fori_loop(0, s_len // LANE, knorm, 0, unroll=True)

    def step(i, _):
        r0 = NA_QROWS * i
        ws = jnp.clip(r0 - NA_MAX_ROWS // 2, 0, rows - NA_WIN)
        wp = ws // 2
        qsl = pl.ds(pl.multiple_of(i * step_tok, step_tok), step_tok)
        wsl = pl.ds(pl.multiple_of(ws * GRID_W, LANE), win_tok)
        qn = _rms(q_ref[0, qsl, :].astype(F32), gq).astype(BF16)
        kwin = jnp.concatenate([kt_ref[wp + t] for t in range(n_tiles)], axis=1)
        s = jnp.dot(qn, kwin, preferred_element_type=F32)
        bias_rows = []
        for qr in range(NA_QROWS):
            r = r0 + qr
            rs = jnp.clip(r - NA_MAX_ROWS // 2, 0, rows - NA_MAX_ROWS)
            tiles = []
            for t in range(n_tiles):
                ka = ws + 2 * t
                dr = ka - r + (NA_MAX_ROWS - 1)
                va = (ka >= rs) & (ka < rs + NA_MAX_ROWS)
                vb = (ka + 1 >= rs) & (ka + 1 < rs + NA_MAX_ROWS)
                idx = jnp.where(va, jnp.where(vb, dr, NA_DR + dr),
                                jnp.where(vb, 2 * NA_DR + dr + 1, 3 * NA_DR))
                tiles.append(t_ref[idx])
            bias_rows.append(jnp.concatenate(tiles, axis=1))
        s = s + jnp.concatenate(bias_rows, axis=0)
        e = jnp.exp(s - jnp.max(s, axis=-1, keepdims=True))
        o = jnp.dot(e.astype(BF16), va_ref[wsl, :], preferred_element_type=F32)
        o_ref[qsl, :] = (o[:, :NA_HEAD_DIM] / o[:, NA_HEAD_DIM:]).astype(BF16)
        return 0

    lax.fori_loop(0, rows // NA_QROWS, step, 0, unroll=NA_UNROLL)


def _na_attention(proj, rpb, gq, gk, batch, s_len):
    t = batch * s_len
    g_spec = pl.BlockSpec((1, NA_HEAD_DIM), lambda h, b: (0, 0))
    return pl.pallas_call(
        _na_kernel,
        grid=(NA_HEADS, batch),
        in_specs=[
            pl.BlockSpec(memory_space=pltpu.SMEM),
            pl.BlockSpec((1, s_len, LANE), lambda h, b: (SLAB_NA_Q + h, b, 0)),
            pl.BlockSpec((1, s_len, LANE), lambda h, b: (SLAB_NA_K + h, b, 0)),
            pl.BlockSpec((1, s_len, LANE), lambda h, b: (SLAB_NA_V + h, b, 0)),
            g_spec, g_spec,
        ],
        out_specs=pl.BlockSpec((s_len, NA_HEAD_DIM), lambda h, b: (b, h)),
        out_shape=jax.ShapeDtypeStruct((t, NA_HEADS * NA_HEAD_DIM), BF16),
        scratch_shapes=[pltpu.VMEM((s_len // LANE, NA_HEAD_DIM, LANE), BF16),
                        pltpu.VMEM((s_len, 2 * NA_HEAD_DIM), BF16),
                        pltpu.VMEM((NA_TBL, GRID_W, LANE), F32),
                        pltpu.VMEM((NA_DR, GRID_W, LANE), F32)],
        compiler_params=_params(("arbitrary", "arbitrary"), 32),
        name="na_attention",
    )(rpb.reshape(-1), proj, proj, proj, gq.reshape(1, -1), gk.reshape(1, -1))


def _log_sigmoid(x):
    return -(jnp.maximum(-x, 0.0) + jnp.log1p(jnp.exp(-jnp.abs(x))))


def _ret_kernel(q_ref, k_ref, v_ref, g_ref, cos_ref, sin_ref, lf_ref, lb_ref, gn_ref,
                o_ref, qr_ref, kt_ref, d_ref, qdf_ref, qdb_ref, kv_ref, s_ref):
    s_len = q_ref.shape[1]
    c = RET_BLOCK
    nb = s_len // c
    dk = RET_QK_DIM
    half = dk // 2

    lgf = _log_sigmoid(lf_ref[0][:, :1])
    lgb = _log_sigmoid(lb_ref[0][:, :1])
    ic = lax.broadcasted_iota(jnp.int32, (c, 1), 0).astype(F32)
    jr = lax.broadcasted_iota(jnp.int32, (1, c), 1).astype(F32)
    diff = ic - jr
    scale = dk ** -0.5
    d_ref[...] = jnp.exp(jnp.where(diff >= 0, lgf, lgb) * jnp.abs(diff)) * scale
    qdf_ref[...] = jnp.broadcast_to(jnp.exp(lgf * (ic + 1.0)), (c, dk))
    qdb_ref[...] = jnp.broadcast_to(jnp.exp(lgb * (c - ic)), (c, dk))
    kdf = jnp.exp(lgf * (c - 1.0 - jr)) * scale
    kdb = jnp.exp(lgb * jr) * scale
    cd_f = jnp.exp(lgf * c)
    cd_b = jnp.exp(lgb * c)

    def block_v(sl):
        return jnp.concatenate([v_ref[0, sl, :], v_ref[1, sl, :]], axis=1)

    def prep(n, _):
        sl = pl.ds(pl.multiple_of(n * c, c), c)
        cos = cos_ref[sl, :]
        sin = sin_ref[sl, :]
        q = q_ref[0, sl, :].astype(F32)
        k = k_ref[0, sl, :].astype(F32)
        qr_ref[sl, :] = (q * cos + pltpu.roll(q, half, 1) * sin).astype(BF16)
        kt = (k * cos + pltpu.roll(k, half, 1) * sin).T
        kt_ref[n] = kt.astype(BF16)
        lhs = jnp.concatenate([(kt * kdf).astype(BF16), (kt * kdb).astype(BF16)], axis=0)
        kv_ref[n] = jnp.dot(lhs, block_v(sl), preferred_element_type=F32)
        return 0

    lax.fori_loop(0, nb, prep, 0, unroll=True)

    def scan_f(n, sf):
        s_ref[n, :dk, :] = sf.astype(BF16)
        return cd_f * sf + kv_ref[n, :dk, :]

    def scan_b(t, sb):
        n = nb - 1 - t
        s_ref[n, dk:, :] = sb.astype(BF16)
        return cd_b * sb + kv_ref[n, dk:, :]

    zero = jnp.zeros((dk, RET_V_DIM), F32)

    def scan(n, carry):
        return scan_f(n, carry[0]), scan_b(n, carry[1])

    lax.fori_loop(0, nb, scan, (zero, zero), unroll=True)

    gn = gn_ref[0]

    def out(n, _):
        sl = pl.ds(pl.multiple_of(n * c, c), c)
        q = qr_ref[sl, :]
        qf32 = q.astype(F32)
        a = jnp.dot(q, kt_ref[n], preferred_element_type=F32) * d_ref[...]
        lhs = jnp.concatenate([a.astype(BF16), (qf32 * qdf_ref[...]).astype(BF16),
                               (qf32 * qdb_ref[...]).astype(BF16)], axis=1)
        rhs = jnp.concatenate([block_v(sl), s_ref[n]], axis=0)
        o = jnp.dot(lhs, rhs, preferred_element_type=F32)
        mu = jnp.mean(o, axis=-1, keepdims=True)
        oc = o - mu
        y = oc * lax.rsqrt(jnp.mean(oc * oc, axis=-1, keepdims=True) + EPS) * gn
        gate = jnp.concatenate([g_ref[0, sl, :], g_ref[1, sl, :]], axis=1).astype(F32)
        o_ref[sl, :] = (y * gate * _sigmoid(gate)).astype(BF16)
        return 0

    lax.fori_loop(0, nb, out, 0, unroll=True)


def _retention(proj, cos2, sin2, lf, lb, gn, batch, s_len):
    t = batch * s_len
    nb = s_len // RET_BLOCK
    dec_spec = pl.BlockSpec((1, 1, LANE), lambda b, h: (h, 0, 0))
    return pl.pallas_call(
        _ret_kernel,
        grid=(batch, RET_HEADS),
        in_specs=[
            pl.BlockSpec((1, s_len, LANE), lambda b, h: (SLAB_RQ + h, b, 0)),
            pl.BlockSpec((1, s_len, LANE), lambda b, h: (SLAB_RK + h, b, 0)),
            pl.BlockSpec((2, s_len, LANE), lambda b, h: (SLAB_RV // 2 + h, b, 0)),
            pl.BlockSpec((2, s_len, LANE), lambda b, h: (SLAB_RG // 2 + h, b, 0)),
            pl.BlockSpec((s_len, RET_QK_DIM), lambda b, h: (0, 0)),
            pl.BlockSpec((s_len, RET_QK_DIM), lambda b, h: (0, 0)),
            dec_spec, dec_spec,
            pl.BlockSpec((1, 1, RET_V_DIM), lambda b, h: (h, 0, 0)),
        ],
        out_specs=pl.BlockSpec((s_len, RET_V_DIM), lambda b, h: (b, h)),
        out_shape=jax.ShapeDtypeStruct((t, RET_HEADS * RET_V_DIM), BF16),
        scratch_shapes=[
            pltpu.VMEM((s_len, RET_QK_DIM), BF16),
            pltpu.VMEM((nb, RET_QK_DIM, RET_BLOCK), BF16),
            pltpu.VMEM((RET_BLOCK, RET_BLOCK), F32),
            pltpu.VMEM((RET_BLOCK, RET_QK_DIM), F32),
            pltpu.VMEM((RET_BLOCK, RET_QK_DIM), F32),
            pltpu.VMEM((nb, 2 * RET_QK_DIM, RET_V_DIM), F32),
            pltpu.VMEM((nb, 2 * RET_QK_DIM, RET_V_DIM), BF16),
        ],
        compiler_params=_params(("parallel", "parallel"), 48),
        name="retention",
    )(proj, proj, proj, proj, cos2, sin2, lf, lb, gn.reshape(RET_HEADS, 1, RET_V_DIM))


def _xa_kernel(q_ref, k_ref, v_ref, gq_ref, gk_ref, o_ref):
    s_len = q_ref.shape[1]
    tq = 512
    gq = gq_ref[...] * (XA_HEAD_DIM ** -0.5)
    k = jnp.concatenate([k_ref[0], k_ref[1]], axis=1).astype(F32)
    kn = _rms(k, gk_ref[...]).astype(BF16)
    v = jnp.concatenate([v_ref[0], v_ref[1]], axis=1)

    def body(i, _):
        sl = pl.ds(pl.multiple_of(i * tq, tq), tq)
        q = jnp.concatenate([q_ref[0, sl, :], q_ref[1, sl, :]], axis=1).astype(F32)
        qn = _rms(q, gq).astype(BF16)
        s = lax.dot_general(qn, kn, (((1,), (1,)), ((), ())), preferred_element_type=F32)
        e = jnp.exp(s - jnp.max(s, axis=-1, keepdims=True))
        l = jnp.sum(e, axis=-1, keepdims=True)
        o = jnp.dot(e.astype(BF16), v, preferred_element_type=F32)
        o_ref[sl, :] = (o / l).astype(BF16)
        return 0

    lax.fori_loop(0, s_len // tq, body, 0, unroll=True)


def _mem_xattn(proj, mkv, gq, gk, batch, s_len, n_mem):
    t = batch * s_len
    g_spec = pl.BlockSpec((1, XA_HEAD_DIM), lambda b, h: (0, 0))
    return pl.pallas_call(
        _xa_kernel,
        grid=(batch, XA_HEADS),
        in_specs=[
            pl.BlockSpec((2, s_len, LANE), lambda b, h: (SLAB_XQ // 2 + h, b, 0)),
            pl.BlockSpec((2, n_mem, LANE), lambda b, h: (h, b, 0)),
            pl.BlockSpec((2, n_mem, LANE), lambda b, h: (XA_HEADS + h, b, 0)),
            g_spec, g_spec,
        ],
        out_specs=pl.BlockSpec((s_len, XA_HEAD_DIM), lambda b, h: (b, h)),
        out_shape=jax.ShapeDtypeStruct((t, XA_HEADS * XA_HEAD_DIM), BF16),
        compiler_params=_params(("parallel", "parallel"), 32),
        name="mem_xattn",
    )(proj, mkv, mkv, gq.reshape(1, -1), gk.reshape(1, -1))


def _merge_kernel(ona_ref, oret_ref, omem_ref, wna_ref, wret_ref, wmem_ref,
                  gna_ref, gret_ref, gmem_ref, o_ref):
    y_na = jnp.dot(ona_ref[...], wna_ref[...].astype(BF16), preferred_element_type=F32)
    y_ret = jnp.dot(oret_ref[...], wret_ref[...].astype(BF16), preferred_element_type=F32)
    y_mem = jnp.dot(omem_ref[...], wmem_ref[...].astype(BF16), preferred_element_type=F32)
    for j in range(gna_ref.shape[0]):
        cs = slice(j * LANE, (j + 1) * LANE)
        o_ref[:, cs] = (_sigmoid(gna_ref[j].astype(F32)) * y_na[:, cs]
                        + _sigmoid(gret_ref[j].astype(F32)) * y_ret[:, cs]
                        + _sigmoid(gmem_ref[j].astype(F32)) * y_mem[:, cs]).astype(BF16)


def _merge(o_na, o_ret, o_mem, w_na, w_ret, w_mem, proj, tm, tn):
    t = o_na.shape[0]
    d = w_na.shape[1]
    ns = tn // LANE

    def lhs_spec(a):
        return pl.BlockSpec((tm, a.shape[1]), lambda i, j: (i, 0))

    def w_spec(w):
        return pl.BlockSpec((w.shape[0], tn), lambda i, j: (0, j))

    def gate_spec(first):
        return pl.BlockSpec((ns, tm, LANE), lambda i, j: (first // ns + j, i, 0))

    return pl.pallas_call(
        _merge_kernel,
        grid=(t // tm, d // tn),
        in_specs=[lhs_spec(o_na), lhs_spec(o_ret), lhs_spec(o_mem),
                  w_spec(w_na), w_spec(w_ret), w_spec(w_mem),
                  gate_spec(SLAB_G_NA), gate_spec(SLAB_G_RET), gate_spec(SLAB_G_MEM)],
        out_specs=pl.BlockSpec((tm, tn), lambda i, j: (i, j)),
        out_shape=jax.ShapeDtypeStruct((t, d), BF16),
        compiler_params=_params(("parallel", "arbitrary"), 58),
        name="merge",
    )(o_na, o_ret, o_mem, w_na, w_ret, w_mem, proj, proj, proj)


def _out_proj_kernel(m_ref, w_ref, x_ref, o_ref, wb_ref):
    @pl.when(pl.program_id(0) == 0)
    def _():
        wb_ref[...] = w_ref[...].astype(BF16)

    o_ref[...] = x_ref[...] + jnp.dot(m_ref[...], wb_ref[...], preferred_element_type=F32)


def _out_proj(merged, w, x2d, tm):
    t, k = merged.shape
    d = w.shape[1]
    return pl.pallas_call(
        _out_proj_kernel,
        grid=(t // tm,),
        in_specs=[pl.BlockSpec((tm, k), lambda i: (i, 0)),
                  pl.BlockSpec((k, d), lambda i: (0, 0), pipeline_mode=pl.Buffered(1)),
                  pl.BlockSpec((tm, d), lambda i: (i, 0))],
        out_specs=pl.BlockSpec((tm, d), lambda i: (i, 0)),
        out_shape=jax.ShapeDtypeStruct((t, d), F32),
        scratch_shapes=[pltpu.VMEM((k, d), BF16)],
        compiler_params=_params(("arbitrary",), 56),
        name="out_proj",
    )(merged, w, x2d)


def _ffn_kernel(x_ref, g_ref, w1_ref, w2_ref, o_ref, h_ref):
    @pl.when(pl.program_id(1) == 0)
    def _():
        x = x_ref[...]
        h_ref[...] = _rms(x, g_ref[...]).astype(BF16)
        o_ref[...] = x

    a = jnp.maximum(jnp.dot(h_ref[...], w1_ref[...], preferred_element_type=F32), 0.0)
    o_ref[...] += jnp.dot((a * a).astype(BF16), w2_ref[...], preferred_element_type=F32)


def _ffn(x1, g, w1, w2, tm, tf):
    t, d = x1.shape
    dff = w1.shape[1]
    return pl.pallas_call(
        _ffn_kernel,
        grid=(t // tm, dff // tf),
        in_specs=[pl.BlockSpec((tm, d), lambda i, f: (i, 0)),
                  pl.BlockSpec((1, d), lambda i, f: (0, 0)),
                  pl.BlockSpec((d, tf), lambda i, f: (0, f)),
                  pl.BlockSpec((tf, d), lambda i, f: (f, 0))],
        out_specs=pl.BlockSpec((tm, d), lambda i, f: (i, 0)),
        out_shape=jax.ShapeDtypeStruct((t, d), F32),
        scratch_shapes=[pltpu.VMEM((tm, d), BF16)],
        compiler_params=_params(("parallel", "arbitrary"), 48),
        name="ffn",
    )(x1, g.reshape(1, d), w1, w2)


def _rope_tables(s_len):
    half = RET_QK_DIM // 2
    inv = np.power(np.float64(ROPE_BASE), -np.arange(half, dtype=np.float64) / half)
    ang = np.arange(s_len, dtype=np.float64)[:, None] * inv[None, :]
    cos, sin = np.cos(ang), np.sin(ang)
    return (jnp.asarray(np.concatenate([cos, cos], axis=1), F32),
            jnp.asarray(np.concatenate([-sin, sin], axis=1), F32))


def kernel(x, mem, norm_mix_g, w_in, na_q_norm_g, na_k_norm_g, na_rpb, ret_decay_logit_fwd, ret_decay_logit_bwd, ret_gn_g, mem_norm_g, w_mem_kv, xa_q_norm_g, xa_k_norm_g, w_br_na, w_br_ret, w_br_mem, w_out, norm_ffn_g, w_ff1, w_ff2):
    batch, s_len, d = x.shape
    n_mem = mem.shape[1]
    t = batch * s_len
    depth = w_in.shape[0]
    tm = min(1024, t)
    cos2, sin2 = _rope_tables(s_len)
    x2d = x.reshape(t, d)
    mem2d = mem.reshape(batch * n_mem, d)

    def lane_bcast(v):
        return jnp.broadcast_to(v.astype(F32)[:, None, None], (v.shape[0], 1, LANE))

    tn_in = 1024
    gm, gn = t // tm, w_in.shape[2] // tn_in

    def side_job(w):
        r, c = w.shape
        if r % (gm * gn * 16) == 0:
            return w, (r // (gm * gn), c), lambda i, j: (i * gn + j, 0)
        assert r % (gn * 16) == 0 and c % (gm * LANE) == 0, (w.shape, gm, gn)
        return w, (r // gn, c // gm), lambda i, j: (j, i)

    for l in range(depth):
        proj, (w_ff1_bf, w_ff2_bf) = _norm_proj(x2d, norm_mix_g[l], w_in[l], tm, tn_in,
                                                side=(side_job(w_ff1[l]), side_job(w_ff2[l])))
        mkv, _ = _norm_proj(mem2d, mem_norm_g[l], w_mem_kv[l], batch * n_mem, 1024)

        o_mem = _mem_xattn(proj, mkv, xa_q_norm_g[l], xa_k_norm_g[l], batch, s_len, n_mem)
        o_na = _na_attention(proj, na_rpb[l], na_q_norm_g[l], na_k_norm_g[l], batch, s_len)
        o_ret = _retention(proj, cos2, sin2, lane_bcast(ret_decay_logit_fwd[l]),
                           lane_bcast(ret_decay_logit_bwd[l]), ret_gn_g[l], batch, s_len)

        merged = _merge(o_na, o_ret, o_mem, w_br_na[l], w_br_ret[l], w_br_mem[l], proj, tm, 512)
        x1 = _out_proj(merged, w_out[l], x2d, min(512, t))
        x2d = _ffn(x1, norm_ffn_g[l], w_ff1_bf, w_ff2_bf, min(512, t), 1024)
    return x2d.reshape(batch, s_len, d)
```

```python
import functools

import jax
import jax.numpy as jnp
import numpy as np
from jax import lax
from jax.experimental import pallas as pl
from jax.experimental.pallas import tpu as pltpu

F32 = jnp.float32
BF16 = jnp.bfloat16

LANE = 128
EPS = 1e-6
NEG = -1e30

GRID_W = 64
NA_HEADS = 8
NA_HEAD_DIM = 128
NA_MAX_ROWS = 8
NA_COLS = 16
NA_DR = 2 * NA_MAX_ROWS - 1
NA_DC = 2 * NA_COLS - 1
NA_TBL = 3 * NA_DR + 1
NA_QROWS = 4
NA_WIN = NA_QROWS + NA_MAX_ROWS
NA_UNROLL = 16

RET_HEADS = 8
RET_QK_DIM = 128
RET_V_DIM = 256
RET_BLOCK = 512
ROPE_BASE = 10000.0

XA_HEADS = 4
XA_HEAD_DIM = 256

SLAB_NA_Q, SLAB_NA_K, SLAB_NA_V = 0, 8, 16
SLAB_RQ, SLAB_RK, SLAB_RV, SLAB_RG = 24, 32, 40, 56
SLAB_XQ = 72
SLAB_G_NA, SLAB_G_RET, SLAB_G_MEM = 80, 96, 112


def _params(sem, vmem_mib):
    return pltpu.CompilerParams(dimension_semantics=sem, vmem_limit_bytes=vmem_mib * 2**20)


def _rms(x, g):
    return x * lax.rsqrt(jnp.mean(x * x, axis=-1, keepdims=True) + EPS) * g


def _sigmoid(x):
    return 0.5 * jnp.tanh(0.5 * x) + 0.5


def _norm_proj_kernel(n_side, x_ref, g_ref, w_ref, *refs):
    side_in, o_ref = refs[:n_side], refs[n_side]
    side_out, hn_ref = refs[n_side + 1:2 * n_side + 1], refs[2 * n_side + 1]

    @pl.when(pl.program_id(1) == 0)
    def _():
        hn_ref[...] = _rms(x_ref[...], g_ref[...]).astype(BF16)

    acc = jnp.dot(hn_ref[...], w_ref[...].astype(BF16), preferred_element_type=F32)
    for s in range(o_ref.shape[0]):
        o_ref[s] = acc[:, s * LANE:(s + 1) * LANE].astype(BF16)
    for src, dst in zip(side_in, side_out):
        dst[...] = src[...].astype(BF16)


def _norm_proj(x2d, g, w, tm, tn, side=()):
    m, k = x2d.shape
    n = w.shape[1]
    side_specs = [pl.BlockSpec(blk, imap) for _, blk, imap in side]
    out = pl.pallas_call(
        functools.partial(_norm_proj_kernel, len(side)),
        grid=(m // tm, n // tn),
        in_specs=[
            pl.BlockSpec((tm, k), lambda i, j: (i, 0)),
            pl.BlockSpec((1, k), lambda i, j: (0, 0)),
            pl.BlockSpec((k, tn), lambda i, j: (0, j)),
        ] + side_specs,
        out_specs=[pl.BlockSpec((tn // LANE, tm, LANE), lambda i, j: (j, i, 0))] + side_specs,
        out_shape=[jax.ShapeDtypeStruct((n // LANE, m, LANE), BF16)]
        + [jax.ShapeDtypeStruct(a.shape, BF16) for a, _, _ in side],
        scratch_shapes=[pltpu.VMEM((tm, k), BF16)],
        compiler_params=_params(("parallel", "arbitrary"), 58),
        name="norm_proj",
    )(x2d, g.reshape(1, k), w, *[a for a, _, _ in side])
    return out[0], out[1:]


def _na_build_tables(h, rpb_ref, t_ref, base_ref):
    qc = lax.broadcasted_iota(jnp.int32, (GRID_W, LANE), 0)
    lane = lax.broadcasted_iota(jnp.int32, (GRID_W, LANE), 1)
    kc = lane & (GRID_W - 1)
    d = jnp.clip(kc - qc, -(NA_COLS - 1), NA_COLS - 1) + (NA_COLS - 1)
    cs = jnp.clip(qc - NA_COLS // 2, 0, GRID_W - NA_COLS)
    col_ok = (kc >= cs) & (kc < cs + NA_COLS)
    left = lane < GRID_W
    neg = jnp.full((GRID_W, LANE), NEG, F32)

    def body(dr, _):
        base = (h * NA_DR + dr) * NA_DC
        val = jnp.zeros((GRID_W, LANE), F32)
        for dd in range(NA_DC):
            val = jnp.where(d == dd, rpb_ref[base + dd], val)
        base_ref[dr] = jnp.where(col_ok, val, NEG)
        return 0

    lax.fori_loop(0, NA_DR, body, 0)
    for dr in range(NA_DR):
        second = base_ref[dr + 1] if dr + 1 < NA_DR else neg
        t_ref[dr] = jnp.where(left, base_ref[dr], second)
        t_ref[NA_DR + dr] = jnp.where(left, base_ref[dr], neg)
        t_ref[2 * NA_DR + dr] = jnp.where(left, neg, base_ref[dr])
    t_ref[3 * NA_DR] = neg


def _na_kernel(rpb_ref, q_ref, k_ref, v_ref, gq_ref, gk_ref, o_ref, kt_ref, va_ref, t_ref, base_ref):
    @pl.when(pl.program_id(1) == 0)
    def _():
        _na_build_tables(pl.program_id(0), rpb_ref, t_ref, base_ref)

    s_len = q_ref.shape[1]
    rows = s_len // GRID_W
    step_tok = NA_QROWS * GRID_W
    win_tok = NA_WIN * GRID_W
    n_tiles = NA_WIN // 2
    gq = gq_ref[...] * (NA_HEAD_DIM ** -0.5)
    gk = gk_ref[...]
    va_ref[:, :NA_HEAD_DIM] = v_ref[0]
    va_ref[:, NA_HEAD_DIM:] = jnp.ones((s_len, NA_HEAD_DIM), BF16)

    def knorm(c, _):
        sl = pl.ds(pl.multiple_of(c * LANE, LANE), LANE)
        kt_ref[c] = _rms(k_ref[0, sl, :].astype(F32), gk).astype(BF16).T
        return 0

    lax.fori_loop(0, s_len // LANE, knorm, 0, unroll=True)

    def step(i, _):
        r0 = NA_QROWS * i
        ws = jnp.clip(r0 - NA_MAX_ROWS // 2, 0, rows - NA_WIN)
        wp = ws // 2
        qsl = pl.ds(pl.multiple_of(i * step_tok, step_tok), step_tok)
        wsl = pl.ds(pl.multiple_of(ws * GRID_W, LANE), win_tok)
        qn = _rms(q_ref[0, qsl, :].astype(F32), gq).astype(BF16)
        kwin = jnp.concatenate([kt_ref[wp + t] for t in range(n_tiles)], axis=1)
        s = jnp.dot(qn, kwin, preferred_element_type=F32)
        bias_rows = []
        for qr in range(NA_QROWS):
            r = r0 + qr
            rs = jnp.clip(r - NA_MAX_ROWS // 2, 0, rows - NA_MAX_ROWS)
            tiles = []
            for t in range(n_tiles):
                ka = ws + 2 * t
                dr = ka - r + (NA_MAX_ROWS - 1)
                va = (ka >= rs) & (ka < rs + NA_MAX_ROWS)
                vb = (ka + 1 >= rs) & (ka + 1 < rs + NA_MAX_ROWS)
                idx = jnp.where(va, jnp.where(vb, dr, NA_DR + dr),
                                jnp.where(vb, 2 * NA_DR + dr + 1, 3 * NA_DR))
                tiles.append(t_ref[idx])
            bias_rows.append(jnp.concatenate(tiles, axis=1))
        s = s + jnp.concatenate(bias_rows, axis=0)
        e = jnp.exp(s - jnp.max(s, axis=-1, keepdims=True))
        o = jnp.dot(e.astype(BF16), va_ref[wsl, :], preferred_element_type=F32)
        o_ref[qsl, :] = (o[:, :NA_HEAD_DIM] / o[:, NA_HEAD_DIM:]).astype(BF16)
        return 0

    lax.fori_loop(0, rows // NA_QROWS, step, 0, unroll=NA_UNROLL)


def _na_attention(proj, rpb, gq, gk, batch, s_len):
    t = batch * s_len
    g_spec = pl.BlockSpec((1, NA_HEAD_DIM), lambda h, b: (0, 0))
    return pl.pallas_call(
        _na_kernel,
        grid=(NA_HEADS, batch),
        in_specs=[
            pl.BlockSpec(memory_space=pltpu.SMEM),
            pl.BlockSpec((1, s_len, LANE), lambda h, b: (SLAB_NA_Q + h, b, 0)),
            pl.BlockSpec((1, s_len, LANE), lambda h, b: (SLAB_NA_K + h, b, 0)),
            pl.BlockSpec((1, s_len, LANE), lambda h, b: (SLAB_NA_V + h, b, 0)),
            g_spec, g_spec,
        ],
        out_specs=pl.BlockSpec((s_len, NA_HEAD_DIM), lambda h, b: (b, h)),
        out_shape=jax.ShapeDtypeStruct((t, NA_HEADS * NA_HEAD_DIM), BF16),
        scratch_shapes=[pltpu.VMEM((s_len // LANE, NA_HEAD_DIM, LANE), BF16),
                        pltpu.VMEM((s_len, 2 * NA_HEAD_DIM), BF16),
                        pltpu.VMEM((NA_TBL, GRID_W, LANE), F32),
                        pltpu.VMEM((NA_DR, GRID_W, LANE), F32)],
        compiler_params=_params(("arbitrary", "arbitrary"), 56),
        name="na_attention",
    )(rpb.reshape(-1), proj, proj, proj, gq.reshape(1, -1), gk.reshape(1, -1))


def _log_sigmoid(x):
    return -(jnp.maximum(-x, 0.0) + jnp.log1p(jnp.exp(-jnp.abs(x))))


def _ret_kernel(q_ref, k_ref, v_ref, g_ref, cos_ref, sin_ref, lf_ref, lb_ref, gn_ref,
                o_ref, qr_ref, kt_ref, d_ref, qdf_ref, qdb_ref, kv_ref, s_ref):
    s_len = q_ref.shape[1]
    c = RET_BLOCK
    nb = s_len // c
    dk = RET_QK_DIM
    half = dk // 2

    lgf = _log_sigmoid(lf_ref[0][:, :1])
    lgb = _log_sigmoid(lb_ref[0][:, :1])
    ic = lax.broadcasted_iota(jnp.int32, (c, 1), 0).astype(F32)
    jr = lax.broadcasted_iota(jnp.int32, (1, c), 1).astype(F32)
    diff = ic - jr
    scale = dk ** -0.5
    d_ref[...] = jnp.exp(jnp.where(diff >= 0, lgf, lgb) * jnp.abs(diff)) * scale
    qdf_ref[...] = jnp.broadcast_to(jnp.exp(lgf * (ic + 1.0)), (c, dk))
    qdb_ref[...] = jnp.broadcast_to(jnp.exp(lgb * (c - ic)), (c, dk))
    kdf = jnp.exp(lgf * (c - 1.0 - jr)) * scale
    kdb = jnp.exp(lgb * jr) * scale
    cd_f = jnp.exp(lgf * c)
    cd_b = jnp.exp(lgb * c)

    def block_v(sl):
        return jnp.concatenate([v_ref[0, sl, :], v_ref[1, sl, :]], axis=1)

    def prep(n, _):
        sl = pl.ds(pl.multiple_of(n * c, c), c)
        cos = cos_ref[sl, :]
        sin = sin_ref[sl, :]
        q = q_ref[0, sl, :].astype(F32)
        k = k_ref[0, sl, :].astype(F32)
        qr_ref[sl, :] = (q * cos + pltpu.roll(q, half, 1) * sin).astype(BF16)
        kt = (k * cos + pltpu.roll(k, half, 1) * sin).T
        kt_ref[n] = kt.astype(BF16)
        lhs = jnp.concatenate([(kt * kdf).astype(BF16), (kt * kdb).astype(BF16)], axis=0)
        kv_ref[n] = jnp.dot(lhs, block_v(sl), preferred_element_type=F32)
        return 0

    lax.fori_loop(0, nb, prep, 0, unroll=True)

    def scan_f(n, sf):
        s_ref[n, :dk, :] = sf.astype(BF16)
        return cd_f * sf + kv_ref[n, :dk, :]

    def scan_b(t, sb):
        n = nb - 1 - t
        s_ref[n, dk:, :] = sb.astype(BF16)
        return cd_b * sb + kv_ref[n, dk:, :]

    zero = jnp.zeros((dk, RET_V_DIM), F32)

    def scan(n, carry):
        return scan_f(n, carry[0]), scan_b(n, carry[1])

    lax.fori_loop(0, nb, scan, (zero, zero), unroll=True)

    gn = gn_ref[0]

    def out(n, _):
        sl = pl.ds(pl.multiple_of(n * c, c), c)
        q = qr_ref[sl, :]
        qf32 = q.astype(F32)
        a = jnp.dot(q, kt_ref[n], preferred_element_type=F32) * d_ref[...]
        lhs = jnp.concatenate([a.astype(BF16), (qf32 * qdf_ref[...]).astype(BF16),
                               (qf32 * qdb_ref[...]).astype(BF16)], axis=1)
        rhs = jnp.concatenate([block_v(sl), s_ref[n]], axis=0)
        o = jnp.dot(lhs, rhs, preferred_element_type=F32)
        mu = jnp.mean(o, axis=-1, keepdims=True)
        oc = o - mu
        y = oc * lax.rsqrt(jnp.mean(oc * oc, axis=-1, keepdims=True) + EPS) * gn
        gate = jnp.concatenate([g_ref[0, sl, :], g_ref[1, sl, :]], axis=1).astype(F32)
        o_ref[sl, :] = (y * gate * _sigmoid(gate)).astype(BF16)
        return 0

    lax.fori_loop(0, nb, out, 0, unroll=True)


def _retention(proj, cos2, sin2, lf, lb, gn, batch, s_len):
    t = batch * s_len
    nb = s_len // RET_BLOCK
    dec_spec = pl.BlockSpec((1, 1, LANE), lambda b, h: (h, 0, 0))
    return pl.pallas_call(
        _ret_kernel,
        grid=(batch, RET_HEADS),
        in_specs=[
            pl.BlockSpec((1, s_len, LANE), lambda b, h: (SLAB_RQ + h, b, 0)),
            pl.BlockSpec((1, s_len, LANE), lambda b, h: (SLAB_RK + h, b, 0)),
            pl.BlockSpec((2, s_len, LANE), lambda b, h: (SLAB_RV // 2 + h, b, 0)),
            pl.BlockSpec((2, s_len, LANE), lambda b, h: (SLAB_RG // 2 + h, b, 0)),
            pl.BlockSpec((s_len, RET_QK_DIM), lambda b, h: (0, 0)),
            pl.BlockSpec((s_len, RET_QK_DIM), lambda b, h: (0, 0)),
            dec_spec, dec_spec,
            pl.BlockSpec((1, 1, RET_V_DIM), lambda b, h: (h, 0, 0)),
        ],
        out_specs=pl.BlockSpec((s_len, RET_V_DIM), lambda b, h: (b, h)),
        out_shape=jax.ShapeDtypeStruct((t, RET_HEADS * RET_V_DIM), BF16),
        scratch_shapes=[
            pltpu.VMEM((s_len, RET_QK_DIM), BF16),
            pltpu.VMEM((nb, RET_QK_DIM, RET_BLOCK), BF16),
            pltpu.VMEM((RET_BLOCK, RET_BLOCK), F32),
            pltpu.VMEM((RET_BLOCK, RET_QK_DIM), F32),
            pltpu.VMEM((RET_BLOCK, RET_QK_DIM), F32),
            pltpu.VMEM((nb, 2 * RET_QK_DIM, RET_V_DIM), F32),
            pltpu.VMEM((nb, 2 * RET_QK_DIM, RET_V_DIM), BF16),
        ],
        compiler_params=_params(("parallel", "parallel"), 48),
        name="retention",
    )(proj, proj, proj, proj, cos2, sin2, lf, lb, gn.reshape(RET_HEADS, 1, RET_V_DIM))


def _xa_kernel(q_ref, k_ref, v_ref, gq_ref, gk_ref, o_ref):
    s_len = q_ref.shape[1]
    tq = 512
    gq = gq_ref[...] * (XA_HEAD_DIM ** -0.5)
    k = jnp.concatenate([k_ref[0], k_ref[1]], axis=1).astype(F32)
    kn = _rms(k, gk_ref[...]).astype(BF16)
    v = jnp.concatenate([v_ref[0], v_ref[1]], axis=1)

    def body(i, _):
        sl = pl.ds(pl.multiple_of(i * tq, tq), tq)
        q = jnp.concatenate([q_ref[0, sl, :], q_ref[1, sl, :]], axis=1).astype(F32)
        qn = _rms(q, gq).astype(BF16)
        s = lax.dot_general(qn, kn, (((1,), (1,)), ((), ())), preferred_element_type=F32)
        e = jnp.exp(s - jnp.max(s, axis=-1, keepdims=True))
        l = jnp.sum(e, axis=-1, keepdims=True)
        o = jnp.dot(e.astype(BF16), v, preferred_element_type=F32)
        o_ref[sl, :] = (o / l).astype(BF16)
        return 0

    lax.fori_loop(0, s_len // tq, body, 0, unroll=True)


def _mem_xattn(proj, mkv, gq, gk, batch, s_len, n_mem):
    t = batch * s_len
    g_spec = pl.BlockSpec((1, XA_HEAD_DIM), lambda b, h: (0, 0))
    return pl.pallas_call(
        _xa_kernel,
        grid=(batch, XA_HEADS),
        in_specs=[
            pl.BlockSpec((2, s_len, LANE), lambda b, h: (SLAB_XQ // 2 + h, b, 0)),
            pl.BlockSpec((2, n_mem, LANE), lambda b, h: (h, b, 0)),
            pl.BlockSpec((2, n_mem, LANE), lambda b, h: (XA_HEADS + h, b, 0)),
            g_spec, g_spec,
        ],
        out_specs=pl.BlockSpec((s_len, XA_HEAD_DIM), lambda b, h: (b, h)),
        out_shape=jax.ShapeDtypeStruct((t, XA_HEADS * XA_HEAD_DIM), BF16),
        compiler_params=_params(("parallel", "parallel"), 56),
        name="mem_xattn",
    )(proj, mkv, mkv, gq.reshape(1, -1), gk.reshape(1, -1))


def _merge_kernel(ona_ref, oret_ref, omem_ref, wna_ref, wret_ref, wmem_ref,
                  gna_ref, gret_ref, gmem_ref, o_ref):
    y_na = jnp.dot(ona_ref[...], wna_ref[...], preferred_element_type=F32)
    y_ret = jnp.dot(oret_ref[...], wret_ref[...], preferred_element_type=F32)
    y_mem = jnp.dot(omem_ref[...], wmem_ref[...], preferred_element_type=F32)
    for j in range(gna_ref.shape[0]):
        cs = slice(j * LANE, (j + 1) * LANE)
        o_ref[:, cs] = (_sigmoid(gna_ref[j].astype(F32)) * y_na[:, cs]
                        + _sigmoid(gret_ref[j].astype(F32)) * y_ret[:, cs]
                        + _sigmoid(gmem_ref[j].astype(F32)) * y_mem[:, cs]).astype(BF16)


def _merge(o_na, o_ret, o_mem, w_na, w_ret, w_mem, proj, tm, tn):
    t = o_na.shape[0]
    d = w_na.shape[1]
    ns = tn // LANE

    def lhs_spec(a):
        return pl.BlockSpec((tm, a.shape[1]), lambda i, j: (i, 0))

    def w_spec(w):
        return pl.BlockSpec((w.shape[0], tn), lambda i, j: (0, j))

    def gate_spec(first):
        return pl.BlockSpec((ns, tm, LANE), lambda i, j: (first // ns + j, i, 0))

    return pl.pallas_call(
        _merge_kernel,
        grid=(t // tm, d // tn),
        in_specs=[lhs_spec(o_na), lhs_spec(o_ret), lhs_spec(o_mem),
                  w_spec(w_na), w_spec(w_ret), w_spec(w_mem),
                  gate_spec(SLAB_G_NA), gate_spec(SLAB_G_RET), gate_spec(SLAB_G_MEM)],
        out_specs=pl.BlockSpec((tm, tn), lambda i, j: (i, j)),
        out_shape=jax.ShapeDtypeStruct((t, d), BF16),
        compiler_params=_params(("parallel", "arbitrary"), 58),
        name="merge",
    )(o_na, o_ret, o_mem, w_na, w_ret, w_mem, proj, proj, proj)


def _out_proj_kernel(m_ref, w_ref, x_ref, o_ref, wb_ref):
    @pl.when(pl.program_id(0) == 0)
    def _():
        wb_ref[...] = w_ref[...].astype(BF16)

    o_ref[...] = x_ref[...] + jnp.dot(m_ref[...], wb_ref[...], preferred_element_type=F32)


def _out_proj(merged, w, x2d, tm):
    t, k = merged.shape
    d = w.shape[1]
    return pl.pallas_call(
        _out_proj_kernel,
        grid=(t // tm,),
        in_specs=[pl.BlockSpec((tm, k), lambda i: (i, 0)),
                  pl.BlockSpec((k, d), lambda i: (0, 0), pipeline_mode=pl.Buffered(1)),
                  pl.BlockSpec((tm, d), lambda i: (i, 0))],
        out_specs=pl.BlockSpec((tm, d), lambda i: (i, 0)),
        out_shape=jax.ShapeDtypeStruct((t, d), F32),
        scratch_shapes=[pltpu.VMEM((k, d), BF16)],
        compiler_params=_params(("arbitrary",), 56),
        name="out_proj",
    )(merged, w, x2d)


def _ffn_kernel(x_ref, g_ref, w1_ref, w2_ref, o_ref, h_ref):
    @pl.when(pl.program_id(1) == 0)
    def _():
        x = x_ref[...]
        h_ref[...] = _rms(x, g_ref[...]).astype(BF16)
        o_ref[...] = x

    a = jnp.maximum(jnp.dot(h_ref[...], w1_ref[...], preferred_element_type=F32), 0.0)
    o_ref[...] += jnp.dot((a * a).astype(BF16), w2_ref[...], preferred_element_type=F32)


def _ffn(x1, g, w1, w2, tm, tf):
    t, d = x1.shape
    dff = w1.shape[1]
    return pl.pallas_call(
        _ffn_kernel,
        grid=(t // tm, dff // tf),
        in_specs=[pl.BlockSpec((tm, d), lambda i, f: (i, 0)),
                  pl.BlockSpec((1, d), lambda i, f: (0, 0)),
                  pl.BlockSpec((d, tf), lambda i, f: (0, f)),
                  pl.BlockSpec((tf, d), lambda i, f: (f, 0))],
        out_specs=pl.BlockSpec((tm, d), lambda i, f: (i, 0)),
        out_shape=jax.ShapeDtypeStruct((t, d), F32),
        scratch_shapes=[pltpu.VMEM((tm, d), BF16)],
        compiler_params=_params(("parallel", "arbitrary"), 48),
        name="ffn",
    )(x1, g.reshape(1, d), w1, w2)


def _rope_tables(s_len):
    half = RET_QK_DIM // 2
    inv = np.power(np.float64(ROPE_BASE), -np.arange(half, dtype=np.float64) / half)
    ang = np.arange(s_len, dtype=np.float64)[:, None] * inv[None, :]
    cos, sin = np.cos(ang), np.sin(ang)
    return (jnp.asarray(np.concatenate([cos, cos], axis=1), F32),
            jnp.asarray(np.concatenate([-sin, sin], axis=1), F32))


def kernel(x, mem, norm_mix_g, w_in, na_q_norm_g, na_k_norm_g, na_rpb, ret_decay_logit_fwd, ret_decay_logit_bwd, ret_gn_g, mem_norm_g, w_mem_kv, xa_q_norm_g, xa_k_norm_g, w_br_na, w_br_ret, w_br_mem, w_out, norm_ffn_g, w_ff1, w_ff2):
    batch, s_len, d = x.shape
    n_mem = mem.shape[1]
    t = batch * s_len
    depth = w_in.shape[0]
    tm = min(1024, t)
    cos2, sin2 = _rope_tables(s_len)
    x2d = x.reshape(t, d)
    mem2d = mem.reshape(batch * n_mem, d)

    def lane_bcast(v):
        return jnp.broadcast_to(v.astype(F32)[:, None, None], (v.shape[0], 1, LANE))

    tn_in = 1024
    gm, gn = t // tm, w_in.shape[2] // tn_in

    def side_job(w):
        r, c = w.shape
        if r % (gm * gn * 16) == 0:
            return w, (r // (gm * gn), c), lambda i, j: (i * gn + j, 0)
        assert r % (gn * 16) == 0 and c % (gm * LANE) == 0, (w.shape, gm, gn)
        return w, (r // gn, c // gm), lambda i, j: (j, i)

    for l in range(depth):
        sides = [side_job(w) for w in (w_ff1[l], w_ff2[l], w_br_na[l], w_br_ret[l], w_br_mem[l])]
        proj, (w_ff1_bf, w_ff2_bf, w_na_bf, w_ret_bf, w_mem_bf) = _norm_proj(
            x2d, norm_mix_g[l], w_in[l], tm, tn_in, side=sides)
        mkv, _ = _norm_proj(mem2d, mem_norm_g[l], w_mem_kv[l], batch * n_mem, 1024)

        o_mem = _mem_xattn(proj, mkv, xa_q_norm_g[l], xa_k_norm_g[l], batch, s_len, n_mem)
        o_na = _na_attention(proj, na_rpb[l], na_q_norm_g[l], na_k_norm_g[l], batch, s_len)
        o_ret = _retention(proj, cos2, sin2, lane_bcast(ret_decay_logit_fwd[l]),
                           lane_bcast(ret_decay_logit_bwd[l]), ret_gn_g[l], batch, s_len)

        merged = _merge(o_na, o_ret, o_mem, w_na_bf, w_ret_bf, w_mem_bf, proj, tm, 512)
        x1 = _out_proj(merged, w_out[l], x2d, min(512, t))
        x2d = _ffn(x1, norm_ffn_g[l], w_ff1_bf, w_ff2_bf, min(512, t), 1024)
    return x2d.reshape(batch, s_len, d)
```

```python
import functools

import jax
import jax.numpy as jnp
import numpy as np
from jax import lax
from jax.experimental import pallas as pl
from jax.experimental.pallas import tpu as pltpu

F32 = jnp.float32
BF16 = jnp.bfloat16

LANE = 128
EPS = 1e-6
NEG = -1e30
LOG2E = 1.4426950408889634

GRID_W = 64
NA_HEADS = 8
NA_HEAD_DIM = 128
NA_MAX_ROWS = 8
NA_COLS = 16
NA_DR = 2 * NA_MAX_ROWS - 1
NA_DC = 2 * NA_COLS - 1
NA_TBL = 3 * NA_DR + 1
NA_QROWS = 4
NA_WIN = NA_QROWS + NA_MAX_ROWS

RET_HEADS = 8
RET_QK_DIM = 128
RET_V_DIM = 256
RET_BLOCK = 512
ROPE_BASE = 10000.0

XA_HEADS = 4
XA_HEAD_DIM = 256

SLAB_NA_Q, SLAB_NA_K, SLAB_NA_V = 0, 8, 16
SLAB_RQ, SLAB_RK, SLAB_RV, SLAB_RG = 24, 32, 40, 56
SLAB_XQ = 72
SLAB_G_NA, SLAB_G_RET, SLAB_G_MEM = 80, 96, 112


def _params(sem, vmem_mib):
    return pltpu.CompilerParams(dimension_semantics=sem, vmem_limit_bytes=vmem_mib * 2**20)


def _rms(x, g):
    return x * lax.rsqrt(jnp.mean(x * x, axis=-1, keepdims=True) + EPS) * g


def _unit_rms(x):
    return x * lax.rsqrt(jnp.mean(x * x, axis=-1, keepdims=True) + EPS)


def _sigmoid(x):
    return 0.5 * jnp.tanh(0.5 * x) + 0.5


def _norm_proj_kernel(n_side, x_ref, g_ref, w_ref, *refs):
    side_in, o_ref = refs[:n_side], refs[n_side]
    side_out, hn_ref = refs[n_side + 1:2 * n_side + 1], refs[2 * n_side + 1]

    @pl.when(pl.program_id(1) == 0)
    def _():
        hn_ref[...] = _rms(x_ref[...], g_ref[...]).astype(BF16)

    acc = jnp.dot(hn_ref[...], w_ref[...].astype(BF16), preferred_element_type=F32)
    for s in range(o_ref.shape[0]):
        o_ref[s] = acc[:, s * LANE:(s + 1) * LANE].astype(BF16)
    for src, dst in zip(side_in, side_out):
        dst[...] = src[...].astype(BF16)


def _norm_proj(x2d, g, w, tm, tn, side=()):
    m, k = x2d.shape
    n = w.shape[1]
    side_specs = [pl.BlockSpec(blk, imap) for _, blk, imap in side]
    out = pl.pallas_call(
        functools.partial(_norm_proj_kernel, len(side)),
        grid=(m // tm, n // tn),
        in_specs=[
            pl.BlockSpec((tm, k), lambda i, j: (i, 0)),
            pl.BlockSpec((1, k), lambda i, j: (0, 0)),
            pl.BlockSpec((k, tn), lambda i, j: (0, j)),
        ] + side_specs,
        out_specs=[pl.BlockSpec((tn // LANE, tm, LANE), lambda i, j: (j, i, 0))] + side_specs,
        out_shape=[jax.ShapeDtypeStruct((n // LANE, m, LANE), BF16)]
        + [jax.ShapeDtypeStruct(a.shape, BF16) for a, _, _ in side],
        scratch_shapes=[pltpu.VMEM((tm, k), BF16)],
        compiler_params=_params(("parallel", "arbitrary"), 58),
        name="norm_proj",
    )(x2d, g.reshape(1, k), w, *[a for a, _, _ in side])
    return out[0], out[1:]


def _na_build_tables(h, rpb_ref, t_ref, base_ref):
    qc = lax.broadcasted_iota(jnp.int32, (GRID_W, LANE), 0)
    lane = lax.broadcasted_iota(jnp.int32, (GRID_W, LANE), 1)
    kc = lane & (GRID_W - 1)
    d = jnp.clip(kc - qc, -(NA_COLS - 1), NA_COLS - 1) + (NA_COLS - 1)
    cs = jnp.clip(qc - NA_COLS // 2, 0, GRID_W - NA_COLS)
    col_ok = (kc >= cs) & (kc < cs + NA_COLS)
    left = lane < GRID_W
    neg = jnp.full((GRID_W, LANE), NEG, F32)

    def body(dr, _):
        base = (h * NA_DR + dr) * NA_DC
        val = jnp.zeros((GRID_W, LANE), F32)
        for dd in range(NA_DC):
            val = jnp.where(d == dd, rpb_ref[base + dd], val)
        base_ref[dr] = jnp.where(col_ok, val * LOG2E, NEG)
        return 0

    lax.fori_loop(0, NA_DR, body, 0)
    for dr in range(NA_DR):
        second = base_ref[dr + 1] if dr + 1 < NA_DR else neg
        t_ref[dr] = jnp.where(left, base_ref[dr], second)
        t_ref[NA_DR + dr] = jnp.where(left, base_ref[dr], neg)
        t_ref[2 * NA_DR + dr] = jnp.where(left, neg, base_ref[dr])
    t_ref[3 * NA_DR] = neg


def _na_kernel(rpb_ref, q_ref, k_ref, v_ref, gq_ref, gk_ref, wsrc_ref, o_ref, wdst_ref,
               kt_ref, va_ref, t_ref, base_ref):
    @pl.when(pl.program_id(1) == 0)
    def _():
        _na_build_tables(pl.program_id(0), rpb_ref, t_ref, base_ref)

    s_len = q_ref.shape[1]
    rows = s_len // GRID_W
    step_tok = NA_QROWS * GRID_W
    win_tok = NA_WIN * GRID_W
    n_tiles = NA_WIN // 2
    gq = gq_ref[...] * gk_ref[...] * (NA_HEAD_DIM ** -0.5 * LOG2E)
    wdst_ref[...] = wsrc_ref[...].astype(BF16)
    va_ref[:, :NA_HEAD_DIM] = v_ref[0]
    va_ref[:, NA_HEAD_DIM:] = jnp.ones((s_len, NA_HEAD_DIM), BF16)

    def knorm(c, _):
        sl = pl.ds(pl.multiple_of(c * LANE, LANE), LANE)
        kt_ref[c] = _unit_rms(k_ref[0, sl, :].astype(F32)).astype(BF16).T
        return 0

    lax.fori_loop(0, s_len // LANE, knorm, 0, unroll=True)

    def step(i, _):
        r0 = NA_QROWS * i
        ws = jnp.clip(r0 - NA_MAX_ROWS // 2, 0, rows - NA_WIN)
        wp = ws // 2
        qsl = pl.ds(pl.multiple_of(i * step_tok, step_tok), step_tok)
        wsl = pl.ds(pl.multiple_of(ws * GRID_W, LANE), win_tok)
        qn = _rms(q_ref[0, qsl, :].astype(F32), gq).astype(BF16)
        kwin = jnp.concatenate([kt_ref[wp + t] for t in range(n_tiles)], axis=1)
        s = jnp.dot(qn, kwin, preferred_element_type=F32)
        bias_rows = []
        for qr in range(NA_QROWS):
            r = r0 + qr
            rs = jnp.clip(r - NA_MAX_ROWS // 2, 0, rows - NA_MAX_ROWS)
            tiles = []
            for t in range(n_tiles):
                ka = ws + 2 * t
                dr = ka - r + (NA_MAX_ROWS - 1)
                va = (ka >= rs) & (ka < rs + NA_MAX_ROWS)
                vb = (ka + 1 >= rs) & (ka + 1 < rs + NA_MAX_ROWS)
                idx = jnp.where(va, jnp.where(vb, dr, NA_DR + dr),
                                jnp.where(vb, 2 * NA_DR + dr + 1, 3 * NA_DR))
                tiles.append(t_ref[idx])
            bias_rows.append(jnp.concatenate(tiles, axis=1))
        s = s + jnp.concatenate(bias_rows, axis=0)
        e = jnp.exp2(s - jnp.max(s, axis=-1, keepdims=True))
        o = jnp.dot(e.astype(BF16), va_ref[wsl, :], preferred_element_type=F32)
        o_ref[qsl, :] = (o[:, :NA_HEAD_DIM] / o[:, NA_HEAD_DIM:]).astype(BF16)
        return 0

    lax.fori_loop(0, rows // NA_QROWS, step, 0, unroll=True)


def _na_attention(proj, rpb, gq, gk, w_side, batch, s_len):
    t = batch * s_len
    g_spec = pl.BlockSpec((1, NA_HEAD_DIM), lambda h, b: (0, 0))
    side_spec = pl.BlockSpec((w_side.shape[0] // (NA_HEADS * batch), w_side.shape[1]),
                             lambda h, b: (h * batch + b, 0))
    return pl.pallas_call(
        _na_kernel,
        grid=(NA_HEADS, batch),
        in_specs=[
            pl.BlockSpec(memory_space=pltpu.SMEM),
            pl.BlockSpec((1, s_len, LANE), lambda h, b: (SLAB_NA_Q + h, b, 0)),
            pl.BlockSpec((1, s_len, LANE), lambda h, b: (SLAB_NA_K + h, b, 0)),
            pl.BlockSpec((1, s_len, LANE), lambda h, b: (SLAB_NA_V + h, b, 0)),
            g_spec, g_spec, side_spec,
        ],
        out_specs=[pl.BlockSpec((s_len, NA_HEAD_DIM), lambda h, b: (b, h)), side_spec],
        out_shape=[jax.ShapeDtypeStruct((t, NA_HEADS * NA_HEAD_DIM), BF16),
                   jax.ShapeDtypeStruct(w_side.shape, BF16)],
        scratch_shapes=[pltpu.VMEM((s_len // LANE, NA_HEAD_DIM, LANE), BF16),
                        pltpu.VMEM((s_len, 2 * NA_HEAD_DIM), BF16),
                        pltpu.VMEM((NA_TBL, GRID_W, LANE), F32),
                        pltpu.VMEM((NA_DR, GRID_W, LANE), F32)],
        compiler_params=_params(("arbitrary", "arbitrary"), 56),
        name="na_attention",
    )(rpb.reshape(-1), proj, proj, proj, gq.reshape(1, -1), gk.reshape(1, -1), w_side)


def _log_sigmoid(x):
    return -(jnp.maximum(-x, 0.0) + jnp.log1p(jnp.exp(-jnp.abs(x))))


def _ret_kernel(q_ref, k_ref, v_ref, g_ref, cos_ref, sin_ref, lf_ref, lb_ref, gn_ref, wsrc_ref,
                o_ref, wdst_ref, qr_ref, kt_ref, d_ref, qdf_ref, qdb_ref, kv_ref, s_ref):
    s_len = q_ref.shape[1]
    c = RET_BLOCK
    nb = s_len // c
    dk = RET_QK_DIM
    half = dk // 2
    wdst_ref[...] = wsrc_ref[...].astype(BF16)

    lgf = _log_sigmoid(lf_ref[0][:, :1])
    lgb = _log_sigmoid(lb_ref[0][:, :1])
    ic = lax.broadcasted_iota(jnp.int32, (c, 1), 0).astype(F32)
    jr = lax.broadcasted_iota(jnp.int32, (1, c), 1).astype(F32)
    diff = ic - jr
    scale = dk ** -0.5
    d_ref[...] = jnp.exp(jnp.where(diff >= 0, lgf, lgb) * jnp.abs(diff)) * scale
    qdf_ref[...] = jnp.broadcast_to(jnp.exp(lgf * (ic + 1.0)), (c, dk))
    qdb_ref[...] = jnp.broadcast_to(jnp.exp(lgb * (c - ic)), (c, dk))
    kdf = jnp.exp(lgf * (c - 1.0 - jr)) * scale
    kdb = jnp.exp(lgb * jr) * scale
    cd_f = jnp.exp(lgf * c)
    cd_b = jnp.exp(lgb * c)

    def block_v(sl):
        return jnp.concatenate([v_ref[0, sl, :], v_ref[1, sl, :]], axis=1)

    def prep(n, _):
        sl = pl.ds(pl.multiple_of(n * c, c), c)
        cos = cos_ref[sl, :]
        sin = sin_ref[sl, :]
        q = q_ref[0, sl, :].astype(F32)
        k = k_ref[0, sl, :].astype(F32)
        qr_ref[sl, :] = (q * cos + pltpu.roll(q, half, 1) * sin).astype(BF16)
        kt = (k * cos + pltpu.roll(k, half, 1) * sin).T
        kt_ref[n] = kt.astype(BF16)
        lhs = jnp.concatenate([(kt * kdf).astype(BF16), (kt * kdb).astype(BF16)], axis=0)
        kv_ref[n] = jnp.dot(lhs, block_v(sl), preferred_element_type=F32)
        return 0

    lax.fori_loop(0, nb, prep, 0, unroll=True)

    def scan_f(n, sf):
        s_ref[n, :dk, :] = sf.astype(BF16)
        return cd_f * sf + kv_ref[n, :dk, :]

    def scan_b(t, sb):
        n = nb - 1 - t
        s_ref[n, dk:, :] = sb.astype(BF16)
        return cd_b * sb + kv_ref[n, dk:, :]

    zero = jnp.zeros((dk, RET_V_DIM), F32)

    def scan(n, carry):
        return scan_f(n, carry[0]), scan_b(n, carry[1])

    lax.fori_loop(0, nb, scan, (zero, zero), unroll=True)

    gn = gn_ref[0]

    def out(n, _):
        sl = pl.ds(pl.multiple_of(n * c, c), c)
        q = qr_ref[sl, :]
        qf32 = q.astype(F32)
        a = jnp.dot(q, kt_ref[n], preferred_element_type=F32) * d_ref[...]
        lhs = jnp.concatenate([a.astype(BF16), (qf32 * qdf_ref[...]).astype(BF16),
                               (qf32 * qdb_ref[...]).astype(BF16)], axis=1)
        rhs = jnp.concatenate([block_v(sl), s_ref[n]], axis=0)
        o = jnp.dot(lhs, rhs, preferred_element_type=F32)
        mu = jnp.mean(o, axis=-1, keepdims=True)
        oc = o - mu
        y = oc * lax.rsqrt(jnp.mean(oc * oc, axis=-1, keepdims=True) + EPS) * gn
        gate = jnp.concatenate([g_ref[0, sl, :], g_ref[1, sl, :]], axis=1).astype(F32)
        o_ref[sl, :] = (y * gate * _sigmoid(gate)).astype(BF16)
        return 0

    lax.fori_loop(0, nb, out, 0, unroll=True)


def _retention(proj, cos2, sin2, lf, lb, gn, w_side, batch, s_len):
    t = batch * s_len
    nb = s_len // RET_BLOCK
    dec_spec = pl.BlockSpec((1, 1, LANE), lambda b, h: (h, 0, 0))
    side_spec = pl.BlockSpec((w_side.shape[0] // (RET_HEADS * batch), w_side.shape[1]),
                             lambda b, h: (b * RET_HEADS + h, 0))
    return pl.pallas_call(
        _ret_kernel,
        grid=(batch, RET_HEADS),
        in_specs=[
            pl.BlockSpec((1, s_len, LANE), lambda b, h: (SLAB_RQ + h, b, 0)),
            pl.BlockSpec((1, s_len, LANE), lambda b, h: (SLAB_RK + h, b, 0)),
            pl.BlockSpec((2, s_len, LANE), lambda b, h: (SLAB_RV // 2 + h, b, 0)),
            pl.BlockSpec((2, s_len, LANE), lambda b, h: (SLAB_RG // 2 + h, b, 0)),
            pl.BlockSpec((s_len, RET_QK_DIM), lambda b, h: (0, 0)),
            pl.BlockSpec((s_len, RET_QK_DIM), lambda b, h: (0, 0)),
            dec_spec, dec_spec,
            pl.BlockSpec((1, 1, RET_V_DIM), lambda b, h: (h, 0, 0)),
            side_spec,
        ],
        out_specs=[pl.BlockSpec((s_len, RET_V_DIM), lambda b, h: (b, h)), side_spec],
        out_shape=[jax.ShapeDtypeStruct((t, RET_HEADS * RET_V_DIM), BF16),
                   jax.ShapeDtypeStruct(w_side.shape, BF16)],
        scratch_shapes=[
            pltpu.VMEM((s_len, RET_QK_DIM), BF16),
            pltpu.VMEM((nb, RET_QK_DIM, RET_BLOCK), BF16),
            pltpu.VMEM((RET_BLOCK, RET_BLOCK), F32),
            pltpu.VMEM((RET_BLOCK, RET_QK_DIM), F32),
            pltpu.VMEM((RET_BLOCK, RET_QK_DIM), F32),
            pltpu.VMEM((nb, 2 * RET_QK_DIM, RET_V_DIM), F32),
            pltpu.VMEM((nb, 2 * RET_QK_DIM, RET_V_DIM), BF16),
        ],
        compiler_params=_params(("parallel", "parallel"), 56),
        name="retention",
    )(proj, proj, proj, proj, cos2, sin2, lf, lb, gn.reshape(RET_HEADS, 1, RET_V_DIM), w_side)


def _xa_kernel(q_ref, k_ref, v_ref, gq_ref, gk_ref, o_ref):
    s_len = q_ref.shape[1]
    tq = 512
    gq = gq_ref[...] * gk_ref[...] * (XA_HEAD_DIM ** -0.5 * LOG2E)
    k = jnp.concatenate([k_ref[0], k_ref[1]], axis=1).astype(F32)
    kn = _unit_rms(k).astype(BF16)
    v = jnp.concatenate([v_ref[0], v_ref[1]], axis=1)

    def body(i, _):
        sl = pl.ds(pl.multiple_of(i * tq, tq), tq)
        q = jnp.concatenate([q_ref[0, sl, :], q_ref[1, sl, :]], axis=1).astype(F32)
        qn = _rms(q, gq).astype(BF16)
        s = lax.dot_general(qn, kn, (((1,), (1,)), ((), ())), preferred_element_type=F32)
        e = jnp.exp2(s - jnp.max(s, axis=-1, keepdims=True))
        l = jnp.sum(e, axis=-1, keepdims=True)
        o = jnp.dot(e.astype(BF16), v, preferred_element_type=F32)
        o_ref[sl, :] = (o / l).astype(BF16)
        return 0

    lax.fori_loop(0, s_len // tq, body, 0, unroll=True)


def _mem_xattn(proj, mkv, gq, gk, batch, s_len, n_mem):
    t = batch * s_len
    g_spec = pl.BlockSpec((1, XA_HEAD_DIM), lambda b, h: (0, 0))
    return pl.pallas_call(
        _xa_kernel,
        grid=(batch, XA_HEADS),
        in_specs=[
            pl.BlockSpec((2, s_len, LANE), lambda b, h: (SLAB_XQ // 2 + h, b, 0)),
            pl.BlockSpec((2, n_mem, LANE), lambda b, h: (h, b, 0)),
            pl.BlockSpec((2, n_mem, LANE), lambda b, h: (XA_HEADS + h, b, 0)),
            g_spec, g_spec,
        ],
        out_specs=pl.BlockSpec((s_len, XA_HEAD_DIM), lambda b, h: (b, h)),
        out_shape=jax.ShapeDtypeStruct((t, XA_HEADS * XA_HEAD_DIM), BF16),
        compiler_params=_params(("parallel", "parallel"), 56),
        name="mem_xattn",
    )(proj, mkv, mkv, gq.reshape(1, -1), gk.reshape(1, -1))


def _merge_kernel(ona_ref, oret_ref, omem_ref, wna_ref, wret_ref, wmem_ref,
                  gna_ref, gret_ref, gmem_ref, o_ref):
    y_na = jnp.dot(ona_ref[...], wna_ref[...], preferred_element_type=F32)
    y_ret = jnp.dot(oret_ref[...], wret_ref[...], preferred_element_type=F32)
    y_mem = jnp.dot(omem_ref[...], wmem_ref[...], preferred_element_type=F32)
    for j in range(gna_ref.shape[0]):
        cs = slice(j * LANE, (j + 1) * LANE)
        o_ref[:, cs] = (_sigmoid(gna_ref[j].astype(F32)) * y_na[:, cs]
                        + _sigmoid(gret_ref[j].astype(F32)) * y_ret[:, cs]
                        + _sigmoid(gmem_ref[j].astype(F32)) * y_mem[:, cs]).astype(BF16)


def _merge(o_na, o_ret, o_mem, w_na, w_ret, w_mem, proj, tm, tn):
    t = o_na.shape[0]
    d = w_na.shape[1]
    ns = tn // LANE

    def lhs_spec(a):
        return pl.BlockSpec((tm, a.shape[1]), lambda i, j: (i, 0))

    def w_spec(w):
        return pl.BlockSpec((w.shape[0], tn), lambda i, j: (0, j))

    def gate_spec(first):
        return pl.BlockSpec((ns, tm, LANE), lambda i, j: (first // ns + j, i, 0))

    return pl.pallas_call(
        _merge_kernel,
        grid=(t // tm, d // tn),
        in_specs=[lhs_spec(o_na), lhs_spec(o_ret), lhs_spec(o_mem),
                  w_spec(w_na), w_spec(w_ret), w_spec(w_mem),
                  gate_spec(SLAB_G_NA), gate_spec(SLAB_G_RET), gate_spec(SLAB_G_MEM)],
        out_specs=pl.BlockSpec((tm, tn), lambda i, j: (i, j)),
        out_shape=jax.ShapeDtypeStruct((t, d), BF16),
        compiler_params=_params(("parallel", "arbitrary"), 58),
        name="merge",
    )(o_na, o_ret, o_mem, w_na, w_ret, w_mem, proj, proj, proj)


def _out_proj_kernel(m_ref, w_ref, x_ref, o_ref, wb_ref):
    @pl.when(pl.program_id(0) == 0)
    def _():
        wb_ref[...] = w_ref[...].astype(BF16)

    o_ref[...] = x_ref[...] + jnp.dot(m_ref[...], wb_ref[...], preferred_element_type=F32)


def _out_proj(merged, w, x2d, tm):
    t, k = merged.shape
    d = w.shape[1]
    return pl.pallas_call(
        _out_proj_kernel,
        grid=(t // tm,),
        in_specs=[pl.BlockSpec((tm, k), lambda i: (i, 0)),
                  pl.BlockSpec((k, d), lambda i: (0, 0), pipeline_mode=pl.Buffered(1)),
                  pl.BlockSpec((tm, d), lambda i: (i, 0))],
        out_specs=pl.BlockSpec((tm, d), lambda i: (i, 0)),
        out_shape=jax.ShapeDtypeStruct((t, d), F32),
        scratch_shapes=[pltpu.VMEM((k, d), BF16)],
        compiler_params=_params(("arbitrary",), 56),
        name="out_proj",
    )(merged, w, x2d)


def _ffn_kernel(x_ref, g_ref, w1_ref, w2_ref, o_ref, h_ref):
    @pl.when(pl.program_id(1) == 0)
    def _():
        x = x_ref[...]
        h_ref[...] = _rms(x, g_ref[...]).astype(BF16)
        o_ref[...] = x

    a = jnp.maximum(jnp.dot(h_ref[...], w1_ref[...], preferred_element_type=F32), 0.0)
    o_ref[...] += jnp.dot((a * a).astype(BF16), w2_ref[...], preferred_element_type=F32)


def _ffn(x1, g, w1, w2, tm, tf):
    t, d = x1.shape
    dff = w1.shape[1]
    return pl.pallas_call(
        _ffn_kernel,
        grid=(t // tm, dff // tf),
        in_specs=[pl.BlockSpec((tm, d), lambda i, f: (i, 0)),
                  pl.BlockSpec((1, d), lambda i, f: (0, 0)),
                  pl.BlockSpec((d, tf), lambda i, f: (0, f)),
                  pl.BlockSpec((tf, d), lambda i, f: (f, 0))],
        out_specs=pl.BlockSpec((tm, d), lambda i, f: (i, 0)),
        out_shape=jax.ShapeDtypeStruct((t, d), F32),
        scratch_shapes=[pltpu.VMEM((tm, d), BF16)],
        compiler_params=_params(("parallel", "arbitrary"), 48),
        name="ffn",
    )(x1, g.reshape(1, d), w1, w2)


def _rope_tables(s_len):
    half = RET_QK_DIM // 2
    inv = np.power(np.float64(ROPE_BASE), -np.arange(half, dtype=np.float64) / half)
    ang = np.arange(s_len, dtype=np.float64)[:, None] * inv[None, :]
    cos, sin = np.cos(ang), np.sin(ang)
    return (jnp.asarray(np.concatenate([cos, cos], axis=1), F32),
            jnp.asarray(np.concatenate([-sin, sin], axis=1), F32))


def kernel(x, mem, norm_mix_g, w_in, na_q_norm_g, na_k_norm_g, na_rpb, ret_decay_logit_fwd, ret_decay_logit_bwd, ret_gn_g, mem_norm_g, w_mem_kv, xa_q_norm_g, xa_k_norm_g, w_br_na, w_br_ret, w_br_mem, w_out, norm_ffn_g, w_ff1, w_ff2):
    batch, s_len, d = x.shape
    n_mem = mem.shape[1]
    t = batch * s_len
    depth = w_in.shape[0]
    tm = min(1024, t)
    cos2, sin2 = _rope_tables(s_len)
    x2d = x.reshape(t, d)
    mem2d = mem.reshape(batch * n_mem, d)

    def lane_bcast(v):
        return jnp.broadcast_to(v.astype(F32)[:, None, None], (v.shape[0], 1, LANE))

    tn_in = 1024
    gm, gn = t // tm, w_in.shape[2] // tn_in

    def side_job(w):
        r, c = w.shape
        if r % (gm * gn * 16) == 0:
            return w, (r // (gm * gn), c), lambda i, j: (i * gn + j, 0)
        assert r % (gn * 16) == 0 and c % (gm * LANE) == 0, (w.shape, gm, gn)
        return w, (r // gn, c // gm), lambda i, j: (j, i)

    for l in range(depth):
        sides = [side_job(w) for w in (w_br_na[l], w_br_ret[l], w_br_mem[l])]
        proj, (w_na_bf, w_ret_bf, w_mem_bf) = _norm_proj(
            x2d, norm_mix_g[l], w_in[l], tm, tn_in, side=sides)
        mkv, _ = _norm_proj(mem2d, mem_norm_g[l], w_mem_kv[l], batch * n_mem, 1024)

        o_mem = _mem_xattn(proj, mkv, xa_q_norm_g[l], xa_k_norm_g[l], batch, s_len, n_mem)
        o_na, w_ff1_bf = _na_attention(proj, na_rpb[l], na_q_norm_g[l], na_k_norm_g[l], w_ff1[l], batch, s_len)
        o_ret, w_ff2_bf = _retention(proj, cos2, sin2, lane_bcast(ret_decay_logit_fwd[l]),
                                     lane_bcast(ret_decay_logit_bwd[l]), ret_gn_g[l], w_ff2[l], batch, s_len)

        merged = _merge(o_na, o_ret, o_mem, w_na_bf, w_ret_bf, w_mem_bf, proj, tm, 512)
        x1 = _out_proj(merged, w_out[l], x2d, min(512, t))
        x2d = _ffn(x1, norm_ffn_g[l], w_ff1_bf, w_ff2_bf, min(512, t), 1024)
    return x2d.reshape(batch, s_len, d)
```

```python
import functools

import jax
import jax.numpy as jnp
import numpy as np
from jax import lax
from jax.experimental import pallas as pl
from jax.experimental.pallas import tpu as pltpu

F32 = jnp.float32
BF16 = jnp.bfloat16

LANE = 128
EPS = 1e-6
NEG = -1e30
LOG2E = 1.4426950408889634

GRID_W = 64
NA_HEADS = 8
NA_HEAD_DIM = 128
NA_MAX_ROWS = 8
NA_COLS = 16
NA_DR = 2 * NA_MAX_ROWS - 1
NA_DC = 2 * NA_COLS - 1
NA_TBL = 3 * NA_DR + 1
NA_QROWS = 4
NA_WIN = NA_QROWS + NA_MAX_ROWS

RET_HEADS = 8
RET_QK_DIM = 128
RET_V_DIM = 256
RET_BLOCK = 512
ROPE_BASE = 10000.0

XA_HEADS = 4
XA_HEAD_DIM = 256

SLAB_NA_Q, SLAB_NA_K, SLAB_NA_V = 0, 8, 16
SLAB_RQ, SLAB_RK, SLAB_RV, SLAB_RG = 24, 32, 40, 56
SLAB_XQ = 72
SLAB_G_NA, SLAB_G_RET, SLAB_G_MEM = 80, 96, 112


def _params(sem, vmem_mib):
    return pltpu.CompilerParams(dimension_semantics=sem, vmem_limit_bytes=vmem_mib * 2**20)


def _rms(x, g):
    return x * lax.rsqrt(jnp.mean(x * x, axis=-1, keepdims=True) + EPS) * g


def _unit_rms(x):
    return x * lax.rsqrt(jnp.mean(x * x, axis=-1, keepdims=True) + EPS)


def _sigmoid(x):
    return 0.5 * jnp.tanh(0.5 * x) + 0.5


def _norm_proj_kernel(n_side, x_ref, g_ref, w_ref, *refs):
    side_in, o_ref = refs[:n_side], refs[n_side]
    side_out, hn_ref = refs[n_side + 1:2 * n_side + 1], refs[2 * n_side + 1]

    @pl.when(pl.program_id(1) == 0)
    def _():
        hn_ref[...] = _rms(x_ref[...], g_ref[...]).astype(BF16)

    acc = jnp.dot(hn_ref[...], w_ref[...].astype(BF16), preferred_element_type=F32)
    for s in range(o_ref.shape[0]):
        o_ref[s] = acc[:, s * LANE:(s + 1) * LANE].astype(BF16)
    for src, dst in zip(side_in, side_out):
        dst[...] = src[...].astype(BF16)


def _norm_proj(x2d, g, w, tm, tn, side=()):
    m, k = x2d.shape
    n = w.shape[1]
    side_specs = [pl.BlockSpec(blk, imap) for _, blk, imap in side]
    out = pl.pallas_call(
        functools.partial(_norm_proj_kernel, len(side)),
        grid=(m // tm, n // tn),
        in_specs=[
            pl.BlockSpec((tm, k), lambda i, j: (i, 0)),
            pl.BlockSpec((1, k), lambda i, j: (0, 0)),
            pl.BlockSpec((k, tn), lambda i, j: (0, j)),
        ] + side_specs,
        out_specs=[pl.BlockSpec((tn // LANE, tm, LANE), lambda i, j: (j, i, 0))] + side_specs,
        out_shape=[jax.ShapeDtypeStruct((n // LANE, m, LANE), BF16)]
        + [jax.ShapeDtypeStruct(a.shape, BF16) for a, _, _ in side],
        scratch_shapes=[pltpu.VMEM((tm, k), BF16)],
        compiler_params=_params(("parallel", "arbitrary"), 58),
        name="norm_proj",
    )(x2d, g.reshape(1, k), w, *[a for a, _, _ in side])
    return out[0], out[1:]


def _na_build_tables(h, rpb_ref, t_ref, base_ref):
    qc = lax.broadcasted_iota(jnp.int32, (GRID_W, LANE), 0)
    lane = lax.broadcasted_iota(jnp.int32, (GRID_W, LANE), 1)
    kc = lane & (GRID_W - 1)
    d = jnp.clip(kc - qc, -(NA_COLS - 1), NA_COLS - 1) + (NA_COLS - 1)
    cs = jnp.clip(qc - NA_COLS // 2, 0, GRID_W - NA_COLS)
    col_ok = (kc >= cs) & (kc < cs + NA_COLS)
    left = lane < GRID_W
    neg = jnp.full((GRID_W, LANE), NEG, F32)

    def body(dr, _):
        base = (h * NA_DR + dr) * NA_DC
        val = jnp.zeros((GRID_W, LANE), F32)
        for dd in range(NA_DC):
            val = jnp.where(d == dd, rpb_ref[base + dd], val)
        base_ref[dr] = jnp.where(col_ok, val * LOG2E, NEG)
        return 0

    lax.fori_loop(0, NA_DR, body, 0)
    for dr in range(NA_DR):
        second = base_ref[dr + 1] if dr + 1 < NA_DR else neg
        t_ref[dr] = jnp.where(left, base_ref[dr], second)
        t_ref[NA_DR + dr] = jnp.where(left, base_ref[dr], neg)
        t_ref[2 * NA_DR + dr] = jnp.where(left, neg, base_ref[dr])
    t_ref[3 * NA_DR] = neg


def _na_kernel(rpb_ref, q_ref, k_ref, v_ref, gq_ref, gk_ref, wsrc_ref, o_ref, wdst_ref,
               kt_ref, va_ref, t_ref, base_ref):
    @pl.when(pl.program_id(1) == 0)
    def _():
        _na_build_tables(pl.program_id(0), rpb_ref, t_ref, base_ref)

    s_len = q_ref.shape[1]
    rows = s_len // GRID_W
    step_tok = NA_QROWS * GRID_W
    win_tok = NA_WIN * GRID_W
    n_tiles = NA_WIN // 2
    gq = gq_ref[...] * gk_ref[...] * (NA_HEAD_DIM ** -0.5 * LOG2E)
    wdst_ref[...] = wsrc_ref[...].astype(BF16)
    va_ref[:, :NA_HEAD_DIM] = v_ref[0]
    va_ref[:, NA_HEAD_DIM:] = jnp.ones((s_len, NA_HEAD_DIM), BF16)

    def knorm(c, _):
        sl = pl.ds(pl.multiple_of(c * LANE, LANE), LANE)
        kt_ref[c] = _unit_rms(k_ref[0, sl, :].astype(F32)).astype(BF16).T
        return 0

    lax.fori_loop(0, s_len // LANE, knorm, 0, unroll=True)

    def step(i, _):
        r0 = NA_QROWS * i
        ws = jnp.clip(r0 - NA_MAX_ROWS // 2, 0, rows - NA_WIN)
        wp = ws // 2
        qsl = pl.ds(pl.multiple_of(i * step_tok, step_tok), step_tok)
        wsl = pl.ds(pl.multiple_of(ws * GRID_W, LANE), win_tok)
        qn = _rms(q_ref[0, qsl, :].astype(F32), gq).astype(BF16)
        kwin = jnp.concatenate([kt_ref[wp + t] for t in range(n_tiles)], axis=1)
        s = jnp.dot(qn, kwin, preferred_element_type=F32)
        bias_rows = []
        for qr in range(NA_QROWS):
            r = r0 + qr
            rs = jnp.clip(r - NA_MAX_ROWS // 2, 0, rows - NA_MAX_ROWS)
            tiles = []
            for t in range(n_tiles):
                ka = ws + 2 * t
                dr = ka - r + (NA_MAX_ROWS - 1)
                va = (ka >= rs) & (ka < rs + NA_MAX_ROWS)
                vb = (ka + 1 >= rs) & (ka + 1 < rs + NA_MAX_ROWS)
                idx = jnp.where(va, jnp.where(vb, dr, NA_DR + dr),
                                jnp.where(vb, 2 * NA_DR + dr + 1, 3 * NA_DR))
                tiles.append(t_ref[idx])
            bias_rows.append(jnp.concatenate(tiles, axis=1))
        s = s + jnp.concatenate(bias_rows, axis=0)
        e = jnp.exp2(s - jnp.max(s, axis=-1, keepdims=True))
        o = jnp.dot(e.astype(BF16), va_ref[wsl, :], preferred_element_type=F32)
        o_ref[qsl, :] = (o[:, :NA_HEAD_DIM] / o[:, NA_HEAD_DIM:]).astype(BF16)
        return 0

    lax.fori_loop(0, rows // NA_QROWS, step, 0, unroll=True)


def _na_attention(proj, rpb, gq, gk, w_side, batch, s_len):
    t = batch * s_len
    g_spec = pl.BlockSpec((1, NA_HEAD_DIM), lambda h, b: (0, 0))
    side_spec = pl.BlockSpec((w_side.shape[0] // (NA_HEADS * batch), w_side.shape[1]),
                             lambda h, b: (h * batch + b, 0))
    return pl.pallas_call(
        _na_kernel,
        grid=(NA_HEADS, batch),
        in_specs=[
            pl.BlockSpec(memory_space=pltpu.SMEM),
            pl.BlockSpec((1, s_len, LANE), lambda h, b: (SLAB_NA_Q + h, b, 0)),
            pl.BlockSpec((1, s_len, LANE), lambda h, b: (SLAB_NA_K + h, b, 0)),
            pl.BlockSpec((1, s_len, LANE), lambda h, b: (SLAB_NA_V + h, b, 0)),
            g_spec, g_spec, side_spec,
        ],
        out_specs=[pl.BlockSpec((s_len, NA_HEAD_DIM), lambda h, b: (b, h)), side_spec],
        out_shape=[jax.ShapeDtypeStruct((t, NA_HEADS * NA_HEAD_DIM), BF16),
                   jax.ShapeDtypeStruct(w_side.shape, BF16)],
        scratch_shapes=[pltpu.VMEM((s_len // LANE, NA_HEAD_DIM, LANE), BF16),
                        pltpu.VMEM((s_len, 2 * NA_HEAD_DIM), BF16),
                        pltpu.VMEM((NA_TBL, GRID_W, LANE), F32),
                        pltpu.VMEM((NA_DR, GRID_W, LANE), F32)],
        compiler_params=_params(("arbitrary", "arbitrary"), 56),
        name="na_attention",
    )(rpb.reshape(-1), proj, proj, proj, gq.reshape(1, -1), gk.reshape(1, -1), w_side)


def _log_sigmoid(x):
    return -(jnp.maximum(-x, 0.0) + jnp.log1p(jnp.exp(-jnp.abs(x))))


def _ret_kernel(q_ref, k_ref, v_ref, g_ref, cos_ref, sin_ref, lf_ref, lb_ref, gn_ref, wsrc_ref,
                o_ref, wdst_ref, qr_ref, kt_ref, d_ref, qdf_ref, qdb_ref, kv_ref, s_ref):
    s_len = q_ref.shape[1]
    c = RET_BLOCK
    nb = s_len // c
    dk = RET_QK_DIM
    half = dk // 2
    wdst_ref[...] = wsrc_ref[...].astype(BF16)

    lgf = _log_sigmoid(lf_ref[0][:, :1])
    lgb = _log_sigmoid(lb_ref[0][:, :1])
    ic = lax.broadcasted_iota(jnp.int32, (c, 1), 0).astype(F32)
    jr = lax.broadcasted_iota(jnp.int32, (1, c), 1).astype(F32)
    diff = ic - jr
    scale = dk ** -0.5
    d_ref[...] = jnp.exp(jnp.where(diff >= 0, lgf, lgb) * jnp.abs(diff)) * scale
    qdf_ref[...] = jnp.broadcast_to(jnp.exp(lgf * (ic + 1.0)), (c, dk))
    qdb_ref[...] = jnp.broadcast_to(jnp.exp(lgb * (c - ic)), (c, dk))
    kdf = jnp.exp(lgf * (c - 1.0 - jr)) * scale
    kdb = jnp.exp(lgb * jr) * scale
    cd_f = jnp.exp(lgf * c)
    cd_b = jnp.exp(lgb * c)

    def block_v(sl):
        return jnp.concatenate([v_ref[0, sl, :], v_ref[1, sl, :]], axis=1)

    def prep(n, _):
        sl = pl.ds(pl.multiple_of(n * c, c), c)
        cos = cos_ref[sl, :]
        sin = sin_ref[sl, :]
        q = q_ref[0, sl, :].astype(F32)
        k = k_ref[0, sl, :].astype(F32)
        qr_ref[sl, :] = (q * cos + pltpu.roll(q, half, 1) * sin).astype(BF16)
        kt = (k * cos + pltpu.roll(k, half, 1) * sin).T
        kt_ref[n] = kt.astype(BF16)
        lhs = jnp.concatenate([(kt * kdf).astype(BF16), (kt * kdb).astype(BF16)], axis=0)
        kv_ref[n] = jnp.dot(lhs, block_v(sl), preferred_element_type=F32)
        return 0

    lax.fori_loop(0, nb, prep, 0, unroll=True)

    def scan_f(n, sf):
        s_ref[n, :dk, :] = sf.astype(BF16)
        return cd_f * sf + kv_ref[n, :dk, :]

    def scan_b(t, sb):
        n = nb - 1 - t
        s_ref[n, dk:, :] = sb.astype(BF16)
        return cd_b * sb + kv_ref[n, dk:, :]

    zero = jnp.zeros((dk, RET_V_DIM), F32)

    def scan(n, carry):
        return scan_f(n, carry[0]), scan_b(n, carry[1])

    lax.fori_loop(0, nb, scan, (zero, zero), unroll=True)

    gn = gn_ref[0]

    def out(n, _):
        sl = pl.ds(pl.multiple_of(n * c, c), c)
        q = qr_ref[sl, :]
        qf32 = q.astype(F32)
        a = jnp.dot(q, kt_ref[n], preferred_element_type=F32) * d_ref[...]
        lhs = jnp.concatenate([a.astype(BF16), (qf32 * qdf_ref[...]).astype(BF16),
                               (qf32 * qdb_ref[...]).astype(BF16)], axis=1)
        rhs = jnp.concatenate([block_v(sl), s_ref[n]], axis=0)
        o = jnp.dot(lhs, rhs, preferred_element_type=F32)
        mu = jnp.mean(o, axis=-1, keepdims=True)
        oc = o - mu
        y = oc * lax.rsqrt(jnp.mean(oc * oc, axis=-1, keepdims=True) + EPS) * gn
        gate = jnp.concatenate([g_ref[0, sl, :], g_ref[1, sl, :]], axis=1).astype(F32)
        o_ref[sl, :] = (y * gate * _sigmoid(gate)).astype(BF16)
        return 0

    lax.fori_loop(0, nb, out, 0, unroll=True)


def _retention(proj, cos2, sin2, lf, lb, gn, w_side, batch, s_len):
    t = batch * s_len
    nb = s_len // RET_BLOCK
    dec_spec = pl.BlockSpec((1, 1, LANE), lambda b, h: (h, 0, 0))
    side_spec = pl.BlockSpec((w_side.shape[0] // (RET_HEADS * batch), w_side.shape[1]),
                             lambda b, h: (b * RET_HEADS + h, 0))
    return pl.pallas_call(
        _ret_kernel,
        grid=(batch, RET_HEADS),
        in_specs=[
            pl.BlockSpec((1, s_len, LANE), lambda b, h: (SLAB_RQ + h, b, 0)),
            pl.BlockSpec((1, s_len, LANE), lambda b, h: (SLAB_RK + h, b, 0)),
            pl.BlockSpec((2, s_len, LANE), lambda b, h: (SLAB_RV // 2 + h, b, 0)),
            pl.BlockSpec((2, s_len, LANE), lambda b, h: (SLAB_RG // 2 + h, b, 0)),
            pl.BlockSpec((s_len, RET_QK_DIM), lambda b, h: (0, 0)),
            pl.BlockSpec((s_len, RET_QK_DIM), lambda b, h: (0, 0)),
            dec_spec, dec_spec,
            pl.BlockSpec((1, 1, RET_V_DIM), lambda b, h: (h, 0, 0)),
            side_spec,
        ],
        out_specs=[pl.BlockSpec((s_len, RET_V_DIM), lambda b, h: (b, h)), side_spec],
        out_shape=[jax.ShapeDtypeStruct((t, RET_HEADS * RET_V_DIM), BF16),
                   jax.ShapeDtypeStruct(w_side.shape, BF16)],
        scratch_shapes=[
            pltpu.VMEM((s_len, RET_QK_DIM), BF16),
            pltpu.VMEM((nb, RET_QK_DIM, RET_BLOCK), BF16),
            pltpu.VMEM((RET_BLOCK, RET_BLOCK), F32),
            pltpu.VMEM((RET_BLOCK, RET_QK_DIM), F32),
            pltpu.VMEM((RET_BLOCK, RET_QK_DIM), F32),
            pltpu.VMEM((nb, 2 * RET_QK_DIM, RET_V_DIM), F32),
            pltpu.VMEM((nb, 2 * RET_QK_DIM, RET_V_DIM), BF16),
        ],
        compiler_params=_params(("parallel", "parallel"), 56),
        name="retention",
    )(proj, proj, proj, proj, cos2, sin2, lf, lb, gn.reshape(RET_HEADS, 1, RET_V_DIM), w_side)


def _xa_kernel(q_ref, k_ref, v_ref, gq_ref, gk_ref, o_ref):
    s_len = q_ref.shape[1]
    tq = 512
    gq = gq_ref[...] * gk_ref[...] * (XA_HEAD_DIM ** -0.5 * LOG2E)
    k = jnp.concatenate([k_ref[0], k_ref[1]], axis=1).astype(F32)
    kn = _unit_rms(k).astype(BF16)
    v = jnp.concatenate([v_ref[0], v_ref[1]], axis=1)

    def body(i, _):
        sl = pl.ds(pl.multiple_of(i * tq, tq), tq)
        q = jnp.concatenate([q_ref[0, sl, :], q_ref[1, sl, :]], axis=1).astype(F32)
        qn = _rms(q, gq).astype(BF16)
        s = lax.dot_general(qn, kn, (((1,), (1,)), ((), ())), preferred_element_type=F32)
        e = jnp.exp2(s - jnp.max(s, axis=-1, keepdims=True))
        l = jnp.sum(e, axis=-1, keepdims=True)
        o = jnp.dot(e.astype(BF16), v, preferred_element_type=F32)
        o_ref[sl, :] = (o / l).astype(BF16)
        return 0

    lax.fori_loop(0, s_len // tq, body, 0, unroll=True)


def _mem_xattn(proj, mkv, gq, gk, batch, s_len, n_mem):
    t = batch * s_len
    g_spec = pl.BlockSpec((1, XA_HEAD_DIM), lambda b, h: (0, 0))
    return pl.pallas_call(
        _xa_kernel,
        grid=(batch, XA_HEADS),
        in_specs=[
            pl.BlockSpec((2, s_len, LANE), lambda b, h: (SLAB_XQ // 2 + h, b, 0)),
            pl.BlockSpec((2, n_mem, LANE), lambda b, h: (h, b, 0)),
            pl.BlockSpec((2, n_mem, LANE), lambda b, h: (XA_HEADS + h, b, 0)),
            g_spec, g_spec,
        ],
        out_specs=pl.BlockSpec((s_len, XA_HEAD_DIM), lambda b, h: (b, h)),
        out_shape=jax.ShapeDtypeStruct((t, XA_HEADS * XA_HEAD_DIM), BF16),
        compiler_params=_params(("parallel", "parallel"), 56),
        name="mem_xattn",
    )(proj, mkv, mkv, gq.reshape(1, -1), gk.reshape(1, -1))


def _merge_kernel(ona_ref, oret_ref, omem_ref, wna_ref, wret_ref, wmem_ref,
                  gna_ref, gret_ref, gmem_ref, o_ref):
    y_na = jnp.dot(ona_ref[...], wna_ref[...], preferred_element_type=F32)
    y_ret = jnp.dot(oret_ref[...], wret_ref[...], preferred_element_type=F32)
    y_mem = jnp.dot(omem_ref[...], wmem_ref[...], preferred_element_type=F32)
    for j in range(gna_ref.shape[0]):
        cs = slice(j * LANE, (j + 1) * LANE)
        o_ref[:, cs] = (_sigmoid(gna_ref[j].astype(F32)) * y_na[:, cs]
                        + _sigmoid(gret_ref[j].astype(F32)) * y_ret[:, cs]
                        + _sigmoid(gmem_ref[j].astype(F32)) * y_mem[:, cs]).astype(BF16)


def _merge(o_na, o_ret, o_mem, w_na, w_ret, w_mem, proj, tm, tn):
    t = o_na.shape[0]
    d = w_na.shape[1]
    ns = tn // LANE

    def lhs_spec(a):
        return pl.BlockSpec((tm, a.shape[1]), lambda i, j: (i, 0))

    def w_spec(w):
        return pl.BlockSpec((w.shape[0], tn), lambda i, j: (0, j))

    def gate_spec(first):
        return pl.BlockSpec((ns, tm, LANE), lambda i, j: (first // ns + j, i, 0))

    return pl.pallas_call(
        _merge_kernel,
        grid=(t // tm, d // tn),
        in_specs=[lhs_spec(o_na), lhs_spec(o_ret), lhs_spec(o_mem),
                  w_spec(w_na), w_spec(w_ret), w_spec(w_mem),
                  gate_spec(SLAB_G_NA), gate_spec(SLAB_G_RET), gate_spec(SLAB_G_MEM)],
        out_specs=pl.BlockSpec((tm, tn), lambda i, j: (i, j)),
        out_shape=jax.ShapeDtypeStruct((t, d), BF16),
        compiler_params=_params(("parallel", "arbitrary"), 58),
        name="merge",
    )(o_na, o_ret, o_mem, w_na, w_ret, w_mem, proj, proj, proj)


def _out_proj_kernel(m_ref, w_ref, x_ref, o_ref, wb_ref):
    @pl.when(pl.program_id(0) == 0)
    def _():
        wb_ref[...] = w_ref[...].astype(BF16)

    o_ref[...] = x_ref[...] + jnp.dot(m_ref[...], wb_ref[...], preferred_element_type=F32)


def _out_proj(merged, w, x2d, tm):
    t, k = merged.shape
    d = w.shape[1]
    return pl.pallas_call(
        _out_proj_kernel,
        grid=(t // tm,),
        in_specs=[pl.BlockSpec((tm, k), lambda i: (i, 0)),
                  pl.BlockSpec((k, d), lambda i: (0, 0), pipeline_mode=pl.Buffered(1)),
                  pl.BlockSpec((tm, d), lambda i: (i, 0))],
        out_specs=pl.BlockSpec((tm, d), lambda i: (i, 0)),
        out_shape=jax.ShapeDtypeStruct((t, d), F32),
        scratch_shapes=[pltpu.VMEM((k, d), BF16)],
        compiler_params=_params(("arbitrary",), 56),
        name="out_proj",
    )(merged, w, x2d)


def _ffn_kernel(x_ref, g_ref, w1_ref, w2_ref, o_ref, h_ref):
    @pl.when(pl.program_id(1) == 0)
    def _():
        x = x_ref[...]
        h_ref[...] = _rms(x, g_ref[...]).astype(BF16)
        o_ref[...] = x

    a = jnp.maximum(jnp.dot(h_ref[...], w1_ref[...], preferred_element_type=F32), 0.0)
    o_ref[...] += jnp.dot((a * a).astype(BF16), w2_ref[...], preferred_element_type=F32)


def _ffn(x1, g, w1, w2, tm, tf):
    t, d = x1.shape
    dff = w1.shape[1]
    return pl.pallas_call(
        _ffn_kernel,
        grid=(t // tm, dff // tf),
        in_specs=[pl.BlockSpec((tm, d), lambda i, f: (i, 0)),
                  pl.BlockSpec((1, d), lambda i, f: (0, 0)),
                  pl.BlockSpec((d, tf), lambda i, f: (0, f)),
                  pl.BlockSpec((tf, d), lambda i, f: (f, 0))],
        out_specs=pl.BlockSpec((tm, d), lambda i, f: (i, 0)),
        out_shape=jax.ShapeDtypeStruct((t, d), F32),
        scratch_shapes=[pltpu.VMEM((tm, d), BF16)],
        compiler_params=_params(("parallel", "arbitrary"), 60),
        name="ffn",
    )(x1, g.reshape(1, d), w1, w2)


def _rope_tables(s_len):
    half = RET_QK_DIM // 2
    inv = np.power(np.float64(ROPE_BASE), -np.arange(half, dtype=np.float64) / half)
    ang = np.arange(s_len, dtype=np.float64)[:, None] * inv[None, :]
    cos, sin = np.cos(ang), np.sin(ang)
    return (jnp.asarray(np.concatenate([cos, cos], axis=1), F32),
            jnp.asarray(np.concatenate([-sin, sin], axis=1), F32))


def kernel(x, mem, norm_mix_g, w_in, na_q_norm_g, na_k_norm_g, na_rpb, ret_decay_logit_fwd, ret_decay_logit_bwd, ret_gn_g, mem_norm_g, w_mem_kv, xa_q_norm_g, xa_k_norm_g, w_br_na, w_br_ret, w_br_mem, w_out, norm_ffn_g, w_ff1, w_ff2):
    batch, s_len, d = x.shape
    n_mem = mem.shape[1]
    t = batch * s_len
    depth = w_in.shape[0]
    tm = min(1024, t)
    cos2, sin2 = _rope_tables(s_len)
    x2d = x.reshape(t, d)
    mem2d = mem.reshape(batch * n_mem, d)

    def lane_bcast(v):
        return jnp.broadcast_to(v.astype(F32)[:, None, None], (v.shape[0], 1, LANE))

    tn_in = 1024
    gm, gn = t // tm, w_in.shape[2] // tn_in

    def side_job(w):
        r, c = w.shape
        if r % (gm * gn * 16) == 0:
            return w, (r // (gm * gn), c), lambda i, j: (i * gn + j, 0)
        assert r % (gn * 16) == 0 and c % (gm * LANE) == 0, (w.shape, gm, gn)
        return w, (r // gn, c // gm), lambda i, j: (j, i)

    for l in range(depth):
        sides = [side_job(w) for w in (w_br_na[l], w_br_ret[l], w_br_mem[l])]
        proj, (w_na_bf, w_ret_bf, w_mem_bf) = _norm_proj(
            x2d, norm_mix_g[l], w_in[l], tm, tn_in, side=sides)
        mkv, _ = _norm_proj(mem2d, mem_norm_g[l], w_mem_kv[l], batch * n_mem, 1024)

        o_mem = _mem_xattn(proj, mkv, xa_q_norm_g[l], xa_k_norm_g[l], batch, s_len, n_mem)
        o_na, w_ff1_bf = _na_attention(proj, na_rpb[l], na_q_norm_g[l], na_k_norm_g[l], w_ff1[l], batch, s_len)
        o_ret, w_ff2_bf = _retention(proj, cos2, sin2, lane_bcast(ret_decay_logit_fwd[l]),
                                     lane_bcast(ret_decay_logit_bwd[l]), ret_gn_g[l], w_ff2[l], batch, s_len)

        merged = _merge(o_na, o_ret, o_mem, w_na_bf, w_ret_bf, w_mem_bf, proj, tm, 512)
        x1 = _out_proj(merged, w_out[l], x2d, min(512, t))
        x2d = _ffn(x1, norm_ffn_g[l], w_ff1_bf, w_ff2_bf, tm, 512)
    return x2d.reshape(batch, s_len, d)
```

```python
import functools
from typing import NamedTuple

import jax
import jax.numpy as jnp
import numpy as np
from jax import lax
from jax.experimental import pallas as pl
from jax.experimental.pallas import tpu as pltpu

F32 = jnp.float32
BF16 = jnp.bfloat16

LANE = 128
EPS = 1e-6
NEG = -1e30
LOG2E = 1.4426950408889634

GRID_W = 64
NA_HEADS = 8
NA_HEAD_DIM = 128
NA_MAX_ROWS = 8
NA_COLS = 16
NA_DR = 2 * NA_MAX_ROWS - 1
NA_DC = 2 * NA_COLS - 1
NA_TBL = 3 * NA_DR + 1
NA_QROWS = 4
NA_WIN = NA_QROWS + NA_MAX_ROWS

RET_HEADS = 8
RET_QK_DIM = 128
RET_V_DIM = 256
RET_BLOCK = 512
ROPE_BASE = 10000.0

XA_HEADS = 4
XA_HEAD_DIM = 256

SLAB_NA_Q, SLAB_NA_K, SLAB_NA_V = 0, 8, 16
SLAB_RQ, SLAB_RK, SLAB_RV, SLAB_RG = 24, 32, 40, 56
SLAB_XQ = 72
SLAB_G_NA, SLAB_G_RET, SLAB_G_MEM = 80, 96, 112

V7X_VMEM_BYTES = 64 * 2**20
VMEM_LIMIT_BYTES = V7X_VMEM_BYTES - 6 * 2**20


def _params(*sem):
    return pltpu.CompilerParams(dimension_semantics=sem, vmem_limit_bytes=VMEM_LIMIT_BYTES)


def _rms(x, g):
    return x * lax.rsqrt(jnp.mean(x * x, axis=-1, keepdims=True) + EPS) * g


def _unit_rms(x):
    return x * lax.rsqrt(jnp.mean(x * x, axis=-1, keepdims=True) + EPS)


def _sigmoid(x):
    return 0.5 * jnp.tanh(0.5 * x) + 0.5


def _norm_proj_kernel(n_side, x_ref, g_ref, w_ref, *refs):
    side_in, o_ref = refs[:n_side], refs[n_side]
    side_out, hn_ref = refs[n_side + 1:2 * n_side + 1], refs[2 * n_side + 1]

    @pl.when(pl.program_id(1) == 0)
    def _():
        hn_ref[...] = _rms(x_ref[...], g_ref[...]).astype(BF16)

    acc = jnp.dot(hn_ref[...], w_ref[...].astype(BF16), preferred_element_type=F32)
    for s in range(o_ref.shape[0]):
        o_ref[s] = acc[:, s * LANE:(s + 1) * LANE].astype(BF16)
    for src, dst in zip(side_in, side_out):
        dst[...] = src[...].astype(BF16)


def _norm_proj(x2d, g, w, tm, tn, side=()):
    m, k = x2d.shape
    n = w.shape[1]
    side_specs = [pl.BlockSpec(blk, imap) for _, blk, imap in side]
    out = pl.pallas_call(
        functools.partial(_norm_proj_kernel, len(side)),
        grid=(m // tm, n // tn),
        in_specs=[
            pl.BlockSpec((tm, k), lambda i, j: (i, 0)),
            pl.BlockSpec((1, k), lambda i, j: (0, 0)),
            pl.BlockSpec((k, tn), lambda i, j: (0, j)),
        ] + side_specs,
        out_specs=[pl.BlockSpec((tn // LANE, tm, LANE), lambda i, j: (j, i, 0))] + side_specs,
        out_shape=[jax.ShapeDtypeStruct((n // LANE, m, LANE), BF16)]
        + [jax.ShapeDtypeStruct(a.shape, BF16) for a, _, _ in side],
        scratch_shapes=[pltpu.VMEM((tm, k), BF16)],
        compiler_params=_params("parallel", "arbitrary"),
        name="norm_proj",
    )(x2d, g.reshape(1, k), w, *[a for a, _, _ in side])
    return out[0], out[1:]


def _na_build_tables(h, rpb_ref, t_ref, base_ref):
    qc = lax.broadcasted_iota(jnp.int32, (GRID_W, LANE), 0)
    lane = lax.broadcasted_iota(jnp.int32, (GRID_W, LANE), 1)
    kc = lane & (GRID_W - 1)
    d = jnp.clip(kc - qc, -(NA_COLS - 1), NA_COLS - 1) + (NA_COLS - 1)
    cs = jnp.clip(qc - NA_COLS // 2, 0, GRID_W - NA_COLS)
    col_ok = (kc >= cs) & (kc < cs + NA_COLS)
    left = lane < GRID_W
    neg = jnp.full((GRID_W, LANE), NEG, F32)

    def body(dr, _):
        base = (h * NA_DR + dr) * NA_DC
        val = jnp.zeros((GRID_W, LANE), F32)
        for dd in range(NA_DC):
            val = jnp.where(d == dd, rpb_ref[base + dd], val)
        base_ref[dr] = jnp.where(col_ok, val * LOG2E, NEG)
        return 0

    lax.fori_loop(0, NA_DR, body, 0)
    for dr in range(NA_DR):
        second = base_ref[dr + 1] if dr + 1 < NA_DR else neg
        t_ref[dr] = jnp.where(left, base_ref[dr], second)
        t_ref[NA_DR + dr] = jnp.where(left, base_ref[dr], neg)
        t_ref[2 * NA_DR + dr] = jnp.where(left, neg, base_ref[dr])
    t_ref[3 * NA_DR] = neg


def _na_kernel(rpb_ref, q_ref, k_ref, v_ref, gq_ref, gk_ref, o_ref, kt_ref, va_ref, t_ref, base_ref):
    @pl.when(pl.program_id(1) == 0)
    def _():
        _na_build_tables(pl.program_id(0), rpb_ref, t_ref, base_ref)

    s_len = q_ref.shape[1]
    rows = s_len // GRID_W
    step_tok = NA_QROWS * GRID_W
    win_tok = NA_WIN * GRID_W
    n_tiles = NA_WIN // 2
    gq = gq_ref[...] * gk_ref[...] * (NA_HEAD_DIM ** -0.5 * LOG2E)
    va_ref[:, :NA_HEAD_DIM] = v_ref[0]
    va_ref[:, NA_HEAD_DIM:] = jnp.ones((s_len, NA_HEAD_DIM), BF16)

    def knorm(c, _):
        sl = pl.ds(pl.multiple_of(c * LANE, LANE), LANE)
        kt_ref[c] = _unit_rms(k_ref[0, sl, :].astype(F32)).astype(BF16).T
        return 0

    lax.fori_loop(0, s_len // LANE, knorm, 0, unroll=True)

    def step(i, _):
        r0 = NA_QROWS * i
        ws = jnp.clip(r0 - NA_MAX_ROWS // 2, 0, rows - NA_WIN)
        wp = ws // 2
        qsl = pl.ds(pl.multiple_of(i * step_tok, step_tok), step_tok)
        wsl = pl.ds(pl.multiple_of(ws * GRID_W, LANE), win_tok)
        qn = _rms(q_ref[0, qsl, :].astype(F32), gq).astype(BF16)
        kwin = jnp.concatenate([kt_ref[wp + t] for t in range(n_tiles)], axis=1)
        s = jnp.dot(qn, kwin, preferred_element_type=F32)
        bias_rows = []
        for qr in range(NA_QROWS):
            r = r0 + qr
            rs = jnp.clip(r - NA_MAX_ROWS // 2, 0, rows - NA_MAX_ROWS)
            tiles = []
            for t in range(n_tiles):
                ka = ws + 2 * t
                dr = ka - r + (NA_MAX_ROWS - 1)
                va = (ka >= rs) & (ka < rs + NA_MAX_ROWS)
                vb = (ka + 1 >= rs) & (ka + 1 < rs + NA_MAX_ROWS)
                idx = jnp.where(va, jnp.where(vb, dr, NA_DR + dr),
                                jnp.where(vb, 2 * NA_DR + dr + 1, 3 * NA_DR))
                tiles.append(t_ref[idx])
            bias_rows.append(jnp.concatenate(tiles, axis=1))
        s = s + jnp.concatenate(bias_rows, axis=0)
        e = jnp.exp2(s - jnp.max(s, axis=-1, keepdims=True))
        o = jnp.dot(e.astype(BF16), va_ref[wsl, :], preferred_element_type=F32)
        o_ref[qsl, :] = (o[:, :NA_HEAD_DIM] / o[:, NA_HEAD_DIM:]).astype(BF16)
        return 0

    lax.fori_loop(0, rows // NA_QROWS, step, 0, unroll=True)


def _na_attention(proj, rpb, gq, gk, batch, s_len):
    t = batch * s_len
    g_spec = pl.BlockSpec((1, NA_HEAD_DIM), lambda h, b: (0, 0))
    return pl.pallas_call(
        _na_kernel,
        grid=(NA_HEADS, batch),
        in_specs=[
            pl.BlockSpec(memory_space=pltpu.SMEM),
            pl.BlockSpec((1, s_len, LANE), lambda h, b: (SLAB_NA_Q + h, b, 0)),
            pl.BlockSpec((1, s_len, LANE), lambda h, b: (SLAB_NA_K + h, b, 0)),
            pl.BlockSpec((1, s_len, LANE), lambda h, b: (SLAB_NA_V + h, b, 0)),
            g_spec, g_spec,
        ],
        out_specs=pl.BlockSpec((s_len, NA_HEAD_DIM), lambda h, b: (b, h)),
        out_shape=jax.ShapeDtypeStruct((t, NA_HEADS * NA_HEAD_DIM), BF16),
        scratch_shapes=[pltpu.VMEM((s_len // LANE, NA_HEAD_DIM, LANE), BF16),
                        pltpu.VMEM((s_len, 2 * NA_HEAD_DIM), BF16),
                        pltpu.VMEM((NA_TBL, GRID_W, LANE), F32),
                        pltpu.VMEM((NA_DR, GRID_W, LANE), F32)],
        compiler_params=_params("arbitrary", "arbitrary"),
        name="na_attention",
    )(rpb.reshape(-1), proj, proj, proj, gq.reshape(1, -1), gk.reshape(1, -1))


def _log_sigmoid(x):
    return -(jnp.maximum(-x, 0.0) + jnp.log1p(jnp.exp(-jnp.abs(x))))


def _ret_kernel(q_ref, k_ref, v_ref, g_ref, cos_ref, sin_ref, lf_ref, lb_ref, gn_ref,
                o_ref, qr_ref, kt_ref, d_ref, qdf_ref, qdb_ref, kv_ref, s_ref):
    s_len = q_ref.shape[1]
    c = RET_BLOCK
    nb = s_len // c
    dk = RET_QK_DIM
    half = dk // 2

    lgf = _log_sigmoid(lf_ref[0][:, :1])
    lgb = _log_sigmoid(lb_ref[0][:, :1])
    ic = lax.broadcasted_iota(jnp.int32, (c, 1), 0).astype(F32)
    jr = lax.broadcasted_iota(jnp.int32, (1, c), 1).astype(F32)
    diff = ic - jr
    scale = dk ** -0.5
    d_ref[...] = jnp.exp(jnp.where(diff >= 0, lgf, lgb) * jnp.abs(diff)) * scale
    qdf_ref[...] = jnp.broadcast_to(jnp.exp(lgf * (ic + 1.0)), (c, dk))
    qdb_ref[...] = jnp.broadcast_to(jnp.exp(lgb * (c - ic)), (c, dk))
    kdf = jnp.exp(lgf * (c - 1.0 - jr)) * scale
    kdb = jnp.exp(lgb * jr) * scale
    cd_f = jnp.exp(lgf * c)
    cd_b = jnp.exp(lgb * c)

    def block_v(sl):
        return jnp.concatenate([v_ref[0, sl, :], v_ref[1, sl, :]], axis=1)

    def prep(n, _):
        sl = pl.ds(pl.multiple_of(n * c, c), c)
        cos = cos_ref[sl, :]
        sin = sin_ref[sl, :]
        q = q_ref[0, sl, :].astype(F32)
        k = k_ref[0, sl, :].astype(F32)
        qr_ref[sl, :] = (q * cos + pltpu.roll(q, half, 1) * sin).astype(BF16)
        kt = (k * cos + pltpu.roll(k, half, 1) * sin).T
        kt_ref[n] = kt.astype(BF16)
        lhs = jnp.concatenate([(kt * kdf).astype(BF16), (kt * kdb).astype(BF16)], axis=0)
        kv_ref[n] = jnp.dot(lhs, block_v(sl), preferred_element_type=F32)
        return 0

    lax.fori_loop(0, nb, prep, 0, unroll=True)

    def scan_f(n, sf):
        s_ref[n, :dk, :] = sf.astype(BF16)
        return cd_f * sf + kv_ref[n, :dk, :]

    def scan_b(t, sb):
        n = nb - 1 - t
        s_ref[n, dk:, :] = sb.astype(BF16)
        return cd_b * sb + kv_ref[n, dk:, :]

    zero = jnp.zeros((dk, RET_V_DIM), F32)

    def scan(n, carry):
        return scan_f(n, carry[0]), scan_b(n, carry[1])

    lax.fori_loop(0, nb, scan, (zero, zero), unroll=True)

    gn = gn_ref[0]

    def out(n, _):
        sl = pl.ds(pl.multiple_of(n * c, c), c)
        q = qr_ref[sl, :]
        qf32 = q.astype(F32)
        a = jnp.dot(q, kt_ref[n], preferred_element_type=F32) * d_ref[...]
        lhs = jnp.concatenate([a.astype(BF16), (qf32 * qdf_ref[...]).astype(BF16),
                               (qf32 * qdb_ref[...]).astype(BF16)], axis=1)
        rhs = jnp.concatenate([block_v(sl), s_ref[n]], axis=0)
        o = jnp.dot(lhs, rhs, preferred_element_type=F32)
        mu = jnp.mean(o, axis=-1, keepdims=True)
        oc = o - mu
        y = oc * lax.rsqrt(jnp.mean(oc * oc, axis=-1, keepdims=True) + EPS) * gn
        gate = jnp.concatenate([g_ref[0, sl, :], g_ref[1, sl, :]], axis=1).astype(F32)
        o_ref[sl, :] = (y * gate * _sigmoid(gate)).astype(BF16)
        return 0

    lax.fori_loop(0, nb, out, 0, unroll=True)


def _retention(proj, cos2, sin2, lf, lb, gn, batch, s_len):
    t = batch * s_len
    nb = s_len // RET_BLOCK
    dec_spec = pl.BlockSpec((1, 1, LANE), lambda b, h: (h, 0, 0))
    return pl.pallas_call(
        _ret_kernel,
        grid=(batch, RET_HEADS),
        in_specs=[
            pl.BlockSpec((1, s_len, LANE), lambda b, h: (SLAB_RQ + h, b, 0)),
            pl.BlockSpec((1, s_len, LANE), lambda b, h: (SLAB_RK + h, b, 0)),
            pl.BlockSpec((2, s_len, LANE), lambda b, h: (SLAB_RV // 2 + h, b, 0)),
            pl.BlockSpec((2, s_len, LANE), lambda b, h: (SLAB_RG // 2 + h, b, 0)),
            pl.BlockSpec((s_len, RET_QK_DIM), lambda b, h: (0, 0)),
            pl.BlockSpec((s_len, RET_QK_DIM), lambda b, h: (0, 0)),
            dec_spec, dec_spec,
            pl.BlockSpec((1, 1, RET_V_DIM), lambda b, h: (h, 0, 0)),
        ],
        out_specs=pl.BlockSpec((s_len, RET_V_DIM), lambda b, h: (b, h)),
        out_shape=jax.ShapeDtypeStruct((t, RET_HEADS * RET_V_DIM), BF16),
        scratch_shapes=[
            pltpu.VMEM((s_len, RET_QK_DIM), BF16),
            pltpu.VMEM((nb, RET_QK_DIM, RET_BLOCK), BF16),
            pltpu.VMEM((RET_BLOCK, RET_BLOCK), F32),
            pltpu.VMEM((RET_BLOCK, RET_QK_DIM), F32),
            pltpu.VMEM((RET_BLOCK, RET_QK_DIM), F32),
            pltpu.VMEM((nb, 2 * RET_QK_DIM, RET_V_DIM), F32),
            pltpu.VMEM((nb, 2 * RET_QK_DIM, RET_V_DIM), BF16),
        ],
        compiler_params=_params("parallel", "parallel"),
        name="retention",
    )(proj, proj, proj, proj, cos2, sin2, lf, lb, gn.reshape(RET_HEADS, 1, RET_V_DIM))


def _xa_kernel(q_ref, k_ref, v_ref, gq_ref, gk_ref, o_ref):
    s_len = q_ref.shape[1]
    tq = 512
    gq = gq_ref[...] * gk_ref[...] * (XA_HEAD_DIM ** -0.5 * LOG2E)
    k = jnp.concatenate([k_ref[0], k_ref[1]], axis=1).astype(F32)
    kn = _unit_rms(k).astype(BF16)
    v = jnp.concatenate([v_ref[0], v_ref[1]], axis=1)

    def body(i, _):
        sl = pl.ds(pl.multiple_of(i * tq, tq), tq)
        q = jnp.concatenate([q_ref[0, sl, :], q_ref[1, sl, :]], axis=1).astype(F32)
        qn = _rms(q, gq).astype(BF16)
        s = lax.dot_general(qn, kn, (((1,), (1,)), ((), ())), preferred_element_type=F32)
        e = jnp.exp2(s - jnp.max(s, axis=-1, keepdims=True))
        l = jnp.sum(e, axis=-1, keepdims=True)
        o = jnp.dot(e.astype(BF16), v, preferred_element_type=F32)
        o_ref[sl, :] = (o / l).astype(BF16)
        return 0

    lax.fori_loop(0, s_len // tq, body, 0, unroll=True)


def _mem_xattn(proj, mkv, gq, gk, batch, s_len, n_mem):
    t = batch * s_len
    g_spec = pl.BlockSpec((1, XA_HEAD_DIM), lambda b, h: (0, 0))
    return pl.pallas_call(
        _xa_kernel,
        grid=(batch, XA_HEADS),
        in_specs=[
            pl.BlockSpec((2, s_len, LANE), lambda b, h: (SLAB_XQ // 2 + h, b, 0)),
            pl.BlockSpec((2, n_mem, LANE), lambda b, h: (h, b, 0)),
            pl.BlockSpec((2, n_mem, LANE), lambda b, h: (XA_HEADS + h, b, 0)),
            g_spec, g_spec,
        ],
        out_specs=pl.BlockSpec((s_len, XA_HEAD_DIM), lambda b, h: (b, h)),
        out_shape=jax.ShapeDtypeStruct((t, XA_HEADS * XA_HEAD_DIM), BF16),
        compiler_params=_params("parallel", "parallel"),
        name="mem_xattn",
    )(proj, mkv, mkv, gq.reshape(1, -1), gk.reshape(1, -1))


def _merge_kernel(ona_ref, oret_ref, omem_ref, wna_ref, wret_ref, wmem_ref,
                  gna_ref, gret_ref, gmem_ref, o_ref):
    y_na = jnp.dot(ona_ref[...], wna_ref[...], preferred_element_type=F32)
    y_ret = jnp.dot(oret_ref[...], wret_ref[...], preferred_element_type=F32)
    y_mem = jnp.dot(omem_ref[...], wmem_ref[...], preferred_element_type=F32)
    for j in range(gna_ref.shape[0]):
        cs = slice(j * LANE, (j + 1) * LANE)
        o_ref[:, cs] = (_sigmoid(gna_ref[j].astype(F32)) * y_na[:, cs]
                        + _sigmoid(gret_ref[j].astype(F32)) * y_ret[:, cs]
                        + _sigmoid(gmem_ref[j].astype(F32)) * y_mem[:, cs]).astype(BF16)


def _merge(o_na, o_ret, o_mem, w_na, w_ret, w_mem, proj, tm, tn):
    t = o_na.shape[0]
    d = w_na.shape[1]
    ns = tn // LANE

    def lhs_spec(a):
        return pl.BlockSpec((tm, a.shape[1]), lambda i, j: (i, 0))

    def w_spec(w):
        return pl.BlockSpec((w.shape[0], tn), lambda i, j: (0, j))

    def gate_spec(first):
        return pl.BlockSpec((ns, tm, LANE), lambda i, j: (first // ns + j, i, 0))

    return pl.pallas_call(
        _merge_kernel,
        grid=(t // tm, d // tn),
        in_specs=[lhs_spec(o_na), lhs_spec(o_ret), lhs_spec(o_mem),
                  w_spec(w_na), w_spec(w_ret), w_spec(w_mem),
                  gate_spec(SLAB_G_NA), gate_spec(SLAB_G_RET), gate_spec(SLAB_G_MEM)],
        out_specs=pl.BlockSpec((tm, tn), lambda i, j: (i, j)),
        out_shape=jax.ShapeDtypeStruct((t, d), BF16),
        compiler_params=_params("parallel", "arbitrary"),
        name="merge",
    )(o_na, o_ret, o_mem, w_na, w_ret, w_mem, proj, proj, proj)


def _out_proj_kernel(m_ref, w_ref, x_ref, o_ref, wb_ref):
    @pl.when(pl.program_id(0) == 0)
    def _():
        wb_ref[...] = w_ref[...].astype(BF16)

    o_ref[...] = x_ref[...] + jnp.dot(m_ref[...], wb_ref[...], preferred_element_type=F32)


def _out_proj(merged, w, x2d, tm):
    t, k = merged.shape
    d = w.shape[1]
    return pl.pallas_call(
        _out_proj_kernel,
        grid=(t // tm,),
        in_specs=[pl.BlockSpec((tm, k), lambda i: (i, 0)),
                  pl.BlockSpec((k, d), lambda i: (0, 0), pipeline_mode=pl.Buffered(1)),
                  pl.BlockSpec((tm, d), lambda i: (i, 0))],
        out_specs=pl.BlockSpec((tm, d), lambda i: (i, 0)),
        out_shape=jax.ShapeDtypeStruct((t, d), F32),
        scratch_shapes=[pltpu.VMEM((k, d), BF16)],
        compiler_params=_params("arbitrary"),
        name="out_proj",
    )(merged, w, x2d)


def _ffn_kernel(x_ref, g_ref, w1_ref, w2_ref, o_ref, h_ref):
    @pl.when(pl.program_id(1) == 0)
    def _():
        x = x_ref[...]
        h_ref[...] = _rms(x, g_ref[...]).astype(BF16)
        o_ref[...] = x

    a = jnp.maximum(jnp.dot(h_ref[...], w1_ref[...], preferred_element_type=F32), 0.0)
    o_ref[...] += jnp.dot((a * a).astype(BF16), w2_ref[...], preferred_element_type=F32)


def _ffn(x1, g, w1, w2, tm, tf):
    t, d = x1.shape
    dff = w1.shape[1]
    return pl.pallas_call(
        _ffn_kernel,
        grid=(t // tm, dff // tf),
        in_specs=[pl.BlockSpec((tm, d), lambda i, f: (i, 0)),
                  pl.BlockSpec((1, d), lambda i, f: (0, 0)),
                  pl.BlockSpec((d, tf), lambda i, f: (0, f)),
                  pl.BlockSpec((tf, d), lambda i, f: (f, 0))],
        out_specs=pl.BlockSpec((tm, d), lambda i, f: (i, 0)),
        out_shape=jax.ShapeDtypeStruct((t, d), F32),
        scratch_shapes=[pltpu.VMEM((tm, d), BF16)],
        compiler_params=_params("parallel", "arbitrary"),
        name="ffn",
    )(x1, g.reshape(1, d), w1, w2)


def _rope_tables(s_len):
    half = RET_QK_DIM // 2
    inv = np.power(np.float64(ROPE_BASE), -np.arange(half, dtype=np.float64) / half)
    ang = np.arange(s_len, dtype=np.float64)[:, None] * inv[None, :]
    cos, sin = np.cos(ang), np.sin(ang)
    return (jnp.asarray(np.concatenate([cos, cos], axis=1), F32),
            jnp.asarray(np.concatenate([-sin, sin], axis=1), F32))


class _Tiles(NamedTuple):
    tm: int
    tn_in: int
    tn_merge: int
    tm_out: int
    tf: int


def _tiles(t):
    return _Tiles(tm=min(1024, t), tn_in=1024, tn_merge=512, tm_out=min(512, t), tf=1024)


def kernel(x, mem, norm_mix_g, w_in, na_q_norm_g, na_k_norm_g, na_rpb, ret_decay_logit_fwd, ret_decay_logit_bwd, ret_gn_g, mem_norm_g, w_mem_kv, xa_q_norm_g, xa_k_norm_g, w_br_na, w_br_ret, w_br_mem, w_out, norm_ffn_g, w_ff1, w_ff2):
    batch, s_len, d = x.shape
    n_mem = mem.shape[1]
    t = batch * s_len
    tiles = _tiles(t)
    cos2, sin2 = _rope_tables(s_len)
    x2d = x.reshape(t, d)
    mem2d = mem.reshape(batch * n_mem, d)
    gm, gn = t // tiles.tm, w_in.shape[2] // tiles.tn_in

    def lane_bcast(v):
        return jnp.broadcast_to(v.astype(F32)[:, None, None], (v.shape[0], 1, LANE))

    def side_job(w):
        r, c = w.shape
        if r % (gm * gn * 16) == 0:
            return w, (r // (gm * gn), c), lambda i, j: (i * gn + j, 0)
        assert r % (gn * 16) == 0 and c % (gm * LANE) == 0, (w.shape, gm, gn)
        return w, (r // gn, c // gm), lambda i, j: (j, i)

    for l in range(w_in.shape[0]):
        sides = [side_job(w) for w in (w_ff1[l], w_ff2[l], w_br_na[l], w_br_ret[l], w_br_mem[l])]
        proj, (w_ff1_bf, w_ff2_bf, w_na_bf, w_ret_bf, w_mem_bf) = _norm_proj(
            x2d, norm_mix_g[l], w_in[l], tiles.tm, tiles.tn_in, side=sides)
        mkv, _ = _norm_proj(mem2d, mem_norm_g[l], w_mem_kv[l], batch * n_mem, tiles.tn_in)

        o_mem = _mem_xattn(proj, mkv, xa_q_norm_g[l], xa_k_norm_g[l], batch, s_len, n_mem)
        o_na = _na_attention(proj, na_rpb[l], na_q_norm_g[l], na_k_norm_g[l], batch, s_len)
        o_ret = _retention(proj, cos2, sin2, lane_bcast(ret_decay_logit_fwd[l]),
                           lane_bcast(ret_decay_logit_bwd[l]), ret_gn_g[l], batch, s_len)

        merged = _merge(o_na, o_ret, o_mem, w_na_bf, w_ret_bf, w_mem_bf, proj, tiles.tm, tiles.tn_merge)
        x1 = _out_proj(merged, w_out[l], x2d, tiles.tm_out)
        x2d = _ffn(x1, norm_ffn_g[l], w_ff1_bf, w_ff2_bf, tiles.tm_out, tiles.tf)
    return x2d.reshape(batch, s_len, d)
```

```python
import functools
from typing import NamedTuple

import jax
import jax.numpy as jnp
import numpy as np
from jax import lax
from jax.experimental import pallas as pl
from jax.experimental.pallas import tpu as pltpu

F32 = jnp.float32
BF16 = jnp.bfloat16

LANE = 128
EPS = 1e-6
NEG = -1e30
LOG2E = 1.4426950408889634

GRID_W = 64
NA_HEADS = 8
NA_HEAD_DIM = 128
NA_MAX_ROWS = 8
NA_COLS = 16
NA_DR = 2 * NA_MAX_ROWS - 1
NA_DC = 2 * NA_COLS - 1
NA_TBL = 3 * NA_DR + 1
NA_QROWS = 4
NA_WIN = NA_QROWS + NA_MAX_ROWS

RET_HEADS = 8
RET_QK_DIM = 128
RET_V_DIM = 256
RET_BLOCK = 256
ROPE_BASE = 10000.0

XA_HEADS = 4
XA_HEAD_DIM = 256

SLAB_NA_Q, SLAB_NA_K, SLAB_NA_V = 0, 8, 16
SLAB_RQ, SLAB_RK, SLAB_RV, SLAB_RG = 24, 32, 40, 56
SLAB_XQ = 72
SLAB_G_NA, SLAB_G_RET, SLAB_G_MEM = 80, 96, 112

V7X_VMEM_BYTES = 64 * 2**20
VMEM_LIMIT_BYTES = V7X_VMEM_BYTES - 6 * 2**20


def _params(*sem):
    return pltpu.CompilerParams(dimension_semantics=sem, vmem_limit_bytes=VMEM_LIMIT_BYTES)


def _rms(x, g):
    return x * lax.rsqrt(jnp.mean(x * x, axis=-1, keepdims=True) + EPS) * g


def _unit_rms(x):
    return x * lax.rsqrt(jnp.mean(x * x, axis=-1, keepdims=True) + EPS)


def _sigmoid(x):
    return 0.5 * jnp.tanh(0.5 * x) + 0.5


def _norm_proj_kernel(n_side, x_ref, g_ref, w_ref, *refs):
    side_in, o_ref = refs[:n_side], refs[n_side]
    side_out, hn_ref = refs[n_side + 1:2 * n_side + 1], refs[2 * n_side + 1]

    @pl.when(pl.program_id(1) == 0)
    def _():
        hn_ref[...] = _rms(x_ref[...], g_ref[...]).astype(BF16)

    acc = jnp.dot(hn_ref[...], w_ref[...].astype(BF16), preferred_element_type=F32)
    for s in range(o_ref.shape[0]):
        o_ref[s] = acc[:, s * LANE:(s + 1) * LANE].astype(BF16)
    for src, dst in zip(side_in, side_out):
        dst[...] = src[...].astype(BF16)


def _norm_proj(x2d, g, w, tm, tn, side=()):
    m, k = x2d.shape
    n = w.shape[1]
    side_specs = [pl.BlockSpec(blk, imap) for _, blk, imap in side]
    out = pl.pallas_call(
        functools.partial(_norm_proj_kernel, len(side)),
        grid=(m // tm, n // tn),
        in_specs=[
            pl.BlockSpec((tm, k), lambda i, j: (i, 0)),
            pl.BlockSpec((1, k), lambda i, j: (0, 0)),
            pl.BlockSpec((k, tn), lambda i, j: (0, j)),
        ] + side_specs,
        out_specs=[pl.BlockSpec((tn // LANE, tm, LANE), lambda i, j: (j, i, 0))] + side_specs,
        out_shape=[jax.ShapeDtypeStruct((n // LANE, m, LANE), BF16)]
        + [jax.ShapeDtypeStruct(a.shape, BF16) for a, _, _ in side],
        scratch_shapes=[pltpu.VMEM((tm, k), BF16)],
        compiler_params=_params("parallel", "arbitrary"),
        name="norm_proj",
    )(x2d, g.reshape(1, k), w, *[a for a, _, _ in side])
    return out[0], out[1:]


def _na_build_tables(h, rpb_ref, t_ref, base_ref):
    qc = lax.broadcasted_iota(jnp.int32, (GRID_W, LANE), 0)
    lane = lax.broadcasted_iota(jnp.int32, (GRID_W, LANE), 1)
    kc = lane & (GRID_W - 1)
    d = jnp.clip(kc - qc, -(NA_COLS - 1), NA_COLS - 1) + (NA_COLS - 1)
    cs = jnp.clip(qc - NA_COLS // 2, 0, GRID_W - NA_COLS)
    col_ok = (kc >= cs) & (kc < cs + NA_COLS)
    left = lane < GRID_W
    neg = jnp.full((GRID_W, LANE), NEG, F32)

    def body(dr, _):
        base = (h * NA_DR + dr) * NA_DC
        val = jnp.zeros((GRID_W, LANE), F32)
        for dd in range(NA_DC):
            val = jnp.where(d == dd, rpb_ref[base + dd], val)
        base_ref[dr] = jnp.where(col_ok, val * LOG2E, NEG)
        return 0

    lax.fori_loop(0, NA_DR, body, 0)
    for dr in range(NA_DR):
        second = base_ref[dr + 1] if dr + 1 < NA_DR else neg
        t_ref[dr] = jnp.where(left, base_ref[dr], second)
        t_ref[NA_DR + dr] = jnp.where(left, base_ref[dr], neg)
        t_ref[2 * NA_DR + dr] = jnp.where(left, neg, base_ref[dr])
    t_ref[3 * NA_DR] = neg


def _na_kernel(rpb_ref, q_ref, k_ref, v_ref, gq_ref, gk_ref, o_ref, kt_ref, va_ref, t_ref, base_ref):
    @pl.when(pl.program_id(1) == 0)
    def _():
        _na_build_tables(pl.program_id(0), rpb_ref, t_ref, base_ref)

    s_len = q_ref.shape[1]
    rows = s_len // GRID_W
    step_tok = NA_QROWS * GRID_W
    win_tok = NA_WIN * GRID_W
    n_tiles = NA_WIN // 2
    gq = gq_ref[...] * gk_ref[...] * (NA_HEAD_DIM ** -0.5 * LOG2E)
    va_ref[:, :NA_HEAD_DIM] = v_ref[0]
    va_ref[:, NA_HEAD_DIM:] = jnp.ones((s_len, NA_HEAD_DIM), BF16)

    def knorm(c, _):
        sl = pl.ds(pl.multiple_of(c * LANE, LANE), LANE)
        kt_ref[c] = _unit_rms(k_ref[0, sl, :].astype(F32)).astype(BF16).T
        return 0

    lax.fori_loop(0, s_len // LANE, knorm, 0, unroll=True)

    def step(i, _):
        r0 = NA_QROWS * i
        ws = jnp.clip(r0 - NA_MAX_ROWS // 2, 0, rows - NA_WIN)
        wp = ws // 2
        qsl = pl.ds(pl.multiple_of(i * step_tok, step_tok), step_tok)
        wsl = pl.ds(pl.multiple_of(ws * GRID_W, LANE), win_tok)
        qn = _rms(q_ref[0, qsl, :].astype(F32), gq).astype(BF16)
        kwin = jnp.concatenate([kt_ref[wp + t] for t in range(n_tiles)], axis=1)
        s = jnp.dot(qn, kwin, preferred_element_type=F32)
        bias_rows = []
        for qr in range(NA_QROWS):
            r = r0 + qr
            rs = jnp.clip(r - NA_MAX_ROWS // 2, 0, rows - NA_MAX_ROWS)
            tiles = []
            for t in range(n_tiles):
                ka = ws + 2 * t
                dr = ka - r + (NA_MAX_ROWS - 1)
                va = (ka >= rs) & (ka < rs + NA_MAX_ROWS)
                vb = (ka + 1 >= rs) & (ka + 1 < rs + NA_MAX_ROWS)
                idx = jnp.where(va, jnp.where(vb, dr, NA_DR + dr),
                                jnp.where(vb, 2 * NA_DR + dr + 1, 3 * NA_DR))
                tiles.append(t_ref[idx])
            bias_rows.append(jnp.concatenate(tiles, axis=1))
        s = s + jnp.concatenate(bias_rows, axis=0)
        e = jnp.exp2(s - jnp.max(s, axis=-1, keepdims=True))
        o = jnp.dot(e.astype(BF16), va_ref[wsl, :], preferred_element_type=F32)
        o_ref[qsl, :] = (o[:, :NA_HEAD_DIM] / o[:, NA_HEAD_DIM:]).astype(BF16)
        return 0

    lax.fori_loop(0, rows // NA_QROWS, step, 0, unroll=True)


def _na_attention(proj, rpb, gq, gk, batch, s_len):
    t = batch * s_len
    g_spec = pl.BlockSpec((1, NA_HEAD_DIM), lambda h, b: (0, 0))
    return pl.pallas_call(
        _na_kernel,
        grid=(NA_HEADS, batch),
        in_specs=[
            pl.BlockSpec(memory_space=pltpu.SMEM),
            pl.BlockSpec((1, s_len, LANE), lambda h, b: (SLAB_NA_Q + h, b, 0)),
            pl.BlockSpec((1, s_len, LANE), lambda h, b: (SLAB_NA_K + h, b, 0)),
            pl.BlockSpec((1, s_len, LANE), lambda h, b: (SLAB_NA_V + h, b, 0)),
            g_spec, g_spec,
        ],
        out_specs=pl.BlockSpec((s_len, NA_HEAD_DIM), lambda h, b: (b, h)),
        out_shape=jax.ShapeDtypeStruct((t, NA_HEADS * NA_HEAD_DIM), BF16),
        scratch_shapes=[pltpu.VMEM((s_len // LANE, NA_HEAD_DIM, LANE), BF16),
                        pltpu.VMEM((s_len, 2 * NA_HEAD_DIM), BF16),
                        pltpu.VMEM((NA_TBL, GRID_W, LANE), F32),
                        pltpu.VMEM((NA_DR, GRID_W, LANE), F32)],
        compiler_params=_params("arbitrary", "arbitrary"),
        name="na_attention",
    )(rpb.reshape(-1), proj, proj, proj, gq.reshape(1, -1), gk.reshape(1, -1))


def _log_sigmoid(x):
    return -(jnp.maximum(-x, 0.0) + jnp.log1p(jnp.exp(-jnp.abs(x))))


def _ret_kernel(q_ref, k_ref, v_ref, g_ref, cos_ref, sin_ref, lf_ref, lb_ref, gn_ref,
                o_ref, qr_ref, kt_ref, d_ref, qdf_ref, qdb_ref, kv_ref, s_ref):
    s_len = q_ref.shape[1]
    c = RET_BLOCK
    nb = s_len // c
    dk = RET_QK_DIM
    half = dk // 2

    lgf = _log_sigmoid(lf_ref[0][:, :1])
    lgb = _log_sigmoid(lb_ref[0][:, :1])
    ic = lax.broadcasted_iota(jnp.int32, (c, 1), 0).astype(F32)
    jr = lax.broadcasted_iota(jnp.int32, (1, c), 1).astype(F32)
    diff = ic - jr
    scale = dk ** -0.5
    d_ref[...] = jnp.exp(jnp.where(diff >= 0, lgf, lgb) * jnp.abs(diff)) * scale
    qdf_ref[...] = jnp.broadcast_to(jnp.exp(lgf * (ic + 1.0)), (c, dk))
    qdb_ref[...] = jnp.broadcast_to(jnp.exp(lgb * (c - ic)), (c, dk))
    kdf = jnp.exp(lgf * (c - 1.0 - jr)) * scale
    kdb = jnp.exp(lgb * jr) * scale
    cd_f = jnp.exp(lgf * c)
    cd_b = jnp.exp(lgb * c)

    def block_v(sl):
        return jnp.concatenate([v_ref[0, sl, :], v_ref[1, sl, :]], axis=1)

    def prep(n, _):
        sl = pl.ds(pl.multiple_of(n * c, c), c)
        cos = cos_ref[sl, :]
        sin = sin_ref[sl, :]
        q = q_ref[0, sl, :].astype(F32)
        k = k_ref[0, sl, :].astype(F32)
        qr_ref[sl, :] = (q * cos + pltpu.roll(q, half, 1) * sin).astype(BF16)
        kt = (k * cos + pltpu.roll(k, half, 1) * sin).T
        kt_ref[n] = kt.astype(BF16)
        lhs = jnp.concatenate([(kt * kdf).astype(BF16), (kt * kdb).astype(BF16)], axis=0)
        kv_ref[n] = jnp.dot(lhs, block_v(sl), preferred_element_type=F32)
        return 0

    lax.fori_loop(0, nb, prep, 0, unroll=True)

    def scan_f(n, sf):
        s_ref[n, :dk, :] = sf.astype(BF16)
        return cd_f * sf + kv_ref[n, :dk, :]

    def scan_b(t, sb):
        n = nb - 1 - t
        s_ref[n, dk:, :] = sb.astype(BF16)
        return cd_b * sb + kv_ref[n, dk:, :]

    zero = jnp.zeros((dk, RET_V_DIM), F32)

    def scan(n, carry):
        return scan_f(n, carry[0]), scan_b(n, carry[1])

    lax.fori_loop(0, nb, scan, (zero, zero), unroll=True)

    gn = gn_ref[0]

    def out(n, _):
        sl = pl.ds(pl.multiple_of(n * c, c), c)
        q = qr_ref[sl, :]
        qf32 = q.astype(F32)
        a = jnp.dot(q, kt_ref[n], preferred_element_type=F32) * d_ref[...]
        lhs = jnp.concatenate([a.astype(BF16), (qf32 * qdf_ref[...]).astype(BF16),
                               (qf32 * qdb_ref[...]).astype(BF16)], axis=1)
        rhs = jnp.concatenate([block_v(sl), s_ref[n]], axis=0)
        o = jnp.dot(lhs, rhs, preferred_element_type=F32)
        mu = jnp.mean(o, axis=-1, keepdims=True)
        oc = o - mu
        y = oc * lax.rsqrt(jnp.mean(oc * oc, axis=-1, keepdims=True) + EPS) * gn
        gate = jnp.concatenate([g_ref[0, sl, :], g_ref[1, sl, :]], axis=1).astype(F32)
        o_ref[sl, :] = (y * gate * _sigmoid(gate)).astype(BF16)
        return 0

    lax.fori_loop(0, nb, out, 0, unroll=True)


def _retention(proj, cos2, sin2, lf, lb, gn, batch, s_len):
    t = batch * s_len
    nb = s_len // RET_BLOCK
    dec_spec = pl.BlockSpec((1, 1, LANE), lambda b, h: (h, 0, 0))
    return pl.pallas_call(
        _ret_kernel,
        grid=(batch, RET_HEADS),
        in_specs=[
            pl.BlockSpec((1, s_len, LANE), lambda b, h: (SLAB_RQ + h, b, 0)),
            pl.BlockSpec((1, s_len, LANE), lambda b, h: (SLAB_RK + h, b, 0)),
            pl.BlockSpec((2, s_len, LANE), lambda b, h: (SLAB_RV // 2 + h, b, 0)),
            pl.BlockSpec((2, s_len, LANE), lambda b, h: (SLAB_RG // 2 + h, b, 0)),
            pl.BlockSpec((s_len, RET_QK_DIM), lambda b, h: (0, 0)),
            pl.BlockSpec((s_len, RET_QK_DIM), lambda b, h: (0, 0)),
            dec_spec, dec_spec,
            pl.BlockSpec((1, 1, RET_V_DIM), lambda b, h: (h, 0, 0)),
        ],
        out_specs=pl.BlockSpec((s_len, RET_V_DIM), lambda b, h: (b, h)),
        out_shape=jax.ShapeDtypeStruct((t, RET_HEADS * RET_V_DIM), BF16),
        scratch_shapes=[
            pltpu.VMEM((s_len, RET_QK_DIM), BF16),
            pltpu.VMEM((nb, RET_QK_DIM, RET_BLOCK), BF16),
            pltpu.VMEM((RET_BLOCK, RET_BLOCK), F32),
            pltpu.VMEM((RET_BLOCK, RET_QK_DIM), F32),
            pltpu.VMEM((RET_BLOCK, RET_QK_DIM), F32),
            pltpu.VMEM((nb, 2 * RET_QK_DIM, RET_V_DIM), F32),
            pltpu.VMEM((nb, 2 * RET_QK_DIM, RET_V_DIM), BF16),
        ],
        compiler_params=_params("parallel", "parallel"),
        name="retention",
    )(proj, proj, proj, proj, cos2, sin2, lf, lb, gn.reshape(RET_HEADS, 1, RET_V_DIM))


def _xa_kernel(q_ref, k_ref, v_ref, gq_ref, gk_ref, o_ref):
    s_len = q_ref.shape[1]
    tq = 512
    gq = gq_ref[...] * gk_ref[...] * (XA_HEAD_DIM ** -0.5 * LOG2E)
    k = jnp.concatenate([k_ref[0], k_ref[1]], axis=1).astype(F32)
    kn = _unit_rms(k).astype(BF16)
    v = jnp.concatenate([v_ref[0], v_ref[1]], axis=1)

    def body(i, _):
        sl = pl.ds(pl.multiple_of(i * tq, tq), tq)
        q = jnp.concatenate([q_ref[0, sl, :], q_ref[1, sl, :]], axis=1).astype(F32)
        qn = _rms(q, gq).astype(BF16)
        s = lax.dot_general(qn, kn, (((1,), (1,)), ((), ())), preferred_element_type=F32)
        e = jnp.exp2(s - jnp.max(s, axis=-1, keepdims=True))
        l = jnp.sum(e, axis=-1, keepdims=True)
        o = jnp.dot(e.astype(BF16), v, preferred_element_type=F32)
        o_ref[sl, :] = (o / l).astype(BF16)
        return 0

    lax.fori_loop(0, s_len // tq, body, 0, unroll=True)


def _mem_xattn(proj, mkv, gq, gk, batch, s_len, n_mem):
    t = batch * s_len
    g_spec = pl.BlockSpec((1, XA_HEAD_DIM), lambda b, h: (0, 0))
    return pl.pallas_call(
        _xa_kernel,
        grid=(batch, XA_HEADS),
        in_specs=[
            pl.BlockSpec((2, s_len, LANE), lambda b, h: (SLAB_XQ // 2 + h, b, 0)),
            pl.BlockSpec((2, n_mem, LANE), lambda b, h: (h, b, 0)),
            pl.BlockSpec((2, n_mem, LANE), lambda b, h: (XA_HEADS + h, b, 0)),
            g_spec, g_spec,
        ],
        out_specs=pl.BlockSpec((s_len, XA_HEAD_DIM), lambda b, h: (b, h)),
        out_shape=jax.ShapeDtypeStruct((t, XA_HEADS * XA_HEAD_DIM), BF16),
        compiler_params=_params("parallel", "parallel"),
        name="mem_xattn",
    )(proj, mkv, mkv, gq.reshape(1, -1), gk.reshape(1, -1))


def _merge_kernel(ona_ref, oret_ref, omem_ref, wna_ref, wret_ref, wmem_ref,
                  gna_ref, gret_ref, gmem_ref, o_ref):
    y_na = jnp.dot(ona_ref[...], wna_ref[...], preferred_element_type=F32)
    y_ret = jnp.dot(oret_ref[...], wret_ref[...], preferred_element_type=F32)
    y_mem = jnp.dot(omem_ref[...], wmem_ref[...], preferred_element_type=F32)
    for j in range(gna_ref.shape[0]):
        cs = slice(j * LANE, (j + 1) * LANE)
        o_ref[:, cs] = (_sigmoid(gna_ref[j].astype(F32)) * y_na[:, cs]
                        + _sigmoid(gret_ref[j].astype(F32)) * y_ret[:, cs]
                        + _sigmoid(gmem_ref[j].astype(F32)) * y_mem[:, cs]).astype(BF16)


def _merge(o_na, o_ret, o_mem, w_na, w_ret, w_mem, proj, tm, tn):
    t = o_na.shape[0]
    d = w_na.shape[1]
    ns = tn // LANE

    def lhs_spec(a):
        return pl.BlockSpec((tm, a.shape[1]), lambda i, j: (i, 0))

    def w_spec(w):
        return pl.BlockSpec((w.shape[0], tn), lambda i, j: (0, j))

    def gate_spec(first):
        return pl.BlockSpec((ns, tm, LANE), lambda i, j: (first // ns + j, i, 0))

    return pl.pallas_call(
        _merge_kernel,
        grid=(t // tm, d // tn),
        in_specs=[lhs_spec(o_na), lhs_spec(o_ret), lhs_spec(o_mem),
                  w_spec(w_na), w_spec(w_ret), w_spec(w_mem),
                  gate_spec(SLAB_G_NA), gate_spec(SLAB_G_RET), gate_spec(SLAB_G_MEM)],
        out_specs=pl.BlockSpec((tm, tn), lambda i, j: (i, j)),
        out_shape=jax.ShapeDtypeStruct((t, d), BF16),
        compiler_params=_params("parallel", "arbitrary"),
        name="merge",
    )(o_na, o_ret, o_mem, w_na, w_ret, w_mem, proj, proj, proj)


def _out_proj_kernel(m_ref, w_ref, x_ref, o_ref, wb_ref):
    @pl.when(pl.program_id(0) == 0)
    def _():
        wb_ref[...] = w_ref[...].astype(BF16)

    o_ref[...] = x_ref[...] + jnp.dot(m_ref[...], wb_ref[...], preferred_element_type=F32)


def _out_proj(merged, w, x2d, tm):
    t, k = merged.shape
    d = w.shape[1]
    return pl.pallas_call(
        _out_proj_kernel,
        grid=(t // tm,),
        in_specs=[pl.BlockSpec((tm, k), lambda i: (i, 0)),
                  pl.BlockSpec((k, d), lambda i: (0, 0), pipeline_mode=pl.Buffered(1)),
                  pl.BlockSpec((tm, d), lambda i: (i, 0))],
        out_specs=pl.BlockSpec((tm, d), lambda i: (i, 0)),
        out_shape=jax.ShapeDtypeStruct((t, d), F32),
        scratch_shapes=[pltpu.VMEM((k, d), BF16)],
        compiler_params=_params("arbitrary"),
        name="out_proj",
    )(merged, w, x2d)


def _ffn_kernel(x_ref, g_ref, w1_ref, w2_ref, o_ref, h_ref):
    @pl.when(pl.program_id(1) == 0)
    def _():
        x = x_ref[...]
        h_ref[...] = _rms(x, g_ref[...]).astype(BF16)
        o_ref[...] = x

    a = jnp.maximum(jnp.dot(h_ref[...], w1_ref[...], preferred_element_type=F32), 0.0)
    o_ref[...] += jnp.dot((a * a).astype(BF16), w2_ref[...], preferred_element_type=F32)


def _ffn(x1, g, w1, w2, tm, tf):
    t, d = x1.shape
    dff = w1.shape[1]
    return pl.pallas_call(
        _ffn_kernel,
        grid=(t // tm, dff // tf),
        in_specs=[pl.BlockSpec((tm, d), lambda i, f: (i, 0)),
                  pl.BlockSpec((1, d), lambda i, f: (0, 0)),
                  pl.BlockSpec((d, tf), lambda i, f: (0, f)),
                  pl.BlockSpec((tf, d), lambda i, f: (f, 0))],
        out_specs=pl.BlockSpec((tm, d), lambda i, f: (i, 0)),
        out_shape=jax.ShapeDtypeStruct((t, d), F32),
        scratch_shapes=[pltpu.VMEM((tm, d), BF16)],
        compiler_params=_params("parallel", "arbitrary"),
        name="ffn",
    )(x1, g.reshape(1, d), w1, w2)


def _rope_tables(s_len):
    half = RET_QK_DIM // 2
    inv = np.power(np.float64(ROPE_BASE), -np.arange(half, dtype=np.float64) / half)
    ang = np.arange(s_len, dtype=np.float64)[:, None] * inv[None, :]
    cos, sin = np.cos(ang), np.sin(ang)
    return (jnp.asarray(np.concatenate([cos, cos], axis=1), F32),
            jnp.asarray(np.concatenate([-sin, sin], axis=1), F32))


class _Tiles(NamedTuple):
    tm: int
    tn_in: int
    tn_merge: int
    tm_out: int
    tf: int


def _tiles(t):
    return _Tiles(tm=min(1024, t), tn_in=1024, tn_merge=512, tm_out=min(512, t), tf=1024)


def kernel(x, mem, norm_mix_g, w_in, na_q_norm_g, na_k_norm_g, na_rpb, ret_decay_logit_fwd, ret_decay_logit_bwd, ret_gn_g, mem_norm_g, w_mem_kv, xa_q_norm_g, xa_k_norm_g, w_br_na, w_br_ret, w_br_mem, w_out, norm_ffn_g, w_ff1, w_ff2):
    batch, s_len, d = x.shape
    n_mem = mem.shape[1]
    t = batch * s_len
    tiles = _tiles(t)
    cos2, sin2 = _rope_tables(s_len)
    x2d = x.reshape(t, d)
    mem2d = mem.reshape(batch * n_mem, d)
    gm, gn = t // tiles.tm, w_in.shape[2] // tiles.tn_in

    def lane_bcast(v):
        return jnp.broadcast_to(v.astype(F32)[:, None, None], (v.shape[0], 1, LANE))

    def side_job(w):
        r, c = w.shape
        if r % (gm * gn * 16) == 0:
            return w, (r // (gm * gn), c), lambda i, j: (i * gn + j, 0)
        assert r % (gn * 16) == 0 and c % (gm * LANE) == 0, (w.shape, gm, gn)
        return w, (r // gn, c // gm), lambda i, j: (j, i)

    for l in range(w_in.shape[0]):
        sides = [side_job(w) for w in (w_ff1[l], w_ff2[l], w_br_na[l], w_br_ret[l], w_br_mem[l])]
        proj, (w_ff1_bf, w_ff2_bf, w_na_bf, w_ret_bf, w_mem_bf) = _norm_proj(
            x2d, norm_mix_g[l], w_in[l], tiles.tm, tiles.tn_in, side=sides)
        mkv, _ = _norm_proj(mem2d, mem_norm_g[l], w_mem_kv[l], batch * n_mem, tiles.tn_in)

        o_mem = _mem_xattn(proj, mkv, xa_q_norm_g[l], xa_k_norm_g[l], batch, s_len, n_mem)
        o_na = _na_attention(proj, na_rpb[l], na_q_norm_g[l], na_k_norm_g[l], batch, s_len)
        o_ret = _retention(proj, cos2, sin2, lane_bcast(ret_decay_logit_fwd[l]),
                           lane_bcast(ret_decay_logit_bwd[l]), ret_gn_g[l], batch, s_len)

        merged = _merge(o_na, o_ret, o_mem, w_na_bf, w_ret_bf, w_mem_bf, proj, tiles.tm, tiles.tn_merge)
        x1 = _out_proj(merged, w_out[l], x2d, tiles.tm_out)
        x2d = _ffn(x1, norm_ffn_g[l], w_ff1_bf, w_ff2_bf, tiles.tm_out, tiles.tf)
    return x2d.reshape(batch, s_len, d)
```

```python
import functools
from typing import NamedTuple

import jax
import jax.numpy as jnp
import numpy as np
from jax import lax
from jax.experimental import pallas as pl
from jax.experimental.pallas import tpu as pltpu

F32 = jnp.float32
BF16 = jnp.bfloat16

LANE = 128
EPS = 1e-6
NEG = -1e30
LOG2E = 1.4426950408889634

GRID_W = 64
NA_HEADS = 8
NA_HEAD_DIM = 128
NA_MAX_ROWS = 8
NA_COLS = 16
NA_DR = 2 * NA_MAX_ROWS - 1
NA_DC = 2 * NA_COLS - 1
NA_TBL = 3 * NA_DR + 1
NA_QROWS = 4
NA_WIN = NA_QROWS + NA_MAX_ROWS

RET_HEADS = 8
RET_QK_DIM = 128
RET_V_DIM = 256
RET_BLOCK = 256
ROPE_BASE = 10000.0

XA_HEADS = 4
XA_HEAD_DIM = 256

SLAB_NA_Q, SLAB_NA_K, SLAB_NA_V = 0, 8, 16
SLAB_RQ, SLAB_RK, SLAB_RV, SLAB_RG = 24, 32, 40, 56
SLAB_XQ = 72
SLAB_G_NA, SLAB_G_RET, SLAB_G_MEM = 80, 96, 112

V7X_VMEM_BYTES = 64 * 2**20
VMEM_LIMIT_BYTES = V7X_VMEM_BYTES - 6 * 2**20


def _params(*sem):
    return pltpu.CompilerParams(dimension_semantics=sem, vmem_limit_bytes=VMEM_LIMIT_BYTES)


def _rms(x, g):
    return x * lax.rsqrt(jnp.mean(x * x, axis=-1, keepdims=True) + EPS) * g


def _unit_rms(x):
    return x * lax.rsqrt(jnp.mean(x * x, axis=-1, keepdims=True) + EPS)


def _sigmoid(x):
    return 0.5 * jnp.tanh(0.5 * x) + 0.5


def _norm_proj_kernel(n_side, x_ref, g_ref, w_ref, *refs):
    side_in, o_ref = refs[:n_side], refs[n_side]
    side_out, hn_ref = refs[n_side + 1:2 * n_side + 1], refs[2 * n_side + 1]

    @pl.when(pl.program_id(1) == 0)
    def _():
        hn_ref[...] = _rms(x_ref[...], g_ref[...]).astype(BF16)

    acc = jnp.dot(hn_ref[...], w_ref[...].astype(BF16), preferred_element_type=F32)
    for s in range(o_ref.shape[0]):
        o_ref[s] = acc[:, s * LANE:(s + 1) * LANE].astype(BF16)
    for src, dst in zip(side_in, side_out):
        dst[...] = src[...].astype(BF16)


def _norm_proj(x2d, g, w, tm, tn, side=()):
    m, k = x2d.shape
    n = w.shape[1]
    side_specs = [pl.BlockSpec(blk, imap) for _, blk, imap in side]
    out = pl.pallas_call(
        functools.partial(_norm_proj_kernel, len(side)),
        grid=(m // tm, n // tn),
        in_specs=[
            pl.BlockSpec((tm, k), lambda i, j: (i, 0)),
            pl.BlockSpec((1, k), lambda i, j: (0, 0)),
            pl.BlockSpec((k, tn), lambda i, j: (0, j)),
        ] + side_specs,
        out_specs=[pl.BlockSpec((tn // LANE, tm, LANE), lambda i, j: (j, i, 0))] + side_specs,
        out_shape=[jax.ShapeDtypeStruct((n // LANE, m, LANE), BF16)]
        + [jax.ShapeDtypeStruct(a.shape, BF16) for a, _, _ in side],
        scratch_shapes=[pltpu.VMEM((tm, k), BF16)],
        compiler_params=_params("parallel", "arbitrary"),
        name="norm_proj",
    )(x2d, g.reshape(1, k), w, *[a for a, _, _ in side])
    return out[0], out[1:]


def _na_build_tables(h, rpb_ref, t_ref, base_ref):
    qc = lax.broadcasted_iota(jnp.int32, (GRID_W, LANE), 0)
    lane = lax.broadcasted_iota(jnp.int32, (GRID_W, LANE), 1)
    kc = lane & (GRID_W - 1)
    d = jnp.clip(kc - qc, -(NA_COLS - 1), NA_COLS - 1) + (NA_COLS - 1)
    cs = jnp.clip(qc - NA_COLS // 2, 0, GRID_W - NA_COLS)
    col_ok = (kc >= cs) & (kc < cs + NA_COLS)
    left = lane < GRID_W
    neg = jnp.full((GRID_W, LANE), NEG, F32)

    def body(dr, _):
        base = (h * NA_DR + dr) * NA_DC
        val = jnp.zeros((GRID_W, LANE), F32)
        for dd in range(NA_DC):
            val = jnp.where(d == dd, rpb_ref[base + dd], val)
        base_ref[dr] = jnp.where(col_ok, val * LOG2E, NEG)
        return 0

    lax.fori_loop(0, NA_DR, body, 0)
    for dr in range(NA_DR):
        second = base_ref[dr + 1] if dr + 1 < NA_DR else neg
        t_ref[dr] = jnp.where(left, base_ref[dr], second)
        t_ref[NA_DR + dr] = jnp.where(left, base_ref[dr], neg)
        t_ref[2 * NA_DR + dr] = jnp.where(left, neg, base_ref[dr])
    t_ref[3 * NA_DR] = neg


def _na_kernel(rpb_ref, q_ref, k_ref, v_ref, gq_ref, gk_ref, o_ref, kt_ref, va_ref, t_ref, base_ref):
    @pl.when(pl.program_id(1) == 0)
    def _():
        _na_build_tables(pl.program_id(0), rpb_ref, t_ref, base_ref)

    s_len = q_ref.shape[1]
    rows = s_len // GRID_W
    step_tok = NA_QROWS * GRID_W
    win_tok = NA_WIN * GRID_W
    n_tiles = NA_WIN // 2
    gq = gq_ref[...] * gk_ref[...] * (NA_HEAD_DIM ** -0.5 * LOG2E)
    va_ref[:, :NA_HEAD_DIM] = v_ref[0]
    va_ref[:, NA_HEAD_DIM:] = jnp.ones((s_len, NA_HEAD_DIM), BF16)

    def knorm(c, _):
        sl = pl.ds(pl.multiple_of(c * LANE, LANE), LANE)
        kt_ref[c] = _unit_rms(k_ref[0, sl, :].astype(F32)).astype(BF16).T
        return 0

    lax.fori_loop(0, s_len // LANE, knorm, 0, unroll=True)

    def step(i, _):
        r0 = NA_QROWS * i
        ws = jnp.clip(r0 - NA_MAX_ROWS // 2, 0, rows - NA_WIN)
        wp = ws // 2
        qsl = pl.ds(pl.multiple_of(i * step_tok, step_tok), step_tok)
        wsl = pl.ds(pl.multiple_of(ws * GRID_W, LANE), win_tok)
        qn = _rms(q_ref[0, qsl, :].astype(F32), gq).astype(BF16)
        kwin = jnp.concatenate([kt_ref[wp + t] for t in range(n_tiles)], axis=1)
        s = jnp.dot(qn, kwin, preferred_element_type=F32)
        bias_rows = []
        for qr in range(NA_QROWS):
            r = r0 + qr
            rs = jnp.clip(r - NA_MAX_ROWS // 2, 0, rows - NA_MAX_ROWS)
            tiles = []
            for t in range(n_tiles):
                ka = ws + 2 * t
                dr = ka - r + (NA_MAX_ROWS - 1)
                va = (ka >= rs) & (ka < rs + NA_MAX_ROWS)
                vb = (ka + 1 >= rs) & (ka + 1 < rs + NA_MAX_ROWS)
                idx = jnp.where(va, jnp.where(vb, dr, NA_DR + dr),
                                jnp.where(vb, 2 * NA_DR + dr + 1, 3 * NA_DR))
                tiles.append(t_ref[idx])
            bias_rows.append(jnp.concatenate(tiles, axis=1))
        s = s + jnp.concatenate(bias_rows, axis=0)
        e = jnp.exp2(s - jnp.max(s, axis=-1, keepdims=True))
        o = jnp.dot(e.astype(BF16), va_ref[wsl, :], preferred_element_type=F32)
        o_ref[qsl, :] = (o[:, :NA_HEAD_DIM] / o[:, NA_HEAD_DIM:]).astype(BF16)
        return 0

    lax.fori_loop(0, rows // NA_QROWS, step, 0, unroll=True)


def _na_attention(proj, rpb, gq, gk, batch, s_len):
    t = batch * s_len
    g_spec = pl.BlockSpec((1, NA_HEAD_DIM), lambda h, b: (0, 0))
    return pl.pallas_call(
        _na_kernel,
        grid=(NA_HEADS, batch),
        in_specs=[
            pl.BlockSpec(memory_space=pltpu.SMEM),
            pl.BlockSpec((1, s_len, LANE), lambda h, b: (SLAB_NA_Q + h, b, 0)),
            pl.BlockSpec((1, s_len, LANE), lambda h, b: (SLAB_NA_K + h, b, 0)),
            pl.BlockSpec((1, s_len, LANE), lambda h, b: (SLAB_NA_V + h, b, 0)),
            g_spec, g_spec,
        ],
        out_specs=pl.BlockSpec((s_len, NA_HEAD_DIM), lambda h, b: (b, h)),
        out_shape=jax.ShapeDtypeStruct((t, NA_HEADS * NA_HEAD_DIM), BF16),
        scratch_shapes=[pltpu.VMEM((s_len // LANE, NA_HEAD_DIM, LANE), BF16),
                        pltpu.VMEM((s_len, 2 * NA_HEAD_DIM), BF16),
                        pltpu.VMEM((NA_TBL, GRID_W, LANE), F32),
                        pltpu.VMEM((NA_DR, GRID_W, LANE), F32)],
        compiler_params=_params("arbitrary", "arbitrary"),
        name="na_attention",
    )(rpb.reshape(-1), proj, proj, proj, gq.reshape(1, -1), gk.reshape(1, -1))


def _log_sigmoid(x):
    return -(jnp.maximum(-x, 0.0) + jnp.log1p(jnp.exp(-jnp.abs(x))))


def _ret_kernel(q_ref, k_ref, v_ref, g_ref, cos_ref, sin_ref, lf_ref, lb_ref, gn_ref,
                o_ref, qr_ref, kt_ref, d_ref, qdf_ref, qdb_ref, kv_ref, s_ref):
    s_len = q_ref.shape[1]
    c = RET_BLOCK
    nb = s_len // c
    dk = RET_QK_DIM
    half = dk // 2

    lgf = _log_sigmoid(lf_ref[0][:, :1])
    lgb = _log_sigmoid(lb_ref[0][:, :1])
    ic = lax.broadcasted_iota(jnp.int32, (c, 1), 0).astype(F32)
    jr = lax.broadcasted_iota(jnp.int32, (1, c), 1).astype(F32)
    diff = ic - jr
    scale = dk ** -0.5
    d_ref[...] = jnp.exp(jnp.where(diff >= 0, lgf, lgb) * jnp.abs(diff)) * scale
    qdf_ref[...] = jnp.broadcast_to(jnp.exp(lgf * (ic + 1.0)), (c, dk))
    qdb_ref[...] = jnp.broadcast_to(jnp.exp(lgb * (c - ic)), (c, dk))
    kdf = jnp.exp(lgf * (c - 1.0 - jr)) * scale
    kdb = jnp.exp(lgb * jr) * scale
    cd_f = jnp.exp(lgf * c)
    cd_b = jnp.exp(lgb * c)

    def block_v(sl):
        return jnp.concatenate([v_ref[0, sl, :], v_ref[1, sl, :]], axis=1)

    def prep(n, _):
        sl = pl.ds(pl.multiple_of(n * c, c), c)
        cos = cos_ref[sl, :]
        sin = sin_ref[sl, :]
        q = q_ref[0, sl, :].astype(F32)
        k = k_ref[0, sl, :].astype(F32)
        qr_ref[sl, :] = (q * cos + pltpu.roll(q, half, 1) * sin).astype(BF16)
        kt = (k * cos + pltpu.roll(k, half, 1) * sin).T
        kt_ref[n] = kt.astype(BF16)
        lhs = jnp.concatenate([(kt * kdf).astype(BF16), (kt * kdb).astype(BF16)], axis=0)
        kv_ref[n] = jnp.dot(lhs, block_v(sl), preferred_element_type=F32)
        return 0

    lax.fori_loop(0, nb, prep, 0, unroll=True)

    def scan_f(n, sf):
        s_ref[n, :dk, :] = sf.astype(BF16)
        return cd_f * sf + kv_ref[n, :dk, :]

    def scan_b(t, sb):
        n = nb - 1 - t
        s_ref[n, dk:, :] = sb.astype(BF16)
        return cd_b * sb + kv_ref[n, dk:, :]

    zero = jnp.zeros((dk, RET_V_DIM), F32)

    def scan(n, carry):
        return scan_f(n, carry[0]), scan_b(n, carry[1])

    lax.fori_loop(0, nb, scan, (zero, zero), unroll=True)

    gn = gn_ref[0]

    def out(n, _):
        sl = pl.ds(pl.multiple_of(n * c, c), c)
        q = qr_ref[sl, :]
        qf32 = q.astype(F32)
        a = jnp.dot(q, kt_ref[n], preferred_element_type=F32) * d_ref[...]
        lhs = jnp.concatenate([a.astype(BF16), (qf32 * qdf_ref[...]).astype(BF16),
                               (qf32 * qdb_ref[...]).astype(BF16)], axis=1)
        rhs = jnp.concatenate([block_v(sl), s_ref[n]], axis=0)
        o = jnp.dot(lhs, rhs, preferred_element_type=F32)
        mu = jnp.mean(o, axis=-1, keepdims=True)
        oc = o - mu
        y = oc * lax.rsqrt(jnp.mean(oc * oc, axis=-1, keepdims=True) + EPS) * gn
        gate = jnp.concatenate([g_ref[0, sl, :], g_ref[1, sl, :]], axis=1).astype(F32)
        o_ref[sl, :] = (y * gate * _sigmoid(gate)).astype(BF16)
        return 0

    lax.fori_loop(0, nb, out, 0, unroll=True)


def _retention(proj, cos2, sin2, lf, lb, gn, batch, s_len):
    t = batch * s_len
    nb = s_len // RET_BLOCK
    dec_spec = pl.BlockSpec((1, 1, LANE), lambda b, h: (h, 0, 0))
    return pl.pallas_call(
        _ret_kernel,
        grid=(batch, RET_HEADS),
        in_specs=[
            pl.BlockSpec((1, s_len, LANE), lambda b, h: (SLAB_RQ + h, b, 0)),
            pl.BlockSpec((1, s_len, LANE), lambda b, h: (SLAB_RK + h, b, 0)),
            pl.BlockSpec((2, s_len, LANE), lambda b, h: (SLAB_RV // 2 + h, b, 0)),
            pl.BlockSpec((2, s_len, LANE), lambda b, h: (SLAB_RG // 2 + h, b, 0)),
            pl.BlockSpec((s_len, RET_QK_DIM), lambda b, h: (0, 0)),
            pl.BlockSpec((s_len, RET_QK_DIM), lambda b, h: (0, 0)),
            dec_spec, dec_spec,
            pl.BlockSpec((1, 1, RET_V_DIM), lambda b, h: (h, 0, 0)),
        ],
        out_specs=pl.BlockSpec((s_len, RET_V_DIM), lambda b, h: (b, h)),
        out_shape=jax.ShapeDtypeStruct((t, RET_HEADS * RET_V_DIM), BF16),
        scratch_shapes=[
            pltpu.VMEM((s_len, RET_QK_DIM), BF16),
            pltpu.VMEM((nb, RET_QK_DIM, RET_BLOCK), BF16),
            pltpu.VMEM((RET_BLOCK, RET_BLOCK), F32),
            pltpu.VMEM((RET_BLOCK, RET_QK_DIM), F32),
            pltpu.VMEM((RET_BLOCK, RET_QK_DIM), F32),
            pltpu.VMEM((nb, 2 * RET_QK_DIM, RET_V_DIM), F32),
            pltpu.VMEM((nb, 2 * RET_QK_DIM, RET_V_DIM), BF16),
        ],
        compiler_params=_params("parallel", "parallel"),
        name="retention",
    )(proj, proj, proj, proj, cos2, sin2, lf, lb, gn.reshape(RET_HEADS, 1, RET_V_DIM))


def _xa_kernel(q_ref, k_ref, v_ref, gq_ref, gk_ref, o_ref):
    s_len = q_ref.shape[1]
    tq = 512
    gq = gq_ref[...] * gk_ref[...] * (XA_HEAD_DIM ** -0.5 * LOG2E)
    k = jnp.concatenate([k_ref[0], k_ref[1]], axis=1).astype(F32)
    kn = _unit_rms(k).astype(BF16)
    v = jnp.concatenate([v_ref[0], v_ref[1]], axis=1)

    def body(i, _):
        sl = pl.ds(pl.multiple_of(i * tq, tq), tq)
        q = jnp.concatenate([q_ref[0, sl, :], q_ref[1, sl, :]], axis=1).astype(F32)
        qn = _rms(q, gq).astype(BF16)
        s = lax.dot_general(qn, kn, (((1,), (1,)), ((), ())), preferred_element_type=F32)
        e = jnp.exp2(s - jnp.max(s, axis=-1, keepdims=True))
        l = jnp.sum(e, axis=-1, keepdims=True)
        o = jnp.dot(e.astype(BF16), v, preferred_element_type=F32)
        o_ref[sl, :] = (o / l).astype(BF16)
        return 0

    lax.fori_loop(0, s_len // tq, body, 0, unroll=True)


def _mem_xattn(proj, mkv, gq, gk, batch, s_len, n_mem):
    t = batch * s_len
    g_spec = pl.BlockSpec((1, XA_HEAD_DIM), lambda b, h: (0, 0))
    return pl.pallas_call(
        _xa_kernel,
        grid=(batch, XA_HEADS),
        in_specs=[
            pl.BlockSpec((2, s_len, LANE), lambda b, h: (SLAB_XQ // 2 + h, b, 0)),
            pl.BlockSpec((2, n_mem, LANE), lambda b, h: (h, b, 0)),
            pl.BlockSpec((2, n_mem, LANE), lambda b, h: (XA_HEADS + h, b, 0)),
            g_spec, g_spec,
        ],
        out_specs=pl.BlockSpec((s_len, XA_HEAD_DIM), lambda b, h: (b, h)),
        out_shape=jax.ShapeDtypeStruct((t, XA_HEADS * XA_HEAD_DIM), BF16),
        compiler_params=_params("parallel", "parallel"),
        name="mem_xattn",
    )(proj, mkv, mkv, gq.reshape(1, -1), gk.reshape(1, -1))


def _merge_kernel(ona_ref, oret_ref, omem_ref, wna_ref, wret_ref, wmem_ref,
                  gna_ref, gret_ref, gmem_ref, o_ref):
    y_na = jnp.dot(ona_ref[...], wna_ref[...], preferred_element_type=F32)
    y_ret = jnp.dot(oret_ref[...], wret_ref[...], preferred_element_type=F32)
    y_mem = jnp.dot(omem_ref[...], wmem_ref[...], preferred_element_type=F32)
    for j in range(gna_ref.shape[0]):
        cs = slice(j * LANE, (j + 1) * LANE)
        o_ref[:, cs] = (_sigmoid(gna_ref[j].astype(F32)) * y_na[:, cs]
                        + _sigmoid(gret_ref[j].astype(F32)) * y_ret[:, cs]
                        + _sigmoid(gmem_ref[j].astype(F32)) * y_mem[:, cs]).astype(BF16)


def _merge(o_na, o_ret, o_mem, w_na, w_ret, w_mem, proj, tm, tn):
    t = o_na.shape[0]
    d = w_na.shape[1]
    ns = tn // LANE

    def lhs_spec(a):
        return pl.BlockSpec((tm, a.shape[1]), lambda i, j: (i, 0))

    def w_spec(w):
        return pl.BlockSpec((w.shape[0], tn), lambda i, j: (0, j))

    def gate_spec(first):
        return pl.BlockSpec((ns, tm, LANE), lambda i, j: (first // ns + j, i, 0))

    return pl.pallas_call(
        _merge_kernel,
        grid=(t // tm, d // tn),
        in_specs=[lhs_spec(o_na), lhs_spec(o_ret), lhs_spec(o_mem),
                  w_spec(w_na), w_spec(w_ret), w_spec(w_mem),
                  gate_spec(SLAB_G_NA), gate_spec(SLAB_G_RET), gate_spec(SLAB_G_MEM)],
        out_specs=pl.BlockSpec((tm, tn), lambda i, j: (i, j)),
        out_shape=jax.ShapeDtypeStruct((t, d), BF16),
        compiler_params=_params("parallel", "arbitrary"),
        name="merge",
    )(o_na, o_ret, o_mem, w_na, w_ret, w_mem, proj, proj, proj)


def _out_proj_kernel(m_ref, w_ref, x_ref, o_ref, wb_ref):
    @pl.when(pl.program_id(0) == 0)
    def _():
        wb_ref[...] = w_ref[...].astype(BF16)

    o_ref[...] = x_ref[...] + jnp.dot(m_ref[...], wb_ref[...], preferred_element_type=F32)


def _out_proj(merged, w, x2d, tm):
    t, k = merged.shape
    d = w.shape[1]
    return pl.pallas_call(
        _out_proj_kernel,
        grid=(t // tm,),
        in_specs=[pl.BlockSpec((tm, k), lambda i: (i, 0)),
                  pl.BlockSpec((k, d), lambda i: (0, 0), pipeline_mode=pl.Buffered(1)),
                  pl.BlockSpec((tm, d), lambda i: (i, 0))],
        out_specs=pl.BlockSpec((tm, d), lambda i: (i, 0)),
        out_shape=jax.ShapeDtypeStruct((t, d), F32),
        scratch_shapes=[pltpu.VMEM((k, d), BF16)],
        compiler_params=_params("arbitrary"),
        name="out_proj",
    )(merged, w, x2d)


def _ffn_kernel(x_ref, g_ref, w1_ref, w2_ref, o_ref, h_ref):
    def mlp(h):
        a = jnp.maximum(jnp.dot(h, w1_ref[...], preferred_element_type=F32), 0.0)
        return jnp.dot((a * a).astype(BF16), w2_ref[...], preferred_element_type=F32)

    @pl.when(pl.program_id(1) == 0)
    def _():
        x = x_ref[...]
        h = _rms(x, g_ref[...]).astype(BF16)
        h_ref[...] = h
        o_ref[...] = x + mlp(h)

    @pl.when(pl.program_id(1) != 0)
    def _():
        o_ref[...] += mlp(h_ref[...])


def _ffn(x1, g, w1, w2, tm, tf):
    t, d = x1.shape
    dff = w1.shape[1]
    return pl.pallas_call(
        _ffn_kernel,
        grid=(t // tm, dff // tf),
        in_specs=[pl.BlockSpec((tm, d), lambda i, f: (i, 0)),
                  pl.BlockSpec((1, d), lambda i, f: (0, 0)),
                  pl.BlockSpec((d, tf), lambda i, f: (0, f)),
                  pl.BlockSpec((tf, d), lambda i, f: (f, 0))],
        out_specs=pl.BlockSpec((tm, d), lambda i, f: (i, 0)),
        out_shape=jax.ShapeDtypeStruct((t, d), F32),
        scratch_shapes=[pltpu.VMEM((tm, d), BF16)],
        compiler_params=_params("parallel", "arbitrary"),
        name="ffn",
    )(x1, g.reshape(1, d), w1, w2)


def _rope_tables(s_len):
    half = RET_QK_DIM // 2
    inv = np.power(np.float64(ROPE_BASE), -np.arange(half, dtype=np.float64) / half)
    ang = np.arange(s_len, dtype=np.float64)[:, None] * inv[None, :]
    cos, sin = np.cos(ang), np.sin(ang)
    return (jnp.asarray(np.concatenate([cos, cos], axis=1), F32),
            jnp.asarray(np.concatenate([-sin, sin], axis=1), F32))


class _Tiles(NamedTuple):
    tm: int
    tn_in: int
    tn_merge: int
    tm_out: int
    tf: int


def _tiles(t):
    return _Tiles(tm=min(1024, t), tn_in=1024, tn_merge=512, tm_out=min(512, t), tf=1024)


def kernel(x, mem, norm_mix_g, w_in, na_q_norm_g, na_k_norm_g, na_rpb, ret_decay_logit_fwd, ret_decay_logit_bwd, ret_gn_g, mem_norm_g, w_mem_kv, xa_q_norm_g, xa_k_norm_g, w_br_na, w_br_ret, w_br_mem, w_out, norm_ffn_g, w_ff1, w_ff2):
    batch, s_len, d = x.shape
    n_mem = mem.shape[1]
    t = batch * s_len
    tiles = _tiles(t)
    cos2, sin2 = _rope_tables(s_len)
    x2d = x.reshape(t, d)
    mem2d = mem.reshape(batch * n_mem, d)
    gm, gn = t // tiles.tm, w_in.shape[2] // tiles.tn_in

    def lane_bcast(v):
        return jnp.broadcast_to(v.astype(F32)[:, None, None], (v.shape[0], 1, LANE))

    def side_job(w):
        r, c = w.shape
        if r % (gm * gn * 16) == 0:
            return w, (r // (gm * gn), c), lambda i, j: (i * gn + j, 0)
        assert r % (gn * 16) == 0 and c % (gm * LANE) == 0, (w.shape, gm, gn)
        return w, (r // gn, c // gm), lambda i, j: (j, i)

    for l in range(w_in.shape[0]):
        sides = [side_job(w) for w in (w_ff1[l], w_ff2[l], w_br_na[l], w_br_ret[l], w_br_mem[l])]
        proj, (w_ff1_bf, w_ff2_bf, w_na_bf, w_ret_bf, w_mem_bf) = _norm_proj(
            x2d, norm_mix_g[l], w_in[l], tiles.tm, tiles.tn_in, side=sides)
        mkv, _ = _norm_proj(mem2d, mem_norm_g[l], w_mem_kv[l], batch * n_mem, tiles.tn_in)

        o_mem = _mem_xattn(proj, mkv, xa_q_norm_g[l], xa_k_norm_g[l], batch, s_len, n_mem)
        o_na = _na_attention(proj, na_rpb[l], na_q_norm_g[l], na_k_norm_g[l], batch, s_len)
        o_ret = _retention(proj, cos2, sin2, lane_bcast(ret_decay_logit_fwd[l]),
                           lane_bcast(ret_decay_logit_bwd[l]), ret_gn_g[l], batch, s_len)

        merged = _merge(o_na, o_ret, o_mem, w_na_bf, w_ret_bf, w_mem_bf, proj, tiles.tm, tiles.tn_merge)
        x1 = _out_proj(merged, w_out[l], x2d, tiles.tm_out)
        x2d = _ffn(x1, norm_ffn_g[l], w_ff1_bf, w_ff2_bf, tiles.tm_out, tiles.tf)
    return x2d.reshape(batch, s_len, d)
```

```python
import functools
from typing import NamedTuple

import jax
import jax.numpy as jnp
import numpy as np
from jax import lax
from jax.experimental import pallas as pl
from jax.experimental.pallas import tpu as pltpu

F32 = jnp.float32
BF16 = jnp.bfloat16

LANE = 128
EPS = 1e-6
NEG = -1e30
LOG2E = 1.4426950408889634

GRID_W = 64
NA_HEADS = 8
NA_HEAD_DIM = 128
NA_MAX_ROWS = 8
NA_COLS = 16
NA_DR = 2 * NA_MAX_ROWS - 1
NA_DC = 2 * NA_COLS - 1
NA_TBL = 3 * NA_DR + 1
NA_QROWS = 4
NA_WIN = NA_QROWS + NA_MAX_ROWS

RET_HEADS = 8
RET_QK_DIM = 128
RET_V_DIM = 256
RET_BLOCK = 256
ROPE_BASE = 10000.0

XA_HEADS = 4
XA_HEAD_DIM = 256

SLAB_NA_Q, SLAB_NA_K, SLAB_NA_V = 0, 8, 16
SLAB_RQ, SLAB_RK, SLAB_RV, SLAB_RG = 24, 32, 40, 56
SLAB_XQ = 72
SLAB_G_NA, SLAB_G_RET, SLAB_G_MEM = 80, 96, 112

V7X_VMEM_BYTES = 64 * 2**20
VMEM_LIMIT_BYTES = V7X_VMEM_BYTES - 6 * 2**20


def _params(*sem):
    return pltpu.CompilerParams(dimension_semantics=sem, vmem_limit_bytes=VMEM_LIMIT_BYTES)


def _rms(x, g):
    return x * lax.rsqrt(jnp.mean(x * x, axis=-1, keepdims=True) + EPS) * g


def _unit_rms(x):
    return x * lax.rsqrt(jnp.mean(x * x, axis=-1, keepdims=True) + EPS)


def _sigmoid(x):
    return 0.5 * jnp.tanh(0.5 * x) + 0.5


def _norm_proj_kernel(n_side, x_ref, g_ref, w_ref, *refs):
    side_in, o_ref = refs[:n_side], refs[n_side]
    side_out, hn_ref = refs[n_side + 1:2 * n_side + 1], refs[2 * n_side + 1]

    @pl.when(pl.program_id(1) == 0)
    def _():
        hn_ref[...] = _rms(x_ref[...], g_ref[...]).astype(BF16)

    acc = jnp.dot(hn_ref[...], w_ref[...].astype(BF16), preferred_element_type=F32)
    for s in range(o_ref.shape[0]):
        o_ref[s] = acc[:, s * LANE:(s + 1) * LANE].astype(BF16)
    for src, dst in zip(side_in, side_out):
        dst[...] = src[...].astype(BF16)


def _norm_proj(x2d, g, w, tm, tn, side=()):
    m, k = x2d.shape
    n = w.shape[1]
    side_specs = [pl.BlockSpec(blk, imap) for _, blk, imap in side]
    out = pl.pallas_call(
        functools.partial(_norm_proj_kernel, len(side)),
        grid=(m // tm, n // tn),
        in_specs=[
            pl.BlockSpec((tm, k), lambda i, j: (i, 0)),
            pl.BlockSpec((1, k), lambda i, j: (0, 0)),
            pl.BlockSpec((k, tn), lambda i, j: (0, j)),
        ] + side_specs,
        out_specs=[pl.BlockSpec((tn // LANE, tm, LANE), lambda i, j: (j, i, 0))] + side_specs,
        out_shape=[jax.ShapeDtypeStruct((n // LANE, m, LANE), BF16)]
        + [jax.ShapeDtypeStruct(a.shape, BF16) for a, _, _ in side],
        scratch_shapes=[pltpu.VMEM((tm, k), BF16)],
        compiler_params=_params("parallel", "arbitrary"),
        name="norm_proj",
    )(x2d, g.reshape(1, k), w, *[a for a, _, _ in side])
    return out[0], out[1:]


def _na_build_tables(h, rpb_ref, t_ref, base_ref):
    qc = lax.broadcasted_iota(jnp.int32, (GRID_W, LANE), 0)
    lane = lax.broadcasted_iota(jnp.int32, (GRID_W, LANE), 1)
    kc = lane & (GRID_W - 1)
    d = jnp.clip(kc - qc, -(NA_COLS - 1), NA_COLS - 1) + (NA_COLS - 1)
    cs = jnp.clip(qc - NA_COLS // 2, 0, GRID_W - NA_COLS)
    col_ok = (kc >= cs) & (kc < cs + NA_COLS)
    left = lane < GRID_W
    neg = jnp.full((GRID_W, LANE), NEG, F32)

    def body(dr, _):
        base = (h * NA_DR + dr) * NA_DC
        val = jnp.zeros((GRID_W, LANE), F32)
        for dd in range(NA_DC):
            val = jnp.where(d == dd, rpb_ref[base + dd], val)
        base_ref[dr] = jnp.where(col_ok, val * LOG2E, NEG)
        return 0

    lax.fori_loop(0, NA_DR, body, 0)
    for dr in range(NA_DR):
        second = base_ref[dr + 1] if dr + 1 < NA_DR else neg
        t_ref[dr] = jnp.where(left, base_ref[dr], second)
        t_ref[NA_DR + dr] = jnp.where(left, base_ref[dr], neg)
        t_ref[2 * NA_DR + dr] = jnp.where(left, neg, base_ref[dr])
    t_ref[3 * NA_DR] = neg


def _na_kernel(rpb_ref, q_ref, k_ref, v_ref, gq_ref, gk_ref, o_ref, kt_ref, va_ref, t_ref, base_ref):
    @pl.when(pl.program_id(1) == 0)
    def _():
        _na_build_tables(pl.program_id(0), rpb_ref, t_ref, base_ref)

    s_len = q_ref.shape[1]
    rows = s_len // GRID_W
    step_tok = NA_QROWS * GRID_W
    win_tok = NA_WIN * GRID_W
    n_tiles = NA_WIN // 2
    gq = gq_ref[...] * gk_ref[...] * (NA_HEAD_DIM ** -0.5 * LOG2E)
    va_ref[:, :NA_HEAD_DIM] = v_ref[0]
    va_ref[:, NA_HEAD_DIM:] = jnp.ones((s_len, NA_HEAD_DIM), BF16)

    def knorm(c, _):
        sl = pl.ds(pl.multiple_of(c * LANE, LANE), LANE)
        kt_ref[c] = _unit_rms(k_ref[0, sl, :].astype(F32)).astype(BF16).T
        return 0

    lax.fori_loop(0, s_len // LANE, knorm, 0, unroll=True)

    def step(i, _):
        r0 = NA_QROWS * i
        ws = jnp.clip(r0 - NA_MAX_ROWS // 2, 0, rows - NA_WIN)
        wp = ws // 2
        qsl = pl.ds(pl.multiple_of(i * step_tok, step_tok), step_tok)
        wsl = pl.ds(pl.multiple_of(ws * GRID_W, LANE), win_tok)
        qn = _rms(q_ref[0, qsl, :].astype(F32), gq).astype(BF16)
        kwin = jnp.concatenate([kt_ref[wp + t] for t in range(n_tiles)], axis=1)
        s = jnp.dot(qn, kwin, preferred_element_type=F32)
        bias_rows = []
        for qr in range(NA_QROWS):
            r = r0 + qr
            rs = jnp.clip(r - NA_MAX_ROWS // 2, 0, rows - NA_MAX_ROWS)
            tiles = []
            for t in range(n_tiles):
                ka = ws + 2 * t
                dr = ka - r + (NA_MAX_ROWS - 1)
                va = (ka >= rs) & (ka < rs + NA_MAX_ROWS)
                vb = (ka + 1 >= rs) & (ka + 1 < rs + NA_MAX_ROWS)
                idx = jnp.where(va, jnp.where(vb, dr, NA_DR + dr),
                                jnp.where(vb, 2 * NA_DR + dr + 1, 3 * NA_DR))
                tiles.append(t_ref[idx])
            bias_rows.append(jnp.concatenate(tiles, axis=1))
        s = s + jnp.concatenate(bias_rows, axis=0)
        e = jnp.exp2(s - jnp.max(s, axis=-1, keepdims=True))
        o = jnp.dot(e.astype(BF16), va_ref[wsl, :], preferred_element_type=F32)
        o_ref[qsl, :] = (o[:, :NA_HEAD_DIM] / o[:, NA_HEAD_DIM:]).astype(BF16)
        return 0

    lax.fori_loop(0, rows // NA_QROWS, step, 0, unroll=True)


def _na_attention(proj, rpb, gq, gk, batch, s_len):
    t = batch * s_len
    g_spec = pl.BlockSpec((1, NA_HEAD_DIM), lambda h, b: (0, 0))
    return pl.pallas_call(
        _na_kernel,
        grid=(NA_HEADS, batch),
        in_specs=[
            pl.BlockSpec(memory_space=pltpu.SMEM),
            pl.BlockSpec((1, s_len, LANE), lambda h, b: (SLAB_NA_Q + h, b, 0)),
            pl.BlockSpec((1, s_len, LANE), lambda h, b: (SLAB_NA_K + h, b, 0)),
            pl.BlockSpec((1, s_len, LANE), lambda h, b: (SLAB_NA_V + h, b, 0)),
            g_spec, g_spec,
        ],
        out_specs=pl.BlockSpec((s_len, NA_HEAD_DIM), lambda h, b: (b, h)),
        out_shape=jax.ShapeDtypeStruct((t, NA_HEADS * NA_HEAD_DIM), BF16),
        scratch_shapes=[pltpu.VMEM((s_len // LANE, NA_HEAD_DIM, LANE), BF16),
                        pltpu.VMEM((s_len, 2 * NA_HEAD_DIM), BF16),
                        pltpu.VMEM((NA_TBL, GRID_W, LANE), F32),
                        pltpu.VMEM((NA_DR, GRID_W, LANE), F32)],
        compiler_params=_params("arbitrary", "arbitrary"),
        name="na_attention",
    )(rpb.reshape(-1), proj, proj, proj, gq.reshape(1, -1), gk.reshape(1, -1))


def _log_sigmoid(x):
    return -(jnp.maximum(-x, 0.0) + jnp.log1p(jnp.exp(-jnp.abs(x))))


def _ret_kernel(lf_ref, lb_ref, q_ref, k_ref, v_ref, g_ref, cos_ref, sin_ref, gn_ref,
                o_ref, qr_ref, kt_ref, d_ref, qdf_ref, qdb_ref, kv_ref, s_ref):
    s_len = q_ref.shape[1]
    c = RET_BLOCK
    nb = s_len // c
    dk = RET_QK_DIM
    half = dk // 2

    h = pl.program_id(1)
    lgf = _log_sigmoid(jnp.full((1, 1), lf_ref[h], F32))
    lgb = _log_sigmoid(jnp.full((1, 1), lb_ref[h], F32))
    ic = lax.broadcasted_iota(jnp.int32, (c, 1), 0).astype(F32)
    jr = lax.broadcasted_iota(jnp.int32, (1, c), 1).astype(F32)
    diff = ic - jr
    scale = dk ** -0.5
    d_ref[...] = jnp.exp(jnp.where(diff >= 0, lgf, lgb) * jnp.abs(diff)) * scale
    qdf_ref[...] = jnp.broadcast_to(jnp.exp(lgf * (ic + 1.0)), (c, dk))
    qdb_ref[...] = jnp.broadcast_to(jnp.exp(lgb * (c - ic)), (c, dk))
    kdf = jnp.exp(lgf * (c - 1.0 - jr)) * scale
    kdb = jnp.exp(lgb * jr) * scale
    cd_f = jnp.exp(lgf * c)
    cd_b = jnp.exp(lgb * c)

    def block_v(sl):
        return jnp.concatenate([v_ref[0, sl, :], v_ref[1, sl, :]], axis=1)

    def prep(n, _):
        sl = pl.ds(pl.multiple_of(n * c, c), c)
        cos = cos_ref[sl, :]
        sin = sin_ref[sl, :]
        q = q_ref[0, sl, :].astype(F32)
        k = k_ref[0, sl, :].astype(F32)
        qr_ref[sl, :] = (q * cos + pltpu.roll(q, half, 1) * sin).astype(BF16)
        kt = (k * cos + pltpu.roll(k, half, 1) * sin).T
        kt_ref[n] = kt.astype(BF16)
        lhs = jnp.concatenate([(kt * kdf).astype(BF16), (kt * kdb).astype(BF16)], axis=0)
        kv_ref[n] = jnp.dot(lhs, block_v(sl), preferred_element_type=F32)
        return 0

    lax.fori_loop(0, nb, prep, 0, unroll=True)

    def scan_f(n, sf):
        s_ref[n, :dk, :] = sf.astype(BF16)
        return cd_f * sf + kv_ref[n, :dk, :]

    def scan_b(t, sb):
        n = nb - 1 - t
        s_ref[n, dk:, :] = sb.astype(BF16)
        return cd_b * sb + kv_ref[n, dk:, :]

    zero = jnp.zeros((dk, RET_V_DIM), F32)

    def scan(n, carry):
        return scan_f(n, carry[0]), scan_b(n, carry[1])

    lax.fori_loop(0, nb, scan, (zero, zero), unroll=True)

    gn = gn_ref[0]

    def out(n, _):
        sl = pl.ds(pl.multiple_of(n * c, c), c)
        q = qr_ref[sl, :]
        qf32 = q.astype(F32)
        a = jnp.dot(q, kt_ref[n], preferred_element_type=F32) * d_ref[...]
        lhs = jnp.concatenate([a.astype(BF16), (qf32 * qdf_ref[...]).astype(BF16),
                               (qf32 * qdb_ref[...]).astype(BF16)], axis=1)
        rhs = jnp.concatenate([block_v(sl), s_ref[n]], axis=0)
        o = jnp.dot(lhs, rhs, preferred_element_type=F32)
        mu = jnp.mean(o, axis=-1, keepdims=True)
        oc = o - mu
        y = oc * lax.rsqrt(jnp.mean(oc * oc, axis=-1, keepdims=True) + EPS) * gn
        gate = jnp.concatenate([g_ref[0, sl, :], g_ref[1, sl, :]], axis=1).astype(F32)
        o_ref[sl, :] = (y * gate * _sigmoid(gate)).astype(BF16)
        return 0

    lax.fori_loop(0, nb, out, 0, unroll=True)


def _retention(proj, cos2, sin2, lf, lb, gn, batch, s_len):
    t = batch * s_len
    nb = s_len // RET_BLOCK
    smem = pl.BlockSpec(memory_space=pltpu.SMEM)
    return pl.pallas_call(
        _ret_kernel,
        grid=(batch, RET_HEADS),
        in_specs=[
            smem, smem,
            pl.BlockSpec((1, s_len, LANE), lambda b, h: (SLAB_RQ + h, b, 0)),
            pl.BlockSpec((1, s_len, LANE), lambda b, h: (SLAB_RK + h, b, 0)),
            pl.BlockSpec((2, s_len, LANE), lambda b, h: (SLAB_RV // 2 + h, b, 0)),
            pl.BlockSpec((2, s_len, LANE), lambda b, h: (SLAB_RG // 2 + h, b, 0)),
            pl.BlockSpec((s_len, RET_QK_DIM), lambda b, h: (0, 0)),
            pl.BlockSpec((s_len, RET_QK_DIM), lambda b, h: (0, 0)),
            pl.BlockSpec((1, 1, RET_V_DIM), lambda b, h: (h, 0, 0)),
        ],
        out_specs=pl.BlockSpec((s_len, RET_V_DIM), lambda b, h: (b, h)),
        out_shape=jax.ShapeDtypeStruct((t, RET_HEADS * RET_V_DIM), BF16),
        scratch_shapes=[
            pltpu.VMEM((s_len, RET_QK_DIM), BF16),
            pltpu.VMEM((nb, RET_QK_DIM, RET_BLOCK), BF16),
            pltpu.VMEM((RET_BLOCK, RET_BLOCK), F32),
            pltpu.VMEM((RET_BLOCK, RET_QK_DIM), F32),
            pltpu.VMEM((RET_BLOCK, RET_QK_DIM), F32),
            pltpu.VMEM((nb, 2 * RET_QK_DIM, RET_V_DIM), F32),
            pltpu.VMEM((nb, 2 * RET_QK_DIM, RET_V_DIM), BF16),
        ],
        compiler_params=_params("parallel", "parallel"),
        name="retention",
    )(lf, lb, proj, proj, proj, proj, cos2, sin2, gn.reshape(RET_HEADS, 1, RET_V_DIM))


def _xa_kernel(q_ref, k_ref, v_ref, gq_ref, gk_ref, o_ref):
    s_len = q_ref.shape[1]
    tq = 512
    gq = gq_ref[...] * gk_ref[...] * (XA_HEAD_DIM ** -0.5 * LOG2E)
    k = jnp.concatenate([k_ref[0], k_ref[1]], axis=1).astype(F32)
    kn = _unit_rms(k).astype(BF16)
    v = jnp.concatenate([v_ref[0], v_ref[1]], axis=1)

    def body(i, _):
        sl = pl.ds(pl.multiple_of(i * tq, tq), tq)
        q = jnp.concatenate([q_ref[0, sl, :], q_ref[1, sl, :]], axis=1).astype(F32)
        qn = _rms(q, gq).astype(BF16)
        s = lax.dot_general(qn, kn, (((1,), (1,)), ((), ())), preferred_element_type=F32)
        e = jnp.exp2(s - jnp.max(s, axis=-1, keepdims=True))
        l = jnp.sum(e, axis=-1, keepdims=True)
        o = jnp.dot(e.astype(BF16), v, preferred_element_type=F32)
        o_ref[sl, :] = (o / l).astype(BF16)
        return 0

    lax.fori_loop(0, s_len // tq, body, 0, unroll=True)


def _mem_xattn(proj, mkv, gq, gk, batch, s_len, n_mem):
    t = batch * s_len
    g_spec = pl.BlockSpec((1, XA_HEAD_DIM), lambda b, h: (0, 0))
    return pl.pallas_call(
        _xa_kernel,
        grid=(batch, XA_HEADS),
        in_specs=[
            pl.BlockSpec((2, s_len, LANE), lambda b, h: (SLAB_XQ // 2 + h, b, 0)),
            pl.BlockSpec((2, n_mem, LANE), lambda b, h: (h, b, 0)),
            pl.BlockSpec((2, n_mem, LANE), lambda b, h: (XA_HEADS + h, b, 0)),
            g_spec, g_spec,
        ],
        out_specs=pl.BlockSpec((s_len, XA_HEAD_DIM), lambda b, h: (b, h)),
        out_shape=jax.ShapeDtypeStruct((t, XA_HEADS * XA_HEAD_DIM), BF16),
        compiler_params=_params("parallel", "parallel"),
        name="mem_xattn",
    )(proj, mkv, mkv, gq.reshape(1, -1), gk.reshape(1, -1))


def _merge_kernel(ona_ref, oret_ref, omem_ref, wna_ref, wret_ref, wmem_ref,
                  gna_ref, gret_ref, gmem_ref, o_ref):
    y_na = jnp.dot(ona_ref[...], wna_ref[...], preferred_element_type=F32)
    y_ret = jnp.dot(oret_ref[...], wret_ref[...], preferred_element_type=F32)
    y_mem = jnp.dot(omem_ref[...], wmem_ref[...], preferred_element_type=F32)
    for j in range(gna_ref.shape[0]):
        cs = slice(j * LANE, (j + 1) * LANE)
        o_ref[:, cs] = (_sigmoid(gna_ref[j].astype(F32)) * y_na[:, cs]
                        + _sigmoid(gret_ref[j].astype(F32)) * y_ret[:, cs]
                        + _sigmoid(gmem_ref[j].astype(F32)) * y_mem[:, cs]).astype(BF16)


def _merge(o_na, o_ret, o_mem, w_na, w_ret, w_mem, proj, tm, tn):
    t = o_na.shape[0]
    d = w_na.shape[1]
    ns = tn // LANE

    def lhs_spec(a):
        return pl.BlockSpec((tm, a.shape[1]), lambda i, j: (i, 0))

    def w_spec(w):
        return pl.BlockSpec((w.shape[0], tn), lambda i, j: (0, j))

    def gate_spec(first):
        return pl.BlockSpec((ns, tm, LANE), lambda i, j: (first // ns + j, i, 0))

    return pl.pallas_call(
        _merge_kernel,
        grid=(t // tm, d // tn),
        in_specs=[lhs_spec(o_na), lhs_spec(o_ret), lhs_spec(o_mem),
                  w_spec(w_na), w_spec(w_ret), w_spec(w_mem),
                  gate_spec(SLAB_G_NA), gate_spec(SLAB_G_RET), gate_spec(SLAB_G_MEM)],
        out_specs=pl.BlockSpec((tm, tn), lambda i, j: (i, j)),
        out_shape=jax.ShapeDtypeStruct((t, d), BF16),
        compiler_params=_params("parallel", "arbitrary"),
        name="merge",
    )(o_na, o_ret, o_mem, w_na, w_ret, w_mem, proj, proj, proj)


def _out_proj_kernel(m_ref, w_ref, x_ref, o_ref, wb_ref):
    @pl.when(pl.program_id(0) == 0)
    def _():
        wb_ref[...] = w_ref[...].astype(BF16)

    o_ref[...] = x_ref[...] + jnp.dot(m_ref[...], wb_ref[...], preferred_element_type=F32)


def _out_proj(merged, w, x2d, tm):
    t, k = merged.shape
    d = w.shape[1]
    return pl.pallas_call(
        _out_proj_kernel,
        grid=(t // tm,),
        in_specs=[pl.BlockSpec((tm, k), lambda i: (i, 0)),
                  pl.BlockSpec((k, d), lambda i: (0, 0), pipeline_mode=pl.Buffered(1)),
                  pl.BlockSpec((tm, d), lambda i: (i, 0))],
        out_specs=pl.BlockSpec((tm, d), lambda i: (i, 0)),
        out_shape=jax.ShapeDtypeStruct((t, d), F32),
        scratch_shapes=[pltpu.VMEM((k, d), BF16)],
        compiler_params=_params("arbitrary"),
        name="out_proj",
    )(merged, w, x2d)


def _ffn_kernel(x_ref, g_ref, w1_ref, w2_ref, o_ref, h_ref):
    def mlp(h):
        a = jnp.maximum(jnp.dot(h, w1_ref[...], preferred_element_type=F32), 0.0)
        return jnp.dot((a * a).astype(BF16), w2_ref[...], preferred_element_type=F32)

    @pl.when(pl.program_id(1) == 0)
    def _():
        x = x_ref[...]
        h = _rms(x, g_ref[...]).astype(BF16)
        h_ref[...] = h
        o_ref[...] = x + mlp(h)

    @pl.when(pl.program_id(1) != 0)
    def _():
        o_ref[...] += mlp(h_ref[...])


def _ffn(x1, g, w1, w2, tm, tf):
    t, d = x1.shape
    dff = w1.shape[1]
    return pl.pallas_call(
        _ffn_kernel,
        grid=(t // tm, dff // tf),
        in_specs=[pl.BlockSpec((tm, d), lambda i, f: (i, 0)),
                  pl.BlockSpec((1, d), lambda i, f: (0, 0)),
                  pl.BlockSpec((d, tf), lambda i, f: (0, f)),
                  pl.BlockSpec((tf, d), lambda i, f: (f, 0))],
        out_specs=pl.BlockSpec((tm, d), lambda i, f: (i, 0)),
        out_shape=jax.ShapeDtypeStruct((t, d), F32),
        scratch_shapes=[pltpu.VMEM((tm, d), BF16)],
        compiler_params=_params("parallel", "arbitrary"),
        name="ffn",
    )(x1, g.reshape(1, d), w1, w2)


def _rope_tables(s_len):
    half = RET_QK_DIM // 2
    inv = np.power(np.float64(ROPE_BASE), -np.arange(half, dtype=np.float64) / half)
    ang = np.arange(s_len, dtype=np.float64)[:, None] * inv[None, :]
    cos, sin = np.cos(ang), np.sin(ang)
    return (jnp.asarray(np.concatenate([cos, cos], axis=1), F32),
            jnp.asarray(np.concatenate([-sin, sin], axis=1), F32))


class _Tiles(NamedTuple):
    tm: int
    tn_in: int
    tn_merge: int
    tm_out: int
    tf: int
    tn_mem: int


def _tiles(t):
    return _Tiles(tm=min(1024, t), tn_in=1024, tn_merge=512, tm_out=min(512, t), tf=1024, tn_mem=512)


def kernel(x, mem, norm_mix_g, w_in, na_q_norm_g, na_k_norm_g, na_rpb, ret_decay_logit_fwd, ret_decay_logit_bwd, ret_gn_g, mem_norm_g, w_mem_kv, xa_q_norm_g, xa_k_norm_g, w_br_na, w_br_ret, w_br_mem, w_out, norm_ffn_g, w_ff1, w_ff2):
    batch, s_len, d = x.shape
    n_mem = mem.shape[1]
    t = batch * s_len
    tiles = _tiles(t)
    cos2, sin2 = _rope_tables(s_len)
    x2d = x.reshape(t, d)
    mem2d = mem.reshape(batch * n_mem, d)
    gm, gn = t // tiles.tm, w_in.shape[2] // tiles.tn_in

    def side_job(w):
        r, c = w.shape
        if r % (gm * gn * 16) == 0:
            return w, (r // (gm * gn), c), lambda i, j: (i * gn + j, 0)
        assert r % (gn * 16) == 0 and c % (gm * LANE) == 0, (w.shape, gm, gn)
        return w, (r // gn, c // gm), lambda i, j: (j, i)

    for l in range(w_in.shape[0]):
        sides = [side_job(w) for w in (w_ff1[l], w_ff2[l], w_br_na[l], w_br_ret[l], w_br_mem[l])]
        proj, (w_ff1_bf, w_ff2_bf, w_na_bf, w_ret_bf, w_mem_bf) = _norm_proj(
            x2d, norm_mix_g[l], w_in[l], tiles.tm, tiles.tn_in, side=sides)
        mkv, _ = _norm_proj(mem2d, mem_norm_g[l], w_mem_kv[l], batch * n_mem, tiles.tn_mem)

        o_mem = _mem_xattn(proj, mkv, xa_q_norm_g[l], xa_k_norm_g[l], batch, s_len, n_mem)
        o_na = _na_attention(proj, na_rpb[l], na_q_norm_g[l], na_k_norm_g[l], batch, s_len)
        o_ret = _retention(proj, cos2, sin2, ret_decay_logit_fwd[l], ret_decay_logit_bwd[l], ret_gn_g[l],
                           batch, s_len)

        merged = _merge(o_na, o_ret, o_mem, w_na_bf, w_ret_bf, w_mem_bf, proj, tiles.tm, tiles.tn_merge)
        x1 = _out_proj(merged, w_out[l], x2d, tiles.tm_out)
        x2d = _ffn(x1, norm_ffn_g[l], w_ff1_bf, w_ff2_bf, tiles.tm_out, tiles.tf)
    return x2d.reshape(batch, s_len, d)
```

```python
import functools
from typing import NamedTuple

import jax
import jax.numpy as jnp
import numpy as np
from jax import lax
from jax.experimental import pallas as pl
from jax.experimental.pallas import tpu as pltpu

F32 = jnp.float32
BF16 = jnp.bfloat16

LANE = 128
EPS = 1e-6
NEG = -1e30
LOG2E = 1.4426950408889634

GRID_W = 64
NA_HEADS = 8
NA_HEAD_DIM = 128
NA_MAX_ROWS = 8
NA_COLS = 16
NA_DR = 2 * NA_MAX_ROWS - 1
NA_DC = 2 * NA_COLS - 1
NA_TBL = 3 * NA_DR + 1
NA_QROWS = 4
NA_WIN = NA_QROWS + NA_MAX_ROWS

RET_HEADS = 8
RET_QK_DIM = 128
RET_V_DIM = 256
RET_BLOCK = 256
ROPE_BASE = 10000.0

XA_HEADS = 4
XA_HEAD_DIM = 256

SLAB_NA_Q, SLAB_NA_K, SLAB_NA_V = 0, 8, 16
SLAB_RQ, SLAB_RK, SLAB_RV, SLAB_RG = 24, 32, 40, 56
SLAB_XQ = 72
SLAB_G_NA, SLAB_G_RET, SLAB_G_MEM = 80, 96, 112

V7X_VMEM_BYTES = 64 * 2**20
VMEM_LIMIT_BYTES = V7X_VMEM_BYTES - 6 * 2**20


def _params(*sem):
    return pltpu.CompilerParams(dimension_semantics=sem, vmem_limit_bytes=VMEM_LIMIT_BYTES)


def _rms(x, g):
    return x * lax.rsqrt(jnp.mean(x * x, axis=-1, keepdims=True) + EPS) * g


def _unit_rms(x):
    return x * lax.rsqrt(jnp.mean(x * x, axis=-1, keepdims=True) + EPS)


def _sigmoid(x):
    return 0.5 * jnp.tanh(0.5 * x) + 0.5


def _norm_proj_kernel(n_side, x_ref, g_ref, w_ref, *refs):
    side_in, o_ref = refs[:n_side], refs[n_side]
    side_out, hn_ref = refs[n_side + 1:2 * n_side + 1], refs[2 * n_side + 1]

    @pl.when(pl.program_id(1) == 0)
    def _():
        hn_ref[...] = _rms(x_ref[...], g_ref[...]).astype(BF16)

    acc = jnp.dot(hn_ref[...], w_ref[...].astype(BF16), preferred_element_type=F32)
    for s in range(o_ref.shape[0]):
        o_ref[s] = acc[:, s * LANE:(s + 1) * LANE].astype(BF16)
    for src, dst in zip(side_in, side_out):
        dst[...] = src[...].astype(BF16)


def _norm_proj(x2d, g, w, tm, tn, side=()):
    m, k = x2d.shape
    n = w.shape[1]
    side_specs = [pl.BlockSpec(blk, imap) for _, blk, imap in side]
    out = pl.pallas_call(
        functools.partial(_norm_proj_kernel, len(side)),
        grid=(m // tm, n // tn),
        in_specs=[
            pl.BlockSpec((tm, k), lambda i, j: (i, 0)),
            pl.BlockSpec((1, k), lambda i, j: (0, 0)),
            pl.BlockSpec((k, tn), lambda i, j: (0, j)),
        ] + side_specs,
        out_specs=[pl.BlockSpec((tn // LANE, tm, LANE), lambda i, j: (j, i, 0))] + side_specs,
        out_shape=[jax.ShapeDtypeStruct((n // LANE, m, LANE), BF16)]
        + [jax.ShapeDtypeStruct(a.shape, BF16) for a, _, _ in side],
        scratch_shapes=[pltpu.VMEM((tm, k), BF16)],
        compiler_params=_params("parallel", "arbitrary"),
        name="norm_proj",
    )(x2d, g.reshape(1, k), w, *[a for a, _, _ in side])
    return out[0], out[1:]


def _na_build_tables(h, rpb_ref, t_ref, base_ref):
    qc = lax.broadcasted_iota(jnp.int32, (GRID_W, LANE), 0)
    lane = lax.broadcasted_iota(jnp.int32, (GRID_W, LANE), 1)
    kc = lane & (GRID_W - 1)
    d = jnp.clip(kc - qc, -(NA_COLS - 1), NA_COLS - 1) + (NA_COLS - 1)
    cs = jnp.clip(qc - NA_COLS // 2, 0, GRID_W - NA_COLS)
    col_ok = (kc >= cs) & (kc < cs + NA_COLS)
    left = lane < GRID_W
    neg = jnp.full((GRID_W, LANE), NEG, F32)

    def body(dr, _):
        base = (h * NA_DR + dr) * NA_DC
        val = jnp.zeros((GRID_W, LANE), F32)
        for dd in range(NA_DC):
            val = jnp.where(d == dd, rpb_ref[base + dd], val)
        base_ref[dr] = jnp.where(col_ok, val * LOG2E, NEG)
        return 0

    lax.fori_loop(0, NA_DR, body, 0)
    for dr in range(NA_DR):
        second = base_ref[dr + 1] if dr + 1 < NA_DR else neg
        t_ref[dr] = jnp.where(left, base_ref[dr], second)
        t_ref[NA_DR + dr] = jnp.where(left, base_ref[dr], neg)
        t_ref[2 * NA_DR + dr] = jnp.where(left, neg, base_ref[dr])
    t_ref[3 * NA_DR] = neg


def _na_kernel(rpb_ref, q_ref, k_ref, v_ref, gq_ref, gk_ref, o_ref, kt_ref, va_ref, t_ref, base_ref):
    @pl.when(pl.program_id(1) == 0)
    def _():
        _na_build_tables(pl.program_id(0), rpb_ref, t_ref, base_ref)

    s_len = q_ref.shape[1]
    rows = s_len // GRID_W
    step_tok = NA_QROWS * GRID_W
    win_tok = NA_WIN * GRID_W
    n_tiles = NA_WIN // 2
    gq = gq_ref[...] * gk_ref[...] * (NA_HEAD_DIM ** -0.5 * LOG2E)
    va_ref[:, :NA_HEAD_DIM] = v_ref[0]
    va_ref[:, NA_HEAD_DIM:] = jnp.ones((s_len, NA_HEAD_DIM), BF16)

    def knorm(c, _):
        sl = pl.ds(pl.multiple_of(c * LANE, LANE), LANE)
        kt_ref[c] = _unit_rms(k_ref[0, sl, :].astype(F32)).astype(BF16).T
        return 0

    lax.fori_loop(0, s_len // LANE, knorm, 0, unroll=True)

    def step(i, _):
        r0 = NA_QROWS * i
        ws = jnp.clip(r0 - NA_MAX_ROWS // 2, 0, rows - NA_WIN)
        wp = ws // 2
        qsl = pl.ds(pl.multiple_of(i * step_tok, step_tok), step_tok)
        wsl = pl.ds(pl.multiple_of(ws * GRID_W, LANE), win_tok)
        qn = _rms(q_ref[0, qsl, :].astype(F32), gq).astype(BF16)
        kwin = jnp.concatenate([kt_ref[wp + t] for t in range(n_tiles)], axis=1)
        s = jnp.dot(qn, kwin, preferred_element_type=F32)
        bias_rows = []
        for qr in range(NA_QROWS):
            r = r0 + qr
            rs = jnp.clip(r - NA_MAX_ROWS // 2, 0, rows - NA_MAX_ROWS)
            tiles = []
            for t in range(n_tiles):
                ka = ws + 2 * t
                dr = ka - r + (NA_MAX_ROWS - 1)
                va = (ka >= rs) & (ka < rs + NA_MAX_ROWS)
                vb = (ka + 1 >= rs) & (ka + 1 < rs + NA_MAX_ROWS)
                idx = jnp.where(va, jnp.where(vb, dr, NA_DR + dr),
                                jnp.where(vb, 2 * NA_DR + dr + 1, 3 * NA_DR))
                tiles.append(t_ref[idx])
            bias_rows.append(jnp.concatenate(tiles, axis=1))
        s = s + jnp.concatenate(bias_rows, axis=0)
        e = jnp.exp2(s - jnp.max(s, axis=-1, keepdims=True))
        o = jnp.dot(e.astype(BF16), va_ref[wsl, :], preferred_element_type=F32)
        o_ref[qsl, :] = (o[:, :NA_HEAD_DIM] / o[:, NA_HEAD_DIM:]).astype(BF16)
        return 0

    lax.fori_loop(0, rows // NA_QROWS, step, 0, unroll=True)


def _na_attention(proj, rpb, gq, gk, batch, s_len):
    t = batch * s_len
    g_spec = pl.BlockSpec((1, NA_HEAD_DIM), lambda h, b: (0, 0))
    return pl.pallas_call(
        _na_kernel,
        grid=(NA_HEADS, batch),
        in_specs=[
            pl.BlockSpec(memory_space=pltpu.SMEM),
            pl.BlockSpec((1, s_len, LANE), lambda h, b: (SLAB_NA_Q + h, b, 0)),
            pl.BlockSpec((1, s_len, LANE), lambda h, b: (SLAB_NA_K + h, b, 0)),
            pl.BlockSpec((1, s_len, LANE), lambda h, b: (SLAB_NA_V + h, b, 0)),
            g_spec, g_spec,
        ],
        out_specs=pl.BlockSpec((s_len, NA_HEAD_DIM), lambda h, b: (b, h)),
        out_shape=jax.ShapeDtypeStruct((t, NA_HEADS * NA_HEAD_DIM), BF16),
        scratch_shapes=[pltpu.VMEM((s_len // LANE, NA_HEAD_DIM, LANE), BF16),
                        pltpu.VMEM((s_len, 2 * NA_HEAD_DIM), BF16),
                        pltpu.VMEM((NA_TBL, GRID_W, LANE), F32),
                        pltpu.VMEM((NA_DR, GRID_W, LANE), F32)],
        compiler_params=_params("arbitrary", "arbitrary"),
        name="na_attention",
    )(rpb.reshape(-1), proj, proj, proj, gq.reshape(1, -1), gk.reshape(1, -1))


def _log_sigmoid(x):
    return -(jnp.maximum(-x, 0.0) + jnp.log1p(jnp.exp(-jnp.abs(x))))


def _ret_kernel(lf_ref, lb_ref, q_ref, k_ref, v_ref, g_ref, cos_ref, sin_ref, gn_ref,
                o_ref, qr_ref, kt_ref, d_ref, qdf_ref, qdb_ref, kv_ref, s_ref):
    s_len = q_ref.shape[1]
    c = RET_BLOCK
    nb = s_len // c
    dk = RET_QK_DIM
    half = dk // 2

    h = pl.program_id(1)
    lgf = _log_sigmoid(jnp.full((1, 1), lf_ref[h], F32))
    lgb = _log_sigmoid(jnp.full((1, 1), lb_ref[h], F32))
    ic = lax.broadcasted_iota(jnp.int32, (c, 1), 0).astype(F32)
    jr = lax.broadcasted_iota(jnp.int32, (1, c), 1).astype(F32)
    diff = ic - jr
    scale = dk ** -0.5
    d_ref[...] = jnp.exp(jnp.where(diff >= 0, lgf, lgb) * jnp.abs(diff)) * scale
    qdf_ref[...] = jnp.broadcast_to(jnp.exp(lgf * (ic + 1.0)), (c, dk))
    qdb_ref[...] = jnp.broadcast_to(jnp.exp(lgb * (c - ic)), (c, dk))
    kdf = jnp.exp(lgf * (c - 1.0 - jr)) * scale
    kdb = jnp.exp(lgb * jr) * scale
    cd_f = jnp.exp(lgf * c)
    cd_b = jnp.exp(lgb * c)

    def block_v(sl):
        return jnp.concatenate([v_ref[0, sl, :], v_ref[1, sl, :]], axis=1)

    def prep(n, _):
        sl = pl.ds(pl.multiple_of(n * c, c), c)
        cos = cos_ref[sl, :]
        sin = sin_ref[sl, :]
        q = q_ref[0, sl, :].astype(F32)
        k = k_ref[0, sl, :].astype(F32)
        qr_ref[sl, :] = (q * cos + pltpu.roll(q, half, 1) * sin).astype(BF16)
        kt = (k * cos + pltpu.roll(k, half, 1) * sin).T
        kt_ref[n] = kt.astype(BF16)
        lhs = jnp.concatenate([(kt * kdf).astype(BF16), (kt * kdb).astype(BF16)], axis=0)
        kv_ref[n] = jnp.dot(lhs, block_v(sl), preferred_element_type=F32)
        return 0

    lax.fori_loop(0, nb, prep, 0, unroll=True)

    def scan_f(n, sf):
        s_ref[n, :dk, :] = sf.astype(BF16)
        return cd_f * sf + kv_ref[n, :dk, :]

    def scan_b(t, sb):
        n = nb - 1 - t
        s_ref[n, dk:, :] = sb.astype(BF16)
        return cd_b * sb + kv_ref[n, dk:, :]

    zero = jnp.zeros((dk, RET_V_DIM), F32)

    def scan(n, carry):
        return scan_f(n, carry[0]), scan_b(n, carry[1])

    lax.fori_loop(0, nb, scan, (zero, zero), unroll=True)

    gn = gn_ref[0]

    def out(n, _):
        sl = pl.ds(pl.multiple_of(n * c, c), c)
        q = qr_ref[sl, :]
        qf32 = q.astype(F32)
        a = jnp.dot(q, kt_ref[n], preferred_element_type=F32) * d_ref[...]
        lhs = jnp.concatenate([a.astype(BF16), (qf32 * qdf_ref[...]).astype(BF16),
                               (qf32 * qdb_ref[...]).astype(BF16)], axis=1)
        rhs = jnp.concatenate([block_v(sl), s_ref[n]], axis=0)
        o = jnp.dot(lhs, rhs, preferred_element_type=F32)
        mu = jnp.mean(o, axis=-1, keepdims=True)
        oc = o - mu
        y = oc * lax.rsqrt(jnp.mean(oc * oc, axis=-1, keepdims=True) + EPS) * gn
        gate = jnp.concatenate([g_ref[0, sl, :], g_ref[1, sl, :]], axis=1).astype(F32)
        o_ref[sl, :] = (y * gate * _sigmoid(gate)).astype(BF16)
        return 0

    lax.fori_loop(0, nb, out, 0, unroll=True)


def _retention(proj, cos2, sin2, lf, lb, gn, batch, s_len):
    t = batch * s_len
    nb = s_len // RET_BLOCK
    smem = pl.BlockSpec(memory_space=pltpu.SMEM)
    return pl.pallas_call(
        _ret_kernel,
        grid=(batch, RET_HEADS),
        in_specs=[
            smem, smem,
            pl.BlockSpec((1, s_len, LANE), lambda b, h: (SLAB_RQ + h, b, 0)),
            pl.BlockSpec((1, s_len, LANE), lambda b, h: (SLAB_RK + h, b, 0)),
            pl.BlockSpec((2, s_len, LANE), lambda b, h: (SLAB_RV // 2 + h, b, 0)),
            pl.BlockSpec((2, s_len, LANE), lambda b, h: (SLAB_RG // 2 + h, b, 0)),
            pl.BlockSpec((s_len, RET_QK_DIM), lambda b, h: (0, 0)),
            pl.BlockSpec((s_len, RET_QK_DIM), lambda b, h: (0, 0)),
            pl.BlockSpec((1, 1, RET_V_DIM), lambda b, h: (h, 0, 0)),
        ],
        out_specs=pl.BlockSpec((s_len, RET_V_DIM), lambda b, h: (b, h)),
        out_shape=jax.ShapeDtypeStruct((t, RET_HEADS * RET_V_DIM), BF16),
        scratch_shapes=[
            pltpu.VMEM((s_len, RET_QK_DIM), BF16),
            pltpu.VMEM((nb, RET_QK_DIM, RET_BLOCK), BF16),
            pltpu.VMEM((RET_BLOCK, RET_BLOCK), F32),
            pltpu.VMEM((RET_BLOCK, RET_QK_DIM), F32),
            pltpu.VMEM((RET_BLOCK, RET_QK_DIM), F32),
            pltpu.VMEM((nb, 2 * RET_QK_DIM, RET_V_DIM), F32),
            pltpu.VMEM((nb, 2 * RET_QK_DIM, RET_V_DIM), BF16),
        ],
        compiler_params=_params("parallel", "parallel"),
        name="retention",
    )(lf, lb, proj, proj, proj, proj, cos2, sin2, gn.reshape(RET_HEADS, 1, RET_V_DIM))


def _xa_kernel(q_ref, k_ref, v_ref, gq_ref, gk_ref, o_ref):
    s_len = q_ref.shape[1]
    tq = 512
    gq = gq_ref[...] * gk_ref[...] * (XA_HEAD_DIM ** -0.5 * LOG2E)
    k = jnp.concatenate([k_ref[0], k_ref[1]], axis=1).astype(F32)
    kn = _unit_rms(k).astype(BF16)
    v = jnp.concatenate([v_ref[0], v_ref[1]], axis=1)

    def body(i, _):
        sl = pl.ds(pl.multiple_of(i * tq, tq), tq)
        q = jnp.concatenate([q_ref[0, sl, :], q_ref[1, sl, :]], axis=1).astype(F32)
        qn = _rms(q, gq).astype(BF16)
        s = lax.dot_general(qn, kn, (((1,), (1,)), ((), ())), preferred_element_type=F32)
        e = jnp.exp2(s - jnp.max(s, axis=-1, keepdims=True))
        l = jnp.sum(e, axis=-1, keepdims=True)
        o = jnp.dot(e.astype(BF16), v, preferred_element_type=F32)
        o_ref[sl, :] = (o / l).astype(BF16)
        return 0

    lax.fori_loop(0, s_len // tq, body, 0, unroll=True)


def _mem_xattn(proj, mkv, gq, gk, batch, s_len, n_mem):
    t = batch * s_len
    g_spec = pl.BlockSpec((1, XA_HEAD_DIM), lambda b, h: (0, 0))
    return pl.pallas_call(
        _xa_kernel,
        grid=(batch, XA_HEADS),
        in_specs=[
            pl.BlockSpec((2, s_len, LANE), lambda b, h: (SLAB_XQ // 2 + h, b, 0)),
            pl.BlockSpec((2, n_mem, LANE), lambda b, h: (h, b, 0)),
            pl.BlockSpec((2, n_mem, LANE), lambda b, h: (XA_HEADS + h, b, 0)),
            g_spec, g_spec,
        ],
        out_specs=pl.BlockSpec((s_len, XA_HEAD_DIM), lambda b, h: (b, h)),
        out_shape=jax.ShapeDtypeStruct((t, XA_HEADS * XA_HEAD_DIM), BF16),
        compiler_params=_params("parallel", "parallel"),
        name="mem_xattn",
    )(proj, mkv, mkv, gq.reshape(1, -1), gk.reshape(1, -1))


def _merge_kernel(ona_ref, oret_ref, omem_ref, wna_ref, wret_ref, wmem_ref,
                  gna_ref, gret_ref, gmem_ref, o_ref):
    y_na = jnp.dot(ona_ref[...], wna_ref[...], preferred_element_type=F32)
    y_ret = jnp.dot(oret_ref[...], wret_ref[...], preferred_element_type=F32)
    y_mem = jnp.dot(omem_ref[...], wmem_ref[...], preferred_element_type=F32)
    for j in range(gna_ref.shape[0]):
        cs = slice(j * LANE, (j + 1) * LANE)
        o_ref[:, cs] = (_sigmoid(gna_ref[j].astype(F32)) * y_na[:, cs]
                        + _sigmoid(gret_ref[j].astype(F32)) * y_ret[:, cs]
                        + _sigmoid(gmem_ref[j].astype(F32)) * y_mem[:, cs]).astype(BF16)


def _merge(o_na, o_ret, o_mem, w_na, w_ret, w_mem, proj, tm, tn):
    t = o_na.shape[0]
    d = w_na.shape[1]
    ns = tn // LANE

    def lhs_spec(a):
        return pl.BlockSpec((tm, a.shape[1]), lambda i, j: (i, 0))

    def w_spec(w):
        return pl.BlockSpec((w.shape[0], tn), lambda i, j: (0, j))

    def gate_spec(first):
        return pl.BlockSpec((ns, tm, LANE), lambda i, j: (first // ns + j, i, 0))

    return pl.pallas_call(
        _merge_kernel,
        grid=(t // tm, d // tn),
        in_specs=[lhs_spec(o_na), lhs_spec(o_ret), lhs_spec(o_mem),
                  w_spec(w_na), w_spec(w_ret), w_spec(w_mem),
                  gate_spec(SLAB_G_NA), gate_spec(SLAB_G_RET), gate_spec(SLAB_G_MEM)],
        out_specs=pl.BlockSpec((tm, tn), lambda i, j: (i, j)),
        out_shape=jax.ShapeDtypeStruct((t, d), BF16),
        compiler_params=_params("parallel", "arbitrary"),
        name="merge",
    )(o_na, o_ret, o_mem, w_na, w_ret, w_mem, proj, proj, proj)


def _out_proj_kernel(m_ref, w_ref, x_ref, o_ref):
    o_ref[...] = x_ref[...] + jnp.dot(m_ref[...], w_ref[...], preferred_element_type=F32)


def _out_proj(merged, w, x2d, tm):
    t, k = merged.shape
    d = w.shape[1]
    return pl.pallas_call(
        _out_proj_kernel,
        grid=(t // tm,),
        in_specs=[pl.BlockSpec((tm, k), lambda i: (i, 0)),
                  pl.BlockSpec((k, d), lambda i: (0, 0), pipeline_mode=pl.Buffered(1)),
                  pl.BlockSpec((tm, d), lambda i: (i, 0))],
        out_specs=pl.BlockSpec((tm, d), lambda i: (i, 0)),
        out_shape=jax.ShapeDtypeStruct((t, d), F32),
        compiler_params=_params("parallel"),
        name="out_proj",
    )(merged, w, x2d)


def _ffn_kernel(x_ref, g_ref, w1_ref, w2_ref, o_ref, h_ref):
    def mlp(h):
        a = jnp.maximum(jnp.dot(h, w1_ref[...], preferred_element_type=F32), 0.0)
        return jnp.dot((a * a).astype(BF16), w2_ref[...], preferred_element_type=F32)

    @pl.when(pl.program_id(1) == 0)
    def _():
        x = x_ref[...]
        h = _rms(x, g_ref[...]).astype(BF16)
        h_ref[...] = h
        o_ref[...] = x + mlp(h)

    @pl.when(pl.program_id(1) != 0)
    def _():
        o_ref[...] += mlp(h_ref[...])


def _ffn(x1, g, w1, w2, tm, tf):
    t, d = x1.shape
    dff = w1.shape[1]
    return pl.pallas_call(
        _ffn_kernel,
        grid=(t // tm, dff // tf),
        in_specs=[pl.BlockSpec((tm, d), lambda i, f: (i, 0)),
                  pl.BlockSpec((1, d), lambda i, f: (0, 0)),
                  pl.BlockSpec((d, tf), lambda i, f: (0, f)),
                  pl.BlockSpec((tf, d), lambda i, f: (f, 0))],
        out_specs=pl.BlockSpec((tm, d), lambda i, f: (i, 0)),
        out_shape=jax.ShapeDtypeStruct((t, d), F32),
        scratch_shapes=[pltpu.VMEM((tm, d), BF16)],
        compiler_params=_params("parallel", "arbitrary"),
        name="ffn",
    )(x1, g.reshape(1, d), w1, w2)


def _rope_tables(s_len):
    half = RET_QK_DIM // 2
    inv = np.power(np.float64(ROPE_BASE), -np.arange(half, dtype=np.float64) / half)
    ang = np.arange(s_len, dtype=np.float64)[:, None] * inv[None, :]
    cos, sin = np.cos(ang), np.sin(ang)
    return (jnp.asarray(np.concatenate([cos, cos], axis=1), F32),
            jnp.asarray(np.concatenate([-sin, sin], axis=1), F32))


class _Tiles(NamedTuple):
    tm: int
    tn_in: int
    tn_merge: int
    tm_out: int
    tf: int
    tn_mem: int


def _tiles(t):
    return _Tiles(tm=min(1024, t), tn_in=1024, tn_merge=512, tm_out=min(512, t), tf=1024, tn_mem=512)


def kernel(x, mem, norm_mix_g, w_in, na_q_norm_g, na_k_norm_g, na_rpb, ret_decay_logit_fwd, ret_decay_logit_bwd, ret_gn_g, mem_norm_g, w_mem_kv, xa_q_norm_g, xa_k_norm_g, w_br_na, w_br_ret, w_br_mem, w_out, norm_ffn_g, w_ff1, w_ff2):
    batch, s_len, d = x.shape
    n_mem = mem.shape[1]
    t = batch * s_len
    tiles = _tiles(t)
    cos2, sin2 = _rope_tables(s_len)
    x2d = x.reshape(t, d)
    mem2d = mem.reshape(batch * n_mem, d)
    gm, gn = t // tiles.tm, w_in.shape[2] // tiles.tn_in

    def side_job(w):
        r, c = w.shape
        if r % (gm * gn * 16) == 0:
            return w, (r // (gm * gn), c), lambda i, j: (i * gn + j, 0)
        assert r % (gn * 16) == 0 and c % (gm * LANE) == 0, (w.shape, gm, gn)
        return w, (r // gn, c // gm), lambda i, j: (j, i)

    for l in range(w_in.shape[0]):
        sides = [side_job(w) for w in (w_ff1[l], w_ff2[l], w_br_na[l], w_br_ret[l], w_br_mem[l],
                                       w_out[l], w_mem_kv[l])]
        proj, (w_ff1_bf, w_ff2_bf, w_na_bf, w_ret_bf, w_mem_bf, w_out_bf, w_kv_bf) = _norm_proj(
            x2d, norm_mix_g[l], w_in[l], tiles.tm, tiles.tn_in, side=sides)
        mkv, _ = _norm_proj(mem2d, mem_norm_g[l], w_kv_bf, batch * n_mem, tiles.tn_mem)

        o_mem = _mem_xattn(proj, mkv, xa_q_norm_g[l], xa_k_norm_g[l], batch, s_len, n_mem)
        o_na = _na_attention(proj, na_rpb[l], na_q_norm_g[l], na_k_norm_g[l], batch, s_len)
        o_ret = _retention(proj, cos2, sin2, ret_decay_logit_fwd[l], ret_decay_logit_bwd[l], ret_gn_g[l],
                           batch, s_len)

        merged = _merge(o_na, o_ret, o_mem, w_na_bf, w_ret_bf, w_mem_bf, proj, tiles.tm, tiles.tn_merge)
        x1 = _out_proj(merged, w_out_bf, x2d, tiles.tm_out)
        x2d = _ffn(x1, norm_ffn_g[l], w_ff1_bf, w_ff2_bf, tiles.tm_out, tiles.tf)
    return x2d.reshape(batch, s_len, d)
```

```python
import functools
from typing import NamedTuple

import jax
import jax.numpy as jnp
import numpy as np
from jax import lax
from jax.experimental import pallas as pl
from jax.experimental.pallas import tpu as pltpu

F32 = jnp.float32
BF16 = jnp.bfloat16

LANE = 128
EPS = 1e-6
NEG = -1e30
LOG2E = 1.4426950408889634

GRID_W = 64
NA_HEADS = 8
NA_HEAD_DIM = 128
NA_MAX_ROWS = 8
NA_COLS = 16
NA_DR = 2 * NA_MAX_ROWS - 1
NA_DC = 2 * NA_COLS - 1
NA_TBL = 3 * NA_DR + 1
NA_QROWS = 4
NA_WIN = NA_QROWS + NA_MAX_ROWS

RET_HEADS = 8
RET_QK_DIM = 128
RET_V_DIM = 256
RET_BLOCK = 256
ROPE_BASE = 10000.0

XA_HEADS = 4
XA_HEAD_DIM = 256

SLAB_NA_Q, SLAB_NA_K, SLAB_NA_V = 0, 8, 16
SLAB_RQ, SLAB_RK, SLAB_RV, SLAB_RG = 24, 32, 40, 56
SLAB_XQ = 72
SLAB_G_NA, SLAB_G_RET, SLAB_G_MEM = 80, 96, 112

SIDE_GROUP = 2

V7X_VMEM_BYTES = 64 * 2**20
VMEM_LIMIT_BYTES = V7X_VMEM_BYTES - 6 * 2**20


def _params(*sem):
    return pltpu.CompilerParams(dimension_semantics=sem, vmem_limit_bytes=VMEM_LIMIT_BYTES)


def _rms(x, g):
    return x * lax.rsqrt(jnp.mean(x * x, axis=-1, keepdims=True) + EPS) * g


def _unit_rms(x):
    return x * lax.rsqrt(jnp.mean(x * x, axis=-1, keepdims=True) + EPS)


def _sigmoid(x):
    return 0.5 * jnp.tanh(0.5 * x) + 0.5


def _norm_proj_kernel(n_side, x_ref, g_ref, w_ref, *refs):
    side_in, o_ref = refs[:n_side], refs[n_side]
    side_out, hn_ref = refs[n_side + 1:2 * n_side + 1], refs[2 * n_side + 1]

    @pl.when(pl.program_id(1) == 0)
    def _():
        hn_ref[...] = _rms(x_ref[...], g_ref[...]).astype(BF16)

    acc = jnp.dot(hn_ref[...], w_ref[...].astype(BF16), preferred_element_type=F32)
    for s in range(o_ref.shape[0]):
        o_ref[s] = acc[:, s * LANE:(s + 1) * LANE].astype(BF16)
    q = pl.program_id(1) % SIDE_GROUP
    for src, dst in zip(side_in, side_out):
        rows = src.shape[0] // SIDE_GROUP
        sl = pl.ds(pl.multiple_of(q * rows, rows), rows)
        dst[sl, :] = src[sl, :].astype(BF16)


def _norm_proj(x2d, g, w, tm, tn, side=()):
    m, k = x2d.shape
    n = w.shape[1]
    side_specs = [pl.BlockSpec(blk, imap) for _, blk, imap in side]
    out = pl.pallas_call(
        functools.partial(_norm_proj_kernel, len(side)),
        grid=(m // tm, n // tn),
        in_specs=[
            pl.BlockSpec((tm, k), lambda i, j: (i, 0)),
            pl.BlockSpec((1, k), lambda i, j: (0, 0)),
            pl.BlockSpec((k, tn), lambda i, j: (0, j)),
        ] + side_specs,
        out_specs=[pl.BlockSpec((tn // LANE, tm, LANE), lambda i, j: (j, i, 0))] + side_specs,
        out_shape=[jax.ShapeDtypeStruct((n // LANE, m, LANE), BF16)]
        + [jax.ShapeDtypeStruct(a.shape, BF16) for a, _, _ in side],
        scratch_shapes=[pltpu.VMEM((tm, k), BF16)],
        compiler_params=_params("parallel", "arbitrary"),
        name="norm_proj",
    )(x2d, g.reshape(1, k), w, *[a for a, _, _ in side])
    return out[0], out[1:]


def _na_build_tables(h, rpb_ref, t_ref, base_ref):
    qc = lax.broadcasted_iota(jnp.int32, (GRID_W, LANE), 0)
    lane = lax.broadcasted_iota(jnp.int32, (GRID_W, LANE), 1)
    kc = lane & (GRID_W - 1)
    d = jnp.clip(kc - qc, -(NA_COLS - 1), NA_COLS - 1) + (NA_COLS - 1)
    cs = jnp.clip(qc - NA_COLS // 2, 0, GRID_W - NA_COLS)
    col_ok = (kc >= cs) & (kc < cs + NA_COLS)
    left = lane < GRID_W
    neg = jnp.full((GRID_W, LANE), NEG, F32)

    def body(dr, _):
        base = (h * NA_DR + dr) * NA_DC
        val = jnp.zeros((GRID_W, LANE), F32)
        for dd in range(NA_DC):
            val = jnp.where(d == dd, rpb_ref[base + dd], val)
        base_ref[dr] = jnp.where(col_ok, val * LOG2E, NEG)
        return 0

    lax.fori_loop(0, NA_DR, body, 0)
    for dr in range(NA_DR):
        second = base_ref[dr + 1] if dr + 1 < NA_DR else neg
        t_ref[dr] = jnp.where(left, base_ref[dr], second)
        t_ref[NA_DR + dr] = jnp.where(left, base_ref[dr], neg)
        t_ref[2 * NA_DR + dr] = jnp.where(left, neg, base_ref[dr])
    t_ref[3 * NA_DR] = neg


def _na_kernel(rpb_ref, q_ref, k_ref, v_ref, gq_ref, gk_ref, o_ref, kt_ref, va_ref, t_ref, base_ref):
    @pl.when(pl.program_id(1) == 0)
    def _():
        _na_build_tables(pl.program_id(0), rpb_ref, t_ref, base_ref)

    s_len = q_ref.shape[1]
    rows = s_len // GRID_W
    step_tok = NA_QROWS * GRID_W
    win_tok = NA_WIN * GRID_W
    n_tiles = NA_WIN // 2
    gq = gq_ref[...] * gk_ref[...] * (NA_HEAD_DIM ** -0.5 * LOG2E)
    va_ref[:, :NA_HEAD_DIM] = v_ref[0]
    va_ref[:, NA_HEAD_DIM:] = jnp.ones((s_len, NA_HEAD_DIM), BF16)

    def knorm(c, _):
        sl = pl.ds(pl.multiple_of(c * LANE, LANE), LANE)
        kt_ref[c] = _unit_rms(k_ref[0, sl, :].astype(F32)).astype(BF16).T
        return 0

    lax.fori_loop(0, s_len // LANE, knorm, 0, unroll=True)

    def step(i, _):
        r0 = NA_QROWS * i
        ws = jnp.clip(r0 - NA_MAX_ROWS // 2, 0, rows - NA_WIN)
        wp = ws // 2
        qsl = pl.ds(pl.multiple_of(i * step_tok, step_tok), step_tok)
        wsl = pl.ds(pl.multiple_of(ws * GRID_W, LANE), win_tok)
        qn = _rms(q_ref[0, qsl, :].astype(F32), gq).astype(BF16)
        kwin = jnp.concatenate([kt_ref[wp + t] for t in range(n_tiles)], axis=1)
        s = jnp.dot(qn, kwin, preferred_element_type=F32)
        bias_rows = []
        for qr in range(NA_QROWS):
            r = r0 + qr
            rs = jnp.clip(r - NA_MAX_ROWS // 2, 0, rows - NA_MAX_ROWS)
            tiles = []
            for t in range(n_tiles):
                ka = ws + 2 * t
                dr = ka - r + (NA_MAX_ROWS - 1)
                va = (ka >= rs) & (ka < rs + NA_MAX_ROWS)
                vb = (ka + 1 >= rs) & (ka + 1 < rs + NA_MAX_ROWS)
                idx = jnp.where(va, jnp.where(vb, dr, NA_DR + dr),
                                jnp.where(vb, 2 * NA_DR + dr + 1, 3 * NA_DR))
                tiles.append(t_ref[idx])
            bias_rows.append(jnp.concatenate(tiles, axis=1))
        s = s + jnp.concatenate(bias_rows, axis=0)
        e = jnp.exp2(s - jnp.max(s, axis=-1, keepdims=True))
        o = jnp.dot(e.astype(BF16), va_ref[wsl, :], preferred_element_type=F32)
        o_ref[qsl, :] = (o[:, :NA_HEAD_DIM] / o[:, NA_HEAD_DIM:]).astype(BF16)
        return 0

    lax.fori_loop(0, rows // NA_QROWS, step, 0, unroll=True)


def _na_attention(proj, rpb, gq, gk, batch, s_len):
    t = batch * s_len
    g_spec = pl.BlockSpec((1, NA_HEAD_DIM), lambda h, b: (0, 0))
    return pl.pallas_call(
        _na_kernel,
        grid=(NA_HEADS, batch),
        in_specs=[
            pl.BlockSpec(memory_space=pltpu.SMEM),
            pl.BlockSpec((1, s_len, LANE), lambda h, b: (SLAB_NA_Q + h, b, 0)),
            pl.BlockSpec((1, s_len, LANE), lambda h, b: (SLAB_NA_K + h, b, 0)),
            pl.BlockSpec((1, s_len, LANE), lambda h, b: (SLAB_NA_V + h, b, 0)),
            g_spec, g_spec,
        ],
        out_specs=pl.BlockSpec((s_len, NA_HEAD_DIM), lambda h, b: (b, h)),
        out_shape=jax.ShapeDtypeStruct((t, NA_HEADS * NA_HEAD_DIM), BF16),
        scratch_shapes=[pltpu.VMEM((s_len // LANE, NA_HEAD_DIM, LANE), BF16),
                        pltpu.VMEM((s_len, 2 * NA_HEAD_DIM), BF16),
                        pltpu.VMEM((NA_TBL, GRID_W, LANE), F32),
                        pltpu.VMEM((NA_DR, GRID_W, LANE), F32)],
        compiler_params=_params("arbitrary", "arbitrary"),
        name="na_attention",
    )(rpb.reshape(-1), proj, proj, proj, gq.reshape(1, -1), gk.reshape(1, -1))


def _log_sigmoid(x):
    return -(jnp.maximum(-x, 0.0) + jnp.log1p(jnp.exp(-jnp.abs(x))))


def _ret_kernel(lf_ref, lb_ref, q_ref, k_ref, v_ref, g_ref, cos_ref, sin_ref, gn_ref,
                o_ref, qr_ref, kt_ref, d_ref, qdf_ref, qdb_ref, kv_ref, s_ref):
    s_len = q_ref.shape[1]
    c = RET_BLOCK
    nb = s_len // c
    dk = RET_QK_DIM
    half = dk // 2

    h = pl.program_id(1)
    lgf = _log_sigmoid(jnp.full((1, 1), lf_ref[h], F32))
    lgb = _log_sigmoid(jnp.full((1, 1), lb_ref[h], F32))
    ic = lax.broadcasted_iota(jnp.int32, (c, 1), 0).astype(F32)
    jr = lax.broadcasted_iota(jnp.int32, (1, c), 1).astype(F32)
    diff = ic - jr
    scale = dk ** -0.5
    d_ref[...] = jnp.exp(jnp.where(diff >= 0, lgf, lgb) * jnp.abs(diff)) * scale
    qdf_ref[...] = jnp.broadcast_to(jnp.exp(lgf * (ic + 1.0)), (c, dk))
    qdb_ref[...] = jnp.broadcast_to(jnp.exp(lgb * (c - ic)), (c, dk))
    kdf = jnp.exp(lgf * (c - 1.0 - jr)) * scale
    kdb = jnp.exp(lgb * jr) * scale
    cd_f = jnp.exp(lgf * c)
    cd_b = jnp.exp(lgb * c)

    def block_v(sl):
        return jnp.concatenate([v_ref[0, sl, :], v_ref[1, sl, :]], axis=1)

    def prep(n, _):
        sl = pl.ds(pl.multiple_of(n * c, c), c)
        cos = cos_ref[sl, :]
        sin = sin_ref[sl, :]
        q = q_ref[0, sl, :].astype(F32)
        k = k_ref[0, sl, :].astype(F32)
        qr_ref[sl, :] = (q * cos + pltpu.roll(q, half, 1) * sin).astype(BF16)
        kt = (k * cos + pltpu.roll(k, half, 1) * sin).T
        kt_ref[n] = kt.astype(BF16)
        lhs = jnp.concatenate([(kt * kdf).astype(BF16), (kt * kdb).astype(BF16)], axis=0)
        kv_ref[n] = jnp.dot(lhs, block_v(sl), preferred_element_type=F32)
        return 0

    lax.fori_loop(0, nb, prep, 0, unroll=True)

    def scan_f(n, sf):
        s_ref[n, :dk, :] = sf.astype(BF16)
        return cd_f * sf + kv_ref[n, :dk, :]

    def scan_b(t, sb):
        n = nb - 1 - t
        s_ref[n, dk:, :] = sb.astype(BF16)
        return cd_b * sb + kv_ref[n, dk:, :]

    zero = jnp.zeros((dk, RET_V_DIM), F32)

    def scan(n, carry):
        return scan_f(n, carry[0]), scan_b(n, carry[1])

    lax.fori_loop(0, nb, scan, (zero, zero), unroll=True)

    gn = gn_ref[0]

    def out(n, _):
        sl = pl.ds(pl.multiple_of(n * c, c), c)
        q = qr_ref[sl, :]
        qf32 = q.astype(F32)
        a = jnp.dot(q, kt_ref[n], preferred_element_type=F32) * d_ref[...]
        lhs = jnp.concatenate([a.astype(BF16), (qf32 * qdf_ref[...]).astype(BF16),
                               (qf32 * qdb_ref[...]).astype(BF16)], axis=1)
        rhs = jnp.concatenate([block_v(sl), s_ref[n]], axis=0)
        o = jnp.dot(lhs, rhs, preferred_element_type=F32)
        mu = jnp.mean(o, axis=-1, keepdims=True)
        oc = o - mu
        y = oc * lax.rsqrt(jnp.mean(oc * oc, axis=-1, keepdims=True) + EPS) * gn
        gate = jnp.concatenate([g_ref[0, sl, :], g_ref[1, sl, :]], axis=1).astype(F32)
        o_ref[sl, :] = (y * gate * _sigmoid(gate)).astype(BF16)
        return 0

    lax.fori_loop(0, nb, out, 0, unroll=True)


def _retention(proj, cos2, sin2, lf, lb, gn, batch, s_len):
    t = batch * s_len
    nb = s_len // RET_BLOCK
    smem = pl.BlockSpec(memory_space=pltpu.SMEM)
    return pl.pallas_call(
        _ret_kernel,
        grid=(batch, RET_HEADS),
        in_specs=[
            smem, smem,
            pl.BlockSpec((1, s_len, LANE), lambda b, h: (SLAB_RQ + h, b, 0)),
            pl.BlockSpec((1, s_len, LANE), lambda b, h: (SLAB_RK + h, b, 0)),
            pl.BlockSpec((2, s_len, LANE), lambda b, h: (SLAB_RV // 2 + h, b, 0)),
            pl.BlockSpec((2, s_len, LANE), lambda b, h: (SLAB_RG // 2 + h, b, 0)),
            pl.BlockSpec((s_len, RET_QK_DIM), lambda b, h: (0, 0)),
            pl.BlockSpec((s_len, RET_QK_DIM), lambda b, h: (0, 0)),
            pl.BlockSpec((1, 1, RET_V_DIM), lambda b, h: (h, 0, 0)),
        ],
        out_specs=pl.BlockSpec((s_len, RET_V_DIM), lambda b, h: (b, h)),
        out_shape=jax.ShapeDtypeStruct((t, RET_HEADS * RET_V_DIM), BF16),
        scratch_shapes=[
            pltpu.VMEM((s_len, RET_QK_DIM), BF16),
            pltpu.VMEM((nb, RET_QK_DIM, RET_BLOCK), BF16),
            pltpu.VMEM((RET_BLOCK, RET_BLOCK), F32),
            pltpu.VMEM((RET_BLOCK, RET_QK_DIM), F32),
            pltpu.VMEM((RET_BLOCK, RET_QK_DIM), F32),
            pltpu.VMEM((nb, 2 * RET_QK_DIM, RET_V_DIM), F32),
            pltpu.VMEM((nb, 2 * RET_QK_DIM, RET_V_DIM), BF16),
        ],
        compiler_params=_params("parallel", "parallel"),
        name="retention",
    )(lf, lb, proj, proj, proj, proj, cos2, sin2, gn.reshape(RET_HEADS, 1, RET_V_DIM))


def _xa_kernel(q_ref, k_ref, v_ref, gq_ref, gk_ref, o_ref):
    s_len = q_ref.shape[1]
    tq = 512
    gq = gq_ref[...] * gk_ref[...] * (XA_HEAD_DIM ** -0.5 * LOG2E)
    k = jnp.concatenate([k_ref[0], k_ref[1]], axis=1).astype(F32)
    kn = _unit_rms(k).astype(BF16)
    v = jnp.concatenate([v_ref[0], v_ref[1]], axis=1)

    def body(i, _):
        sl = pl.ds(pl.multiple_of(i * tq, tq), tq)
        q = jnp.concatenate([q_ref[0, sl, :], q_ref[1, sl, :]], axis=1).astype(F32)
        qn = _rms(q, gq).astype(BF16)
        s = lax.dot_general(qn, kn, (((1,), (1,)), ((), ())), preferred_element_type=F32)
        e = jnp.exp2(s - jnp.max(s, axis=-1, keepdims=True))
        l = jnp.sum(e, axis=-1, keepdims=True)
        o = jnp.dot(e.astype(BF16), v, preferred_element_type=F32)
        o_ref[sl, :] = (o / l).astype(BF16)
        return 0

    lax.fori_loop(0, s_len // tq, body, 0, unroll=True)


def _mem_xattn(proj, mkv, gq, gk, batch, s_len, n_mem):
    t = batch * s_len
    g_spec = pl.BlockSpec((1, XA_HEAD_DIM), lambda b, h: (0, 0))
    return pl.pallas_call(
        _xa_kernel,
        grid=(batch, XA_HEADS),
        in_specs=[
            pl.BlockSpec((2, s_len, LANE), lambda b, h: (SLAB_XQ // 2 + h, b, 0)),
            pl.BlockSpec((2, n_mem, LANE), lambda b, h: (h, b, 0)),
            pl.BlockSpec((2, n_mem, LANE), lambda b, h: (XA_HEADS + h, b, 0)),
            g_spec, g_spec,
        ],
        out_specs=pl.BlockSpec((s_len, XA_HEAD_DIM), lambda b, h: (b, h)),
        out_shape=jax.ShapeDtypeStruct((t, XA_HEADS * XA_HEAD_DIM), BF16),
        compiler_params=_params("parallel", "parallel"),
        name="mem_xattn",
    )(proj, mkv, mkv, gq.reshape(1, -1), gk.reshape(1, -1))


def _merge_kernel(ona_ref, oret_ref, omem_ref, wna_ref, wret_ref, wmem_ref,
                  gna_ref, gret_ref, gmem_ref, o_ref):
    y_na = jnp.dot(ona_ref[...], wna_ref[...], preferred_element_type=F32)
    y_ret = jnp.dot(oret_ref[...], wret_ref[...], preferred_element_type=F32)
    y_mem = jnp.dot(omem_ref[...], wmem_ref[...], preferred_element_type=F32)
    for j in range(gna_ref.shape[0]):
        cs = slice(j * LANE, (j + 1) * LANE)
        o_ref[:, cs] = (_sigmoid(gna_ref[j].astype(F32)) * y_na[:, cs]
                        + _sigmoid(gret_ref[j].astype(F32)) * y_ret[:, cs]
                        + _sigmoid(gmem_ref[j].astype(F32)) * y_mem[:, cs]).astype(BF16)


def _merge(o_na, o_ret, o_mem, w_na, w_ret, w_mem, proj, tm, tn):
    t = o_na.shape[0]
    d = w_na.shape[1]
    ns = tn // LANE

    def lhs_spec(a):
        return pl.BlockSpec((tm, a.shape[1]), lambda i, j: (i, 0))

    def w_spec(w):
        return pl.BlockSpec((w.shape[0], tn), lambda i, j: (0, j))

    def gate_spec(first):
        return pl.BlockSpec((ns, tm, LANE), lambda i, j: (first // ns + j, i, 0))

    return pl.pallas_call(
        _merge_kernel,
        grid=(t // tm, d // tn),
        in_specs=[lhs_spec(o_na), lhs_spec(o_ret), lhs_spec(o_mem),
                  w_spec(w_na), w_spec(w_ret), w_spec(w_mem),
                  gate_spec(SLAB_G_NA), gate_spec(SLAB_G_RET), gate_spec(SLAB_G_MEM)],
        out_specs=pl.BlockSpec((tm, tn), lambda i, j: (i, j)),
        out_shape=jax.ShapeDtypeStruct((t, d), BF16),
        compiler_params=_params("parallel", "arbitrary"),
        name="merge",
    )(o_na, o_ret, o_mem, w_na, w_ret, w_mem, proj, proj, proj)


def _out_proj_kernel(m_ref, w_ref, x_ref, o_ref, wb_ref):
    @pl.when(pl.program_id(0) == 0)
    def _():
        wb_ref[...] = w_ref[...].astype(BF16)

    o_ref[...] = x_ref[...] + jnp.dot(m_ref[...], wb_ref[...], preferred_element_type=F32)


def _out_proj(merged, w, x2d, tm):
    t, k = merged.shape
    d = w.shape[1]
    return pl.pallas_call(
        _out_proj_kernel,
        grid=(t // tm,),
        in_specs=[pl.BlockSpec((tm, k), lambda i: (i, 0)),
                  pl.BlockSpec((k, d), lambda i: (0, 0), pipeline_mode=pl.Buffered(1)),
                  pl.BlockSpec((tm, d), lambda i: (i, 0))],
        out_specs=pl.BlockSpec((tm, d), lambda i: (i, 0)),
        out_shape=jax.ShapeDtypeStruct((t, d), F32),
        scratch_shapes=[pltpu.VMEM((k, d), BF16)],
        compiler_params=_params("arbitrary"),
        name="out_proj",
    )(merged, w, x2d)


def _ffn_kernel(x_ref, g_ref, w1_ref, w2_ref, o_ref, h_ref):
    def mlp(h):
        a = jnp.maximum(jnp.dot(h, w1_ref[...], preferred_element_type=F32), 0.0)
        return jnp.dot((a * a).astype(BF16), w2_ref[...], preferred_element_type=F32)

    @pl.when(pl.program_id(1) == 0)
    def _():
        x = x_ref[...]
        h = _rms(x, g_ref[...]).astype(BF16)
        h_ref[...] = h
        o_ref[...] = x + mlp(h)

    @pl.when(pl.program_id(1) != 0)
    def _():
        o_ref[...] += mlp(h_ref[...])


def _ffn(x1, g, w1, w2, tm, tf):
    t, d = x1.shape
    dff = w1.shape[1]
    return pl.pallas_call(
        _ffn_kernel,
        grid=(t // tm, dff // tf),
        in_specs=[pl.BlockSpec((tm, d), lambda i, f: (i, 0)),
                  pl.BlockSpec((1, d), lambda i, f: (0, 0)),
                  pl.BlockSpec((d, tf), lambda i, f: (0, f)),
                  pl.BlockSpec((tf, d), lambda i, f: (f, 0))],
        out_specs=pl.BlockSpec((tm, d), lambda i, f: (i, 0)),
        out_shape=jax.ShapeDtypeStruct((t, d), F32),
        scratch_shapes=[pltpu.VMEM((tm, d), BF16)],
        compiler_params=_params("parallel", "arbitrary"),
        name="ffn",
    )(x1, g.reshape(1, d), w1, w2)


def _rope_tables(s_len):
    half = RET_QK_DIM // 2
    inv = np.power(np.float64(ROPE_BASE), -np.arange(half, dtype=np.float64) / half)
    ang = np.arange(s_len, dtype=np.float64)[:, None] * inv[None, :]
    cos, sin = np.cos(ang), np.sin(ang)
    return (jnp.asarray(np.concatenate([cos, cos], axis=1), F32),
            jnp.asarray(np.concatenate([-sin, sin], axis=1), F32))


class _Tiles(NamedTuple):
    tm: int
    tn_in: int
    tn_merge: int
    tm_out: int
    tf: int
    tn_mem: int


def _tiles(t):
    return _Tiles(tm=min(1024, t), tn_in=1024, tn_merge=512, tm_out=min(512, t), tf=1024, tn_mem=512)


def kernel(x, mem, norm_mix_g, w_in, na_q_norm_g, na_k_norm_g, na_rpb, ret_decay_logit_fwd, ret_decay_logit_bwd, ret_gn_g, mem_norm_g, w_mem_kv, xa_q_norm_g, xa_k_norm_g, w_br_na, w_br_ret, w_br_mem, w_out, norm_ffn_g, w_ff1, w_ff2):
    batch, s_len, d = x.shape
    n_mem = mem.shape[1]
    t = batch * s_len
    tiles = _tiles(t)
    cos2, sin2 = _rope_tables(s_len)
    x2d = x.reshape(t, d)
    mem2d = mem.reshape(batch * n_mem, d)
    gm, gn = t // tiles.tm, w_in.shape[2] // tiles.tn_in

    def side_job(w):
        r, c = w.shape
        assert gn % SIDE_GROUP == 0
        if r % (gm * gn * 16) == 0:
            return w, (SIDE_GROUP * r // (gm * gn), c), lambda i, j: ((i * gn + j) // SIDE_GROUP, 0)
        assert r % (gn * 16) == 0 and c % (gm * LANE) == 0, (w.shape, gm, gn)
        return w, (SIDE_GROUP * r // gn, c // gm), lambda i, j: (j // SIDE_GROUP, i)

    for l in range(w_in.shape[0]):
        sides = [side_job(w) for w in (w_ff1[l], w_ff2[l], w_br_na[l], w_br_ret[l], w_br_mem[l])]
        proj, (w_ff1_bf, w_ff2_bf, w_na_bf, w_ret_bf, w_mem_bf) = _norm_proj(
            x2d, norm_mix_g[l], w_in[l], tiles.tm, tiles.tn_in, side=sides)
        mkv, _ = _norm_proj(mem2d, mem_norm_g[l], w_mem_kv[l], batch * n_mem, tiles.tn_mem)

        o_mem = _mem_xattn(proj, mkv, xa_q_norm_g[l], xa_k_norm_g[l], batch, s_len, n_mem)
        o_na = _na_attention(proj, na_rpb[l], na_q_norm_g[l], na_k_norm_g[l], batch, s_len)
        o_ret = _retention(proj, cos2, sin2, ret_decay_logit_fwd[l], ret_decay_logit_bwd[l], ret_gn_g[l],
                           batch, s_len)

        merged = _merge(o_na, o_ret, o_mem, w_na_bf, w_ret_bf, w_mem_bf, proj, tiles.tm, tiles.tn_merge)
        x1 = _out_proj(merged, w_out[l], x2d, tiles.tm_out)
        x2d = _ffn(x1, norm_ffn_g[l], w_ff1_bf, w_ff2_bf, tiles.tm_out, tiles.tf)
    return x2d.reshape(batch, s_len, d)
```

```python
import functools
from typing import NamedTuple

import jax
import jax.numpy as jnp
import numpy as np
from jax import lax
from jax.experimental import pallas as pl
from jax.experimental.pallas import tpu as pltpu

F32 = jnp.float32
BF16 = jnp.bfloat16

LANE = 128
EPS = 1e-6
NEG = -1e30
LOG2E = 1.4426950408889634

GRID_W = 64
NA_HEADS = 8
NA_HEAD_DIM = 128
NA_MAX_ROWS = 8
NA_COLS = 16
NA_DR = 2 * NA_MAX_ROWS - 1
NA_DC = 2 * NA_COLS - 1
NA_TBL = 3 * NA_DR + 1
NA_QROWS = 4
NA_WIN = NA_QROWS + NA_MAX_ROWS

RET_HEADS = 8
RET_QK_DIM = 128
RET_V_DIM = 256
RET_BLOCK = 256
ROPE_BASE = 10000.0

XA_HEADS = 4
XA_HEAD_DIM = 256

SLAB_NA_Q, SLAB_NA_K, SLAB_NA_V = 0, 8, 16
SLAB_RQ, SLAB_RK, SLAB_RV, SLAB_RG = 24, 32, 40, 56
SLAB_XQ = 72
SLAB_G_NA, SLAB_G_RET, SLAB_G_MEM = 80, 96, 112

V7X_VMEM_BYTES = 64 * 2**20
VMEM_LIMIT_BYTES = V7X_VMEM_BYTES - 6 * 2**20


def _params(*sem):
    return pltpu.CompilerParams(dimension_semantics=sem, vmem_limit_bytes=VMEM_LIMIT_BYTES)


def _rms(x, g):
    return x * lax.rsqrt(jnp.mean(x * x, axis=-1, keepdims=True) + EPS) * g


def _unit_rms(x):
    return x * lax.rsqrt(jnp.mean(x * x, axis=-1, keepdims=True) + EPS)


def _sigmoid(x):
    return 0.5 * jnp.tanh(0.5 * x) + 0.5


def _norm_proj_kernel(n_side, x_ref, g_ref, w_ref, *refs):
    side_in, o_ref = refs[:n_side], refs[n_side]
    side_out, hn_ref = refs[n_side + 1:2 * n_side + 1], refs[2 * n_side + 1]

    @pl.when(pl.program_id(1) == 0)
    def _():
        hn_ref[...] = _rms(x_ref[...], g_ref[...]).astype(BF16)

    acc = jnp.dot(hn_ref[...], w_ref[...].astype(BF16), preferred_element_type=F32)
    for s in range(o_ref.shape[0]):
        o_ref[s] = acc[:, s * LANE:(s + 1) * LANE].astype(BF16)
    for src, dst in zip(side_in, side_out):
        dst[...] = src[...].astype(BF16)


def _norm_proj(x2d, g, w, tm, tn, side=()):
    m, k = x2d.shape
    n = w.shape[1]
    side_specs = [pl.BlockSpec(blk, imap) for _, blk, imap in side]
    out = pl.pallas_call(
        functools.partial(_norm_proj_kernel, len(side)),
        grid=(m // tm, n // tn),
        in_specs=[
            pl.BlockSpec((tm, k), lambda i, j: (i, 0)),
            pl.BlockSpec((1, k), lambda i, j: (0, 0)),
            pl.BlockSpec((k, tn), lambda i, j: (0, j)),
        ] + side_specs,
        out_specs=[pl.BlockSpec((tn // LANE, tm, LANE), lambda i, j: (j, i, 0))] + side_specs,
        out_shape=[jax.ShapeDtypeStruct((n // LANE, m, LANE), BF16)]
        + [jax.ShapeDtypeStruct(a.shape, BF16) for a, _, _ in side],
        scratch_shapes=[pltpu.VMEM((tm, k), BF16)],
        compiler_params=_params("parallel", "arbitrary"),
        name="norm_proj",
    )(x2d, g.reshape(1, k), w, *[a for a, _, _ in side])
    return out[0], out[1:]


def _na_build_tables(h, rpb_ref, t_ref, base_ref):
    qc = lax.broadcasted_iota(jnp.int32, (GRID_W, LANE), 0)
    lane = lax.broadcasted_iota(jnp.int32, (GRID_W, LANE), 1)
    kc = lane & (GRID_W - 1)
    d = jnp.clip(kc - qc, -(NA_COLS - 1), NA_COLS - 1) + (NA_COLS - 1)
    cs = jnp.clip(qc - NA_COLS // 2, 0, GRID_W - NA_COLS)
    col_ok = (kc >= cs) & (kc < cs + NA_COLS)
    left = lane < GRID_W
    neg = jnp.full((GRID_W, LANE), NEG, F32)

    def body(dr, _):
        base = (h * NA_DR + dr) * NA_DC
        val = jnp.zeros((GRID_W, LANE), F32)
        for dd in range(NA_DC):
            val = jnp.where(d == dd, rpb_ref[base + dd], val)
        base_ref[dr] = jnp.where(col_ok, val * LOG2E, NEG)
        return 0

    lax.fori_loop(0, NA_DR, body, 0)
    for dr in range(NA_DR):
        second = base_ref[dr + 1] if dr + 1 < NA_DR else neg
        t_ref[dr] = jnp.where(left, base_ref[dr], second)
        t_ref[NA_DR + dr] = jnp.where(left, base_ref[dr], neg)
        t_ref[2 * NA_DR + dr] = jnp.where(left, neg, base_ref[dr])
    t_ref[3 * NA_DR] = neg


def _na_kernel(rpb_ref, q_ref, k_ref, v_ref, gq_ref, gk_ref, o_ref, kt_ref, va_ref, t_ref, base_ref):
    @pl.when(pl.program_id(1) == 0)
    def _():
        _na_build_tables(pl.program_id(0), rpb_ref, t_ref, base_ref)

    s_len = q_ref.shape[1]
    rows = s_len // GRID_W
    step_tok = NA_QROWS * GRID_W
    win_tok = NA_WIN * GRID_W
    n_tiles = NA_WIN // 2
    gq = gq_ref[...] * gk_ref[...] * (NA_HEAD_DIM ** -0.5 * LOG2E)
    va_ref[:, :NA_HEAD_DIM] = v_ref[0]
    va_ref[:, NA_HEAD_DIM:] = jnp.ones((s_len, NA_HEAD_DIM), BF16)

    def knorm(c, _):
        sl = pl.ds(pl.multiple_of(c * LANE, LANE), LANE)
        kt_ref[c] = _unit_rms(k_ref[0, sl, :].astype(F32)).astype(BF16).T
        return 0

    lax.fori_loop(0, s_len // LANE, knorm, 0, unroll=True)

    def step(i, _):
        r0 = NA_QROWS * i
        ws = jnp.clip(r0 - NA_MAX_ROWS // 2, 0, rows - NA_WIN)
        wp = ws // 2
        qsl = pl.ds(pl.multiple_of(i * step_tok, step_tok), step_tok)
        wsl = pl.ds(pl.multiple_of(ws * GRID_W, LANE), win_tok)
        qn = _rms(q_ref[0, qsl, :].astype(F32), gq).astype(BF16)
        kwin = jnp.concatenate([kt_ref[wp + t] for t in range(n_tiles)], axis=1)
        s = jnp.dot(qn, kwin, preferred_element_type=F32)
        bias_rows = []
        for qr in range(NA_QROWS):
            r = r0 + qr
            rs = jnp.clip(r - NA_MAX_ROWS // 2, 0, rows - NA_MAX_ROWS)
            tiles = []
            for t in range(n_tiles):
                ka = ws + 2 * t
                dr = ka - r + (NA_MAX_ROWS - 1)
                va = (ka >= rs) & (ka < rs + NA_MAX_ROWS)
                vb = (ka + 1 >= rs) & (ka + 1 < rs + NA_MAX_ROWS)
                idx = jnp.where(va, jnp.where(vb, dr, NA_DR + dr),
                                jnp.where(vb, 2 * NA_DR + dr + 1, 3 * NA_DR))
                tiles.append(t_ref[idx])
            bias_rows.append(jnp.concatenate(tiles, axis=1))
        s = s + jnp.concatenate(bias_rows, axis=0)
        e = jnp.exp2(s - jnp.max(s, axis=-1, keepdims=True))
        o = jnp.dot(e.astype(BF16), va_ref[wsl, :], preferred_element_type=F32)
        o_ref[qsl, :] = (o[:, :NA_HEAD_DIM] / o[:, NA_HEAD_DIM:]).astype(BF16)
        return 0

    lax.fori_loop(0, rows // NA_QROWS, step, 0, unroll=True)


def _na_attention(proj, rpb, gq, gk, batch, s_len):
    t = batch * s_len
    g_spec = pl.BlockSpec((1, NA_HEAD_DIM), lambda h, b: (0, 0))
    return pl.pallas_call(
        _na_kernel,
        grid=(NA_HEADS, batch),
        in_specs=[
            pl.BlockSpec(memory_space=pltpu.SMEM),
            pl.BlockSpec((1, s_len, LANE), lambda h, b: (SLAB_NA_Q + h, b, 0)),
            pl.BlockSpec((1, s_len, LANE), lambda h, b: (SLAB_NA_K + h, b, 0)),
            pl.BlockSpec((1, s_len, LANE), lambda h, b: (SLAB_NA_V + h, b, 0)),
            g_spec, g_spec,
        ],
        out_specs=pl.BlockSpec((s_len, NA_HEAD_DIM), lambda h, b: (b, h)),
        out_shape=jax.ShapeDtypeStruct((t, NA_HEADS * NA_HEAD_DIM), BF16),
        scratch_shapes=[pltpu.VMEM((s_len // LANE, NA_HEAD_DIM, LANE), BF16),
                        pltpu.VMEM((s_len, 2 * NA_HEAD_DIM), BF16),
                        pltpu.VMEM((NA_TBL, GRID_W, LANE), F32),
                        pltpu.VMEM((NA_DR, GRID_W, LANE), F32)],
        compiler_params=_params("arbitrary", "arbitrary"),
        name="na_attention",
    )(rpb.reshape(-1), proj, proj, proj, gq.reshape(1, -1), gk.reshape(1, -1))


def _log_sigmoid(x):
    return -(jnp.maximum(-x, 0.0) + jnp.log1p(jnp.exp(-jnp.abs(x))))


def _ret_kernel(lf_ref, lb_ref, q_ref, k_ref, v_ref, g_ref, cos_ref, sin_ref, gn_ref,
                o_ref, qr_ref, kt_ref, d_ref, qdf_ref, qdb_ref, kv_ref, s_ref):
    s_len = q_ref.shape[1]
    c = RET_BLOCK
    nb = s_len // c
    dk = RET_QK_DIM
    half = dk // 2

    h = pl.program_id(1)
    lgf = _log_sigmoid(jnp.full((1, 1), lf_ref[h], F32))
    lgb = _log_sigmoid(jnp.full((1, 1), lb_ref[h], F32))
    ic = lax.broadcasted_iota(jnp.int32, (c, 1), 0).astype(F32)
    jr = lax.broadcasted_iota(jnp.int32, (1, c), 1).astype(F32)
    diff = ic - jr
    scale = dk ** -0.5
    d_ref[...] = jnp.exp(jnp.where(diff >= 0, lgf, lgb) * jnp.abs(diff)) * scale
    qdf_ref[...] = jnp.broadcast_to(jnp.exp(lgf * (ic + 1.0)), (c, dk))
    qdb_ref[...] = jnp.broadcast_to(jnp.exp(lgb * (c - ic)), (c, dk))
    kdf = jnp.exp(lgf * (c - 1.0 - jr)) * scale
    kdb = jnp.exp(lgb * jr) * scale
    cd_f = jnp.exp(lgf * c)
    cd_b = jnp.exp(lgb * c)

    def block_v(sl):
        return jnp.concatenate([v_ref[0, sl, :], v_ref[1, sl, :]], axis=1)

    def prep(n, _):
        sl = pl.ds(pl.multiple_of(n * c, c), c)
        cos = cos_ref[sl, :]
        sin = sin_ref[sl, :]
        q = q_ref[0, sl, :].astype(F32)
        k = k_ref[0, sl, :].astype(F32)
        qr_ref[sl, :] = (q * cos + pltpu.roll(q, half, 1) * sin).astype(BF16)
        kt = (k * cos + pltpu.roll(k, half, 1) * sin).T
        kt_ref[n] = kt.astype(BF16)
        lhs = jnp.concatenate([(kt * kdf).astype(BF16), (kt * kdb).astype(BF16)], axis=0)
        kv_ref[n] = jnp.dot(lhs, block_v(sl), preferred_element_type=F32)
        return 0

    lax.fori_loop(0, nb, prep, 0, unroll=True)

    def scan_f(n, sf):
        s_ref[n, :dk, :] = sf.astype(BF16)
        return cd_f * sf + kv_ref[n, :dk, :]

    def scan_b(t, sb):
        n = nb - 1 - t
        s_ref[n, dk:, :] = sb.astype(BF16)
        return cd_b * sb + kv_ref[n, dk:, :]

    zero = jnp.zeros((dk, RET_V_DIM), F32)

    def scan(n, carry):
        return scan_f(n, carry[0]), scan_b(n, carry[1])

    lax.fori_loop(0, nb, scan, (zero, zero), unroll=True)

    gn = gn_ref[0]

    def out(n, _):
        sl = pl.ds(pl.multiple_of(n * c, c), c)
        q = qr_ref[sl, :]
        qf32 = q.astype(F32)
        a = jnp.dot(q, kt_ref[n], preferred_element_type=F32) * d_ref[...]
        lhs = jnp.concatenate([a.astype(BF16), (qf32 * qdf_ref[...]).astype(BF16),
                               (qf32 * qdb_ref[...]).astype(BF16)], axis=1)
        rhs = jnp.concatenate([block_v(sl), s_ref[n]], axis=0)
        o = jnp.dot(lhs, rhs, preferred_element_type=F32)
        mu = jnp.mean(o, axis=-1, keepdims=True)
        oc = o - mu
        y = oc * lax.rsqrt(jnp.mean(oc * oc, axis=-1, keepdims=True) + EPS) * gn
        gate = jnp.concatenate([g_ref[0, sl, :], g_ref[1, sl, :]], axis=1).astype(F32)
        o_ref[sl, :] = (y * gate * _sigmoid(gate)).astype(BF16)
        return 0

    lax.fori_loop(0, nb, out, 0, unroll=True)


def _retention(proj, cos2, sin2, lf, lb, gn, batch, s_len):
    t = batch * s_len
    nb = s_len // RET_BLOCK
    smem = pl.BlockSpec(memory_space=pltpu.SMEM)
    return pl.pallas_call(
        _ret_kernel,
        grid=(batch, RET_HEADS),
        in_specs=[
            smem, smem,
            pl.BlockSpec((1, s_len, LANE), lambda b, h: (SLAB_RQ + h, b, 0)),
            pl.BlockSpec((1, s_len, LANE), lambda b, h: (SLAB_RK + h, b, 0)),
            pl.BlockSpec((2, s_len, LANE), lambda b, h: (SLAB_RV // 2 + h, b, 0)),
            pl.BlockSpec((2, s_len, LANE), lambda b, h: (SLAB_RG // 2 + h, b, 0)),
            pl.BlockSpec((s_len, RET_QK_DIM), lambda b, h: (0, 0)),
            pl.BlockSpec((s_len, RET_QK_DIM), lambda b, h: (0, 0)),
            pl.BlockSpec((1, 1, RET_V_DIM), lambda b, h: (h, 0, 0)),
        ],
        out_specs=pl.BlockSpec((s_len, RET_V_DIM), lambda b, h: (b, h)),
        out_shape=jax.ShapeDtypeStruct((t, RET_HEADS * RET_V_DIM), BF16),
        scratch_shapes=[
            pltpu.VMEM((s_len, RET_QK_DIM), BF16),
            pltpu.VMEM((nb, RET_QK_DIM, RET_BLOCK), BF16),
            pltpu.VMEM((RET_BLOCK, RET_BLOCK), F32),
            pltpu.VMEM((RET_BLOCK, RET_QK_DIM), F32),
            pltpu.VMEM((RET_BLOCK, RET_QK_DIM), F32),
            pltpu.VMEM((nb, 2 * RET_QK_DIM, RET_V_DIM), F32),
            pltpu.VMEM((nb, 2 * RET_QK_DIM, RET_V_DIM), BF16),
        ],
        compiler_params=_params("parallel", "parallel"),
        name="retention",
    )(lf, lb, proj, proj, proj, proj, cos2, sin2, gn.reshape(RET_HEADS, 1, RET_V_DIM))


def _xa_kernel(q_ref, k_ref, v_ref, gq_ref, gk_ref, o_ref):
    s_len = q_ref.shape[1]
    tq = 512
    gq = gq_ref[...] * gk_ref[...] * (XA_HEAD_DIM ** -0.5 * LOG2E)
    k = jnp.concatenate([k_ref[0], k_ref[1]], axis=1).astype(F32)
    kn = _unit_rms(k).astype(BF16)
    v = jnp.concatenate([v_ref[0], v_ref[1]], axis=1)

    def body(i, _):
        sl = pl.ds(pl.multiple_of(i * tq, tq), tq)
        q = jnp.concatenate([q_ref[0, sl, :], q_ref[1, sl, :]], axis=1).astype(F32)
        qn = _rms(q, gq).astype(BF16)
        s = lax.dot_general(qn, kn, (((1,), (1,)), ((), ())), preferred_element_type=F32)
        e = jnp.exp2(s - jnp.max(s, axis=-1, keepdims=True))
        l = jnp.sum(e, axis=-1, keepdims=True)
        o = jnp.dot(e.astype(BF16), v, preferred_element_type=F32)
        o_ref[sl, :] = (o / l).astype(BF16)
        return 0

    lax.fori_loop(0, s_len // tq, body, 0, unroll=True)


def _mem_xattn(proj, mkv, gq, gk, batch, s_len, n_mem):
    t = batch * s_len
    g_spec = pl.BlockSpec((1, XA_HEAD_DIM), lambda b, h: (0, 0))
    return pl.pallas_call(
        _xa_kernel,
        grid=(batch, XA_HEADS),
        in_specs=[
            pl.BlockSpec((2, s_len, LANE), lambda b, h: (SLAB_XQ // 2 + h, b, 0)),
            pl.BlockSpec((2, n_mem, LANE), lambda b, h: (h, b, 0)),
            pl.BlockSpec((2, n_mem, LANE), lambda b, h: (XA_HEADS + h, b, 0)),
            g_spec, g_spec,
        ],
        out_specs=pl.BlockSpec((s_len, XA_HEAD_DIM), lambda b, h: (b, h)),
        out_shape=jax.ShapeDtypeStruct((t, XA_HEADS * XA_HEAD_DIM), BF16),
        compiler_params=_params("parallel", "parallel"),
        name="mem_xattn",
    )(proj, mkv, mkv, gq.reshape(1, -1), gk.reshape(1, -1))


def _merge_kernel(ona_ref, oret_ref, omem_ref, wna_ref, wret_ref, wmem_ref,
                  gna_ref, gret_ref, gmem_ref, o_ref):
    y_na = jnp.dot(ona_ref[...], wna_ref[...], preferred_element_type=F32)
    y_ret = jnp.dot(oret_ref[...], wret_ref[...], preferred_element_type=F32)
    y_mem = jnp.dot(omem_ref[...], wmem_ref[...], preferred_element_type=F32)
    for j in range(gna_ref.shape[0]):
        cs = slice(j * LANE, (j + 1) * LANE)
        o_ref[:, cs] = (_sigmoid(gna_ref[j].astype(F32)) * y_na[:, cs]
                        + _sigmoid(gret_ref[j].astype(F32)) * y_ret[:, cs]
                        + _sigmoid(gmem_ref[j].astype(F32)) * y_mem[:, cs]).astype(BF16)


def _merge(o_na, o_ret, o_mem, w_na, w_ret, w_mem, proj, tm, tn):
    t = o_na.shape[0]
    d = w_na.shape[1]
    ns = tn // LANE

    def lhs_spec(a):
        return pl.BlockSpec((tm, a.shape[1]), lambda i, j: (i, 0))

    def w_spec(w):
        mode = dict(pipeline_mode=pl.Buffered(1)) if tn == d else {}
        return pl.BlockSpec((w.shape[0], tn), lambda i, j: (0, j), **mode)

    def gate_spec(first):
        return pl.BlockSpec((ns, tm, LANE), lambda i, j: (first // ns + j, i, 0))

    return pl.pallas_call(
        _merge_kernel,
        grid=(t // tm, d // tn),
        in_specs=[lhs_spec(o_na), lhs_spec(o_ret), lhs_spec(o_mem),
                  w_spec(w_na), w_spec(w_ret), w_spec(w_mem),
                  gate_spec(SLAB_G_NA), gate_spec(SLAB_G_RET), gate_spec(SLAB_G_MEM)],
        out_specs=pl.BlockSpec((tm, tn), lambda i, j: (i, j)),
        out_shape=jax.ShapeDtypeStruct((t, d), BF16),
        compiler_params=_params("parallel", "arbitrary"),
        name="merge",
    )(o_na, o_ret, o_mem, w_na, w_ret, w_mem, proj, proj, proj)


def _out_proj_kernel(m_ref, w_ref, x_ref, o_ref, wb_ref):
    @pl.when(pl.program_id(0) == 0)
    def _():
        wb_ref[...] = w_ref[...].astype(BF16)

    o_ref[...] = x_ref[...] + jnp.dot(m_ref[...], wb_ref[...], preferred_element_type=F32)


def _out_proj(merged, w, x2d, tm):
    t, k = merged.shape
    d = w.shape[1]
    return pl.pallas_call(
        _out_proj_kernel,
        grid=(t // tm,),
        in_specs=[pl.BlockSpec((tm, k), lambda i: (i, 0)),
                  pl.BlockSpec((k, d), lambda i: (0, 0), pipeline_mode=pl.Buffered(1)),
                  pl.BlockSpec((tm, d), lambda i: (i, 0))],
        out_specs=pl.BlockSpec((tm, d), lambda i: (i, 0)),
        out_shape=jax.ShapeDtypeStruct((t, d), F32),
        scratch_shapes=[pltpu.VMEM((k, d), BF16)],
        compiler_params=_params("arbitrary"),
        name="out_proj",
    )(merged, w, x2d)


def _ffn_kernel(x_ref, g_ref, w1_ref, w2_ref, o_ref, h_ref):
    def mlp(h):
        a = jnp.maximum(jnp.dot(h, w1_ref[...], preferred_element_type=F32), 0.0)
        return jnp.dot((a * a).astype(BF16), w2_ref[...], preferred_element_type=F32)

    @pl.when(pl.program_id(1) == 0)
    def _():
        x = x_ref[...]
        h = _rms(x, g_ref[...]).astype(BF16)
        h_ref[...] = h
        o_ref[...] = x + mlp(h)

    @pl.when(pl.program_id(1) != 0)
    def _():
        o_ref[...] += mlp(h_ref[...])


def _ffn(x1, g, w1, w2, tm, tf):
    t, d = x1.shape
    dff = w1.shape[1]
    return pl.pallas_call(
        _ffn_kernel,
        grid=(t // tm, dff // tf),
        in_specs=[pl.BlockSpec((tm, d), lambda i, f: (i, 0)),
                  pl.BlockSpec((1, d), lambda i, f: (0, 0)),
                  pl.BlockSpec((d, tf), lambda i, f: (0, f)),
                  pl.BlockSpec((tf, d), lambda i, f: (f, 0))],
        out_specs=pl.BlockSpec((tm, d), lambda i, f: (i, 0)),
        out_shape=jax.ShapeDtypeStruct((t, d), F32),
        scratch_shapes=[pltpu.VMEM((tm, d), BF16)],
        compiler_params=_params("parallel", "arbitrary"),
        name="ffn",
    )(x1, g.reshape(1, d), w1, w2)


def _rope_tables(s_len):
    half = RET_QK_DIM // 2
    inv = np.power(np.float64(ROPE_BASE), -np.arange(half, dtype=np.float64) / half)
    ang = np.arange(s_len, dtype=np.float64)[:, None] * inv[None, :]
    cos, sin = np.cos(ang), np.sin(ang)
    return (jnp.asarray(np.concatenate([cos, cos], axis=1), F32),
            jnp.asarray(np.concatenate([-sin, sin], axis=1), F32))


class _Tiles(NamedTuple):
    tm: int
    tn_in: int
    tm_merge: int
    tn_merge: int
    tm_out: int
    tf: int
    tn_mem: int


def _tiles(t):
    return _Tiles(tm=min(1024, t), tn_in=1024, tm_merge=min(256, t), tn_merge=2048, tm_out=min(512, t), tf=1024, tn_mem=512)


def kernel(x, mem, norm_mix_g, w_in, na_q_norm_g, na_k_norm_g, na_rpb, ret_decay_logit_fwd, ret_decay_logit_bwd, ret_gn_g, mem_norm_g, w_mem_kv, xa_q_norm_g, xa_k_norm_g, w_br_na, w_br_ret, w_br_mem, w_out, norm_ffn_g, w_ff1, w_ff2):
    batch, s_len, d = x.shape
    n_mem = mem.shape[1]
    t = batch * s_len
    tiles = _tiles(t)
    cos2, sin2 = _rope_tables(s_len)
    x2d = x.reshape(t, d)
    mem2d = mem.reshape(batch * n_mem, d)
    gm, gn = t // tiles.tm, w_in.shape[2] // tiles.tn_in

    def side_job(w):
        r, c = w.shape
        if r % (gm * gn * 16) == 0:
            return w, (r // (gm * gn), c), lambda i, j: (i * gn + j, 0)
        assert r % (gn * 16) == 0 and c % (gm * LANE) == 0, (w.shape, gm, gn)
        return w, (r // gn, c // gm), lambda i, j: (j, i)

    for l in range(w_in.shape[0]):
        sides = [side_job(w) for w in (w_ff1[l], w_ff2[l], w_br_na[l], w_br_ret[l], w_br_mem[l])]
        proj, (w_ff1_bf, w_ff2_bf, w_na_bf, w_ret_bf, w_mem_bf) = _norm_proj(
            x2d, norm_mix_g[l], w_in[l], tiles.tm, tiles.tn_in, side=sides)
        mkv, _ = _norm_proj(mem2d, mem_norm_g[l], w_mem_kv[l], batch * n_mem, tiles.tn_mem)

        o_mem = _mem_xattn(proj, mkv, xa_q_norm_g[l], xa_k_norm_g[l], batch, s_len, n_mem)
        o_na = _na_attention(proj, na_rpb[l], na_q_norm_g[l], na_k_norm_g[l], batch, s_len)
        o_ret = _retention(proj, cos2, sin2, ret_decay_logit_fwd[l], ret_decay_logit_bwd[l], ret_gn_g[l],
                           batch, s_len)

        merged = _merge(o_na, o_ret, o_mem, w_na_bf, w_ret_bf, w_mem_bf, proj, tiles.tm_merge, tiles.tn_merge)
        x1 = _out_proj(merged, w_out[l], x2d, tiles.tm_out)
        x2d = _ffn(x1, norm_ffn_g[l], w_ff1_bf, w_ff2_bf, tiles.tm_out, tiles.tf)
    return x2d.reshape(batch, s_len, d)
```

```python
import functools
from typing import NamedTuple

import jax
import jax.numpy as jnp
import numpy as np
from jax import lax
from jax.experimental import pallas as pl
from jax.experimental.pallas import tpu as pltpu

F32 = jnp.float32
BF16 = jnp.bfloat16

LANE = 128
EPS = 1e-6
NEG = -1e30
LOG2E = 1.4426950408889634

GRID_W = 64
NA_HEADS = 8
NA_HEAD_DIM = 128
NA_MAX_ROWS = 8
NA_COLS = 16
NA_DR = 2 * NA_MAX_ROWS - 1
NA_DC = 2 * NA_COLS - 1
NA_TBL = 3 * NA_DR + 1
NA_QROWS = 4
NA_WIN = NA_QROWS + NA_MAX_ROWS

RET_HEADS = 8
RET_QK_DIM = 128
RET_V_DIM = 256
RET_BLOCK = 256
ROPE_BASE = 10000.0

XA_HEADS = 4
XA_HEAD_DIM = 256

SLAB_NA_Q, SLAB_NA_K, SLAB_NA_V = 0, 8, 16
SLAB_RQ, SLAB_RK, SLAB_RV, SLAB_RG = 24, 32, 40, 56
SLAB_XQ = 72
SLAB_G_NA, SLAB_G_RET, SLAB_G_MEM = 80, 96, 112

V7X_VMEM_BYTES = 64 * 2**20
VMEM_LIMIT_BYTES = V7X_VMEM_BYTES - 6 * 2**20


def _params(*sem):
    return pltpu.CompilerParams(dimension_semantics=sem, vmem_limit_bytes=VMEM_LIMIT_BYTES)


def _rms(x, g):
    return x * lax.rsqrt(jnp.mean(x * x, axis=-1, keepdims=True) + EPS) * g


def _unit_rms(x):
    return x * lax.rsqrt(jnp.mean(x * x, axis=-1, keepdims=True) + EPS)


def _sigmoid(x):
    return 0.5 * jnp.tanh(0.5 * x) + 0.5


def _norm_proj_kernel(n_side, x_ref, g_ref, w_ref, *refs):
    side_in, o_ref = refs[:n_side], refs[n_side]
    side_out, hn_ref = refs[n_side + 1:2 * n_side + 1], refs[2 * n_side + 1]

    @pl.when(pl.program_id(1) == 0)
    def _():
        hn_ref[...] = _rms(x_ref[...], g_ref[...]).astype(BF16)

    acc = jnp.dot(hn_ref[...], w_ref[...].astype(BF16), preferred_element_type=F32)
    for s in range(o_ref.shape[0]):
        o_ref[s] = acc[:, s * LANE:(s + 1) * LANE].astype(BF16)
    for src, dst in zip(side_in, side_out):
        dst[...] = src[...].astype(BF16)


def _norm_proj(x2d, g, w, tm, tn, side=()):
    m, k = x2d.shape
    n = w.shape[1]
    side_specs = [pl.BlockSpec(blk, imap) for _, blk, imap in side]
    out = pl.pallas_call(
        functools.partial(_norm_proj_kernel, len(side)),
        grid=(m // tm, n // tn),
        in_specs=[
            pl.BlockSpec((tm, k), lambda i, j: (i, 0)),
            pl.BlockSpec((1, k), lambda i, j: (0, 0)),
            pl.BlockSpec((k, tn), lambda i, j: (0, j)),
        ] + side_specs,
        out_specs=[pl.BlockSpec((tn // LANE, tm, LANE), lambda i, j: (j, i, 0))] + side_specs,
        out_shape=[jax.ShapeDtypeStruct((n // LANE, m, LANE), BF16)]
        + [jax.ShapeDtypeStruct(a.shape, BF16) for a, _, _ in side],
        scratch_shapes=[pltpu.VMEM((tm, k), BF16)],
        compiler_params=_params("parallel", "arbitrary"),
        name="norm_proj",
    )(x2d, g.reshape(1, k), w, *[a for a, _, _ in side])
    return out[0], out[1:]


def _na_build_tables(h, rpb_ref, t_ref, base_ref):
    qc = lax.broadcasted_iota(jnp.int32, (GRID_W, LANE), 0)
    lane = lax.broadcasted_iota(jnp.int32, (GRID_W, LANE), 1)
    kc = lane & (GRID_W - 1)
    d = jnp.clip(kc - qc, -(NA_COLS - 1), NA_COLS - 1) + (NA_COLS - 1)
    cs = jnp.clip(qc - NA_COLS // 2, 0, GRID_W - NA_COLS)
    col_ok = (kc >= cs) & (kc < cs + NA_COLS)
    left = lane < GRID_W
    neg = jnp.full((GRID_W, LANE), NEG, F32)

    def body(dr, _):
        base = (h * NA_DR + dr) * NA_DC
        val = jnp.zeros((GRID_W, LANE), F32)
        for dd in range(NA_DC):
            val = jnp.where(d == dd, rpb_ref[base + dd], val)
        base_ref[dr] = jnp.where(col_ok, val * LOG2E, NEG)
        return 0

    lax.fori_loop(0, NA_DR, body, 0)
    for dr in range(NA_DR):
        second = base_ref[dr + 1] if dr + 1 < NA_DR else neg
        t_ref[dr] = jnp.where(left, base_ref[dr], second)
        t_ref[NA_DR + dr] = jnp.where(left, base_ref[dr], neg)
        t_ref[2 * NA_DR + dr] = jnp.where(left, neg, base_ref[dr])
    t_ref[3 * NA_DR] = neg


def _na_kernel(rpb_ref, q_ref, k_ref, v_ref, gq_ref, gk_ref, o_ref, kt_ref, va_ref, t_ref, base_ref):
    @pl.when(pl.program_id(1) == 0)
    def _():
        _na_build_tables(pl.program_id(0), rpb_ref, t_ref, base_ref)

    s_len = q_ref.shape[1]
    rows = s_len // GRID_W
    step_tok = NA_QROWS * GRID_W
    win_tok = NA_WIN * GRID_W
    n_tiles = NA_WIN // 2
    gq = gq_ref[...] * gk_ref[...] * (NA_HEAD_DIM ** -0.5 * LOG2E)
    va_ref[:, :NA_HEAD_DIM] = v_ref[0]
    va_ref[:, NA_HEAD_DIM:] = jnp.ones((s_len, NA_HEAD_DIM), BF16)

    def knorm(c, _):
        sl = pl.ds(pl.multiple_of(c * LANE, LANE), LANE)
        kt_ref[c] = _unit_rms(k_ref[0, sl, :].astype(F32)).astype(BF16).T
        return 0

    lax.fori_loop(0, s_len // LANE, knorm, 0, unroll=True)

    def step(i, _):
        r0 = NA_QROWS * i
        ws = jnp.clip(r0 - NA_MAX_ROWS // 2, 0, rows - NA_WIN)
        wp = ws // 2
        qsl = pl.ds(pl.multiple_of(i * step_tok, step_tok), step_tok)
        wsl = pl.ds(pl.multiple_of(ws * GRID_W, LANE), win_tok)
        qn = _rms(q_ref[0, qsl, :].astype(F32), gq).astype(BF16)
        kwin = jnp.concatenate([kt_ref[wp + t] for t in range(n_tiles)], axis=1)
        s = jnp.dot(qn, kwin, preferred_element_type=F32)
        bias_rows = []
        for qr in range(NA_QROWS):
            r = r0 + qr
            rs = jnp.clip(r - NA_MAX_ROWS // 2, 0, rows - NA_MAX_ROWS)
            tiles = []
            for t in range(n_tiles):
                ka = ws + 2 * t
                dr = ka - r + (NA_MAX_ROWS - 1)
                va = (ka >= rs) & (ka < rs + NA_MAX_ROWS)
                vb = (ka + 1 >= rs) & (ka + 1 < rs + NA_MAX_ROWS)
                idx = jnp.where(va, jnp.where(vb, dr, NA_DR + dr),
                                jnp.where(vb, 2 * NA_DR + dr + 1, 3 * NA_DR))
                tiles.append(t_ref[idx])
            bias_rows.append(jnp.concatenate(tiles, axis=1))
        s = s + jnp.concatenate(bias_rows, axis=0)
        e = jnp.exp2(s - jnp.max(s, axis=-1, keepdims=True))
        o = jnp.dot(e.astype(BF16), va_ref[wsl, :], preferred_element_type=F32)
        o_ref[qsl, :] = (o[:, :NA_HEAD_DIM] / o[:, NA_HEAD_DIM:]).astype(BF16)
        return 0

    lax.fori_loop(0, rows // NA_QROWS, step, 0, unroll=True)


def _na_attention(proj, rpb, gq, gk, batch, s_len):
    t = batch * s_len
    g_spec = pl.BlockSpec((1, NA_HEAD_DIM), lambda h, b: (0, 0))
    return pl.pallas_call(
        _na_kernel,
        grid=(NA_HEADS, batch),
        in_specs=[
            pl.BlockSpec(memory_space=pltpu.SMEM),
            pl.BlockSpec((1, s_len, LANE), lambda h, b: (SLAB_NA_Q + h, b, 0)),
            pl.BlockSpec((1, s_len, LANE), lambda h, b: (SLAB_NA_K + h, b, 0)),
            pl.BlockSpec((1, s_len, LANE), lambda h, b: (SLAB_NA_V + h, b, 0)),
            g_spec, g_spec,
        ],
        out_specs=pl.BlockSpec((s_len, NA_HEAD_DIM), lambda h, b: (b, h)),
        out_shape=jax.ShapeDtypeStruct((t, NA_HEADS * NA_HEAD_DIM), BF16),
        scratch_shapes=[pltpu.VMEM((s_len // LANE, NA_HEAD_DIM, LANE), BF16),
                        pltpu.VMEM((s_len, 2 * NA_HEAD_DIM), BF16),
                        pltpu.VMEM((NA_TBL, GRID_W, LANE), F32),
                        pltpu.VMEM((NA_DR, GRID_W, LANE), F32)],
        compiler_params=_params("arbitrary", "arbitrary"),
        name="na_attention",
    )(rpb.reshape(-1), proj, proj, proj, gq.reshape(1, -1), gk.reshape(1, -1))


def _log_sigmoid(x):
    return -(jnp.maximum(-x, 0.0) + jnp.log1p(jnp.exp(-jnp.abs(x))))


def _ret_kernel(lf_ref, lb_ref, q_ref, k_ref, v_ref, g_ref, cos_ref, sin_ref, gn_ref,
                o_ref, qr_ref, kt_ref, d_ref, qdf_ref, qdb_ref, kv_ref, s_ref):
    s_len = q_ref.shape[1]
    c = RET_BLOCK
    nb = s_len // c
    dk = RET_QK_DIM
    half = dk // 2

    h = pl.program_id(1)
    lgf = _log_sigmoid(jnp.full((1, 1), lf_ref[h], F32))
    lgb = _log_sigmoid(jnp.full((1, 1), lb_ref[h], F32))
    ic = lax.broadcasted_iota(jnp.int32, (c, 1), 0).astype(F32)
    jr = lax.broadcasted_iota(jnp.int32, (1, c), 1).astype(F32)
    diff = ic - jr
    scale = dk ** -0.5
    d_ref[...] = jnp.exp(jnp.where(diff >= 0, lgf, lgb) * jnp.abs(diff)) * scale
    qdf_ref[...] = jnp.broadcast_to(jnp.exp(lgf * (ic + 1.0)), (c, dk))
    qdb_ref[...] = jnp.broadcast_to(jnp.exp(lgb * (c - ic)), (c, dk))
    kdf = jnp.exp(lgf * (c - 1.0 - jr)) * scale
    kdb = jnp.exp(lgb * jr) * scale
    cd_f = jnp.exp(lgf * c)
    cd_b = jnp.exp(lgb * c)

    def block_v(sl):
        return jnp.concatenate([v_ref[0, sl, :], v_ref[1, sl, :]], axis=1)

    def prep(n, _):
        sl = pl.ds(pl.multiple_of(n * c, c), c)
        cos = cos_ref[sl, :]
        sin = sin_ref[sl, :]
        q = q_ref[0, sl, :].astype(F32)
        k = k_ref[0, sl, :].astype(F32)
        qr_ref[sl, :] = (q * cos + pltpu.roll(q, half, 1) * sin).astype(BF16)
        kt = (k * cos + pltpu.roll(k, half, 1) * sin).T
        kt_ref[n] = kt.astype(BF16)
        lhs = jnp.concatenate([(kt * kdf).astype(BF16), (kt * kdb).astype(BF16)], axis=0)
        kv_ref[n] = jnp.dot(lhs, block_v(sl), preferred_element_type=F32)
        return 0

    lax.fori_loop(0, nb, prep, 0, unroll=True)

    def scan_f(n, sf):
        s_ref[n, :dk, :] = sf.astype(BF16)
        return cd_f * sf + kv_ref[n, :dk, :]

    def scan_b(t, sb):
        n = nb - 1 - t
        s_ref[n, dk:, :] = sb.astype(BF16)
        return cd_b * sb + kv_ref[n, dk:, :]

    zero = jnp.zeros((dk, RET_V_DIM), F32)

    def scan(n, carry):
        return scan_f(n, carry[0]), scan_b(n, carry[1])

    lax.fori_loop(0, nb, scan, (zero, zero), unroll=True)

    gn = gn_ref[0]

    def out(n, _):
        sl = pl.ds(pl.multiple_of(n * c, c), c)
        q = qr_ref[sl, :]
        qf32 = q.astype(F32)
        a = jnp.dot(q, kt_ref[n], preferred_element_type=F32) * d_ref[...]
        lhs = jnp.concatenate([a.astype(BF16), (qf32 * qdf_ref[...]).astype(BF16),
                               (qf32 * qdb_ref[...]).astype(BF16)], axis=1)
        rhs = jnp.concatenate([block_v(sl), s_ref[n]], axis=0)
        o = jnp.dot(lhs, rhs, preferred_element_type=F32)
        mu = jnp.mean(o, axis=-1, keepdims=True)
        oc = o - mu
        y = oc * lax.rsqrt(jnp.mean(oc * oc, axis=-1, keepdims=True) + EPS) * gn
        gate = jnp.concatenate([g_ref[0, sl, :], g_ref[1, sl, :]], axis=1).astype(F32)
        o_ref[sl, :] = (y * gate * _sigmoid(gate)).astype(BF16)
        return 0

    lax.fori_loop(0, nb, out, 0, unroll=True)


def _retention(proj, cos2, sin2, lf, lb, gn, batch, s_len):
    t = batch * s_len
    nb = s_len // RET_BLOCK
    smem = pl.BlockSpec(memory_space=pltpu.SMEM)
    return pl.pallas_call(
        _ret_kernel,
        grid=(batch, RET_HEADS),
        in_specs=[
            smem, smem,
            pl.BlockSpec((1, s_len, LANE), lambda b, h: (SLAB_RQ + h, b, 0)),
            pl.BlockSpec((1, s_len, LANE), lambda b, h: (SLAB_RK + h, b, 0)),
            pl.BlockSpec((2, s_len, LANE), lambda b, h: (SLAB_RV // 2 + h, b, 0)),
            pl.BlockSpec((2, s_len, LANE), lambda b, h: (SLAB_RG // 2 + h, b, 0)),
            pl.BlockSpec((s_len, RET_QK_DIM), lambda b, h: (0, 0)),
            pl.BlockSpec((s_len, RET_QK_DIM), lambda b, h: (0, 0)),
            pl.BlockSpec((1, 1, RET_V_DIM), lambda b, h: (h, 0, 0)),
        ],
        out_specs=pl.BlockSpec((s_len, RET_V_DIM), lambda b, h: (b, h)),
        out_shape=jax.ShapeDtypeStruct((t, RET_HEADS * RET_V_DIM), BF16),
        scratch_shapes=[
            pltpu.VMEM((s_len, RET_QK_DIM), BF16),
            pltpu.VMEM((nb, RET_QK_DIM, RET_BLOCK), BF16),
            pltpu.VMEM((RET_BLOCK, RET_BLOCK), F32),
            pltpu.VMEM((RET_BLOCK, RET_QK_DIM), F32),
            pltpu.VMEM((RET_BLOCK, RET_QK_DIM), F32),
            pltpu.VMEM((nb, 2 * RET_QK_DIM, RET_V_DIM), F32),
            pltpu.VMEM((nb, 2 * RET_QK_DIM, RET_V_DIM), BF16),
        ],
        compiler_params=_params("parallel", "parallel"),
        name="retention",
    )(lf, lb, proj, proj, proj, proj, cos2, sin2, gn.reshape(RET_HEADS, 1, RET_V_DIM))


def _xa_kernel(q_ref, k_ref, v_ref, gq_ref, gk_ref, o_ref):
    s_len = q_ref.shape[1]
    tq = 512
    gq = gq_ref[...] * gk_ref[...] * (XA_HEAD_DIM ** -0.5 * LOG2E)
    k = jnp.concatenate([k_ref[0], k_ref[1]], axis=1).astype(F32)
    kn = _unit_rms(k).astype(BF16)
    v = jnp.concatenate([v_ref[0], v_ref[1]], axis=1)

    def body(i, _):
        sl = pl.ds(pl.multiple_of(i * tq, tq), tq)
        q = jnp.concatenate([q_ref[0, sl, :], q_ref[1, sl, :]], axis=1).astype(F32)
        qn = _rms(q, gq).astype(BF16)
        s = lax.dot_general(qn, kn, (((1,), (1,)), ((), ())), preferred_element_type=F32)
        e = jnp.exp2(s - jnp.max(s, axis=-1, keepdims=True))
        l = jnp.sum(e, axis=-1, keepdims=True)
        o = jnp.dot(e.astype(BF16), v, preferred_element_type=F32)
        o_ref[sl, :] = (o / l).astype(BF16)
        return 0

    lax.fori_loop(0, s_len // tq, body, 0, unroll=True)


def _mem_xattn(proj, mkv, gq, gk, batch, s_len, n_mem):
    t = batch * s_len
    g_spec = pl.BlockSpec((1, XA_HEAD_DIM), lambda b, h: (0, 0))
    return pl.pallas_call(
        _xa_kernel,
        grid=(batch, XA_HEADS),
        in_specs=[
            pl.BlockSpec((2, s_len, LANE), lambda b, h: (SLAB_XQ // 2 + h, b, 0)),
            pl.BlockSpec((2, n_mem, LANE), lambda b, h: (h, b, 0)),
            pl.BlockSpec((2, n_mem, LANE), lambda b, h: (XA_HEADS + h, b, 0)),
            g_spec, g_spec,
        ],
        out_specs=pl.BlockSpec((s_len, XA_HEAD_DIM), lambda b, h: (b, h)),
        out_shape=jax.ShapeDtypeStruct((t, XA_HEADS * XA_HEAD_DIM), BF16),
        compiler_params=_params("parallel", "parallel"),
        name="mem_xattn",
    )(proj, mkv, mkv, gq.reshape(1, -1), gk.reshape(1, -1))


def _merge_kernel(n_side, ona_ref, oret_ref, omem_ref, wna_ref, wret_ref, wmem_ref,
                  gna_ref, gret_ref, gmem_ref, *refs):
    side_in, o_ref, side_out = refs[:n_side], refs[n_side], refs[n_side + 1:]
    for src, dst in zip(side_in, side_out):
        dst[...] = src[...].astype(BF16)
    y_na = jnp.dot(ona_ref[...], wna_ref[...], preferred_element_type=F32)
    y_ret = jnp.dot(oret_ref[...], wret_ref[...], preferred_element_type=F32)
    y_mem = jnp.dot(omem_ref[...], wmem_ref[...], preferred_element_type=F32)
    for j in range(gna_ref.shape[0]):
        cs = slice(j * LANE, (j + 1) * LANE)
        o_ref[:, cs] = (_sigmoid(gna_ref[j].astype(F32)) * y_na[:, cs]
                        + _sigmoid(gret_ref[j].astype(F32)) * y_ret[:, cs]
                        + _sigmoid(gmem_ref[j].astype(F32)) * y_mem[:, cs]).astype(BF16)


def _merge(o_na, o_ret, o_mem, w_na, w_ret, w_mem, proj, tm, tn, side=()):
    t = o_na.shape[0]
    d = w_na.shape[1]
    ns = tn // LANE
    steps = (t // tm) * (d // tn)
    side_specs = [pl.BlockSpec((w.shape[0] // steps, w.shape[1]), lambda i, j: (i * (d // tn) + j, 0))
                  for w in side]

    def lhs_spec(a):
        return pl.BlockSpec((tm, a.shape[1]), lambda i, j: (i, 0))

    def w_spec(w):
        mode = dict(pipeline_mode=pl.Buffered(1)) if tn == d else {}
        return pl.BlockSpec((w.shape[0], tn), lambda i, j: (0, j), **mode)

    def gate_spec(first):
        return pl.BlockSpec((ns, tm, LANE), lambda i, j: (first // ns + j, i, 0))

    out = pl.pallas_call(
        functools.partial(_merge_kernel, len(side)),
        grid=(t // tm, d // tn),
        in_specs=[lhs_spec(o_na), lhs_spec(o_ret), lhs_spec(o_mem),
                  w_spec(w_na), w_spec(w_ret), w_spec(w_mem),
                  gate_spec(SLAB_G_NA), gate_spec(SLAB_G_RET), gate_spec(SLAB_G_MEM)] + side_specs,
        out_specs=[pl.BlockSpec((tm, tn), lambda i, j: (i, j))] + side_specs,
        out_shape=[jax.ShapeDtypeStruct((t, d), BF16)] + [jax.ShapeDtypeStruct(w.shape, BF16) for w in side],
        compiler_params=_params("parallel", "arbitrary"),
        name="merge",
    )(o_na, o_ret, o_mem, w_na, w_ret, w_mem, proj, proj, proj, *side)
    return out[0], out[1:]


def _out_proj_kernel(m_ref, w_ref, x_ref, o_ref, wb_ref):
    @pl.when(pl.program_id(0) == 0)
    def _():
        wb_ref[...] = w_ref[...].astype(BF16)

    o_ref[...] = x_ref[...] + jnp.dot(m_ref[...], wb_ref[...], preferred_element_type=F32)


def _out_proj(merged, w, x2d, tm):
    t, k = merged.shape
    d = w.shape[1]
    return pl.pallas_call(
        _out_proj_kernel,
        grid=(t // tm,),
        in_specs=[pl.BlockSpec((tm, k), lambda i: (i, 0)),
                  pl.BlockSpec((k, d), lambda i: (0, 0), pipeline_mode=pl.Buffered(1)),
                  pl.BlockSpec((tm, d), lambda i: (i, 0))],
        out_specs=pl.BlockSpec((tm, d), lambda i: (i, 0)),
        out_shape=jax.ShapeDtypeStruct((t, d), F32),
        scratch_shapes=[pltpu.VMEM((k, d), BF16)],
        compiler_params=_params("arbitrary"),
        name="out_proj",
    )(merged, w, x2d)


def _ffn_kernel(x_ref, g_ref, w1_ref, w2_ref, o_ref, h_ref):
    def mlp(h):
        a = jnp.maximum(jnp.dot(h, w1_ref[...], preferred_element_type=F32), 0.0)
        return jnp.dot((a * a).astype(BF16), w2_ref[...], preferred_element_type=F32)

    @pl.when(pl.program_id(1) == 0)
    def _():
        x = x_ref[...]
        h = _rms(x, g_ref[...]).astype(BF16)
        h_ref[...] = h
        o_ref[...] = x + mlp(h)

    @pl.when(pl.program_id(1) != 0)
    def _():
        o_ref[...] += mlp(h_ref[...])


def _ffn(x1, g, w1, w2, tm, tf):
    t, d = x1.shape
    dff = w1.shape[1]
    return pl.pallas_call(
        _ffn_kernel,
        grid=(t // tm, dff // tf),
        in_specs=[pl.BlockSpec((tm, d), lambda i, f: (i, 0)),
                  pl.BlockSpec((1, d), lambda i, f: (0, 0)),
                  pl.BlockSpec((d, tf), lambda i, f: (0, f)),
                  pl.BlockSpec((tf, d), lambda i, f: (f, 0))],
        out_specs=pl.BlockSpec((tm, d), lambda i, f: (i, 0)),
        out_shape=jax.ShapeDtypeStruct((t, d), F32),
        scratch_shapes=[pltpu.VMEM((tm, d), BF16)],
        compiler_params=_params("parallel", "arbitrary"),
        name="ffn",
    )(x1, g.reshape(1, d), w1, w2)


def _rope_tables(s_len):
    half = RET_QK_DIM // 2
    inv = np.power(np.float64(ROPE_BASE), -np.arange(half, dtype=np.float64) / half)
    ang = np.arange(s_len, dtype=np.float64)[:, None] * inv[None, :]
    cos, sin = np.cos(ang), np.sin(ang)
    return (jnp.asarray(np.concatenate([cos, cos], axis=1), F32),
            jnp.asarray(np.concatenate([-sin, sin], axis=1), F32))


class _Tiles(NamedTuple):
    tm: int
    tn_in: int
    tm_merge: int
    tn_merge: int
    tm_out: int
    tf: int
    tn_mem: int


def _tiles(t):
    return _Tiles(tm=min(1024, t), tn_in=1024, tm_merge=min(256, t), tn_merge=2048, tm_out=min(512, t), tf=1024, tn_mem=512)


def kernel(x, mem, norm_mix_g, w_in, na_q_norm_g, na_k_norm_g, na_rpb, ret_decay_logit_fwd, ret_decay_logit_bwd, ret_gn_g, mem_norm_g, w_mem_kv, xa_q_norm_g, xa_k_norm_g, w_br_na, w_br_ret, w_br_mem, w_out, norm_ffn_g, w_ff1, w_ff2):
    batch, s_len, d = x.shape
    n_mem = mem.shape[1]
    t = batch * s_len
    tiles = _tiles(t)
    cos2, sin2 = _rope_tables(s_len)
    x2d = x.reshape(t, d)
    mem2d = mem.reshape(batch * n_mem, d)
    gm, gn = t // tiles.tm, w_in.shape[2] // tiles.tn_in

    def side_job(w):
        r, c = w.shape
        if r % (gm * gn * 16) == 0:
            return w, (r // (gm * gn), c), lambda i, j: (i * gn + j, 0)
        assert r % (gn * 16) == 0 and c % (gm * LANE) == 0, (w.shape, gm, gn)
        return w, (r // gn, c // gm), lambda i, j: (j, i)

    for l in range(w_in.shape[0]):
        sides = [side_job(w) for w in (w_br_na[l], w_br_ret[l], w_br_mem[l])]
        proj, (w_na_bf, w_ret_bf, w_mem_bf) = _norm_proj(
            x2d, norm_mix_g[l], w_in[l], tiles.tm, tiles.tn_in, side=sides)
        mkv, _ = _norm_proj(mem2d, mem_norm_g[l], w_mem_kv[l], batch * n_mem, tiles.tn_mem)

        o_mem = _mem_xattn(proj, mkv, xa_q_norm_g[l], xa_k_norm_g[l], batch, s_len, n_mem)
        o_na = _na_attention(proj, na_rpb[l], na_q_norm_g[l], na_k_norm_g[l], batch, s_len)
        o_ret = _retention(proj, cos2, sin2, ret_decay_logit_fwd[l], ret_decay_logit_bwd[l], ret_gn_g[l],
                           batch, s_len)

        merged, (w_ff1_bf, w_ff2_bf) = _merge(o_na, o_ret, o_mem, w_na_bf, w_ret_bf, w_mem_bf, proj,
                                              tiles.tm_merge, tiles.tn_merge, side=(w_ff1[l], w_ff2[l]))
        x1 = _out_proj(merged, w_out[l], x2d, tiles.tm_out)
        x2d = _ffn(x1, norm_ffn_g[l], w_ff1_bf, w_ff2_bf, tiles.tm_out, tiles.tf)
    return x2d.reshape(batch, s_len, d)
```

```python
import functools
from typing import NamedTuple

import jax
import jax.numpy as jnp
import numpy as np
from jax import lax
from jax.experimental import pallas as pl
from jax.experimental.pallas import tpu as pltpu

F32 = jnp.float32
BF16 = jnp.bfloat16

LANE = 128
BF16_SUBLANES = 16
EPS = 1e-6
NEG = -1e30
LOG2E = 1.4426950408889634

GRID_W = 64
NA_HEADS = 8
NA_HEAD_DIM = 128
NA_MAX_ROWS = 8
NA_COLS = 16
NA_DR = 2 * NA_MAX_ROWS - 1
NA_DC = 2 * NA_COLS - 1
NA_TBL = 3 * NA_DR + 1
NA_QROWS = 4
NA_WIN = NA_QROWS + NA_MAX_ROWS

RET_HEADS = 8
RET_QK_DIM = 128
RET_V_DIM = 256
RET_BLOCK = 256
ROPE_BASE = 10000.0

XA_HEADS = 4
XA_HEAD_DIM = 256
XA_QROWS = 512

SLAB_NA_Q, SLAB_NA_K, SLAB_NA_V = 0, 8, 16
SLAB_RQ, SLAB_RK, SLAB_RV, SLAB_RG = 24, 32, 40, 56
SLAB_XQ = 72
SLAB_G_NA, SLAB_G_RET, SLAB_G_MEM = 80, 96, 112

V7X_VMEM_BYTES = 64 * 2**20
VMEM_LIMIT_BYTES = V7X_VMEM_BYTES - 6 * 2**20


def _params(*sem):
    return pltpu.CompilerParams(dimension_semantics=sem, vmem_limit_bytes=VMEM_LIMIT_BYTES)


def _rms(x, g):
    return x * lax.rsqrt(jnp.mean(x * x, axis=-1, keepdims=True) + EPS) * g


def _unit_rms(x):
    return x * lax.rsqrt(jnp.mean(x * x, axis=-1, keepdims=True) + EPS)


def _sigmoid(x):
    return 0.5 * jnp.tanh(0.5 * x) + 0.5


def _norm_proj_kernel(n_side, x_ref, g_ref, w_ref, *refs):
    side_in, o_ref = refs[:n_side], refs[n_side]
    side_out, hn_ref = refs[n_side + 1:2 * n_side + 1], refs[2 * n_side + 1]

    @pl.when(pl.program_id(1) == 0)
    def _():
        hn_ref[...] = _rms(x_ref[...], g_ref[...]).astype(BF16)

    acc = jnp.dot(hn_ref[...], w_ref[...].astype(BF16), preferred_element_type=F32)
    for s in range(o_ref.shape[0]):
        o_ref[s] = acc[:, s * LANE:(s + 1) * LANE].astype(BF16)
    for src, dst in zip(side_in, side_out):
        dst[...] = src[...].astype(BF16)


def _norm_proj(x2d, g, w, tm, tn, side=()):
    m, k = x2d.shape
    n = w.shape[1]
    side_specs = [pl.BlockSpec(blk, imap) for _, blk, imap in side]
    out = pl.pallas_call(
        functools.partial(_norm_proj_kernel, len(side)),
        grid=(m // tm, n // tn),
        in_specs=[
            pl.BlockSpec((tm, k), lambda i, j: (i, 0)),
            pl.BlockSpec((1, k), lambda i, j: (0, 0)),
            pl.BlockSpec((k, tn), lambda i, j: (0, j)),
        ] + side_specs,
        out_specs=[pl.BlockSpec((tn // LANE, tm, LANE), lambda i, j: (j, i, 0))] + side_specs,
        out_shape=[jax.ShapeDtypeStruct((n // LANE, m, LANE), BF16)]
        + [jax.ShapeDtypeStruct(a.shape, BF16) for a, _, _ in side],
        scratch_shapes=[pltpu.VMEM((tm, k), BF16)],
        compiler_params=_params("parallel", "arbitrary"),
        name="norm_proj",
    )(x2d, g.reshape(1, k), w, *[a for a, _, _ in side])
    return out[0], out[1:]


def _na_build_tables(h, rpb_ref, t_ref, base_ref):
    qc = lax.broadcasted_iota(jnp.int32, (GRID_W, LANE), 0)
    lane = lax.broadcasted_iota(jnp.int32, (GRID_W, LANE), 1)
    kc = lane & (GRID_W - 1)
    d = jnp.clip(kc - qc, -(NA_COLS - 1), NA_COLS - 1) + (NA_COLS - 1)
    cs = jnp.clip(qc - NA_COLS // 2, 0, GRID_W - NA_COLS)
    col_ok = (kc >= cs) & (kc < cs + NA_COLS)
    left = lane < GRID_W
    neg = jnp.full((GRID_W, LANE), NEG, F32)

    def body(dr, _):
        base = (h * NA_DR + dr) * NA_DC
        val = jnp.zeros((GRID_W, LANE), F32)
        for dd in range(NA_DC):
            val = jnp.where(d == dd, rpb_ref[base + dd], val)
        base_ref[dr] = jnp.where(col_ok, val * LOG2E, NEG)
        return 0

    lax.fori_loop(0, NA_DR, body, 0)
    for dr in range(NA_DR):
        second = base_ref[dr + 1] if dr + 1 < NA_DR else neg
        t_ref[dr] = jnp.where(left, base_ref[dr], second)
        t_ref[NA_DR + dr] = jnp.where(left, base_ref[dr], neg)
        t_ref[2 * NA_DR + dr] = jnp.where(left, neg, base_ref[dr])
    t_ref[3 * NA_DR] = neg


def _na_kernel(rpb_ref, q_ref, k_ref, v_ref, gq_ref, gk_ref, o_ref, kt_ref, va_ref, t_ref, base_ref):
    s_len = q_ref.shape[1]

    @pl.when(pl.program_id(1) == 0)
    def _():
        _na_build_tables(pl.program_id(0), rpb_ref, t_ref, base_ref)
        va_ref[:, NA_HEAD_DIM:] = jnp.ones((s_len, NA_HEAD_DIM), BF16)

    rows = s_len // GRID_W
    step_tok = NA_QROWS * GRID_W
    win_tok = NA_WIN * GRID_W
    n_tiles = NA_WIN // 2
    gq = gq_ref[...] * gk_ref[...] * (NA_HEAD_DIM ** -0.5 * LOG2E)
    va_ref[:, :NA_HEAD_DIM] = v_ref[0]

    def knorm(c, _):
        sl = pl.ds(pl.multiple_of(c * LANE, LANE), LANE)
        kt_ref[c] = _unit_rms(k_ref[0, sl, :].astype(F32)).astype(BF16).T
        return 0

    lax.fori_loop(0, s_len // LANE, knorm, 0, unroll=True)

    def step(i, _):
        r0 = NA_QROWS * i
        ws = jnp.clip(r0 - NA_MAX_ROWS // 2, 0, rows - NA_WIN)
        wp = ws // 2
        qsl = pl.ds(pl.multiple_of(i * step_tok, step_tok), step_tok)
        wsl = pl.ds(pl.multiple_of(ws * GRID_W, LANE), win_tok)
        qn = _rms(q_ref[0, qsl, :].astype(F32), gq).astype(BF16)
        kwin = jnp.concatenate([kt_ref[wp + t] for t in range(n_tiles)], axis=1)
        s = jnp.dot(qn, kwin, preferred_element_type=F32)
        bias_rows = []
        for qr in range(NA_QROWS):
            r = r0 + qr
            rs = jnp.clip(r - NA_MAX_ROWS // 2, 0, rows - NA_MAX_ROWS)
            tiles = []
            for t in range(n_tiles):
                ka = ws + 2 * t
                dr = ka - r + (NA_MAX_ROWS - 1)
                va = (ka >= rs) & (ka < rs + NA_MAX_ROWS)
                vb = (ka + 1 >= rs) & (ka + 1 < rs + NA_MAX_ROWS)
                idx = jnp.where(va, jnp.where(vb, dr, NA_DR + dr),
                                jnp.where(vb, 2 * NA_DR + dr + 1, 3 * NA_DR))
                tiles.append(t_ref[idx])
            bias_rows.append(jnp.concatenate(tiles, axis=1))
        s = s + jnp.concatenate(bias_rows, axis=0)
        e = jnp.exp2(s - jnp.max(s, axis=-1, keepdims=True))
        o = jnp.dot(e.astype(BF16), va_ref[wsl, :], preferred_element_type=F32)
        o_ref[qsl, :] = (o[:, :NA_HEAD_DIM] / o[:, NA_HEAD_DIM:]).astype(BF16)
        return 0

    lax.fori_loop(0, rows // NA_QROWS, step, 0, unroll=True)


def _na_attention(proj, rpb, gq, gk, batch, s_len):
    t = batch * s_len
    g_spec = pl.BlockSpec((1, NA_HEAD_DIM), lambda h, b: (0, 0))
    return pl.pallas_call(
        _na_kernel,
        grid=(NA_HEADS, batch),
        in_specs=[
            pl.BlockSpec(memory_space=pltpu.SMEM),
            pl.BlockSpec((1, s_len, LANE), lambda h, b: (SLAB_NA_Q + h, b, 0)),
            pl.BlockSpec((1, s_len, LANE), lambda h, b: (SLAB_NA_K + h, b, 0)),
            pl.BlockSpec((1, s_len, LANE), lambda h, b: (SLAB_NA_V + h, b, 0)),
            g_spec, g_spec,
        ],
        out_specs=pl.BlockSpec((s_len, NA_HEAD_DIM), lambda h, b: (b, h)),
        out_shape=jax.ShapeDtypeStruct((t, NA_HEADS * NA_HEAD_DIM), BF16),
        scratch_shapes=[pltpu.VMEM((s_len // LANE, NA_HEAD_DIM, LANE), BF16),
                        pltpu.VMEM((s_len, 2 * NA_HEAD_DIM), BF16),
                        pltpu.VMEM((NA_TBL, GRID_W, LANE), F32),
                        pltpu.VMEM((NA_DR, GRID_W, LANE), F32)],
        compiler_params=_params("arbitrary", "arbitrary"),
        name="na_attention",
    )(rpb.reshape(-1), proj, proj, proj, gq.reshape(1, -1), gk.reshape(1, -1))


def _log_sigmoid(x):
    return -(jnp.maximum(-x, 0.0) + jnp.log1p(jnp.exp(-jnp.abs(x))))


def _ret_kernel(lf_ref, lb_ref, q_ref, k_ref, v_ref, g_ref, cos_ref, sin_ref, gn_ref,
                o_ref, qr_ref, kt_ref, d_ref, qdf_ref, qdb_ref, kv_ref, s_ref):
    s_len = q_ref.shape[1]
    c = RET_BLOCK
    nb = s_len // c
    dk = RET_QK_DIM
    half = dk // 2

    h = pl.program_id(1)
    lgf = _log_sigmoid(jnp.full((1, 1), lf_ref[h], F32))
    lgb = _log_sigmoid(jnp.full((1, 1), lb_ref[h], F32))
    ic = lax.broadcasted_iota(jnp.int32, (c, 1), 0).astype(F32)
    jr = lax.broadcasted_iota(jnp.int32, (1, c), 1).astype(F32)
    diff = ic - jr
    scale = dk ** -0.5
    d_ref[...] = jnp.exp(jnp.where(diff >= 0, lgf, lgb) * jnp.abs(diff)) * scale
    qdf_ref[...] = jnp.broadcast_to(jnp.exp(lgf * (ic + 1.0)), (c, dk))
    qdb_ref[...] = jnp.broadcast_to(jnp.exp(lgb * (c - ic)), (c, dk))
    kdf = jnp.exp(lgf * (c - 1.0 - jr)) * scale
    kdb = jnp.exp(lgb * jr) * scale
    cd_f = jnp.exp(lgf * c)
    cd_b = jnp.exp(lgb * c)

    def block_v(sl):
        return jnp.concatenate([v_ref[0, sl, :], v_ref[1, sl, :]], axis=1)

    def prep(n, _):
        sl = pl.ds(pl.multiple_of(n * c, c), c)
        cos = cos_ref[sl, :]
        sin = sin_ref[sl, :]
        q = q_ref[0, sl, :].astype(F32)
        k = k_ref[0, sl, :].astype(F32)
        qr_ref[sl, :] = (q * cos + pltpu.roll(q, half, 1) * sin).astype(BF16)
        kt = (k * cos + pltpu.roll(k, half, 1) * sin).T
        kt_ref[n] = kt.astype(BF16)
        lhs = jnp.concatenate([(kt * kdf).astype(BF16), (kt * kdb).astype(BF16)], axis=0)
        kv_ref[n] = jnp.dot(lhs, block_v(sl), preferred_element_type=F32)
        return 0

    lax.fori_loop(0, nb, prep, 0, unroll=True)

    def scan_f(n, sf):
        s_ref[n, :dk, :] = sf.astype(BF16)
        return cd_f * sf + kv_ref[n, :dk, :]

    def scan_b(t, sb):
        n = nb - 1 - t
        s_ref[n, dk:, :] = sb.astype(BF16)
        return cd_b * sb + kv_ref[n, dk:, :]

    zero = jnp.zeros((dk, RET_V_DIM), F32)

    def scan(n, carry):
        return scan_f(n, carry[0]), scan_b(n, carry[1])

    lax.fori_loop(0, nb, scan, (zero, zero), unroll=True)

    gn = gn_ref[0]

    def out(n, _):
        sl = pl.ds(pl.multiple_of(n * c, c), c)
        q = qr_ref[sl, :]
        qf32 = q.astype(F32)
        a = jnp.dot(q, kt_ref[n], preferred_element_type=F32) * d_ref[...]
        lhs = jnp.concatenate([a.astype(BF16), (qf32 * qdf_ref[...]).astype(BF16),
                               (qf32 * qdb_ref[...]).astype(BF16)], axis=1)
        rhs = jnp.concatenate([block_v(sl), s_ref[n]], axis=0)
        o = jnp.dot(lhs, rhs, preferred_element_type=F32)
        mu = jnp.mean(o, axis=-1, keepdims=True)
        oc = o - mu
        y = oc * lax.rsqrt(jnp.mean(oc * oc, axis=-1, keepdims=True) + EPS) * gn
        gate = jnp.concatenate([g_ref[0, sl, :], g_ref[1, sl, :]], axis=1).astype(F32)
        o_ref[sl, :] = (y * gate * _sigmoid(gate)).astype(BF16)
        return 0

    lax.fori_loop(0, nb, out, 0, unroll=True)


def _retention(proj, cos2, sin2, lf, lb, gn, batch, s_len):
    t = batch * s_len
    nb = s_len // RET_BLOCK
    smem = pl.BlockSpec(memory_space=pltpu.SMEM)
    return pl.pallas_call(
        _ret_kernel,
        grid=(batch, RET_HEADS),
        in_specs=[
            smem, smem,
            pl.BlockSpec((1, s_len, LANE), lambda b, h: (SLAB_RQ + h, b, 0)),
            pl.BlockSpec((1, s_len, LANE), lambda b, h: (SLAB_RK + h, b, 0)),
            pl.BlockSpec((2, s_len, LANE), lambda b, h: (SLAB_RV // 2 + h, b, 0)),
            pl.BlockSpec((2, s_len, LANE), lambda b, h: (SLAB_RG // 2 + h, b, 0)),
            pl.BlockSpec((s_len, RET_QK_DIM), lambda b, h: (0, 0)),
            pl.BlockSpec((s_len, RET_QK_DIM), lambda b, h: (0, 0)),
            pl.BlockSpec((1, 1, RET_V_DIM), lambda b, h: (h, 0, 0)),
        ],
        out_specs=pl.BlockSpec((s_len, RET_V_DIM), lambda b, h: (b, h)),
        out_shape=jax.ShapeDtypeStruct((t, RET_HEADS * RET_V_DIM), BF16),
        scratch_shapes=[
            pltpu.VMEM((s_len, RET_QK_DIM), BF16),
            pltpu.VMEM((nb, RET_QK_DIM, RET_BLOCK), BF16),
            pltpu.VMEM((RET_BLOCK, RET_BLOCK), F32),
            pltpu.VMEM((RET_BLOCK, RET_QK_DIM), F32),
            pltpu.VMEM((RET_BLOCK, RET_QK_DIM), F32),
            pltpu.VMEM((nb, 2 * RET_QK_DIM, RET_V_DIM), F32),
            pltpu.VMEM((nb, 2 * RET_QK_DIM, RET_V_DIM), BF16),
        ],
        compiler_params=_params("parallel", "parallel"),
        name="retention",
    )(lf, lb, proj, proj, proj, proj, cos2, sin2, gn.reshape(RET_HEADS, 1, RET_V_DIM))


def _xa_kernel(q_ref, k_ref, v_ref, gq_ref, gk_ref, o_ref):
    s_len = q_ref.shape[1]
    tq = XA_QROWS
    gq = gq_ref[...] * gk_ref[...] * (XA_HEAD_DIM ** -0.5 * LOG2E)
    k = jnp.concatenate([k_ref[0], k_ref[1]], axis=1).astype(F32)
    kn = _unit_rms(k).astype(BF16)
    v = jnp.concatenate([v_ref[0], v_ref[1]], axis=1)

    def body(i, _):
        sl = pl.ds(pl.multiple_of(i * tq, tq), tq)
        q = jnp.concatenate([q_ref[0, sl, :], q_ref[1, sl, :]], axis=1).astype(F32)
        qn = _rms(q, gq).astype(BF16)
        s = lax.dot_general(qn, kn, (((1,), (1,)), ((), ())), preferred_element_type=F32)
        e = jnp.exp2(s - jnp.max(s, axis=-1, keepdims=True))
        l = jnp.sum(e, axis=-1, keepdims=True)
        o = jnp.dot(e.astype(BF16), v, preferred_element_type=F32)
        o_ref[sl, :] = (o / l).astype(BF16)
        return 0

    lax.fori_loop(0, s_len // tq, body, 0, unroll=True)


def _mem_xattn(proj, mkv, gq, gk, batch, s_len, n_mem):
    t = batch * s_len
    g_spec = pl.BlockSpec((1, XA_HEAD_DIM), lambda b, h: (0, 0))
    return pl.pallas_call(
        _xa_kernel,
        grid=(batch, XA_HEADS),
        in_specs=[
            pl.BlockSpec((2, s_len, LANE), lambda b, h: (SLAB_XQ // 2 + h, b, 0)),
            pl.BlockSpec((2, n_mem, LANE), lambda b, h: (h, b, 0)),
            pl.BlockSpec((2, n_mem, LANE), lambda b, h: (XA_HEADS + h, b, 0)),
            g_spec, g_spec,
        ],
        out_specs=pl.BlockSpec((s_len, XA_HEAD_DIM), lambda b, h: (b, h)),
        out_shape=jax.ShapeDtypeStruct((t, XA_HEADS * XA_HEAD_DIM), BF16),
        compiler_params=_params("parallel", "parallel"),
        name="mem_xattn",
    )(proj, mkv, mkv, gq.reshape(1, -1), gk.reshape(1, -1))


def _merge_kernel(n_side, ona_ref, oret_ref, omem_ref, wna_ref, wret_ref, wmem_ref,
                  gna_ref, gret_ref, gmem_ref, *refs):
    side_in, o_ref, side_out = refs[:n_side], refs[n_side], refs[n_side + 1:]
    for src, dst in zip(side_in, side_out):
        dst[...] = src[...].astype(BF16)
    y_na = jnp.dot(ona_ref[...], wna_ref[...], preferred_element_type=F32)
    y_ret = jnp.dot(oret_ref[...], wret_ref[...], preferred_element_type=F32)
    y_mem = jnp.dot(omem_ref[...], wmem_ref[...], preferred_element_type=F32)
    for j in range(gna_ref.shape[0]):
        cs = slice(j * LANE, (j + 1) * LANE)
        o_ref[:, cs] = (_sigmoid(gna_ref[j].astype(F32)) * y_na[:, cs]
                        + _sigmoid(gret_ref[j].astype(F32)) * y_ret[:, cs]
                        + _sigmoid(gmem_ref[j].astype(F32)) * y_mem[:, cs]).astype(BF16)


def _merge(o_na, o_ret, o_mem, w_na, w_ret, w_mem, proj, tm, tn, side=()):
    t = o_na.shape[0]
    d = w_na.shape[1]
    ns = tn // LANE
    steps = (t // tm) * (d // tn)
    assert all(w.shape[0] % (steps * BF16_SUBLANES) == 0 for w in side)
    side_specs = [pl.BlockSpec((w.shape[0] // steps, w.shape[1]), lambda i, j: (i * (d // tn) + j, 0))
                  for w in side]

    def lhs_spec(a):
        return pl.BlockSpec((tm, a.shape[1]), lambda i, j: (i, 0))

    def w_spec(w):
        mode = dict(pipeline_mode=pl.Buffered(1)) if tn == d else {}
        return pl.BlockSpec((w.shape[0], tn), lambda i, j: (0, j), **mode)

    def gate_spec(first):
        return pl.BlockSpec((ns, tm, LANE), lambda i, j: (first // ns + j, i, 0))

    out = pl.pallas_call(
        functools.partial(_merge_kernel, len(side)),
        grid=(t // tm, d // tn),
        in_specs=[lhs_spec(o_na), lhs_spec(o_ret), lhs_spec(o_mem),
                  w_spec(w_na), w_spec(w_ret), w_spec(w_mem),
                  gate_spec(SLAB_G_NA), gate_spec(SLAB_G_RET), gate_spec(SLAB_G_MEM)] + side_specs,
        out_specs=[pl.BlockSpec((tm, tn), lambda i, j: (i, j))] + side_specs,
        out_shape=[jax.ShapeDtypeStruct((t, d), BF16)] + [jax.ShapeDtypeStruct(w.shape, BF16) for w in side],
        compiler_params=_params("parallel", "arbitrary"),
        name="merge",
    )(o_na, o_ret, o_mem, w_na, w_ret, w_mem, proj, proj, proj, *side)
    return out[0], out[1:]


def _out_proj_kernel(m_ref, w_ref, x_ref, o_ref, wb_ref):
    @pl.when(pl.program_id(0) == 0)
    def _():
        wb_ref[...] = w_ref[...].astype(BF16)

    o_ref[...] = x_ref[...] + jnp.dot(m_ref[...], wb_ref[...], preferred_element_type=F32)


def _out_proj(merged, w, x2d, tm):
    t, k = merged.shape
    d = w.shape[1]
    return pl.pallas_call(
        _out_proj_kernel,
        grid=(t // tm,),
        in_specs=[pl.BlockSpec((tm, k), lambda i: (i, 0)),
                  pl.BlockSpec((k, d), lambda i: (0, 0), pipeline_mode=pl.Buffered(1)),
                  pl.BlockSpec((tm, d), lambda i: (i, 0))],
        out_specs=pl.BlockSpec((tm, d), lambda i: (i, 0)),
        out_shape=jax.ShapeDtypeStruct((t, d), F32),
        scratch_shapes=[pltpu.VMEM((k, d), BF16)],
        compiler_params=_params("arbitrary"),
        name="out_proj",
    )(merged, w, x2d)


def _ffn_kernel(x_ref, g_ref, w1_ref, w2_ref, o_ref, h_ref):
    def mlp(h):
        a = jnp.maximum(jnp.dot(h, w1_ref[...], preferred_element_type=F32), 0.0)
        return jnp.dot((a * a).astype(BF16), w2_ref[...], preferred_element_type=F32)

    @pl.when(pl.program_id(1) == 0)
    def _():
        x = x_ref[...]
        h = _rms(x, g_ref[...]).astype(BF16)
        h_ref[...] = h
        o_ref[...] = x + mlp(h)

    @pl.when(pl.program_id(1) != 0)
    def _():
        o_ref[...] += mlp(h_ref[...])


def _ffn(x1, g, w1, w2, tm, tf):
    t, d = x1.shape
    dff = w1.shape[1]
    return pl.pallas_call(
        _ffn_kernel,
        grid=(t // tm, dff // tf),
        in_specs=[pl.BlockSpec((tm, d), lambda i, f: (i, 0)),
                  pl.BlockSpec((1, d), lambda i, f: (0, 0)),
                  pl.BlockSpec((d, tf), lambda i, f: (0, f)),
                  pl.BlockSpec((tf, d), lambda i, f: (f, 0))],
        out_specs=pl.BlockSpec((tm, d), lambda i, f: (i, 0)),
        out_shape=jax.ShapeDtypeStruct((t, d), F32),
        scratch_shapes=[pltpu.VMEM((tm, d), BF16)],
        compiler_params=_params("parallel", "arbitrary"),
        name="ffn",
    )(x1, g.reshape(1, d), w1, w2)


def _rope_tables(s_len):
    half = RET_QK_DIM // 2
    inv = np.power(np.float64(ROPE_BASE), -np.arange(half, dtype=np.float64) / half)
    ang = np.arange(s_len, dtype=np.float64)[:, None] * inv[None, :]
    cos, sin = np.cos(ang), np.sin(ang)
    return (jnp.asarray(np.concatenate([cos, cos], axis=1), F32),
            jnp.asarray(np.concatenate([-sin, sin], axis=1), F32))


class _Tiles(NamedTuple):
    tm: int
    tn_in: int
    tm_merge: int
    tn_merge: int
    tm_out: int
    tf: int
    tn_mem: int


def _tiles(t):
    return _Tiles(tm=min(1024, t), tn_in=1024, tm_merge=min(256, t), tn_merge=2048, tm_out=min(512, t), tf=1024, tn_mem=512)


def kernel(x, mem, norm_mix_g, w_in, na_q_norm_g, na_k_norm_g, na_rpb, ret_decay_logit_fwd, ret_decay_logit_bwd, ret_gn_g, mem_norm_g, w_mem_kv, xa_q_norm_g, xa_k_norm_g, w_br_na, w_br_ret, w_br_mem, w_out, norm_ffn_g, w_ff1, w_ff2):
    batch, s_len, d = x.shape
    n_mem = mem.shape[1]
    t = batch * s_len
    tiles = _tiles(t)
    cos2, sin2 = _rope_tables(s_len)
    x2d = x.reshape(t, d)
    mem2d = mem.reshape(batch * n_mem, d)
    gm, gn = t // tiles.tm, w_in.shape[2] // tiles.tn_in

    def side_job(w):
        r, c = w.shape
        if r % (gm * gn * BF16_SUBLANES) == 0:
            return w, (r // (gm * gn), c), lambda i, j: (i * gn + j, 0)
        assert r % (gn * BF16_SUBLANES) == 0 and c % (gm * LANE) == 0, (w.shape, gm, gn)
        return w, (r // gn, c // gm), lambda i, j: (j, i)

    for l in range(w_in.shape[0]):
        sides = [side_job(w) for w in (w_br_na[l], w_br_ret[l], w_br_mem[l])]
        proj, (w_na_bf, w_ret_bf, w_mem_bf) = _norm_proj(
            x2d, norm_mix_g[l], w_in[l], tiles.tm, tiles.tn_in, side=sides)
        mkv, _ = _norm_proj(mem2d, mem_norm_g[l], w_mem_kv[l], batch * n_mem, tiles.tn_mem)

        o_mem = _mem_xattn(proj, mkv, xa_q_norm_g[l], xa_k_norm_g[l], batch, s_len, n_mem)
        o_na = _na_attention(proj, na_rpb[l], na_q_norm_g[l], na_k_norm_g[l], batch, s_len)
        o_ret = _retention(proj, cos2, sin2, ret_decay_logit_fwd[l], ret_decay_logit_bwd[l], ret_gn_g[l],
                           batch, s_len)

        merged, (w_ff1_bf, w_ff2_bf) = _merge(o_na, o_ret, o_mem, w_na_bf, w_ret_bf, w_mem_bf, proj,
                                              tiles.tm_merge, tiles.tn_merge, side=(w_ff1[l], w_ff2[l]))
        x1 = _out_proj(merged, w_out[l], x2d, tiles.tm_out)
        x2d = _ffn(x1, norm_ffn_g[l], w_ff1_bf, w_ff2_bf, tiles.tm_out, tiles.tf)
    return x2d.reshape(batch, s_len, d)
```

```python
import functools
from typing import NamedTuple

import jax
import jax.numpy as jnp
import numpy as np
from jax import lax
from jax.experimental import pallas as pl
from jax.experimental.pallas import tpu as pltpu

F32 = jnp.float32
BF16 = jnp.bfloat16

LANE = 128
BF16_SUBLANES = 16
EPS = 1e-6
NEG = -1e30
LOG2E = 1.4426950408889634

GRID_W = 64
NA_HEADS = 8
NA_HEAD_DIM = 128
NA_MAX_ROWS = 8
NA_COLS = 16
NA_DR = 2 * NA_MAX_ROWS - 1
NA_DC = 2 * NA_COLS - 1
NA_TBL = 3 * NA_DR + 1
NA_QROWS = 4
NA_WIN = NA_QROWS + NA_MAX_ROWS

RET_HEADS = 8
RET_QK_DIM = 128
RET_V_DIM = 256
RET_BLOCK = 256
ROPE_BASE = 10000.0

XA_HEADS = 4
XA_HEAD_DIM = 256
XA_QROWS = 512

SLAB_NA_Q, SLAB_NA_K, SLAB_NA_V = 0, 8, 16
SLAB_RQ, SLAB_RK, SLAB_RV, SLAB_RG = 24, 32, 40, 56
SLAB_XQ = 72
SLAB_G_NA, SLAB_G_RET, SLAB_G_MEM = 80, 96, 112

V7X_VMEM_BYTES = 64 * 2**20
VMEM_LIMIT_BYTES = V7X_VMEM_BYTES - 6 * 2**20


def _params(*sem):
    return pltpu.CompilerParams(dimension_semantics=sem, vmem_limit_bytes=VMEM_LIMIT_BYTES)


def _rms(x, g):
    return x * lax.rsqrt(jnp.mean(x * x, axis=-1, keepdims=True) + EPS) * g


def _unit_rms(x):
    return x * lax.rsqrt(jnp.mean(x * x, axis=-1, keepdims=True) + EPS)


def _sigmoid(x):
    return 0.5 * jnp.tanh(0.5 * x) + 0.5


def _norm_proj_kernel(n_side, x_ref, g_ref, w_ref, *refs):
    side_in, o_ref = refs[:n_side], refs[n_side]
    side_out, hn_ref = refs[n_side + 1:2 * n_side + 1], refs[2 * n_side + 1]

    @pl.when(pl.program_id(1) == 0)
    def _():
        hn_ref[...] = _rms(x_ref[...], g_ref[...]).astype(BF16)

    acc = jnp.dot(hn_ref[...], w_ref[...].astype(BF16), preferred_element_type=F32)
    for s in range(o_ref.shape[0]):
        o_ref[s] = acc[:, s * LANE:(s + 1) * LANE].astype(BF16)
    for src, dst in zip(side_in, side_out):
        dst[...] = src[...].astype(BF16)


def _norm_proj(x2d, g, w, tm, tn, side=()):
    m, k = x2d.shape
    n = w.shape[1]
    side_specs = [pl.BlockSpec(blk, imap) for _, blk, imap in side]
    out = pl.pallas_call(
        functools.partial(_norm_proj_kernel, len(side)),
        grid=(m // tm, n // tn),
        in_specs=[
            pl.BlockSpec((tm, k), lambda i, j: (i, 0)),
            pl.BlockSpec((1, k), lambda i, j: (0, 0)),
            pl.BlockSpec((k, tn), lambda i, j: (0, j)),
        ] + side_specs,
        out_specs=[pl.BlockSpec((tn // LANE, tm, LANE), lambda i, j: (j, i, 0))] + side_specs,
        out_shape=[jax.ShapeDtypeStruct((n // LANE, m, LANE), BF16)]
        + [jax.ShapeDtypeStruct(a.shape, BF16) for a, _, _ in side],
        scratch_shapes=[pltpu.VMEM((tm, k), BF16)],
        compiler_params=_params("parallel", "arbitrary"),
        name="norm_proj",
    )(x2d, g.reshape(1, k), w, *[a for a, _, _ in side])
    return out[0], out[1:]


def _na_build_tables(h, rpb_ref, t_ref, base_ref):
    qc = lax.broadcasted_iota(jnp.int32, (GRID_W, LANE), 0)
    lane = lax.broadcasted_iota(jnp.int32, (GRID_W, LANE), 1)
    kc = lane & (GRID_W - 1)
    d = jnp.clip(kc - qc, -(NA_COLS - 1), NA_COLS - 1) + (NA_COLS - 1)
    cs = jnp.clip(qc - NA_COLS // 2, 0, GRID_W - NA_COLS)
    col_ok = (kc >= cs) & (kc < cs + NA_COLS)
    left = lane < GRID_W
    neg = jnp.full((GRID_W, LANE), NEG, F32)

    def body(dr, _):
        base = (h * NA_DR + dr) * NA_DC
        val = jnp.zeros((GRID_W, LANE), F32)
        for dd in range(NA_DC):
            val = jnp.where(d == dd, rpb_ref[base + dd], val)
        base_ref[dr] = jnp.where(col_ok, val * LOG2E, NEG)
        return 0

    lax.fori_loop(0, NA_DR, body, 0)
    for dr in range(NA_DR):
        second = base_ref[dr + 1] if dr + 1 < NA_DR else neg
        t_ref[dr] = jnp.where(left, base_ref[dr], second)
        t_ref[NA_DR + dr] = jnp.where(left, base_ref[dr], neg)
        t_ref[2 * NA_DR + dr] = jnp.where(left, neg, base_ref[dr])
    t_ref[3 * NA_DR] = neg


def _na_kernel(rpb_ref, q_ref, k_ref, v_ref, gq_ref, gk_ref, o_ref, kt_ref, va_ref, t_ref, base_ref):
    s_len = q_ref.shape[1]

    @pl.when(pl.program_id(1) == 0)
    def _():
        _na_build_tables(pl.program_id(0), rpb_ref, t_ref, base_ref)
        va_ref[:, NA_HEAD_DIM:] = jnp.ones((s_len, NA_HEAD_DIM), BF16)

    rows = s_len // GRID_W
    step_tok = NA_QROWS * GRID_W
    win_tok = NA_WIN * GRID_W
    n_tiles = NA_WIN // 2
    gq = gq_ref[...] * gk_ref[...] * (NA_HEAD_DIM ** -0.5 * LOG2E)
    va_ref[:, :NA_HEAD_DIM] = v_ref[0]

    def knorm(c, _):
        sl = pl.ds(pl.multiple_of(c * LANE, LANE), LANE)
        kt_ref[c] = _unit_rms(k_ref[0, sl, :].astype(F32)).astype(BF16).T
        return 0

    lax.fori_loop(0, s_len // LANE, knorm, 0, unroll=True)

    def step(i, _):
        r0 = NA_QROWS * i
        ws = jnp.clip(r0 - NA_MAX_ROWS // 2, 0, rows - NA_WIN)
        wp = ws // 2
        qsl = pl.ds(pl.multiple_of(i * step_tok, step_tok), step_tok)
        wsl = pl.ds(pl.multiple_of(ws * GRID_W, LANE), win_tok)
        qn = _rms(q_ref[0, qsl, :].astype(F32), gq).astype(BF16)
        kwin = jnp.concatenate([kt_ref[wp + t] for t in range(n_tiles)], axis=1)
        s = jnp.dot(qn, kwin, preferred_element_type=F32)
        bias_rows = []
        for qr in range(NA_QROWS):
            r = r0 + qr
            rs = jnp.clip(r - NA_MAX_ROWS // 2, 0, rows - NA_MAX_ROWS)
            tiles = []
            for t in range(n_tiles):
                ka = ws + 2 * t
                dr = ka - r + (NA_MAX_ROWS - 1)
                va = (ka >= rs) & (ka < rs + NA_MAX_ROWS)
                vb = (ka + 1 >= rs) & (ka + 1 < rs + NA_MAX_ROWS)
                idx = jnp.where(va, jnp.where(vb, dr, NA_DR + dr),
                                jnp.where(vb, 2 * NA_DR + dr + 1, 3 * NA_DR))
                tiles.append(t_ref[idx])
            bias_rows.append(jnp.concatenate(tiles, axis=1))
        s = s + jnp.concatenate(bias_rows, axis=0)
        e = jnp.exp2(s - jnp.max(s, axis=-1, keepdims=True))
        o = jnp.dot(e.astype(BF16), va_ref[wsl, :], preferred_element_type=F32)
        o_ref[qsl, :] = (o[:, :NA_HEAD_DIM] / o[:, NA_HEAD_DIM:]).astype(BF16)
        return 0

    lax.fori_loop(0, rows // NA_QROWS, step, 0, unroll=True)


def _na_attention(proj, rpb, gq, gk, batch, s_len):
    t = batch * s_len
    g_spec = pl.BlockSpec((1, NA_HEAD_DIM), lambda h, b: (0, 0))
    return pl.pallas_call(
        _na_kernel,
        grid=(NA_HEADS, batch),
        in_specs=[
            pl.BlockSpec(memory_space=pltpu.SMEM),
            pl.BlockSpec((1, s_len, LANE), lambda h, b: (SLAB_NA_Q + h, b, 0)),
            pl.BlockSpec((1, s_len, LANE), lambda h, b: (SLAB_NA_K + h, b, 0)),
            pl.BlockSpec((1, s_len, LANE), lambda h, b: (SLAB_NA_V + h, b, 0)),
            g_spec, g_spec,
        ],
        out_specs=pl.BlockSpec((s_len, NA_HEAD_DIM), lambda h, b: (b, h)),
        out_shape=jax.ShapeDtypeStruct((t, NA_HEADS * NA_HEAD_DIM), BF16),
        scratch_shapes=[pltpu.VMEM((s_len // LANE, NA_HEAD_DIM, LANE), BF16),
                        pltpu.VMEM((s_len, 2 * NA_HEAD_DIM), BF16),
                        pltpu.VMEM((NA_TBL, GRID_W, LANE), F32),
                        pltpu.VMEM((NA_DR, GRID_W, LANE), F32)],
        compiler_params=_params("arbitrary", "arbitrary"),
        name="na_attention",
    )(rpb.reshape(-1), proj, proj, proj, gq.reshape(1, -1), gk.reshape(1, -1))


def _log_sigmoid(x):
    return -(jnp.maximum(-x, 0.0) + jnp.log1p(jnp.exp(-jnp.abs(x))))


def _ret_kernel(lf_ref, lb_ref, q_ref, k_ref, v_ref, cos_ref, sin_ref, gn_ref,
                o_ref, qr_ref, kt_ref, d_ref, qdf_ref, qdb_ref, kv_ref, s_ref):
    s_len = q_ref.shape[1]
    c = RET_BLOCK
    nb = s_len // c
    dk = RET_QK_DIM
    half = dk // 2

    h = pl.program_id(1)
    lgf = _log_sigmoid(jnp.full((1, 1), lf_ref[h], F32))
    lgb = _log_sigmoid(jnp.full((1, 1), lb_ref[h], F32))
    ic = lax.broadcasted_iota(jnp.int32, (c, 1), 0).astype(F32)
    jr = lax.broadcasted_iota(jnp.int32, (1, c), 1).astype(F32)
    diff = ic - jr
    scale = dk ** -0.5
    d_ref[...] = jnp.exp(jnp.where(diff >= 0, lgf, lgb) * jnp.abs(diff)) * scale
    qdf_ref[...] = jnp.broadcast_to(jnp.exp(lgf * (ic + 1.0)), (c, dk))
    qdb_ref[...] = jnp.broadcast_to(jnp.exp(lgb * (c - ic)), (c, dk))
    kdf = jnp.exp(lgf * (c - 1.0 - jr)) * scale
    kdb = jnp.exp(lgb * jr) * scale
    cd_f = jnp.exp(lgf * c)
    cd_b = jnp.exp(lgb * c)

    def block_v(sl):
        return jnp.concatenate([v_ref[0, sl, :], v_ref[1, sl, :]], axis=1)

    def prep(n, _):
        sl = pl.ds(pl.multiple_of(n * c, c), c)
        cos = cos_ref[sl, :]
        sin = sin_ref[sl, :]
        q = q_ref[0, sl, :].astype(F32)
        k = k_ref[0, sl, :].astype(F32)
        qr_ref[sl, :] = (q * cos + pltpu.roll(q, half, 1) * sin).astype(BF16)
        kt = (k * cos + pltpu.roll(k, half, 1) * sin).T
        kt_ref[n] = kt.astype(BF16)
        lhs = jnp.concatenate([(kt * kdf).astype(BF16), (kt * kdb).astype(BF16)], axis=0)
        kv_ref[n] = jnp.dot(lhs, block_v(sl), preferred_element_type=F32)
        return 0

    lax.fori_loop(0, nb, prep, 0, unroll=True)

    def scan_f(n, sf):
        s_ref[n, :dk, :] = sf.astype(BF16)
        return cd_f * sf + kv_ref[n, :dk, :]

    def scan_b(t, sb):
        n = nb - 1 - t
        s_ref[n, dk:, :] = sb.astype(BF16)
        return cd_b * sb + kv_ref[n, dk:, :]

    zero = jnp.zeros((dk, RET_V_DIM), F32)

    def scan(n, carry):
        return scan_f(n, carry[0]), scan_b(n, carry[1])

    lax.fori_loop(0, nb, scan, (zero, zero), unroll=True)

    gn = gn_ref[0]

    def out(n, _):
        sl = pl.ds(pl.multiple_of(n * c, c), c)
        q = qr_ref[sl, :]
        qf32 = q.astype(F32)
        a = jnp.dot(q, kt_ref[n], preferred_element_type=F32) * d_ref[...]
        lhs = jnp.concatenate([a.astype(BF16), (qf32 * qdf_ref[...]).astype(BF16),
                               (qf32 * qdb_ref[...]).astype(BF16)], axis=1)
        rhs = jnp.concatenate([block_v(sl), s_ref[n]], axis=0)
        o = jnp.dot(lhs, rhs, preferred_element_type=F32)
        mu = jnp.mean(o, axis=-1, keepdims=True)
        oc = o - mu
        y = oc * lax.rsqrt(jnp.mean(oc * oc, axis=-1, keepdims=True) + EPS) * gn
        o_ref[sl, :] = y.astype(BF16)
        return 0

    lax.fori_loop(0, nb, out, 0, unroll=True)


def _retention(proj, cos2, sin2, lf, lb, gn, batch, s_len):
    t = batch * s_len
    nb = s_len // RET_BLOCK
    smem = pl.BlockSpec(memory_space=pltpu.SMEM)
    return pl.pallas_call(
        _ret_kernel,
        grid=(batch, RET_HEADS),
        in_specs=[
            smem, smem,
            pl.BlockSpec((1, s_len, LANE), lambda b, h: (SLAB_RQ + h, b, 0)),
            pl.BlockSpec((1, s_len, LANE), lambda b, h: (SLAB_RK + h, b, 0)),
            pl.BlockSpec((2, s_len, LANE), lambda b, h: (SLAB_RV // 2 + h, b, 0)),
            pl.BlockSpec((s_len, RET_QK_DIM), lambda b, h: (0, 0)),
            pl.BlockSpec((s_len, RET_QK_DIM), lambda b, h: (0, 0)),
            pl.BlockSpec((1, 1, RET_V_DIM), lambda b, h: (h, 0, 0)),
        ],
        out_specs=pl.BlockSpec((s_len, RET_V_DIM), lambda b, h: (b, h)),
        out_shape=jax.ShapeDtypeStruct((t, RET_HEADS * RET_V_DIM), BF16),
        scratch_shapes=[
            pltpu.VMEM((s_len, RET_QK_DIM), BF16),
            pltpu.VMEM((nb, RET_QK_DIM, RET_BLOCK), BF16),
            pltpu.VMEM((RET_BLOCK, RET_BLOCK), F32),
            pltpu.VMEM((RET_BLOCK, RET_QK_DIM), F32),
            pltpu.VMEM((RET_BLOCK, RET_QK_DIM), F32),
            pltpu.VMEM((nb, 2 * RET_QK_DIM, RET_V_DIM), F32),
            pltpu.VMEM((nb, 2 * RET_QK_DIM, RET_V_DIM), BF16),
        ],
        compiler_params=_params("parallel", "parallel"),
        name="retention",
    )(lf, lb, proj, proj, proj, cos2, sin2, gn.reshape(RET_HEADS, 1, RET_V_DIM))


def _xa_kernel(q_ref, k_ref, v_ref, gq_ref, gk_ref, o_ref):
    s_len = q_ref.shape[1]
    tq = XA_QROWS
    gq = gq_ref[...] * gk_ref[...] * (XA_HEAD_DIM ** -0.5 * LOG2E)
    k = jnp.concatenate([k_ref[0], k_ref[1]], axis=1).astype(F32)
    kn = _unit_rms(k).astype(BF16)
    v = jnp.concatenate([v_ref[0], v_ref[1]], axis=1)

    def body(i, _):
        sl = pl.ds(pl.multiple_of(i * tq, tq), tq)
        q = jnp.concatenate([q_ref[0, sl, :], q_ref[1, sl, :]], axis=1).astype(F32)
        qn = _rms(q, gq).astype(BF16)
        s = lax.dot_general(qn, kn, (((1,), (1,)), ((), ())), preferred_element_type=F32)
        e = jnp.exp2(s - jnp.max(s, axis=-1, keepdims=True))
        l = jnp.sum(e, axis=-1, keepdims=True)
        o = jnp.dot(e.astype(BF16), v, preferred_element_type=F32)
        o_ref[sl, :] = (o / l).astype(BF16)
        return 0

    lax.fori_loop(0, s_len // tq, body, 0, unroll=True)


def _mem_xattn(proj, mkv, gq, gk, batch, s_len, n_mem):
    t = batch * s_len
    g_spec = pl.BlockSpec((1, XA_HEAD_DIM), lambda b, h: (0, 0))
    return pl.pallas_call(
        _xa_kernel,
        grid=(batch, XA_HEADS),
        in_specs=[
            pl.BlockSpec((2, s_len, LANE), lambda b, h: (SLAB_XQ // 2 + h, b, 0)),
            pl.BlockSpec((2, n_mem, LANE), lambda b, h: (h, b, 0)),
            pl.BlockSpec((2, n_mem, LANE), lambda b, h: (XA_HEADS + h, b, 0)),
            g_spec, g_spec,
        ],
        out_specs=pl.BlockSpec((s_len, XA_HEAD_DIM), lambda b, h: (b, h)),
        out_shape=jax.ShapeDtypeStruct((t, XA_HEADS * XA_HEAD_DIM), BF16),
        compiler_params=_params("parallel", "parallel"),
        name="mem_xattn",
    )(proj, mkv, mkv, gq.reshape(1, -1), gk.reshape(1, -1))


def _merge_kernel(n_side, ona_ref, oret_ref, omem_ref, wna_ref, wret_ref, wmem_ref,
                  gna_ref, gret_ref, gmem_ref, rga_ref, rgb_ref, *refs):
    side_in, o_ref, side_out = refs[:n_side], refs[n_side], refs[n_side + 1:]
    for src, dst in zip(side_in, side_out):
        dst[...] = src[...].astype(BF16)
    y_na = jnp.dot(ona_ref[...], wna_ref[...], preferred_element_type=F32)
    rg = jnp.concatenate([r[s] for r in (rga_ref, rgb_ref) for s in range(r.shape[0])], axis=1).astype(F32)
    ret_in = (oret_ref[...].astype(F32) * rg * _sigmoid(rg)).astype(BF16)
    y_ret = jnp.dot(ret_in, wret_ref[...], preferred_element_type=F32)
    y_mem = jnp.dot(omem_ref[...], wmem_ref[...], preferred_element_type=F32)
    for j in range(gna_ref.shape[0]):
        cs = slice(j * LANE, (j + 1) * LANE)
        o_ref[:, cs] = (_sigmoid(gna_ref[j].astype(F32)) * y_na[:, cs]
                        + _sigmoid(gret_ref[j].astype(F32)) * y_ret[:, cs]
                        + _sigmoid(gmem_ref[j].astype(F32)) * y_mem[:, cs]).astype(BF16)


def _merge(o_na, o_ret, o_mem, w_na, w_ret, w_mem, proj, tm, tn, side=()):
    t = o_na.shape[0]
    d = w_na.shape[1]
    ns = tn // LANE
    steps = (t // tm) * (d // tn)
    assert all(w.shape[0] % (steps * BF16_SUBLANES) == 0 for w in side)
    side_specs = [pl.BlockSpec((w.shape[0] // steps, w.shape[1]), lambda i, j: (i * (d // tn) + j, 0))
                  for w in side]

    def lhs_spec(a):
        return pl.BlockSpec((tm, a.shape[1]), lambda i, j: (i, 0))

    def w_spec(w):
        mode = dict(pipeline_mode=pl.Buffered(1)) if tn == d else {}
        return pl.BlockSpec((w.shape[0], tn), lambda i, j: (0, j), **mode)

    def gate_spec(first):
        return pl.BlockSpec((ns, tm, LANE), lambda i, j: (first // ns + j, i, 0))

    rg_half = (SLAB_XQ - SLAB_RG) // 2
    assert SLAB_RG % rg_half == 0 and o_ret.shape[1] == 2 * rg_half * LANE
    rg_specs = [pl.BlockSpec((rg_half, tm, LANE), lambda i, j, k=k: (SLAB_RG // rg_half + k, i, 0)) for k in (0, 1)]

    out = pl.pallas_call(
        functools.partial(_merge_kernel, len(side)),
        grid=(t // tm, d // tn),
        in_specs=[lhs_spec(o_na), lhs_spec(o_ret), lhs_spec(o_mem),
                  w_spec(w_na), w_spec(w_ret), w_spec(w_mem),
                  gate_spec(SLAB_G_NA), gate_spec(SLAB_G_RET), gate_spec(SLAB_G_MEM)] + rg_specs + side_specs,
        out_specs=[pl.BlockSpec((tm, tn), lambda i, j: (i, j))] + side_specs,
        out_shape=[jax.ShapeDtypeStruct((t, d), BF16)] + [jax.ShapeDtypeStruct(w.shape, BF16) for w in side],
        compiler_params=_params("parallel", "arbitrary"),
        name="merge",
    )(o_na, o_ret, o_mem, w_na, w_ret, w_mem, proj, proj, proj, proj, proj, *side)
    return out[0], out[1:]


def _out_proj_kernel(m_ref, w_ref, x_ref, o_ref, wb_ref):
    @pl.when(pl.program_id(0) == 0)
    def _():
        wb_ref[...] = w_ref[...].astype(BF16)

    o_ref[...] = x_ref[...] + jnp.dot(m_ref[...], wb_ref[...], preferred_element_type=F32)


def _out_proj(merged, w, x2d, tm):
    t, k = merged.shape
    d = w.shape[1]
    return pl.pallas_call(
        _out_proj_kernel,
        grid=(t // tm,),
        in_specs=[pl.BlockSpec((tm, k), lambda i: (i, 0)),
                  pl.BlockSpec((k, d), lambda i: (0, 0), pipeline_mode=pl.Buffered(1)),
                  pl.BlockSpec((tm, d), lambda i: (i, 0))],
        out_specs=pl.BlockSpec((tm, d), lambda i: (i, 0)),
        out_shape=jax.ShapeDtypeStruct((t, d), F32),
        scratch_shapes=[pltpu.VMEM((k, d), BF16)],
        compiler_params=_params("arbitrary"),
        name="out_proj",
    )(merged, w, x2d)


def _ffn_kernel(x_ref, g_ref, w1_ref, w2_ref, o_ref, h_ref):
    def mlp(h):
        a = jnp.maximum(jnp.dot(h, w1_ref[...], preferred_element_type=F32), 0.0)
        return jnp.dot((a * a).astype(BF16), w2_ref[...], preferred_element_type=F32)

    @pl.when(pl.program_id(1) == 0)
    def _():
        x = x_ref[...]
        h = _rms(x, g_ref[...]).astype(BF16)
        h_ref[...] = h
        o_ref[...] = x + mlp(h)

    @pl.when(pl.program_id(1) != 0)
    def _():
        o_ref[...] += mlp(h_ref[...])


def _ffn(x1, g, w1, w2, tm, tf):
    t, d = x1.shape
    dff = w1.shape[1]
    return pl.pallas_call(
        _ffn_kernel,
        grid=(t // tm, dff // tf),
        in_specs=[pl.BlockSpec((tm, d), lambda i, f: (i, 0)),
                  pl.BlockSpec((1, d), lambda i, f: (0, 0)),
                  pl.BlockSpec((d, tf), lambda i, f: (0, f)),
                  pl.BlockSpec((tf, d), lambda i, f: (f, 0))],
        out_specs=pl.BlockSpec((tm, d), lambda i, f: (i, 0)),
        out_shape=jax.ShapeDtypeStruct((t, d), F32),
        scratch_shapes=[pltpu.VMEM((tm, d), BF16)],
        compiler_params=_params("parallel", "arbitrary"),
        name="ffn",
    )(x1, g.reshape(1, d), w1, w2)


def _rope_tables(s_len):
    half = RET_QK_DIM // 2
    inv = np.power(np.float64(ROPE_BASE), -np.arange(half, dtype=np.float64) / half)
    ang = np.arange(s_len, dtype=np.float64)[:, None] * inv[None, :]
    cos, sin = np.cos(ang), np.sin(ang)
    return (jnp.asarray(np.concatenate([cos, cos], axis=1), F32),
            jnp.asarray(np.concatenate([-sin, sin], axis=1), F32))


class _Tiles(NamedTuple):
    tm: int
    tn_in: int
    tm_merge: int
    tn_merge: int
    tm_out: int
    tf: int
    tn_mem: int


def _tiles(t):
    return _Tiles(tm=min(1024, t), tn_in=1024, tm_merge=min(256, t), tn_merge=2048, tm_out=min(512, t), tf=1024, tn_mem=512)


def kernel(x, mem, norm_mix_g, w_in, na_q_norm_g, na_k_norm_g, na_rpb, ret_decay_logit_fwd, ret_decay_logit_bwd, ret_gn_g, mem_norm_g, w_mem_kv, xa_q_norm_g, xa_k_norm_g, w_br_na, w_br_ret, w_br_mem, w_out, norm_ffn_g, w_ff1, w_ff2):
    batch, s_len, d = x.shape
    n_mem = mem.shape[1]
    t = batch * s_len
    tiles = _tiles(t)
    cos2, sin2 = _rope_tables(s_len)
    x2d = x.reshape(t, d)
    mem2d = mem.reshape(batch * n_mem, d)
    gm, gn = t // tiles.tm, w_in.shape[2] // tiles.tn_in

    def side_job(w):
        r, c = w.shape
        if r % (gm * gn * BF16_SUBLANES) == 0:
            return w, (r // (gm * gn), c), lambda i, j: (i * gn + j, 0)
        assert r % (gn * BF16_SUBLANES) == 0 and c % (gm * LANE) == 0, (w.shape, gm, gn)
        return w, (r // gn, c // gm), lambda i, j: (j, i)

    for l in range(w_in.shape[0]):
        sides = [side_job(w) for w in (w_br_na[l], w_br_ret[l], w_br_mem[l])]
        proj, (w_na_bf, w_ret_bf, w_mem_bf) = _norm_proj(
            x2d, norm_mix_g[l], w_in[l], tiles.tm, tiles.tn_in, side=sides)
        mkv, _ = _norm_proj(mem2d, mem_norm_g[l], w_mem_kv[l], batch * n_mem, tiles.tn_mem)

        o_mem = _mem_xattn(proj, mkv, xa_q_norm_g[l], xa_k_norm_g[l], batch, s_len, n_mem)
        o_na = _na_attention(proj, na_rpb[l], na_q_norm_g[l], na_k_norm_g[l], batch, s_len)
        o_ret = _retention(proj, cos2, sin2, ret_decay_logit_fwd[l], ret_decay_logit_bwd[l], ret_gn_g[l],
                           batch, s_len)

        merged, (w_ff1_bf, w_ff2_bf) = _merge(o_na, o_ret, o_mem, w_na_bf, w_ret_bf, w_mem_bf, proj,
                                              tiles.tm_merge, tiles.tn_merge, side=(w_ff1[l], w_ff2[l]))
        x1 = _out_proj(merged, w_out[l], x2d, tiles.tm_out)
        x2d = _ffn(x1, norm_ffn_g[l], w_ff1_bf, w_ff2_bf, tiles.tm_out, tiles.tf)
    return x2d.reshape(batch, s_len, d)
```

```python
import functools
from typing import NamedTuple

import jax
import jax.numpy as jnp
import numpy as np
from jax import lax
from jax.experimental import pallas as pl
from jax.experimental.pallas import tpu as pltpu

F32 = jnp.float32
BF16 = jnp.bfloat16

LANE = 128
BF16_SUBLANES = 16
EPS = 1e-6
NEG = -1e30
LOG2E = 1.4426950408889634

GRID_W = 64
NA_HEADS = 8
NA_HEAD_DIM = 128
NA_MAX_ROWS = 8
NA_COLS = 16
NA_DR = 2 * NA_MAX_ROWS - 1
NA_DC = 2 * NA_COLS - 1
NA_TBL = 3 * NA_DR + 1
NA_QROWS = 4
NA_WIN = NA_QROWS + NA_MAX_ROWS

RET_HEADS = 8
RET_QK_DIM = 128
RET_V_DIM = 256
RET_BLOCK = 256
ROPE_BASE = 10000.0

XA_HEADS = 4
XA_HEAD_DIM = 256
XA_QROWS = 512

SLAB_NA_Q, SLAB_NA_K, SLAB_NA_V = 0, 8, 16
SLAB_RQ, SLAB_RK, SLAB_RV, SLAB_RG = 24, 32, 40, 56
SLAB_XQ = 72
SLAB_G_NA, SLAB_G_RET, SLAB_G_MEM = 80, 96, 112

V7X_VMEM_BYTES = 64 * 2**20
VMEM_LIMIT_BYTES = V7X_VMEM_BYTES - 6 * 2**20


def _params(*sem):
    return pltpu.CompilerParams(dimension_semantics=sem, vmem_limit_bytes=VMEM_LIMIT_BYTES)


def _rms(x, g):
    return x * lax.rsqrt(jnp.mean(x * x, axis=-1, keepdims=True) + EPS) * g


def _unit_rms(x):
    return x * lax.rsqrt(jnp.mean(x * x, axis=-1, keepdims=True) + EPS)


def _sigmoid(x):
    return 0.5 * jnp.tanh(0.5 * x) + 0.5


def _norm_proj_kernel(n_side, x_ref, g_ref, w_ref, *refs):
    side_in, o_ref = refs[:n_side], refs[n_side]
    side_out, hn_ref = refs[n_side + 1:2 * n_side + 1], refs[2 * n_side + 1]

    @pl.when(pl.program_id(1) == 0)
    def _():
        hn_ref[...] = _rms(x_ref[...], g_ref[...]).astype(BF16)

    acc = jnp.dot(hn_ref[...], w_ref[...].astype(BF16), preferred_element_type=F32)
    for s in range(o_ref.shape[0]):
        o_ref[s] = acc[:, s * LANE:(s + 1) * LANE].astype(BF16)
    for src, dst in zip(side_in, side_out):
        dst[...] = src[...].astype(BF16)


def _norm_proj(x2d, g, w, tm, tn, side=()):
    m, k = x2d.shape
    n = w.shape[1]
    side_specs = [pl.BlockSpec(blk, imap) for _, blk, imap in side]
    out = pl.pallas_call(
        functools.partial(_norm_proj_kernel, len(side)),
        grid=(m // tm, n // tn),
        in_specs=[
            pl.BlockSpec((tm, k), lambda i, j: (i, 0)),
            pl.BlockSpec((1, k), lambda i, j: (0, 0)),
            pl.BlockSpec((k, tn), lambda i, j: (0, j)),
        ] + side_specs,
        out_specs=[pl.BlockSpec((tn // LANE, tm, LANE), lambda i, j: (j, i, 0))] + side_specs,
        out_shape=[jax.ShapeDtypeStruct((n // LANE, m, LANE), BF16)]
        + [jax.ShapeDtypeStruct(a.shape, BF16) for a, _, _ in side],
        scratch_shapes=[pltpu.VMEM((tm, k), BF16)],
        compiler_params=_params("parallel", "arbitrary"),
        name="norm_proj",
    )(x2d, g.reshape(1, k), w, *[a for a, _, _ in side])
    return out[0], out[1:]


def _na_build_tables(h, rpb_ref, t_ref, base_ref):
    qc = lax.broadcasted_iota(jnp.int32, (GRID_W, LANE), 0)
    lane = lax.broadcasted_iota(jnp.int32, (GRID_W, LANE), 1)
    kc = lane & (GRID_W - 1)
    d = jnp.clip(kc - qc, -(NA_COLS - 1), NA_COLS - 1) + (NA_COLS - 1)
    cs = jnp.clip(qc - NA_COLS // 2, 0, GRID_W - NA_COLS)
    col_ok = (kc >= cs) & (kc < cs + NA_COLS)
    left = lane < GRID_W
    neg = jnp.full((GRID_W, LANE), NEG, F32)

    def body(dr, _):
        base = (h * NA_DR + dr) * NA_DC
        val = jnp.zeros((GRID_W, LANE), F32)
        for dd in range(NA_DC):
            val = jnp.where(d == dd, rpb_ref[base + dd], val)
        base_ref[dr] = jnp.where(col_ok, val * LOG2E, NEG)
        return 0

    lax.fori_loop(0, NA_DR, body, 0)
    for dr in range(NA_DR):
        second = base_ref[dr + 1] if dr + 1 < NA_DR else neg
        t_ref[dr] = jnp.where(left, base_ref[dr], second)
        t_ref[NA_DR + dr] = jnp.where(left, base_ref[dr], neg)
        t_ref[2 * NA_DR + dr] = jnp.where(left, neg, base_ref[dr])
    t_ref[3 * NA_DR] = neg


def _na_kernel(rpb_ref, q_ref, k_ref, v_ref, gq_ref, gk_ref, o_ref, kt_ref, va_ref, t_ref, base_ref):
    s_len = q_ref.shape[1]

    @pl.when(pl.program_id(1) == 0)
    def _():
        _na_build_tables(pl.program_id(0), rpb_ref, t_ref, base_ref)
        va_ref[:, NA_HEAD_DIM:] = jnp.ones((s_len, NA_HEAD_DIM), BF16)

    rows = s_len // GRID_W
    step_tok = NA_QROWS * GRID_W
    win_tok = NA_WIN * GRID_W
    n_tiles = NA_WIN // 2
    gq = gq_ref[...] * gk_ref[...] * (NA_HEAD_DIM ** -0.5 * LOG2E)
    va_ref[:, :NA_HEAD_DIM] = v_ref[0]

    def knorm(c, _):
        sl = pl.ds(pl.multiple_of(c * LANE, LANE), LANE)
        kt_ref[c] = _unit_rms(k_ref[0, sl, :].astype(F32)).astype(BF16).T
        return 0

    lax.fori_loop(0, s_len // LANE, knorm, 0, unroll=True)

    def step(i, _):
        r0 = NA_QROWS * i
        ws = jnp.clip(r0 - NA_MAX_ROWS // 2, 0, rows - NA_WIN)
        wp = ws // 2
        qsl = pl.ds(pl.multiple_of(i * step_tok, step_tok), step_tok)
        wsl = pl.ds(pl.multiple_of(ws * GRID_W, LANE), win_tok)
        qn = _rms(q_ref[0, qsl, :].astype(F32), gq).astype(BF16)
        kwin = jnp.concatenate([kt_ref[wp + t] for t in range(n_tiles)], axis=1)
        s = jnp.dot(qn, kwin, preferred_element_type=F32)
        bias_rows = []
        for qr in range(NA_QROWS):
            r = r0 + qr
            rs = jnp.clip(r - NA_MAX_ROWS // 2, 0, rows - NA_MAX_ROWS)
            tiles = []
            for t in range(n_tiles):
                ka = ws + 2 * t
                dr = ka - r + (NA_MAX_ROWS - 1)
                va = (ka >= rs) & (ka < rs + NA_MAX_ROWS)
                vb = (ka + 1 >= rs) & (ka + 1 < rs + NA_MAX_ROWS)
                idx = jnp.where(va, jnp.where(vb, dr, NA_DR + dr),
                                jnp.where(vb, 2 * NA_DR + dr + 1, 3 * NA_DR))
                tiles.append(t_ref[idx])
            bias_rows.append(jnp.concatenate(tiles, axis=1))
        s = s + jnp.concatenate(bias_rows, axis=0)
        e = jnp.exp2(s - jnp.max(s, axis=-1, keepdims=True))
        o = jnp.dot(e.astype(BF16), va_ref[wsl, :], preferred_element_type=F32)
        o_ref[qsl, :] = (o[:, :NA_HEAD_DIM] / o[:, NA_HEAD_DIM:]).astype(BF16)
        return 0

    lax.fori_loop(0, rows // NA_QROWS, step, 0, unroll=True)


def _na_attention(proj, rpb, gq, gk, batch, s_len):
    t = batch * s_len
    g_spec = pl.BlockSpec((1, NA_HEAD_DIM), lambda h, b: (0, 0))
    return pl.pallas_call(
        _na_kernel,
        grid=(NA_HEADS, batch),
        in_specs=[
            pl.BlockSpec(memory_space=pltpu.SMEM),
            pl.BlockSpec((1, s_len, LANE), lambda h, b: (SLAB_NA_Q + h, b, 0)),
            pl.BlockSpec((1, s_len, LANE), lambda h, b: (SLAB_NA_K + h, b, 0)),
            pl.BlockSpec((1, s_len, LANE), lambda h, b: (SLAB_NA_V + h, b, 0)),
            g_spec, g_spec,
        ],
        out_specs=pl.BlockSpec((s_len, NA_HEAD_DIM), lambda h, b: (b, h)),
        out_shape=jax.ShapeDtypeStruct((t, NA_HEADS * NA_HEAD_DIM), BF16),
        scratch_shapes=[pltpu.VMEM((s_len // LANE, NA_HEAD_DIM, LANE), BF16),
                        pltpu.VMEM((s_len, 2 * NA_HEAD_DIM), BF16),
                        pltpu.VMEM((NA_TBL, GRID_W, LANE), F32),
                        pltpu.VMEM((NA_DR, GRID_W, LANE), F32)],
        compiler_params=_params("arbitrary", "arbitrary"),
        name="na_attention",
    )(rpb.reshape(-1), proj, proj, proj, gq.reshape(1, -1), gk.reshape(1, -1))


def _log_sigmoid(x):
    return -(jnp.maximum(-x, 0.0) + jnp.log1p(jnp.exp(-jnp.abs(x))))


def _ret_kernel(lf_ref, lb_ref, q_ref, k_ref, v_ref, cos_ref, sin_ref, gn_ref,
                o_ref, qr_ref, kt_ref, d_ref, qdf_ref, qdb_ref, kv_ref, s_ref):
    s_len = q_ref.shape[1]
    c = RET_BLOCK
    nb = s_len // c
    dk = RET_QK_DIM
    half = dk // 2

    h = pl.program_id(1)
    lgf = _log_sigmoid(jnp.full((1, 1), lf_ref[h], F32))
    lgb = _log_sigmoid(jnp.full((1, 1), lb_ref[h], F32))
    ic = lax.broadcasted_iota(jnp.int32, (c, 1), 0).astype(F32)
    jr = lax.broadcasted_iota(jnp.int32, (1, c), 1).astype(F32)
    diff = ic - jr
    scale = dk ** -0.5
    d_ref[...] = jnp.exp(jnp.where(diff >= 0, lgf, lgb) * jnp.abs(diff)) * scale
    qdf_ref[...] = jnp.broadcast_to(jnp.exp(lgf * (ic + 1.0)), (c, dk))
    qdb_ref[...] = jnp.broadcast_to(jnp.exp(lgb * (c - ic)), (c, dk))
    kdf = jnp.exp(lgf * (c - 1.0 - jr)) * scale
    kdb = jnp.exp(lgb * jr) * scale
    cd_f = jnp.exp(lgf * c)
    cd_b = jnp.exp(lgb * c)

    def block_v(sl):
        return jnp.concatenate([v_ref[0, sl, :], v_ref[1, sl, :]], axis=1)

    def prep(n, _):
        sl = pl.ds(pl.multiple_of(n * c, c), c)
        cos = cos_ref[sl, :]
        sin = sin_ref[sl, :]
        q = q_ref[0, sl, :].astype(F32)
        k = k_ref[0, sl, :].astype(F32)
        qr_ref[sl, :] = (q * cos + pltpu.roll(q, half, 1) * sin).astype(BF16)
        kt = (k * cos + pltpu.roll(k, half, 1) * sin).T
        kt_ref[n] = kt.astype(BF16)
        lhs = jnp.concatenate([(kt * kdf).astype(BF16), (kt * kdb).astype(BF16)], axis=0)
        kv_ref[n] = jnp.dot(lhs, block_v(sl), preferred_element_type=F32)
        return 0

    lax.fori_loop(0, nb, prep, 0, unroll=True)

    def scan_f(n, sf):
        s_ref[n, :dk, :] = sf.astype(BF16)
        return cd_f * sf + kv_ref[n, :dk, :]

    def scan_b(t, sb):
        n = nb - 1 - t
        s_ref[n, dk:, :] = sb.astype(BF16)
        return cd_b * sb + kv_ref[n, dk:, :]

    zero = jnp.zeros((dk, RET_V_DIM), F32)

    def scan(n, carry):
        return scan_f(n, carry[0]), scan_b(n, carry[1])

    lax.fori_loop(0, nb, scan, (zero, zero), unroll=True)

    gn = gn_ref[0]

    def out(n, _):
        sl = pl.ds(pl.multiple_of(n * c, c), c)
        q = qr_ref[sl, :]
        qf32 = q.astype(F32)
        a = jnp.dot(q, kt_ref[n], preferred_element_type=F32) * d_ref[...]
        lhs = jnp.concatenate([a.astype(BF16), (qf32 * qdf_ref[...]).astype(BF16),
                               (qf32 * qdb_ref[...]).astype(BF16)], axis=1)
        rhs = jnp.concatenate([block_v(sl), s_ref[n]], axis=0)
        o = jnp.dot(lhs, rhs, preferred_element_type=F32)
        mu = jnp.mean(o, axis=-1, keepdims=True)
        oc = o - mu
        y = oc * lax.rsqrt(jnp.mean(oc * oc, axis=-1, keepdims=True) + EPS) * gn
        o_ref[sl, :] = y.astype(BF16)
        return 0

    lax.fori_loop(0, nb, out, 0, unroll=True)


def _retention(proj, cos2, sin2, lf, lb, gn, batch, s_len):
    t = batch * s_len
    nb = s_len // RET_BLOCK
    smem = pl.BlockSpec(memory_space=pltpu.SMEM)
    return pl.pallas_call(
        _ret_kernel,
        grid=(batch, RET_HEADS),
        in_specs=[
            smem, smem,
            pl.BlockSpec((1, s_len, LANE), lambda b, h: (SLAB_RQ + h, b, 0)),
            pl.BlockSpec((1, s_len, LANE), lambda b, h: (SLAB_RK + h, b, 0)),
            pl.BlockSpec((2, s_len, LANE), lambda b, h: (SLAB_RV // 2 + h, b, 0)),
            pl.BlockSpec((s_len, RET_QK_DIM), lambda b, h: (0, 0)),
            pl.BlockSpec((s_len, RET_QK_DIM), lambda b, h: (0, 0)),
            pl.BlockSpec((1, 1, RET_V_DIM), lambda b, h: (h, 0, 0)),
        ],
        out_specs=pl.BlockSpec((s_len, RET_V_DIM), lambda b, h: (b, h)),
        out_shape=jax.ShapeDtypeStruct((t, RET_HEADS * RET_V_DIM), BF16),
        scratch_shapes=[
            pltpu.VMEM((s_len, RET_QK_DIM), BF16),
            pltpu.VMEM((nb, RET_QK_DIM, RET_BLOCK), BF16),
            pltpu.VMEM((RET_BLOCK, RET_BLOCK), F32),
            pltpu.VMEM((RET_BLOCK, RET_QK_DIM), F32),
            pltpu.VMEM((RET_BLOCK, RET_QK_DIM), F32),
            pltpu.VMEM((nb, 2 * RET_QK_DIM, RET_V_DIM), F32),
            pltpu.VMEM((nb, 2 * RET_QK_DIM, RET_V_DIM), BF16),
        ],
        compiler_params=_params("parallel", "parallel"),
        name="retention",
    )(lf, lb, proj, proj, proj, cos2, sin2, gn.reshape(RET_HEADS, 1, RET_V_DIM))


def _xa_kernel(q_ref, mem_ref, gm_ref, wk_ref, wv_ref, gq_ref, gk_ref, o_ref):
    s_len = q_ref.shape[1]
    tq = XA_QROWS
    gq = gq_ref[...] * gk_ref[...] * (XA_HEAD_DIM ** -0.5 * LOG2E)
    mn = _rms(mem_ref[...], gm_ref[...]).astype(BF16)
    k = jnp.dot(mn, wk_ref[...].astype(BF16), preferred_element_type=F32)
    kn = _unit_rms(k).astype(BF16)
    v = jnp.dot(mn, wv_ref[...].astype(BF16), preferred_element_type=F32).astype(BF16)

    def body(i, _):
        sl = pl.ds(pl.multiple_of(i * tq, tq), tq)
        q = jnp.concatenate([q_ref[0, sl, :], q_ref[1, sl, :]], axis=1).astype(F32)
        qn = _rms(q, gq).astype(BF16)
        s = lax.dot_general(qn, kn, (((1,), (1,)), ((), ())), preferred_element_type=F32)
        e = jnp.exp2(s - jnp.max(s, axis=-1, keepdims=True))
        l = jnp.sum(e, axis=-1, keepdims=True)
        o = jnp.dot(e.astype(BF16), v, preferred_element_type=F32)
        o_ref[sl, :] = (o / l).astype(BF16)
        return 0

    lax.fori_loop(0, s_len // tq, body, 0, unroll=True)


def _mem_xattn(proj, mem2d, gm, w_kv, gq, gk, batch, s_len, n_mem):
    t = batch * s_len
    d = mem2d.shape[1]
    g_spec = pl.BlockSpec((1, XA_HEAD_DIM), lambda b, h: (0, 0))
    return pl.pallas_call(
        _xa_kernel,
        grid=(batch, XA_HEADS),
        in_specs=[
            pl.BlockSpec((2, s_len, LANE), lambda b, h: (SLAB_XQ // 2 + h, b, 0)),
            pl.BlockSpec((n_mem, d), lambda b, h: (b, 0)),
            pl.BlockSpec((1, d), lambda b, h: (0, 0)),
            pl.BlockSpec((d, XA_HEAD_DIM), lambda b, h: (0, h)),
            pl.BlockSpec((d, XA_HEAD_DIM), lambda b, h: (0, XA_HEADS + h)),
            g_spec, g_spec,
        ],
        out_specs=pl.BlockSpec((s_len, XA_HEAD_DIM), lambda b, h: (b, h)),
        out_shape=jax.ShapeDtypeStruct((t, XA_HEADS * XA_HEAD_DIM), BF16),
        compiler_params=_params("parallel", "parallel"),
        name="mem_xattn",
    )(proj, mem2d, gm.reshape(1, d), w_kv, w_kv, gq.reshape(1, -1), gk.reshape(1, -1))


def _merge_kernel(n_side, ona_ref, oret_ref, omem_ref, wna_ref, wret_ref, wmem_ref,
                  gna_ref, gret_ref, gmem_ref, rga_ref, rgb_ref, *refs):
    side_in, o_ref, side_out = refs[:n_side], refs[n_side], refs[n_side + 1:]
    for src, dst in zip(side_in, side_out):
        dst[...] = src[...].astype(BF16)
    y_na = jnp.dot(ona_ref[...], wna_ref[...], preferred_element_type=F32)
    rg = jnp.concatenate([r[s] for r in (rga_ref, rgb_ref) for s in range(r.shape[0])], axis=1).astype(F32)
    ret_in = (oret_ref[...].astype(F32) * rg * _sigmoid(rg)).astype(BF16)
    y_ret = jnp.dot(ret_in, wret_ref[...], preferred_element_type=F32)
    y_mem = jnp.dot(omem_ref[...], wmem_ref[...], preferred_element_type=F32)
    for j in range(gna_ref.shape[0]):
        cs = slice(j * LANE, (j + 1) * LANE)
        o_ref[:, cs] = (_sigmoid(gna_ref[j].astype(F32)) * y_na[:, cs]
                        + _sigmoid(gret_ref[j].astype(F32)) * y_ret[:, cs]
                        + _sigmoid(gmem_ref[j].astype(F32)) * y_mem[:, cs]).astype(BF16)


def _merge(o_na, o_ret, o_mem, w_na, w_ret, w_mem, proj, tm, tn, side=()):
    t = o_na.shape[0]
    d = w_na.shape[1]
    ns = tn // LANE
    steps = (t // tm) * (d // tn)
    assert all(w.shape[0] % (steps * BF16_SUBLANES) == 0 for w in side)
    side_specs = [pl.BlockSpec((w.shape[0] // steps, w.shape[1]), lambda i, j: (i * (d // tn) + j, 0))
                  for w in side]

    def lhs_spec(a):
        return pl.BlockSpec((tm, a.shape[1]), lambda i, j: (i, 0))

    def w_spec(w):
        mode = dict(pipeline_mode=pl.Buffered(1)) if tn == d else {}
        return pl.BlockSpec((w.shape[0], tn), lambda i, j: (0, j), **mode)

    def gate_spec(first):
        return pl.BlockSpec((ns, tm, LANE), lambda i, j: (first // ns + j, i, 0))

    rg_half = (SLAB_XQ - SLAB_RG) // 2
    assert SLAB_RG % rg_half == 0 and o_ret.shape[1] == 2 * rg_half * LANE
    rg_specs = [pl.BlockSpec((rg_half, tm, LANE), lambda i, j, k=k: (SLAB_RG // rg_half + k, i, 0)) for k in (0, 1)]

    out = pl.pallas_call(
        functools.partial(_merge_kernel, len(side)),
        grid=(t // tm, d // tn),
        in_specs=[lhs_spec(o_na), lhs_spec(o_ret), lhs_spec(o_mem),
                  w_spec(w_na), w_spec(w_ret), w_spec(w_mem),
                  gate_spec(SLAB_G_NA), gate_spec(SLAB_G_RET), gate_spec(SLAB_G_MEM)] + rg_specs + side_specs,
        out_specs=[pl.BlockSpec((tm, tn), lambda i, j: (i, j))] + side_specs,
        out_shape=[jax.ShapeDtypeStruct((t, d), BF16)] + [jax.ShapeDtypeStruct(w.shape, BF16) for w in side],
        compiler_params=_params("parallel", "arbitrary"),
        name="merge",
    )(o_na, o_ret, o_mem, w_na, w_ret, w_mem, proj, proj, proj, proj, proj, *side)
    return out[0], out[1:]


def _out_proj_kernel(m_ref, w_ref, x_ref, o_ref, wb_ref):
    @pl.when(pl.program_id(0) == 0)
    def _():
        wb_ref[...] = w_ref[...].astype(BF16)

    o_ref[...] = x_ref[...] + jnp.dot(m_ref[...], wb_ref[...], preferred_element_type=F32)


def _out_proj(merged, w, x2d, tm):
    t, k = merged.shape
    d = w.shape[1]
    return pl.pallas_call(
        _out_proj_kernel,
        grid=(t // tm,),
        in_specs=[pl.BlockSpec((tm, k), lambda i: (i, 0)),
                  pl.BlockSpec((k, d), lambda i: (0, 0), pipeline_mode=pl.Buffered(1)),
                  pl.BlockSpec((tm, d), lambda i: (i, 0))],
        out_specs=pl.BlockSpec((tm, d), lambda i: (i, 0)),
        out_shape=jax.ShapeDtypeStruct((t, d), F32),
        scratch_shapes=[pltpu.VMEM((k, d), BF16)],
        compiler_params=_params("arbitrary"),
        name="out_proj",
    )(merged, w, x2d)


def _ffn_kernel(x_ref, g_ref, w1_ref, w2_ref, o_ref, h_ref):
    def mlp(h):
        a = jnp.maximum(jnp.dot(h, w1_ref[...], preferred_element_type=F32), 0.0)
        return jnp.dot((a * a).astype(BF16), w2_ref[...], preferred_element_type=F32)

    @pl.when(pl.program_id(1) == 0)
    def _():
        x = x_ref[...]
        h = _rms(x, g_ref[...]).astype(BF16)
        h_ref[...] = h
        o_ref[...] = x + mlp(h)

    @pl.when(pl.program_id(1) != 0)
    def _():
        o_ref[...] += mlp(h_ref[...])


def _ffn(x1, g, w1, w2, tm, tf):
    t, d = x1.shape
    dff = w1.shape[1]
    return pl.pallas_call(
        _ffn_kernel,
        grid=(t // tm, dff // tf),
        in_specs=[pl.BlockSpec((tm, d), lambda i, f: (i, 0)),
                  pl.BlockSpec((1, d), lambda i, f: (0, 0)),
                  pl.BlockSpec((d, tf), lambda i, f: (0, f)),
                  pl.BlockSpec((tf, d), lambda i, f: (f, 0))],
        out_specs=pl.BlockSpec((tm, d), lambda i, f: (i, 0)),
        out_shape=jax.ShapeDtypeStruct((t, d), F32),
        scratch_shapes=[pltpu.VMEM((tm, d), BF16)],
        compiler_params=_params("parallel", "arbitrary"),
        name="ffn",
    )(x1, g.reshape(1, d), w1, w2)


def _rope_tables(s_len):
    half = RET_QK_DIM // 2
    inv = np.power(np.float64(ROPE_BASE), -np.arange(half, dtype=np.float64) / half)
    ang = np.arange(s_len, dtype=np.float64)[:, None] * inv[None, :]
    cos, sin = np.cos(ang), np.sin(ang)
    return (jnp.asarray(np.concatenate([cos, cos], axis=1), F32),
            jnp.asarray(np.concatenate([-sin, sin], axis=1), F32))


class _Tiles(NamedTuple):
    tm: int
    tn_in: int
    tm_merge: int
    tn_merge: int
    tm_out: int
    tf: int


def _tiles(t):
    return _Tiles(tm=min(1024, t), tn_in=1024, tm_merge=min(256, t), tn_merge=2048, tm_out=min(512, t), tf=1024)


def kernel(x, mem, norm_mix_g, w_in, na_q_norm_g, na_k_norm_g, na_rpb, ret_decay_logit_fwd, ret_decay_logit_bwd, ret_gn_g, mem_norm_g, w_mem_kv, xa_q_norm_g, xa_k_norm_g, w_br_na, w_br_ret, w_br_mem, w_out, norm_ffn_g, w_ff1, w_ff2):
    batch, s_len, d = x.shape
    n_mem = mem.shape[1]
    t = batch * s_len
    tiles = _tiles(t)
    cos2, sin2 = _rope_tables(s_len)
    x2d = x.reshape(t, d)
    mem2d = mem.reshape(batch * n_mem, d)
    gm, gn = t // tiles.tm, w_in.shape[2] // tiles.tn_in

    def side_job(w):
        r, c = w.shape
        if r % (gm * gn * BF16_SUBLANES) == 0:
            return w, (r // (gm * gn), c), lambda i, j: (i * gn + j, 0)
        assert r % (gn * BF16_SUBLANES) == 0 and c % (gm * LANE) == 0, (w.shape, gm, gn)
        return w, (r // gn, c // gm), lambda i, j: (j, i)

    for l in range(w_in.shape[0]):
        sides = [side_job(w) for w in (w_br_na[l], w_br_ret[l], w_br_mem[l])]
        proj, (w_na_bf, w_ret_bf, w_mem_bf) = _norm_proj(
            x2d, norm_mix_g[l], w_in[l], tiles.tm, tiles.tn_in, side=sides)

        o_mem = _mem_xattn(proj, mem2d, mem_norm_g[l], w_mem_kv[l], xa_q_norm_g[l], xa_k_norm_g[l],
                           batch, s_len, n_mem)
        o_na = _na_attention(proj, na_rpb[l], na_q_norm_g[l], na_k_norm_g[l], batch, s_len)
        o_ret = _retention(proj, cos2, sin2, ret_decay_logit_fwd[l], ret_decay_logit_bwd[l], ret_gn_g[l],
                           batch, s_len)

        merged, (w_ff1_bf, w_ff2_bf) = _merge(o_na, o_ret, o_mem, w_na_bf, w_ret_bf, w_mem_bf, proj,
                                              tiles.tm_merge, tiles.tn_merge, side=(w_ff1[l], w_ff2[l]))
        x1 = _out_proj(merged, w_out[l], x2d, tiles.tm_out)
        x2d = _ffn(x1, norm_ffn_g[l], w_ff1_bf, w_ff2_bf, tiles.tm_out, tiles.tf)
    return x2d.reshape(batch, s_len, d)
```

```python
import functools
from typing import NamedTuple

import jax
import jax.numpy as jnp
import numpy as np
from jax import lax
from jax.experimental import pallas as pl
from jax.experimental.pallas import tpu as pltpu

F32 = jnp.float32
BF16 = jnp.bfloat16

LANE = 128
BF16_SUBLANES = 16
EPS = 1e-6
NEG = -1e30
LOG2E = 1.4426950408889634

GRID_W = 64
NA_HEADS = 8
NA_HEAD_DIM = 128
NA_MAX_ROWS = 8
NA_COLS = 16
NA_DR = 2 * NA_MAX_ROWS - 1
NA_DC = 2 * NA_COLS - 1
NA_TBL = 3 * NA_DR + 1
NA_QROWS = 4
NA_WIN = NA_QROWS + NA_MAX_ROWS

RET_HEADS = 8
RET_QK_DIM = 128
RET_V_DIM = 256
RET_BLOCK = 256
ROPE_BASE = 10000.0

XA_HEADS = 4
XA_HEAD_DIM = 256
XA_QROWS = 512

SLAB_NA_Q, SLAB_NA_K, SLAB_NA_V = 0, 8, 16
SLAB_RQ, SLAB_RK, SLAB_RV, SLAB_RG = 24, 32, 40, 56
SLAB_XQ = 72
SLAB_G_NA, SLAB_G_RET, SLAB_G_MEM = 80, 96, 112

V7X_VMEM_BYTES = 64 * 2**20
VMEM_LIMIT_BYTES = V7X_VMEM_BYTES - 6 * 2**20


def _params(*sem):
    return pltpu.CompilerParams(dimension_semantics=sem, vmem_limit_bytes=VMEM_LIMIT_BYTES)


def _rms(x, g):
    return x * lax.rsqrt(jnp.mean(x * x, axis=-1, keepdims=True) + EPS) * g


def _unit_rms(x):
    return x * lax.rsqrt(jnp.mean(x * x, axis=-1, keepdims=True) + EPS)


def _sigmoid(x):
    return 0.5 * jnp.tanh(0.5 * x) + 0.5


def _norm_proj_kernel(n_side, x_ref, g_ref, w_ref, *refs):
    side_in, o_ref = refs[:n_side], refs[n_side]
    side_out, hn_ref = refs[n_side + 1:2 * n_side + 1], refs[2 * n_side + 1]

    @pl.when(pl.program_id(1) == 0)
    def _():
        hn_ref[...] = _rms(x_ref[...], g_ref[...]).astype(BF16)

    acc = jnp.dot(hn_ref[...], w_ref[...].astype(BF16), preferred_element_type=F32)
    for s in range(o_ref.shape[0]):
        o_ref[s] = acc[:, s * LANE:(s + 1) * LANE].astype(BF16)
    for src, dst in zip(side_in, side_out):
        dst[...] = src[...].astype(BF16)


def _norm_proj(x2d, g, w, tm, tn, side=()):
    m, k = x2d.shape
    n = w.shape[1]
    side_specs = [pl.BlockSpec(blk, imap) for _, blk, imap in side]
    out = pl.pallas_call(
        functools.partial(_norm_proj_kernel, len(side)),
        grid=(m // tm, n // tn),
        in_specs=[
            pl.BlockSpec((tm, k), lambda i, j: (i, 0)),
            pl.BlockSpec((1, k), lambda i, j: (0, 0)),
            pl.BlockSpec((k, tn), lambda i, j: (0, j)),
        ] + side_specs,
        out_specs=[pl.BlockSpec((tn // LANE, tm, LANE), lambda i, j: (j, i, 0))] + side_specs,
        out_shape=[jax.ShapeDtypeStruct((n // LANE, m, LANE), BF16)]
        + [jax.ShapeDtypeStruct(a.shape, BF16) for a, _, _ in side],
        scratch_shapes=[pltpu.VMEM((tm, k), BF16)],
        compiler_params=_params("parallel", "arbitrary"),
        name="norm_proj",
    )(x2d, g.reshape(1, k), w, *[a for a, _, _ in side])
    return out[0], out[1:]


def _na_build_tables(h, rpb_ref, t_ref, base_ref):
    qc = lax.broadcasted_iota(jnp.int32, (GRID_W, LANE), 0)
    lane = lax.broadcasted_iota(jnp.int32, (GRID_W, LANE), 1)
    kc = lane & (GRID_W - 1)
    d = jnp.clip(kc - qc, -(NA_COLS - 1), NA_COLS - 1) + (NA_COLS - 1)
    cs = jnp.clip(qc - NA_COLS // 2, 0, GRID_W - NA_COLS)
    col_ok = (kc >= cs) & (kc < cs + NA_COLS)
    left = lane < GRID_W
    neg = jnp.full((GRID_W, LANE), NEG, F32)

    def body(dr, _):
        base = (h * NA_DR + dr) * NA_DC
        val = jnp.zeros((GRID_W, LANE), F32)
        for dd in range(NA_DC):
            val = jnp.where(d == dd, rpb_ref[base + dd], val)
        base_ref[dr] = jnp.where(col_ok, val * LOG2E, NEG)
        return 0

    lax.fori_loop(0, NA_DR, body, 0)
    for dr in range(NA_DR):
        second = base_ref[dr + 1] if dr + 1 < NA_DR else neg
        t_ref[dr] = jnp.where(left, base_ref[dr], second)
        t_ref[NA_DR + dr] = jnp.where(left, base_ref[dr], neg)
        t_ref[2 * NA_DR + dr] = jnp.where(left, neg, base_ref[dr])
    t_ref[3 * NA_DR] = neg


def _na_kernel(rpb_ref, q_ref, k_ref, v_ref, gq_ref, gk_ref, o_ref, kt_ref, va_ref, t_ref, base_ref):
    s_len = q_ref.shape[1]

    @pl.when(pl.program_id(1) == 0)
    def _():
        _na_build_tables(pl.program_id(0), rpb_ref, t_ref, base_ref)
        va_ref[:, NA_HEAD_DIM:] = jnp.ones((s_len, NA_HEAD_DIM), BF16)

    rows = s_len // GRID_W
    step_tok = NA_QROWS * GRID_W
    win_tok = NA_WIN * GRID_W
    n_tiles = NA_WIN // 2
    gq = gq_ref[...] * gk_ref[...] * (NA_HEAD_DIM ** -0.5 * LOG2E)
    va_ref[:, :NA_HEAD_DIM] = v_ref[0]

    def knorm(c, _):
        sl = pl.ds(pl.multiple_of(c * LANE, LANE), LANE)
        kt_ref[c] = _unit_rms(k_ref[0, sl, :].astype(F32)).astype(BF16).T
        return 0

    lax.fori_loop(0, s_len // LANE, knorm, 0, unroll=True)

    def step(i, _):
        r0 = NA_QROWS * i
        ws = jnp.clip(r0 - NA_MAX_ROWS // 2, 0, rows - NA_WIN)
        wp = ws // 2
        qsl = pl.ds(pl.multiple_of(i * step_tok, step_tok), step_tok)
        wsl = pl.ds(pl.multiple_of(ws * GRID_W, LANE), win_tok)
        qn = _rms(q_ref[0, qsl, :].astype(F32), gq).astype(BF16)
        kwin = jnp.concatenate([kt_ref[wp + t] for t in range(n_tiles)], axis=1)
        s = jnp.dot(qn, kwin, preferred_element_type=F32)
        bias_rows = []
        for qr in range(NA_QROWS):
            r = r0 + qr
            rs = jnp.clip(r - NA_MAX_ROWS // 2, 0, rows - NA_MAX_ROWS)
            tiles = []
            for t in range(n_tiles):
                ka = ws + 2 * t
                dr = ka - r + (NA_MAX_ROWS - 1)
                va = (ka >= rs) & (ka < rs + NA_MAX_ROWS)
                vb = (ka + 1 >= rs) & (ka + 1 < rs + NA_MAX_ROWS)
                idx = jnp.where(va, jnp.where(vb, dr, NA_DR + dr),
                                jnp.where(vb, 2 * NA_DR + dr + 1, 3 * NA_DR))
                tiles.append(t_ref[idx])
            bias_rows.append(jnp.concatenate(tiles, axis=1))
        s = s + jnp.concatenate(bias_rows, axis=0)
        e = jnp.exp2(s - jnp.max(s, axis=-1, keepdims=True))
        o = jnp.dot(e.astype(BF16), va_ref[wsl, :], preferred_element_type=F32)
        o_ref[qsl, :] = (o[:, :NA_HEAD_DIM] / o[:, NA_HEAD_DIM:]).astype(BF16)
        return 0

    lax.fori_loop(0, rows // NA_QROWS, step, 0, unroll=True)


def _na_attention(proj, rpb, gq, gk, batch, s_len):
    t = batch * s_len
    g_spec = pl.BlockSpec((1, NA_HEAD_DIM), lambda h, b: (0, 0))
    return pl.pallas_call(
        _na_kernel,
        grid=(NA_HEADS, batch),
        in_specs=[
            pl.BlockSpec(memory_space=pltpu.SMEM),
            pl.BlockSpec((1, s_len, LANE), lambda h, b: (SLAB_NA_Q + h, b, 0)),
            pl.BlockSpec((1, s_len, LANE), lambda h, b: (SLAB_NA_K + h, b, 0)),
            pl.BlockSpec((1, s_len, LANE), lambda h, b: (SLAB_NA_V + h, b, 0)),
            g_spec, g_spec,
        ],
        out_specs=pl.BlockSpec((s_len, NA_HEAD_DIM), lambda h, b: (b, h)),
        out_shape=jax.ShapeDtypeStruct((t, NA_HEADS * NA_HEAD_DIM), BF16),
        scratch_shapes=[pltpu.VMEM((s_len // LANE, NA_HEAD_DIM, LANE), BF16),
                        pltpu.VMEM((s_len, 2 * NA_HEAD_DIM), BF16),
                        pltpu.VMEM((NA_TBL, GRID_W, LANE), F32),
                        pltpu.VMEM((NA_DR, GRID_W, LANE), F32)],
        compiler_params=_params("arbitrary", "arbitrary"),
        name="na_attention",
    )(rpb.reshape(-1), proj, proj, proj, gq.reshape(1, -1), gk.reshape(1, -1))


def _log_sigmoid(x):
    return -(jnp.maximum(-x, 0.0) + jnp.log1p(jnp.exp(-jnp.abs(x))))


def _ret_kernel(lf_ref, lb_ref, q_ref, k_ref, v_ref, cos_ref, sin_ref, gn_ref,
                o_ref, qr_ref, kt_ref, d_ref, qdf_ref, qdb_ref, kv_ref, s_ref):
    s_len = q_ref.shape[1]
    c = RET_BLOCK
    nb = s_len // c
    dk = RET_QK_DIM
    half = dk // 2

    h = pl.program_id(1)
    lgf = _log_sigmoid(jnp.full((1, 1), lf_ref[h], F32))
    lgb = _log_sigmoid(jnp.full((1, 1), lb_ref[h], F32))
    ic = lax.broadcasted_iota(jnp.int32, (c, 1), 0).astype(F32)
    jr = lax.broadcasted_iota(jnp.int32, (1, c), 1).astype(F32)
    diff = ic - jr
    scale = dk ** -0.5
    d_ref[...] = jnp.exp(jnp.where(diff >= 0, lgf, lgb) * jnp.abs(diff)) * scale
    qdf_ref[...] = jnp.broadcast_to(jnp.exp(lgf * (ic + 1.0)), (c, dk))
    qdb_ref[...] = jnp.broadcast_to(jnp.exp(lgb * (c - ic)), (c, dk))
    kdf = jnp.exp(lgf * (c - 1.0 - jr)) * scale
    kdb = jnp.exp(lgb * jr) * scale
    cd_f = jnp.exp(lgf * c)
    cd_b = jnp.exp(lgb * c)

    def block_v(sl):
        return jnp.concatenate([v_ref[0, sl, :], v_ref[1, sl, :]], axis=1)

    def prep(n, _):
        sl = pl.ds(pl.multiple_of(n * c, c), c)
        cos = cos_ref[sl, :]
        sin = sin_ref[sl, :]
        q = q_ref[0, sl, :].astype(F32)
        k = k_ref[0, sl, :].astype(F32)
        qr_ref[sl, :] = (q * cos + pltpu.roll(q, half, 1) * sin).astype(BF16)
        kt = (k * cos + pltpu.roll(k, half, 1) * sin).T
        kt_ref[n] = kt.astype(BF16)
        lhs = jnp.concatenate([(kt * kdf).astype(BF16), (kt * kdb).astype(BF16)], axis=0)
        kv_ref[n] = jnp.dot(lhs, block_v(sl), preferred_element_type=F32)
        return 0

    lax.fori_loop(0, nb, prep, 0, unroll=True)

    def scan_f(n, sf):
        s_ref[n, :dk, :] = sf.astype(BF16)
        return cd_f * sf + kv_ref[n, :dk, :]

    def scan_b(t, sb):
        n = nb - 1 - t
        s_ref[n, dk:, :] = sb.astype(BF16)
        return cd_b * sb + kv_ref[n, dk:, :]

    zero = jnp.zeros((dk, RET_V_DIM), F32)

    def scan(n, carry):
        return scan_f(n, carry[0]), scan_b(n, carry[1])

    lax.fori_loop(0, nb, scan, (zero, zero), unroll=True)

    gn = gn_ref[0]

    def out(n, _):
        sl = pl.ds(pl.multiple_of(n * c, c), c)
        q = qr_ref[sl, :]
        qf32 = q.astype(F32)
        a = jnp.dot(q, kt_ref[n], preferred_element_type=F32) * d_ref[...]
        lhs = jnp.concatenate([a.astype(BF16), (qf32 * qdf_ref[...]).astype(BF16),
                               (qf32 * qdb_ref[...]).astype(BF16)], axis=1)
        rhs = jnp.concatenate([block_v(sl), s_ref[n]], axis=0)
        o = jnp.dot(lhs, rhs, preferred_element_type=F32)
        mu = jnp.mean(o, axis=-1, keepdims=True)
        oc = o - mu
        y = oc * lax.rsqrt(jnp.mean(oc * oc, axis=-1, keepdims=True) + EPS) * gn
        o_ref[sl, :] = y.astype(BF16)
        return 0

    lax.fori_loop(0, nb, out, 0, unroll=True)


def _retention(proj, cos2, sin2, lf, lb, gn, batch, s_len):
    t = batch * s_len
    nb = s_len // RET_BLOCK
    smem = pl.BlockSpec(memory_space=pltpu.SMEM)
    return pl.pallas_call(
        _ret_kernel,
        grid=(batch, RET_HEADS),
        in_specs=[
            smem, smem,
            pl.BlockSpec((1, s_len, LANE), lambda b, h: (SLAB_RQ + h, b, 0)),
            pl.BlockSpec((1, s_len, LANE), lambda b, h: (SLAB_RK + h, b, 0)),
            pl.BlockSpec((2, s_len, LANE), lambda b, h: (SLAB_RV // 2 + h, b, 0)),
            pl.BlockSpec((s_len, RET_QK_DIM), lambda b, h: (0, 0)),
            pl.BlockSpec((s_len, RET_QK_DIM), lambda b, h: (0, 0)),
            pl.BlockSpec((1, 1, RET_V_DIM), lambda b, h: (h, 0, 0)),
        ],
        out_specs=pl.BlockSpec((s_len, RET_V_DIM), lambda b, h: (b, h)),
        out_shape=jax.ShapeDtypeStruct((t, RET_HEADS * RET_V_DIM), BF16),
        scratch_shapes=[
            pltpu.VMEM((s_len, RET_QK_DIM), BF16),
            pltpu.VMEM((nb, RET_QK_DIM, RET_BLOCK), BF16),
            pltpu.VMEM((RET_BLOCK, RET_BLOCK), F32),
            pltpu.VMEM((RET_BLOCK, RET_QK_DIM), F32),
            pltpu.VMEM((RET_BLOCK, RET_QK_DIM), F32),
            pltpu.VMEM((nb, 2 * RET_QK_DIM, RET_V_DIM), F32),
            pltpu.VMEM((nb, 2 * RET_QK_DIM, RET_V_DIM), BF16),
        ],
        compiler_params=_params("parallel", "parallel"),
        name="retention",
    )(lf, lb, proj, proj, proj, cos2, sin2, gn.reshape(RET_HEADS, 1, RET_V_DIM))


def _xa_kernel(q_ref, mem_ref, gm_ref, wk_ref, wv_ref, gq_ref, gk_ref, o_ref):
    s_len = q_ref.shape[1]
    tq = XA_QROWS
    gq = gq_ref[...] * gk_ref[...] * (XA_HEAD_DIM ** -0.5 * LOG2E)
    mn = _rms(mem_ref[...], gm_ref[...]).astype(BF16)
    k = jnp.dot(mn, wk_ref[...].astype(BF16), preferred_element_type=F32)
    kn = _unit_rms(k).astype(BF16)
    v = jnp.dot(mn, wv_ref[...].astype(BF16), preferred_element_type=F32).astype(BF16)

    def body(i, _):
        sl = pl.ds(pl.multiple_of(i * tq, tq), tq)
        q = jnp.concatenate([q_ref[0, sl, :], q_ref[1, sl, :]], axis=1).astype(F32)
        qn = _rms(q, gq).astype(BF16)
        s = lax.dot_general(qn, kn, (((1,), (1,)), ((), ())), preferred_element_type=F32)
        e = jnp.exp2(s - jnp.max(s, axis=-1, keepdims=True))
        l = jnp.sum(e, axis=-1, keepdims=True)
        o = jnp.dot(e.astype(BF16), v, preferred_element_type=F32)
        o_ref[sl, :] = (o / l).astype(BF16)
        return 0

    lax.fori_loop(0, s_len // tq, body, 0, unroll=True)


def _mem_xattn(proj, mem2d, gm, w_kv, gq, gk, batch, s_len, n_mem):
    t = batch * s_len
    d = mem2d.shape[1]
    g_spec = pl.BlockSpec((1, XA_HEAD_DIM), lambda h, b: (0, 0))
    return pl.pallas_call(
        _xa_kernel,
        grid=(XA_HEADS, batch),
        in_specs=[
            pl.BlockSpec((2, s_len, LANE), lambda h, b: (SLAB_XQ // 2 + h, b, 0)),
            pl.BlockSpec((n_mem, d), lambda h, b: (b, 0)),
            pl.BlockSpec((1, d), lambda h, b: (0, 0)),
            pl.BlockSpec((d, XA_HEAD_DIM), lambda h, b: (0, h)),
            pl.BlockSpec((d, XA_HEAD_DIM), lambda h, b: (0, XA_HEADS + h)),
            g_spec, g_spec,
        ],
        out_specs=pl.BlockSpec((s_len, XA_HEAD_DIM), lambda h, b: (b, h)),
        out_shape=jax.ShapeDtypeStruct((t, XA_HEADS * XA_HEAD_DIM), BF16),
        compiler_params=_params("parallel", "parallel"),
        name="mem_xattn",
    )(proj, mem2d, gm.reshape(1, d), w_kv, w_kv, gq.reshape(1, -1), gk.reshape(1, -1))


def _merge_kernel(n_side, ona_ref, oret_ref, omem_ref, wna_ref, wret_ref, wmem_ref,
                  gna_ref, gret_ref, gmem_ref, rga_ref, rgb_ref, *refs):
    side_in, o_ref, side_out = refs[:n_side], refs[n_side], refs[n_side + 1:]
    for src, dst in zip(side_in, side_out):
        dst[...] = src[...].astype(BF16)
    y_na = jnp.dot(ona_ref[...], wna_ref[...], preferred_element_type=F32)
    rg = jnp.concatenate([r[s] for r in (rga_ref, rgb_ref) for s in range(r.shape[0])], axis=1).astype(F32)
    ret_in = (oret_ref[...].astype(F32) * rg * _sigmoid(rg)).astype(BF16)
    y_ret = jnp.dot(ret_in, wret_ref[...], preferred_element_type=F32)
    y_mem = jnp.dot(omem_ref[...], wmem_ref[...], preferred_element_type=F32)
    for j in range(gna_ref.shape[0]):
        cs = slice(j * LANE, (j + 1) * LANE)
        o_ref[:, cs] = (_sigmoid(gna_ref[j].astype(F32)) * y_na[:, cs]
                        + _sigmoid(gret_ref[j].astype(F32)) * y_ret[:, cs]
                        + _sigmoid(gmem_ref[j].astype(F32)) * y_mem[:, cs]).astype(BF16)


def _merge(o_na, o_ret, o_mem, w_na, w_ret, w_mem, proj, tm, tn, side=()):
    t = o_na.shape[0]
    d = w_na.shape[1]
    ns = tn // LANE
    steps = (t // tm) * (d // tn)
    assert all(w.shape[0] % (steps * BF16_SUBLANES) == 0 for w in side)
    side_specs = [pl.BlockSpec((w.shape[0] // steps, w.shape[1]), lambda i, j: (i * (d // tn) + j, 0))
                  for w in side]

    def lhs_spec(a):
        return pl.BlockSpec((tm, a.shape[1]), lambda i, j: (i, 0))

    def w_spec(w):
        mode = dict(pipeline_mode=pl.Buffered(1)) if tn == d else {}
        return pl.BlockSpec((w.shape[0], tn), lambda i, j: (0, j), **mode)

    def gate_spec(first):
        return pl.BlockSpec((ns, tm, LANE), lambda i, j: (first // ns + j, i, 0))

    rg_half = (SLAB_XQ - SLAB_RG) // 2
    assert SLAB_RG % rg_half == 0 and o_ret.shape[1] == 2 * rg_half * LANE
    rg_specs = [pl.BlockSpec((rg_half, tm, LANE), lambda i, j, k=k: (SLAB_RG // rg_half + k, i, 0)) for k in (0, 1)]

    out = pl.pallas_call(
        functools.partial(_merge_kernel, len(side)),
        grid=(t // tm, d // tn),
        in_specs=[lhs_spec(o_na), lhs_spec(o_ret), lhs_spec(o_mem),
                  w_spec(w_na), w_spec(w_ret), w_spec(w_mem),
                  gate_spec(SLAB_G_NA), gate_spec(SLAB_G_RET), gate_spec(SLAB_G_MEM)] + rg_specs + side_specs,
        out_specs=[pl.BlockSpec((tm, tn), lambda i, j: (i, j))] + side_specs,
        out_shape=[jax.ShapeDtypeStruct((t, d), BF16)] + [jax.ShapeDtypeStruct(w.shape, BF16) for w in side],
        compiler_params=_params("parallel", "arbitrary"),
        name="merge",
    )(o_na, o_ret, o_mem, w_na, w_ret, w_mem, proj, proj, proj, proj, proj, *side)
    return out[0], out[1:]


def _out_proj_kernel(m_ref, w_ref, x_ref, o_ref, wb_ref):
    @pl.when(pl.program_id(0) == 0)
    def _():
        wb_ref[...] = w_ref[...].astype(BF16)

    o_ref[...] = x_ref[...] + jnp.dot(m_ref[...], wb_ref[...], preferred_element_type=F32)


def _out_proj(merged, w, x2d, tm):
    t, k = merged.shape
    d = w.shape[1]
    return pl.pallas_call(
        _out_proj_kernel,
        grid=(t // tm,),
        in_specs=[pl.BlockSpec((tm, k), lambda i: (i, 0)),
                  pl.BlockSpec((k, d), lambda i: (0, 0), pipeline_mode=pl.Buffered(1)),
                  pl.BlockSpec((tm, d), lambda i: (i, 0))],
        out_specs=pl.BlockSpec((tm, d), lambda i: (i, 0)),
        out_shape=jax.ShapeDtypeStruct((t, d), F32),
        scratch_shapes=[pltpu.VMEM((k, d), BF16)],
        compiler_params=_params("arbitrary"),
        name="out_proj",
    )(merged, w, x2d)


def _ffn_kernel(x_ref, g_ref, w1_ref, w2_ref, o_ref, h_ref):
    def mlp(h):
        a = jnp.maximum(jnp.dot(h, w1_ref[...], preferred_element_type=F32), 0.0)
        return jnp.dot((a * a).astype(BF16), w2_ref[...], preferred_element_type=F32)

    @pl.when(pl.program_id(1) == 0)
    def _():
        x = x_ref[...]
        h = _rms(x, g_ref[...]).astype(BF16)
        h_ref[...] = h
        o_ref[...] = x + mlp(h)

    @pl.when(pl.program_id(1) != 0)
    def _():
        o_ref[...] += mlp(h_ref[...])


def _ffn(x1, g, w1, w2, tm, tf):
    t, d = x1.shape
    dff = w1.shape[1]
    return pl.pallas_call(
        _ffn_kernel,
        grid=(t // tm, dff // tf),
        in_specs=[pl.BlockSpec((tm, d), lambda i, f: (i, 0)),
                  pl.BlockSpec((1, d), lambda i, f: (0, 0)),
                  pl.BlockSpec((d, tf), lambda i, f: (0, f)),
                  pl.BlockSpec((tf, d), lambda i, f: (f, 0))],
        out_specs=pl.BlockSpec((tm, d), lambda i, f: (i, 0)),
        out_shape=jax.ShapeDtypeStruct((t, d), F32),
        scratch_shapes=[pltpu.VMEM((tm, d), BF16)],
        compiler_params=_params("parallel", "arbitrary"),
        name="ffn",
    )(x1, g.reshape(1, d), w1, w2)


def _rope_tables(s_len):
    half = RET_QK_DIM // 2
    inv = np.power(np.float64(ROPE_BASE), -np.arange(half, dtype=np.float64) / half)
    ang = np.arange(s_len, dtype=np.float64)[:, None] * inv[None, :]
    cos, sin = np.cos(ang), np.sin(ang)
    return (jnp.asarray(np.concatenate([cos, cos], axis=1), F32),
            jnp.asarray(np.concatenate([-sin, sin], axis=1), F32))


class _Tiles(NamedTuple):
    tm: int
    tn_in: int
    tm_merge: int
    tn_merge: int
    tm_out: int
    tf: int


def _tiles(t):
    return _Tiles(tm=min(1024, t), tn_in=1024, tm_merge=min(256, t), tn_merge=2048, tm_out=min(512, t), tf=1024)


def kernel(x, mem, norm_mix_g, w_in, na_q_norm_g, na_k_norm_g, na_rpb, ret_decay_logit_fwd, ret_decay_logit_bwd, ret_gn_g, mem_norm_g, w_mem_kv, xa_q_norm_g, xa_k_norm_g, w_br_na, w_br_ret, w_br_mem, w_out, norm_ffn_g, w_ff1, w_ff2):
    batch, s_len, d = x.shape
    n_mem = mem.shape[1]
    t = batch * s_len
    tiles = _tiles(t)
    cos2, sin2 = _rope_tables(s_len)
    x2d = x.reshape(t, d)
    mem2d = mem.reshape(batch * n_mem, d)
    gm, gn = t // tiles.tm, w_in.shape[2] // tiles.tn_in

    def side_job(w):
        r, c = w.shape
        if r % (gm * gn * BF16_SUBLANES) == 0:
            return w, (r // (gm * gn), c), lambda i, j: (i * gn + j, 0)
        assert r % (gn * BF16_SUBLANES) == 0 and c % (gm * LANE) == 0, (w.shape, gm, gn)
        return w, (r // gn, c // gm), lambda i, j: (j, i)

    for l in range(w_in.shape[0]):
        sides = [side_job(w) for w in (w_br_na[l], w_br_ret[l], w_br_mem[l])]
        proj, (w_na_bf, w_ret_bf, w_mem_bf) = _norm_proj(
            x2d, norm_mix_g[l], w_in[l], tiles.tm, tiles.tn_in, side=sides)

        o_mem = _mem_xattn(proj, mem2d, mem_norm_g[l], w_mem_kv[l], xa_q_norm_g[l], xa_k_norm_g[l],
                           batch, s_len, n_mem)
        o_na = _na_attention(proj, na_rpb[l], na_q_norm_g[l], na_k_norm_g[l], batch, s_len)
        o_ret = _retention(proj, cos2, sin2, ret_decay_logit_fwd[l], ret_decay_logit_bwd[l], ret_gn_g[l],
                           batch, s_len)

        merged, (w_ff1_bf, w_ff2_bf) = _merge(o_na, o_ret, o_mem, w_na_bf, w_ret_bf, w_mem_bf, proj,
                                              tiles.tm_merge, tiles.tn_merge, side=(w_ff1[l], w_ff2[l]))
        x1 = _out_proj(merged, w_out[l], x2d, tiles.tm_out)
        x2d = _ffn(x1, norm_ffn_g[l], w_ff1_bf, w_ff2_bf, tiles.tm_out, tiles.tf)
    return x2d.reshape(batch, s_len, d)
```

```python
import functools
from typing import NamedTuple

import jax
import jax.numpy as jnp
import numpy as np
from jax import lax
from jax.experimental import pallas as pl
from jax.experimental.pallas import tpu as pltpu

F32 = jnp.float32
BF16 = jnp.bfloat16

LANE = 128
BF16_SUBLANES = 16
EPS = 1e-6
NEG = -1e30
LOG2E = 1.4426950408889634

GRID_W = 64
NA_HEADS = 8
NA_HEAD_DIM = 128
NA_MAX_ROWS = 8
NA_COLS = 16
NA_DR = 2 * NA_MAX_ROWS - 1
NA_DC = 2 * NA_COLS - 1
NA_TBL = 3 * NA_DR + 1
NA_QROWS = 4
NA_WIN = NA_QROWS + NA_MAX_ROWS

RET_HEADS = 8
RET_QK_DIM = 128
RET_V_DIM = 256
RET_BLOCK = 256
ROPE_BASE = 10000.0

XA_HEADS = 4
XA_HEAD_DIM = 256
XA_QROWS = 512

SLAB_NA_Q, SLAB_NA_K, SLAB_NA_V = 0, 8, 16
SLAB_RQ, SLAB_RK, SLAB_RV, SLAB_RG = 24, 32, 40, 56
SLAB_XQ = 72
SLAB_G_NA, SLAB_G_RET, SLAB_G_MEM = 80, 96, 112

V7X_VMEM_BYTES = 64 * 2**20
VMEM_LIMIT_BYTES = V7X_VMEM_BYTES - 6 * 2**20


def _params(*sem):
    return pltpu.CompilerParams(dimension_semantics=sem, vmem_limit_bytes=VMEM_LIMIT_BYTES)


def _rms(x, g):
    return x * lax.rsqrt(jnp.mean(x * x, axis=-1, keepdims=True) + EPS) * g


def _unit_rms(x):
    return x * lax.rsqrt(jnp.mean(x * x, axis=-1, keepdims=True) + EPS)


def _sigmoid(x):
    return 0.5 * jnp.tanh(0.5 * x) + 0.5


def _norm_proj_kernel(n_side, x_ref, g_ref, w_ref, *refs):
    side_in, o_ref = refs[:n_side], refs[n_side]
    side_out, hn_ref = refs[n_side + 1:2 * n_side + 1], refs[2 * n_side + 1]

    @pl.when(pl.program_id(1) == 0)
    def _():
        hn_ref[...] = _rms(x_ref[...], g_ref[...]).astype(BF16)

    acc = jnp.dot(hn_ref[...], w_ref[...].astype(BF16), preferred_element_type=F32)
    for s in range(o_ref.shape[0]):
        o_ref[s] = acc[:, s * LANE:(s + 1) * LANE].astype(BF16)
    for src, dst in zip(side_in, side_out):
        dst[...] = src[...].astype(BF16)


def _norm_proj(x2d, g, w, tm, tn, side=()):
    m, k = x2d.shape
    n = w.shape[1]
    side_specs = [pl.BlockSpec(blk, imap) for _, blk, imap in side]
    out = pl.pallas_call(
        functools.partial(_norm_proj_kernel, len(side)),
        grid=(m // tm, n // tn),
        in_specs=[
            pl.BlockSpec((tm, k), lambda i, j: (i, 0)),
            pl.BlockSpec((1, k), lambda i, j: (0, 0)),
            pl.BlockSpec((k, tn), lambda i, j: (0, j)),
        ] + side_specs,
        out_specs=[pl.BlockSpec((tn // LANE, tm, LANE), lambda i, j: (j, i, 0))] + side_specs,
        out_shape=[jax.ShapeDtypeStruct((n // LANE, m, LANE), BF16)]
        + [jax.ShapeDtypeStruct(a.shape, BF16) for a, _, _ in side],
        scratch_shapes=[pltpu.VMEM((tm, k), BF16)],
        compiler_params=_params("parallel", "arbitrary"),
        name="norm_proj",
    )(x2d, g.reshape(1, k), w, *[a for a, _, _ in side])
    return out[0], out[1:]


def _na_build_tables(h, rpb_ref, t_ref, base_ref):
    qc = lax.broadcasted_iota(jnp.int32, (GRID_W, LANE), 0)
    lane = lax.broadcasted_iota(jnp.int32, (GRID_W, LANE), 1)
    kc = lane & (GRID_W - 1)
    d = jnp.clip(kc - qc, -(NA_COLS - 1), NA_COLS - 1) + (NA_COLS - 1)
    cs = jnp.clip(qc - NA_COLS // 2, 0, GRID_W - NA_COLS)
    col_ok = (kc >= cs) & (kc < cs + NA_COLS)
    left = lane < GRID_W
    neg = jnp.full((GRID_W, LANE), NEG, F32)

    def body(dr, _):
        base = (h * NA_DR + dr) * NA_DC
        val = jnp.zeros((GRID_W, LANE), F32)
        for dd in range(NA_DC):
            val = jnp.where(d == dd, rpb_ref[base + dd], val)
        base_ref[dr] = jnp.where(col_ok, val * LOG2E, NEG)
        return 0

    lax.fori_loop(0, NA_DR, body, 0)
    for dr in range(NA_DR):
        second = base_ref[dr + 1] if dr + 1 < NA_DR else neg
        t_ref[dr] = jnp.where(left, base_ref[dr], second)
        t_ref[NA_DR + dr] = jnp.where(left, base_ref[dr], neg)
        t_ref[2 * NA_DR + dr] = jnp.where(left, neg, base_ref[dr])
    t_ref[3 * NA_DR] = neg


def _na_kernel(rpb_ref, q_ref, k_ref, v_ref, gq_ref, gk_ref, o_ref, kt_ref, va_ref, t_ref, base_ref):
    s_len = q_ref.shape[1]

    @pl.when(pl.program_id(1) == 0)
    def _():
        _na_build_tables(pl.program_id(0), rpb_ref, t_ref, base_ref)
        va_ref[:, NA_HEAD_DIM:] = jnp.ones((s_len, NA_HEAD_DIM), BF16)

    rows = s_len // GRID_W
    step_tok = NA_QROWS * GRID_W
    win_tok = NA_WIN * GRID_W
    n_tiles = NA_WIN // 2
    gq = gq_ref[...] * gk_ref[...] * (NA_HEAD_DIM ** -0.5 * LOG2E)
    va_ref[:, :NA_HEAD_DIM] = v_ref[0]

    def knorm(c, _):
        sl = pl.ds(pl.multiple_of(c * LANE, LANE), LANE)
        kt_ref[c] = _unit_rms(k_ref[0, sl, :].astype(F32)).astype(BF16).T
        return 0

    lax.fori_loop(0, s_len // LANE, knorm, 0, unroll=True)

    def step(i, _):
        r0 = NA_QROWS * i
        ws = jnp.clip(r0 - NA_MAX_ROWS // 2, 0, rows - NA_WIN)
        wp = ws // 2
        qsl = pl.ds(pl.multiple_of(i * step_tok, step_tok), step_tok)
        wsl = pl.ds(pl.multiple_of(ws * GRID_W, LANE), win_tok)
        qn = _rms(q_ref[0, qsl, :].astype(F32), gq).astype(BF16)
        kwin = jnp.concatenate([kt_ref[wp + t] for t in range(n_tiles)], axis=1)
        s = jnp.dot(qn, kwin, preferred_element_type=F32)
        bias_rows = []
        for qr in range(NA_QROWS):
            r = r0 + qr
            rs = jnp.clip(r - NA_MAX_ROWS // 2, 0, rows - NA_MAX_ROWS)
            tiles = []
            for t in range(n_tiles):
                ka = ws + 2 * t
                dr = ka - r + (NA_MAX_ROWS - 1)
                va = (ka >= rs) & (ka < rs + NA_MAX_ROWS)
                vb = (ka + 1 >= rs) & (ka + 1 < rs + NA_MAX_ROWS)
                idx = jnp.where(va, jnp.where(vb, dr, NA_DR + dr),
                                jnp.where(vb, 2 * NA_DR + dr + 1, 3 * NA_DR))
                tiles.append(t_ref[idx])
            bias_rows.append(jnp.concatenate(tiles, axis=1))
        s = s + jnp.concatenate(bias_rows, axis=0)
        e = jnp.exp2(s - jnp.max(s, axis=-1, keepdims=True))
        o = jnp.dot(e.astype(BF16), va_ref[wsl, :], preferred_element_type=F32)
        o_ref[qsl, :] = (o[:, :NA_HEAD_DIM] / o[:, NA_HEAD_DIM:]).astype(BF16)
        return 0

    lax.fori_loop(0, rows // NA_QROWS, step, 0, unroll=True)


def _na_attention(proj, rpb, gq, gk, batch, s_len):
    t = batch * s_len
    g_spec = pl.BlockSpec((1, NA_HEAD_DIM), lambda h, b: (0, 0))
    return pl.pallas_call(
        _na_kernel,
        grid=(NA_HEADS, batch),
        in_specs=[
            pl.BlockSpec(memory_space=pltpu.SMEM),
            pl.BlockSpec((1, s_len, LANE), lambda h, b: (SLAB_NA_Q + h, b, 0)),
            pl.BlockSpec((1, s_len, LANE), lambda h, b: (SLAB_NA_K + h, b, 0)),
            pl.BlockSpec((1, s_len, LANE), lambda h, b: (SLAB_NA_V + h, b, 0)),
            g_spec, g_spec,
        ],
        out_specs=pl.BlockSpec((s_len, NA_HEAD_DIM), lambda h, b: (b, h)),
        out_shape=jax.ShapeDtypeStruct((t, NA_HEADS * NA_HEAD_DIM), BF16),
        scratch_shapes=[pltpu.VMEM((s_len // LANE, NA_HEAD_DIM, LANE), BF16),
                        pltpu.VMEM((s_len, 2 * NA_HEAD_DIM), BF16),
                        pltpu.VMEM((NA_TBL, GRID_W, LANE), F32),
                        pltpu.VMEM((NA_DR, GRID_W, LANE), F32)],
        compiler_params=_params("arbitrary", "arbitrary"),
        name="na_attention",
    )(rpb.reshape(-1), proj, proj, proj, gq.reshape(1, -1), gk.reshape(1, -1))


def _log_sigmoid(x):
    return -(jnp.maximum(-x, 0.0) + jnp.log1p(jnp.exp(-jnp.abs(x))))


def _ret_kernel(lf_ref, lb_ref, q_ref, k_ref, v_ref, cos_ref, sin_ref,
                o_ref, qr_ref, kt_ref, d_ref, qdf_ref, qdb_ref, kv_ref, s_ref):
    s_len = q_ref.shape[1]
    c = RET_BLOCK
    nb = s_len // c
    dk = RET_QK_DIM
    half = dk // 2

    h = pl.program_id(1)
    lgf = _log_sigmoid(jnp.full((1, 1), lf_ref[h], F32))
    lgb = _log_sigmoid(jnp.full((1, 1), lb_ref[h], F32))
    ic = lax.broadcasted_iota(jnp.int32, (c, 1), 0).astype(F32)
    jr = lax.broadcasted_iota(jnp.int32, (1, c), 1).astype(F32)
    diff = ic - jr
    scale = dk ** -0.5
    d_ref[...] = jnp.exp(jnp.where(diff >= 0, lgf, lgb) * jnp.abs(diff)) * scale
    qdf_ref[...] = jnp.broadcast_to(jnp.exp(lgf * (ic + 1.0)), (c, dk))
    qdb_ref[...] = jnp.broadcast_to(jnp.exp(lgb * (c - ic)), (c, dk))
    kdf = jnp.exp(lgf * (c - 1.0 - jr)) * scale
    kdb = jnp.exp(lgb * jr) * scale
    cd_f = jnp.exp(lgf * c)
    cd_b = jnp.exp(lgb * c)

    def block_v(sl):
        return jnp.concatenate([v_ref[0, sl, :], v_ref[1, sl, :]], axis=1)

    def prep(n, _):
        sl = pl.ds(pl.multiple_of(n * c, c), c)
        cos = cos_ref[sl, :]
        sin = sin_ref[sl, :]
        q = q_ref[0, sl, :].astype(F32)
        k = k_ref[0, sl, :].astype(F32)
        qr_ref[sl, :] = (q * cos + pltpu.roll(q, half, 1) * sin).astype(BF16)
        kt = (k * cos + pltpu.roll(k, half, 1) * sin).T
        kt_ref[n] = kt.astype(BF16)
        lhs = jnp.concatenate([(kt * kdf).astype(BF16), (kt * kdb).astype(BF16)], axis=0)
        kv_ref[n] = jnp.dot(lhs, block_v(sl), preferred_element_type=F32)
        return 0

    lax.fori_loop(0, nb, prep, 0, unroll=True)

    def scan_f(n, sf):
        s_ref[n, :dk, :] = sf.astype(BF16)
        return cd_f * sf + kv_ref[n, :dk, :]

    def scan_b(t, sb):
        n = nb - 1 - t
        s_ref[n, dk:, :] = sb.astype(BF16)
        return cd_b * sb + kv_ref[n, dk:, :]

    zero = jnp.zeros((dk, RET_V_DIM), F32)

    def scan(n, carry):
        return scan_f(n, carry[0]), scan_b(n, carry[1])

    lax.fori_loop(0, nb, scan, (zero, zero), unroll=True)

    def out(n, _):
        sl = pl.ds(pl.multiple_of(n * c, c), c)
        q = qr_ref[sl, :]
        qf32 = q.astype(F32)
        a = jnp.dot(q, kt_ref[n], preferred_element_type=F32) * d_ref[...]
        lhs = jnp.concatenate([a.astype(BF16), (qf32 * qdf_ref[...]).astype(BF16),
                               (qf32 * qdb_ref[...]).astype(BF16)], axis=1)
        rhs = jnp.concatenate([block_v(sl), s_ref[n]], axis=0)
        o = jnp.dot(lhs, rhs, preferred_element_type=F32)
        mu = jnp.mean(o, axis=-1, keepdims=True)
        oc = o - mu
        y = oc * lax.rsqrt(jnp.mean(oc * oc, axis=-1, keepdims=True) + EPS)
        o_ref[sl, :] = y.astype(BF16)
        return 0

    lax.fori_loop(0, nb, out, 0, unroll=True)


def _retention(proj, cos2, sin2, lf, lb, batch, s_len):
    t = batch * s_len
    nb = s_len // RET_BLOCK
    smem = pl.BlockSpec(memory_space=pltpu.SMEM)
    return pl.pallas_call(
        _ret_kernel,
        grid=(batch, RET_HEADS),
        in_specs=[
            smem, smem,
            pl.BlockSpec((1, s_len, LANE), lambda b, h: (SLAB_RQ + h, b, 0)),
            pl.BlockSpec((1, s_len, LANE), lambda b, h: (SLAB_RK + h, b, 0)),
            pl.BlockSpec((2, s_len, LANE), lambda b, h: (SLAB_RV // 2 + h, b, 0)),
            pl.BlockSpec((s_len, RET_QK_DIM), lambda b, h: (0, 0)),
            pl.BlockSpec((s_len, RET_QK_DIM), lambda b, h: (0, 0)),
        ],
        out_specs=pl.BlockSpec((s_len, RET_V_DIM), lambda b, h: (b, h)),
        out_shape=jax.ShapeDtypeStruct((t, RET_HEADS * RET_V_DIM), BF16),
        scratch_shapes=[
            pltpu.VMEM((s_len, RET_QK_DIM), BF16),
            pltpu.VMEM((nb, RET_QK_DIM, RET_BLOCK), BF16),
            pltpu.VMEM((RET_BLOCK, RET_BLOCK), F32),
            pltpu.VMEM((RET_BLOCK, RET_QK_DIM), F32),
            pltpu.VMEM((RET_BLOCK, RET_QK_DIM), F32),
            pltpu.VMEM((nb, 2 * RET_QK_DIM, RET_V_DIM), F32),
            pltpu.VMEM((nb, 2 * RET_QK_DIM, RET_V_DIM), BF16),
        ],
        compiler_params=_params("parallel", "parallel"),
        name="retention",
    )(lf, lb, proj, proj, proj, cos2, sin2)


def _xa_kernel(q_ref, mem_ref, gm_ref, wk_ref, wv_ref, gq_ref, gk_ref, o_ref):
    s_len = q_ref.shape[1]
    tq = XA_QROWS
    gq = gq_ref[...] * gk_ref[...] * (XA_HEAD_DIM ** -0.5 * LOG2E)
    mn = _rms(mem_ref[...], gm_ref[...]).astype(BF16)
    k = jnp.dot(mn, wk_ref[...].astype(BF16), preferred_element_type=F32)
    kn = _unit_rms(k).astype(BF16)
    v = jnp.dot(mn, wv_ref[...].astype(BF16), preferred_element_type=F32).astype(BF16)

    def body(i, _):
        sl = pl.ds(pl.multiple_of(i * tq, tq), tq)
        q = jnp.concatenate([q_ref[0, sl, :], q_ref[1, sl, :]], axis=1).astype(F32)
        qn = _rms(q, gq).astype(BF16)
        s = lax.dot_general(qn, kn, (((1,), (1,)), ((), ())), preferred_element_type=F32)
        e = jnp.exp2(s - jnp.max(s, axis=-1, keepdims=True))
        l = jnp.sum(e, axis=-1, keepdims=True)
        o = jnp.dot(e.astype(BF16), v, preferred_element_type=F32)
        o_ref[sl, :] = (o / l).astype(BF16)
        return 0

    lax.fori_loop(0, s_len // tq, body, 0, unroll=True)


def _mem_xattn(proj, mem2d, gm, w_kv, gq, gk, batch, s_len, n_mem):
    t = batch * s_len
    d = mem2d.shape[1]
    g_spec = pl.BlockSpec((1, XA_HEAD_DIM), lambda b, h: (0, 0))
    return pl.pallas_call(
        _xa_kernel,
        grid=(batch, XA_HEADS),
        in_specs=[
            pl.BlockSpec((2, s_len, LANE), lambda b, h: (SLAB_XQ // 2 + h, b, 0)),
            pl.BlockSpec((n_mem, d), lambda b, h: (b, 0)),
            pl.BlockSpec((1, d), lambda b, h: (0, 0)),
            pl.BlockSpec((d, XA_HEAD_DIM), lambda b, h: (0, h)),
            pl.BlockSpec((d, XA_HEAD_DIM), lambda b, h: (0, XA_HEADS + h)),
            g_spec, g_spec,
        ],
        out_specs=pl.BlockSpec((s_len, XA_HEAD_DIM), lambda b, h: (b, h)),
        out_shape=jax.ShapeDtypeStruct((t, XA_HEADS * XA_HEAD_DIM), BF16),
        compiler_params=_params("parallel", "parallel"),
        name="mem_xattn",
    )(proj, mem2d, gm.reshape(1, d), w_kv, w_kv, gq.reshape(1, -1), gk.reshape(1, -1))


def _merge_kernel(n_side, ona_ref, oret_ref, omem_ref, wna_ref, wret_ref, wmem_ref,
                  gna_ref, gret_ref, gmem_ref, rga_ref, rgb_ref, gn_ref, *refs):
    side_in, o_ref, side_out = refs[:n_side], refs[n_side], refs[n_side + 1:]
    for src, dst in zip(side_in, side_out):
        dst[...] = src[...].astype(BF16)
    y_na = jnp.dot(ona_ref[...], wna_ref[...], preferred_element_type=F32)
    rg = jnp.concatenate([r[s] for r in (rga_ref, rgb_ref) for s in range(r.shape[0])], axis=1).astype(F32)
    ret_in = (oret_ref[...].astype(F32) * (gn_ref[...] * rg * _sigmoid(rg))).astype(BF16)
    y_ret = jnp.dot(ret_in, wret_ref[...], preferred_element_type=F32)
    y_mem = jnp.dot(omem_ref[...], wmem_ref[...], preferred_element_type=F32)
    for j in range(gna_ref.shape[0]):
        cs = slice(j * LANE, (j + 1) * LANE)
        o_ref[:, cs] = (_sigmoid(gna_ref[j].astype(F32)) * y_na[:, cs]
                        + _sigmoid(gret_ref[j].astype(F32)) * y_ret[:, cs]
                        + _sigmoid(gmem_ref[j].astype(F32)) * y_mem[:, cs]).astype(BF16)


def _merge(o_na, o_ret, o_mem, w_na, w_ret, w_mem, proj, gn, tm, tn, side=()):
    t = o_na.shape[0]
    d = w_na.shape[1]
    ns = tn // LANE
    steps = (t // tm) * (d // tn)
    assert all(w.shape[0] % (steps * BF16_SUBLANES) == 0 for w in side)
    side_specs = [pl.BlockSpec((w.shape[0] // steps, w.shape[1]), lambda i, j: (i * (d // tn) + j, 0))
                  for w in side]

    def lhs_spec(a):
        return pl.BlockSpec((tm, a.shape[1]), lambda i, j: (i, 0))

    def w_spec(w):
        mode = dict(pipeline_mode=pl.Buffered(1)) if tn == d else {}
        return pl.BlockSpec((w.shape[0], tn), lambda i, j: (0, j), **mode)

    def gate_spec(first):
        return pl.BlockSpec((ns, tm, LANE), lambda i, j: (first // ns + j, i, 0))

    rg_half = (SLAB_XQ - SLAB_RG) // 2
    assert SLAB_RG % rg_half == 0 and o_ret.shape[1] == 2 * rg_half * LANE
    rg_specs = [pl.BlockSpec((rg_half, tm, LANE), lambda i, j, k=k: (SLAB_RG // rg_half + k, i, 0)) for k in (0, 1)]

    out = pl.pallas_call(
        functools.partial(_merge_kernel, len(side)),
        grid=(t // tm, d // tn),
        in_specs=[lhs_spec(o_na), lhs_spec(o_ret), lhs_spec(o_mem),
                  w_spec(w_na), w_spec(w_ret), w_spec(w_mem),
                  gate_spec(SLAB_G_NA), gate_spec(SLAB_G_RET), gate_spec(SLAB_G_MEM)] + rg_specs
        + [pl.BlockSpec((1, gn.shape[0]), lambda i, j: (0, 0))] + side_specs,
        out_specs=[pl.BlockSpec((tm, tn), lambda i, j: (i, j))] + side_specs,
        out_shape=[jax.ShapeDtypeStruct((t, d), BF16)] + [jax.ShapeDtypeStruct(w.shape, BF16) for w in side],
        compiler_params=_params("parallel", "arbitrary"),
        name="merge",
    )(o_na, o_ret, o_mem, w_na, w_ret, w_mem, proj, proj, proj, proj, proj, gn.reshape(1, -1), *side)
    return out[0], out[1:]


def _out_proj_kernel(m_ref, w_ref, x_ref, o_ref, wb_ref):
    @pl.when(pl.program_id(0) == 0)
    def _():
        wb_ref[...] = w_ref[...].astype(BF16)

    o_ref[...] = x_ref[...] + jnp.dot(m_ref[...], wb_ref[...], preferred_element_type=F32)


def _out_proj(merged, w, x2d, tm):
    t, k = merged.shape
    d = w.shape[1]
    return pl.pallas_call(
        _out_proj_kernel,
        grid=(t // tm,),
        in_specs=[pl.BlockSpec((tm, k), lambda i: (i, 0)),
                  pl.BlockSpec((k, d), lambda i: (0, 0), pipeline_mode=pl.Buffered(1)),
                  pl.BlockSpec((tm, d), lambda i: (i, 0))],
        out_specs=pl.BlockSpec((tm, d), lambda i: (i, 0)),
        out_shape=jax.ShapeDtypeStruct((t, d), F32),
        scratch_shapes=[pltpu.VMEM((k, d), BF16)],
        compiler_params=_params("arbitrary"),
        name="out_proj",
    )(merged, w, x2d)


def _ffn_kernel(x_ref, g_ref, w1_ref, w2_ref, o_ref, h_ref):
    def mlp(h):
        a = jnp.maximum(jnp.dot(h, w1_ref[...], preferred_element_type=F32), 0.0)
        return jnp.dot((a * a).astype(BF16), w2_ref[...], preferred_element_type=F32)

    @pl.when(pl.program_id(1) == 0)
    def _():
        x = x_ref[...]
        h = _rms(x, g_ref[...]).astype(BF16)
        h_ref[...] = h
        o_ref[...] = x + mlp(h)

    @pl.when(pl.program_id(1) != 0)
    def _():
        o_ref[...] += mlp(h_ref[...])


def _ffn(x1, g, w1, w2, tm, tf):
    t, d = x1.shape
    dff = w1.shape[1]
    return pl.pallas_call(
        _ffn_kernel,
        grid=(t // tm, dff // tf),
        in_specs=[pl.BlockSpec((tm, d), lambda i, f: (i, 0)),
                  pl.BlockSpec((1, d), lambda i, f: (0, 0)),
                  pl.BlockSpec((d, tf), lambda i, f: (0, f)),
                  pl.BlockSpec((tf, d), lambda i, f: (f, 0))],
        out_specs=pl.BlockSpec((tm, d), lambda i, f: (i, 0)),
        out_shape=jax.ShapeDtypeStruct((t, d), F32),
        scratch_shapes=[pltpu.VMEM((tm, d), BF16)],
        compiler_params=_params("parallel", "arbitrary"),
        name="ffn",
    )(x1, g.reshape(1, d), w1, w2)


def _rope_tables(s_len):
    half = RET_QK_DIM // 2
    inv = np.power(np.float64(ROPE_BASE), -np.arange(half, dtype=np.float64) / half)
    ang = np.arange(s_len, dtype=np.float64)[:, None] * inv[None, :]
    cos, sin = np.cos(ang), np.sin(ang)
    return (jnp.asarray(np.concatenate([cos, cos], axis=1), F32),
            jnp.asarray(np.concatenate([-sin, sin], axis=1), F32))


class _Tiles(NamedTuple):
    tm: int
    tn_in: int
    tm_merge: int
    tn_merge: int
    tm_out: int
    tf: int


def _tiles(t):
    return _Tiles(tm=min(1024, t), tn_in=1024, tm_merge=min(256, t), tn_merge=2048, tm_out=min(512, t), tf=1024)


def kernel(x, mem, norm_mix_g, w_in, na_q_norm_g, na_k_norm_g, na_rpb, ret_decay_logit_fwd, ret_decay_logit_bwd, ret_gn_g, mem_norm_g, w_mem_kv, xa_q_norm_g, xa_k_norm_g, w_br_na, w_br_ret, w_br_mem, w_out, norm_ffn_g, w_ff1, w_ff2):
    batch, s_len, d = x.shape
    n_mem = mem.shape[1]
    t = batch * s_len
    tiles = _tiles(t)
    cos2, sin2 = _rope_tables(s_len)
    x2d = x.reshape(t, d)
    mem2d = mem.reshape(batch * n_mem, d)
    gm, gn = t // tiles.tm, w_in.shape[2] // tiles.tn_in

    def side_job(w):
        r, c = w.shape
        if r % (gm * gn * BF16_SUBLANES) == 0:
            return w, (r // (gm * gn), c), lambda i, j: (i * gn + j, 0)
        assert r % (gn * BF16_SUBLANES) == 0 and c % (gm * LANE) == 0, (w.shape, gm, gn)
        return w, (r // gn, c // gm), lambda i, j: (j, i)

    for l in range(w_in.shape[0]):
        sides = [side_job(w) for w in (w_br_na[l], w_br_ret[l], w_br_mem[l])]
        proj, (w_na_bf, w_ret_bf, w_mem_bf) = _norm_proj(
            x2d, norm_mix_g[l], w_in[l], tiles.tm, tiles.tn_in, side=sides)

        o_mem = _mem_xattn(proj, mem2d, mem_norm_g[l], w_mem_kv[l], xa_q_norm_g[l], xa_k_norm_g[l],
                           batch, s_len, n_mem)
        o_na = _na_attention(proj, na_rpb[l], na_q_norm_g[l], na_k_norm_g[l], batch, s_len)
        o_ret = _retention(proj, cos2, sin2, ret_decay_logit_fwd[l], ret_decay_logit_bwd[l], batch, s_len)

        merged, (w_ff1_bf, w_ff2_bf) = _merge(o_na, o_ret, o_mem, w_na_bf, w_ret_bf, w_mem_bf, proj, ret_gn_g[l],
                                              tiles.tm_merge, tiles.tn_merge, side=(w_ff1[l], w_ff2[l]))
        x1 = _out_proj(merged, w_out[l], x2d, tiles.tm_out)
        x2d = _ffn(x1, norm_ffn_g[l], w_ff1_bf, w_ff2_bf, tiles.tm_out, tiles.tf)
    return x2d.reshape(batch, s_len, d)
```
